```python
import jax, jax.numpy as jnp
from jax import lax
import numpy as np

D_MODEL = 1024
BATCH = 8
SEQ = 2048
DEPTH = 1
DEC_BATCH = 128
DEC_SEQ = 1
PAST_LEN = 16384
PAGE_SIZE = 128

GLA_HEADS = 4
GLA_DK_T = D_MODEL // 2
GLA_DV_T = D_MODEL
GLA_DK = GLA_DK_T // GLA_HEADS
GLA_DV = GLA_DV_T // GLA_HEADS
GK_RANK = 16
GATE_NORMALIZER = 16.0
GLA_CHUNK = 64
D_GMLP = D_MODEL
MIX_HEADS = 4
MIX_DH = D_GMLP // MIX_HEADS
MIX_CHUNK = 128
N_EXPERTS = 32
TOP_K = 4
D_FF = D_MODEL
SWIGLU_LIMIT = 7.0
SWIGLU_ALPHA = 1.702
MOE_BLOCK = 128
D_PLE = 256
EPS = 1e-6
SPLIT_SIZES = (GLA_DK_T, GLA_DK_T, GLA_DV_T, GLA_DV_T, GK_RANK, D_GMLP, D_GMLP, D_MODEL, D_MODEL)
N_IN = 2 * GLA_DK_T + 2 * GLA_DV_T + GK_RANK + 2 * D_GMLP + 2 * D_MODEL

kernel_name = 'gla_gmlp_moe_hybrid_step'


def rms_norm(x, g):
    xf = x.astype(jnp.float32)
    y = xf * lax.rsqrt(jnp.mean(xf * xf, axis=-1, keepdims=True) + EPS)
    return (y * g.astype(jnp.float32)).astype(x.dtype)


def layer_norm(x, g, b):
    xf = x.astype(jnp.float32)
    mu = jnp.mean(xf, axis=-1, keepdims=True)
    var = jnp.mean(jnp.square(xf - mu), axis=-1, keepdims=True)
    y = (xf - mu) * lax.rsqrt(var + EPS)
    return (y * g.astype(jnp.float32) + b.astype(jnp.float32)).astype(x.dtype)


def gla_chunked(q, k, v, log_a, s0):
    N, L, H, _ = q.shape
    C = min(GLA_CHUNK, L)
    n_c = -(-L // C)
    pad = n_c * C - L

    def blocks(t):
        t = jnp.pad(t.astype(jnp.float32), ((0, 0), (0, pad), (0, 0), (0, 0)))
        return t.reshape(N, n_c, C, H, t.shape[-1]).transpose(1, 0, 3, 2, 4)

    mask = jnp.tril(jnp.ones((C, C), dtype=bool))

    def step(S, blk):
        qc, kc, vc, gc = blk
        b = jnp.cumsum(gc, axis=2)
        qe = qc * jnp.exp(b)
        ke = kc * jnp.exp(-b)
        att = jnp.where(mask, jnp.einsum('nhtk,nhsk->nhts', qe, ke), 0.0)
        o = jnp.einsum('nhts,nhsv->nhtv', att, vc) + jnp.einsum('nhtk,nhkv->nhtv', qe, S)
        b_last = b[:, :, -1:, :]
        S = jnp.exp(b_last[:, :, 0, :, None]) * S + jnp.einsum('nhsk,nhsv->nhkv', kc * jnp.exp(b_last - b), vc)
        return S, o

    S, o = lax.scan(step, s0.astype(jnp.float32), (blocks(q), blocks(k), blocks(v), blocks(log_a)))
    o = o.transpose(1, 0, 3, 2, 4).reshape(N, n_c * C, H, -1)[:, :L]
    return o, S


def causal_chunk_mix(v, w_s, b_s):
    N, L = v.shape[:2]
    C = min(MIX_CHUNK, L)
    n_c = -(-L // C)
    v = jnp.pad(v, ((0, 0), (0, n_c * C - L), (0, 0), (0, 0)))
    vc = v.reshape(N, n_c, C, MIX_HEADS, MIX_DH)
    w = jnp.tril(w_s[:, :C, :C])
    out = jnp.einsum('hts,ncshd->ncthd', w, vc) + b_s[:, :C].T[None, None, :, :, None]
    return out.reshape(N, n_c * C, MIX_HEADS, MIX_DH)[:, :L]


def moe(x, w_router, b_router, w_gate, b_gate, w_up, b_up, w_down, b_down):
    T, D = x.shape
    logits = (x @ w_router + b_router).astype(jnp.float32)
    top_v, top_i = lax.top_k(logits, TOP_K)
    wts = jax.nn.softmax(top_v, axis=-1)
    TK = T * TOP_K
    e = top_i.reshape(TK)
    tok = jnp.repeat(jnp.arange(T, dtype=jnp.int32), TOP_K)
    w = wts.reshape(TK)
    order = jnp.argsort(e, stable=True)
    e_s, tok_s, w_s = e[order], tok[order], w[order]
    counts = jnp.bincount(e, length=N_EXPERTS)
    padded = (counts + MOE_BLOCK - 1) // MOE_BLOCK * MOE_BLOCK
    pad_end = jnp.cumsum(padded)
    pad_start = pad_end - padded
    raw_start = jnp.cumsum(counts) - counts
    dest = pad_start[e_s] + jnp.arange(TK) - raw_start[e_s]
    n_blocks = (TK + N_EXPERTS * (MOE_BLOCK - 1) + MOE_BLOCK - 1) // MOE_BLOCK
    P = n_blocks * MOE_BLOCK
    row_tok = jnp.full((P,), T, dtype=jnp.int32).at[dest].set(tok_s)
    row_w = jnp.zeros((P,), jnp.float32).at[dest].set(w_s)
    block_e = jnp.minimum(jnp.searchsorted(pad_end, jnp.arange(n_blocks) * MOE_BLOCK, side='right'), N_EXPERTS - 1)
    x_pad = jnp.concatenate([x, jnp.zeros((1, D), x.dtype)], axis=0)
    xb = x_pad[row_tok].reshape(n_blocks, MOE_BLOCK, D)

    def expert_block(args):
        xblk, eid = args
        g = xblk @ w_gate[eid] + b_gate[eid]
        u = xblk @ w_up[eid] + b_up[eid]
        g = jnp.minimum(g, SWIGLU_LIMIT)
        u = jnp.clip(u, -SWIGLU_LIMIT, SWIGLU_LIMIT)
        hmid = (u + 1.0) * (g * jax.nn.sigmoid(SWIGLU_ALPHA * g))
        return hmid @ w_down[eid] + b_down[eid]

    yb = lax.map(expert_block, (xb, block_e)).reshape(P, D)
    y = jax.ops.segment_sum(yb * row_w[:, None].astype(yb.dtype), row_tok, num_segments=T + 1)
    return y[:T]


def hybrid_layer(x, p, s0, norm1_g, w_in, w_gk2, b_gk, gla_norm_g, v_norm_g, v_norm_b, w_sp, b_sp, w_o,
                 norm2_g, w_router, b_router, w_gate, b_gate, w_up, b_up, w_down, b_down,
                 norm3_g, w_ple_gate, w_ple_proj):
    N, L, _ = x.shape
    hn = rms_norm(x, norm1_g)
    z = hn @ w_in
    idx = [int(c) for c in np.cumsum(SPLIT_SIZES)[:-1]]
    q, k, v, r, gk, u, vg, ga, gb = jnp.split(z, idx, axis=-1)
    q = q.reshape(N, L, GLA_HEADS, GLA_DK) * (GLA_DK ** -0.5)
    k = k.reshape(N, L, GLA_HEADS, GLA_DK)
    v = v.reshape(N, L, GLA_HEADS, GLA_DV)
    log_a = jax.nn.log_sigmoid((gk @ w_gk2 + b_gk).astype(jnp.float32)) / GATE_NORMALIZER
    log_a = log_a.reshape(N, L, GLA_HEADS, GLA_DK)
    o, s_new = gla_chunked(q, k, v, log_a, s0)
    o_a = rms_norm(o, gla_norm_g).reshape(N, L, GLA_DV_T).astype(x.dtype) * jax.nn.silu(r)
    u = jax.nn.gelu(u, approximate=False)
    vn = layer_norm(jax.nn.gelu(vg, approximate=False), v_norm_g, v_norm_b)
    mix = causal_chunk_mix(vn.reshape(N, L, MIX_HEADS, MIX_DH), w_sp, b_sp).reshape(N, L, D_GMLP)
    o_b = u * mix
    h = x + (jax.nn.sigmoid(ga) * o_a + jax.nn.sigmoid(gb) * o_b) @ w_o
    h = h + moe(rms_norm(h, norm2_g).reshape(N * L, D_MODEL), w_router, b_router,
                w_gate, b_gate, w_up, b_up, w_down, b_down).reshape(N, L, D_MODEL)
    h = h + jax.nn.sigmoid(rms_norm(h, norm3_g) @ w_ple_gate) * (p @ w_ple_proj)
    return h, s_new, vn


def setup_inputs(seed: int = 0) -> dict:
    key = jax.random.key(seed)
    ks = jax.random.split(key, 32)

    def nrm(k, shape, scale):
        return jax.random.normal(k, shape, jnp.float32) * scale

    return {
        'x_prompt': nrm(ks[0], (BATCH, SEQ, D_MODEL), 1.0),
        'x_sample': nrm(ks[1], (DEC_BATCH, DEC_SEQ, D_MODEL), 1.0),
        'state_gla': nrm(ks[2], (DEPTH, DEC_BATCH, GLA_HEADS, GLA_DK, GLA_DV), 0.5),
        'p_prompt': nrm(ks[3], (DEPTH, BATCH, SEQ, D_PLE), 1.0),
        'p_sample': nrm(ks[4], (DEPTH, DEC_BATCH, DEC_SEQ, D_PLE), 1.0),
        'norm1_g': 1.0 + nrm(ks[5], (DEPTH, D_MODEL), 0.05),
        'w_in': nrm(ks[6], (DEPTH, D_MODEL, N_IN), D_MODEL ** -0.5),
        'w_gk2': nrm(ks[7], (DEPTH, GK_RANK, GLA_DK_T), GK_RANK ** -0.5),
        'b_gk': nrm(ks[8], (DEPTH, GLA_DK_T), 0.1),
        'gla_norm_g': 1.0 + nrm(ks[9], (DEPTH, GLA_DV), 0.05),
        'v_norm_g': 1.0 + nrm(ks[10], (DEPTH, D_GMLP), 0.05),
        'v_norm_b': nrm(ks[11], (DEPTH, D_GMLP), 0.02),
        'w_sp': nrm(ks[12], (DEPTH, MIX_HEADS, MIX_CHUNK, MIX_CHUNK), MIX_CHUNK ** -0.5),
        'b_sp': 1.0 + nrm(ks[13], (DEPTH, MIX_HEADS, MIX_CHUNK), 0.1),
        'w_o': nrm(ks[14], (DEPTH, D_MODEL, D_MODEL), D_MODEL ** -0.5),
        'norm2_g': 1.0 + nrm(ks[15], (DEPTH, D_MODEL), 0.05),
        'w_router': nrm(ks[16], (DEPTH, D_MODEL, N_EXPERTS), D_MODEL ** -0.5),
        'b_router': nrm(ks[17], (DEPTH, N_EXPERTS), 0.01),
        'w_gate': nrm(ks[18], (DEPTH, N_EXPERTS, D_MODEL, D_FF), D_MODEL ** -0.5),
        'b_gate': nrm(ks[19], (DEPTH, N_EXPERTS, D_FF), 0.01),
        'w_up': nrm(ks[20], (DEPTH, N_EXPERTS, D_MODEL, D_FF), D_MODEL ** -0.5),
        'b_up': nrm(ks[21], (DEPTH, N_EXPERTS, D_FF), 0.01),
        'w_down': nrm(ks[22], (DEPTH, N_EXPERTS, D_FF, D_MODEL), D_FF ** -0.5),
        'b_down': nrm(ks[23], (DEPTH, N_EXPERTS, D_MODEL), 0.01),
        'norm3_g': 1.0 + nrm(ks[24], (DEPTH, D_MODEL), 0.05),
        'w_ple_gate': nrm(ks[25], (DEPTH, D_MODEL, D_MODEL), D_MODEL ** -0.5),
        'w_ple_proj': nrm(ks[26], (DEPTH, D_PLE, D_MODEL), D_PLE ** -0.5),
        'final_g': 1.0 + nrm(ks[27], (D_MODEL,), 0.05),
    }


def reference(x_prompt, x_sample, state_gla, p_prompt, p_sample, norm1_g, w_in, w_gk2, b_gk, gla_norm_g,
              v_norm_g, v_norm_b, w_sp, b_sp, w_o, norm2_g, w_router, b_router, w_gate, b_gate, w_up, b_up,
              w_down, b_down, norm3_g, w_ple_gate, w_ple_proj, final_g):
    hp, hs = x_prompt, x_sample
    st_prompt, st_sample, v_sample = [], [], []
    for i in range(DEPTH):
        lw = (norm1_g[i], w_in[i], w_gk2[i], b_gk[i], gla_norm_g[i], v_norm_g[i], v_norm_b[i], w_sp[i], b_sp[i],
              w_o[i], norm2_g[i], w_router[i], b_router[i], w_gate[i], b_gate[i], w_up[i], b_up[i],
              w_down[i], b_down[i], norm3_g[i], w_ple_gate[i], w_ple_proj[i])
        s0 = jnp.zeros((hp.shape[0], GLA_HEADS, GLA_DK, GLA_DV), jnp.float32)
        hp, sp, _ = hybrid_layer(hp, p_prompt[i], s0, *lw)
        hs, ss, vs = hybrid_layer(hs, p_sample[i], state_gla[i], *lw)
        st_prompt.append(sp.astype(state_gla.dtype))
        st_sample.append(ss.astype(state_gla.dtype))
        v_sample.append(vs)
    y_prompt = rms_norm(hp, final_g)
    y_sample = rms_norm(hs, final_g)
    new_state_gla_prompt = jnp.stack(st_prompt)
    new_state_gla_sample = jnp.stack(st_sample)
    new_mix_v_sample = jnp.stack(v_sample)
    return (y_prompt, y_sample, new_state_gla_prompt, new_state_gla_sample, new_mix_v_sample)
```

```python
import functools

import jax
import jax.numpy as jnp
from jax import lax
from jax.experimental import pallas as pl
from jax.experimental.pallas import tpu as pltpu

F32 = jnp.float32
BF16 = jnp.bfloat16
I32 = jnp.int32
U32 = jnp.uint32

D_MODEL = 1024
GLA_HEADS = 4
GLA_DK = 128
GLA_DV = 256
GLA_DK_T = GLA_HEADS * GLA_DK
GLA_DV_T = GLA_HEADS * GLA_DV
GK_RANK = 16
GATE_NORMALIZER = 16.0
GLA_CHUNK = 64
D_GMLP = 1024
MIX_HEADS = 4
MIX_DH = D_GMLP // MIX_HEADS
MIX_CHUNK = 128
N_EXPERTS = 32
TOP_K = 4
D_FF = 1024
SWIGLU_LIMIT = 7.0
SWIGLU_ALPHA = 1.702
D_PLE = 256
EPS = 1e-6

LANES = 128
VMEM_LIMIT_BYTES = 56 * 1024 * 1024

OFF_Q, OFF_K, OFF_V, OFF_R, OFF_U, OFF_VG, OFF_GA, OFF_GB, N_MAIN = (
    0, 512, 1024, 2048, 3072, 4096, 5120, 6144, 7168)

MIXER_TILE = 256
STATE_TOKENS = 8
EXPERT_BLOCK = 256
COMBINE_TILE = 256


def _dot(a, b):
    return jnp.dot(a, b, preferred_element_type=F32)


def _dot_nt(a, b):
    return lax.dot_general(a, b, (((1,), (1,)), ((), ())), preferred_element_type=F32)


def _dot_tn(a, b):
    return lax.dot_general(a, b, (((0,), (0,)), ((), ())), preferred_element_type=F32)


def _rms(x, g):
    return x * lax.rsqrt(jnp.mean(x * x, axis=-1, keepdims=True) + EPS) * g


def _sigmoid(x):
    return 1.0 / (1.0 + jnp.exp(-x))


def _gelu(x):
    return 0.5 * x * (1.0 + lax.erf(x * (2.0 ** -0.5)))


def _log_sigmoid(x):
    return jnp.minimum(x, 0.0) - jnp.log1p(jnp.exp(-jnp.abs(x)))


def _split3(x):
    hi = x.astype(BF16)
    r1 = x - hi.astype(F32)
    mid = r1.astype(BF16)
    lo = (r1 - mid.astype(F32)).astype(BF16)
    return hi, mid, lo


def _const_spec(shape):
    nd = len(shape)
    return pl.BlockSpec(shape, lambda *_: (0,) * nd, pipeline_mode=pl.Buffered(1))


def _params(sem):
    return pltpu.CompilerParams(dimension_semantics=sem, vmem_limit_bytes=VMEM_LIMIT_BYTES)


def _project_gla_inputs(n, wm_ref, wgk1_ref, wgk2_ref, bgk_ref):
    q = _dot(n, wm_ref[:, OFF_Q:OFF_K]) * (GLA_DK ** -0.5)
    k = _dot(n, wm_ref[:, OFF_K:OFF_V])
    v = _dot(n, wm_ref[:, OFF_V:OFF_R])
    gk = _dot(n, wgk1_ref[...]).astype(BF16)
    log_a = _log_sigmoid(_dot(gk, wgk2_ref[...]) + bgk_ref[...]) * (1.0 / GATE_NORMALIZER)
    return q, k, v, log_a


def _gmlp_inputs(n, wm_ref, vng_ref, vnb_ref):
    u = _gelu(_dot(n, wm_ref[:, OFF_U:OFF_VG]))
    vg = _gelu(_dot(n, wm_ref[:, OFF_VG:OFF_GA]))
    mu = jnp.mean(vg, axis=-1, keepdims=True)
    vc = vg - mu
    var = jnp.mean(vc * vc, axis=-1, keepdims=True)
    vn = vc * lax.rsqrt(var + EPS) * vng_ref[...] + vnb_ref[...]
    ug = u * _sigmoid(_dot(n, wm_ref[:, OFF_GB:N_MAIN]))
    return ug, vn


def _gla_out_gate(n, wm_ref):
    r = _dot(n, wm_ref[:, OFF_R:OFF_U])
    ga = _dot(n, wm_ref[:, OFF_GA:OFF_GB])
    return r * _sigmoid(r) * _sigmoid(ga)


def _head_rms(o, g):
    return o * lax.rsqrt(jnp.mean(o * o, axis=-1, keepdims=True) + EPS) * g


def _mixer_prompt_kernel(x_ref, hs_ref, *refs, tm, n_seq, n_tiles):
    b = pl.program_id(0)
    t = pl.program_id(1)
    h_ref = refs[11]

    @pl.when(b < n_seq)
    def _():
        _mixer_prompt_tile(x_ref, *refs, tm=tm, n_tiles=n_tiles)

    @pl.when((b == n_seq) & (t == 0))
    def _():
        h_ref[0:hs_ref.shape[0], :] = hs_ref[...]


def _mixer_prompt_tile(x_ref, g1_ref, wm_ref, wgk1_ref, wgk2_ref, bgk_ref, glag_ref, vng_ref, vnb_ref,
                       wsp_ref, bspt_ref, wo_ref,
                       h_ref, st_ref,
                       n_scr, q_scr, k_scr, la_scr, v_scr, oa_scr, s_scr, *, tm, n_tiles):
    t = pl.program_id(1)

    @pl.when(t == 0)
    def _():
        s_scr[...] = jnp.zeros_like(s_scr)

    x = x_ref[0]
    n = _rms(x, g1_ref[...]).astype(BF16)
    n_scr[...] = n
    q, k, v, log_a = _project_gla_inputs(n, wm_ref, wgk1_ref, wgk2_ref, bgk_ref)
    q_scr[...] = q
    k_scr[...] = k
    v_scr[...] = v.astype(BF16)
    la_scr[...] = log_a

    row = lax.broadcasted_iota(I32, (GLA_CHUNK, GLA_CHUNK), 0)
    col = lax.broadcasted_iota(I32, (GLA_CHUNK, GLA_CHUNK), 1)
    causal = row >= col
    tri = causal.astype(BF16)
    glag = glag_ref[...]

    def chunk(c, carry):
        rows = pl.ds(pl.multiple_of(c * GLA_CHUNK, GLA_CHUNK), GLA_CHUNK)
        hi, mid, lo = _split3(la_scr[rows, :])
        b = _dot(tri, hi) + _dot(tri, mid) + _dot(tri, lo)
        b_last = b[GLA_CHUNK - 1:GLA_CHUNK, :]
        kk = k_scr[rows, :]
        qe = (q_scr[rows, :] * jnp.exp(b)).astype(BF16)
        ke = (kk * jnp.exp(-b)).astype(BF16)
        kd = (kk * jnp.exp(b_last - b)).astype(BF16)
        dec = jnp.exp(b_last)
        for h in range(GLA_HEADS):
            ks = slice(h * GLA_DK, (h + 1) * GLA_DK)
            vs = slice(h * GLA_DV, (h + 1) * GLA_DV)
            att = jnp.where(causal, _dot_nt(qe[:, ks], ke[:, ks]), 0.0).astype(BF16)
            vh = v_scr[rows, vs]
            s_t = s_scr[h]
            o = _dot(att, vh) + _dot_nt(qe[:, ks], s_t.astype(BF16))
            s_scr[h] = s_t * dec[:, ks] + _dot_tn(vh, kd[:, ks])
            oa_scr[rows, vs] = _head_rms(o, glag)
        return carry

    lax.fori_loop(0, tm // GLA_CHUNK, chunk, 0)

    n = n_scr[...]
    oa_scr[...] = oa_scr[...] * _gla_out_gate(n, wm_ref)
    ug, vn = _gmlp_inputs(n, wm_ref, vng_ref, vnb_ref)
    vn = vn.astype(BF16)
    mrow = lax.broadcasted_iota(I32, (MIX_CHUNK, MIX_CHUNK), 0)
    mcol = lax.broadcasted_iota(I32, (MIX_CHUNK, MIX_CHUNK), 1)
    for h in range(MIX_HEADS):
        w = jnp.where(mrow >= mcol, wsp_ref[h], 0.0).astype(BF16)
        cs = slice(h * MIX_DH, (h + 1) * MIX_DH)
        for j in range(tm // MIX_CHUNK):
            rs = slice(j * MIX_CHUNK, (j + 1) * MIX_CHUNK)
            mix = _dot(w, vn[rs, cs]) + bspt_ref[:, h:h + 1]
            oa_scr[rs, cs] = oa_scr[rs, cs] + ug[rs, cs] * mix

    h_ref[...] = x + _dot(oa_scr[...].astype(BF16), wo_ref[...])

    @pl.when(t == n_tiles - 1)
    def _():
        for h in range(GLA_HEADS):
            st_ref[0, h] = s_scr[h].T


def _mixer_prompt(x, h_sample, wts):
    n_seq, seq_len, _ = x.shape
    n_s = h_sample.shape[0]
    tm = min(MIXER_TILE, seq_len)
    assert seq_len % tm == 0 and tm % MIX_CHUNK == 0 and n_s <= tm
    n_tiles = seq_len // tm
    consts = [wts['g1'], wts['w_main'], wts['w_gk1'], wts['w_gk2'], wts['b_gk'], wts['gla_g'], wts['vn_g'],
              wts['vn_b'], wts['w_sp'], wts['b_sp_t'], wts['w_o']]
    kern = functools.partial(_mixer_prompt_kernel, tm=tm, n_seq=n_seq, n_tiles=n_tiles)
    last = n_seq - 1

    def prompt_tile(b, t):
        return jnp.minimum(b, last), jnp.where(b < n_seq, t, n_tiles - 1)

    def x_map(b, t):
        bb, tt = prompt_tile(b, t)
        return bb, tt, 0

    def h_map(b, t):
        return jnp.where(b < n_seq, b * n_tiles + t, n_seq * n_tiles), 0

    return pl.pallas_call(
        kern,
        grid=(n_seq + 1, n_tiles),
        in_specs=[pl.BlockSpec((1, tm, D_MODEL), x_map), _const_spec(h_sample.shape)]
        + [_const_spec(c.shape) for c in consts],
        out_specs=[pl.BlockSpec((tm, D_MODEL), h_map),
                   pl.BlockSpec((1, GLA_HEADS, GLA_DK, GLA_DV), lambda b, t: (jnp.minimum(b, last), 0, 0, 0))],
        out_shape=[jax.ShapeDtypeStruct((n_seq * seq_len + n_s, D_MODEL), F32),
                   jax.ShapeDtypeStruct((n_seq, GLA_HEADS, GLA_DK, GLA_DV), F32)],
        scratch_shapes=[pltpu.VMEM((tm, D_MODEL), BF16),
                        pltpu.VMEM((tm, GLA_DK_T), F32),
                        pltpu.VMEM((tm, GLA_DK_T), F32),
                        pltpu.VMEM((tm, GLA_DK_T), F32),
                        pltpu.VMEM((tm, GLA_DV_T), BF16),
                        pltpu.VMEM((tm, D_MODEL), F32),
                        pltpu.VMEM((GLA_HEADS, GLA_DV, GLA_DK), F32)],
        compiler_params=_params(("arbitrary", "arbitrary")),
        name="mixer_prompt",
    )(x, h_sample, *consts)


def _mixer_sample_in_kernel(x_ref, g1_ref, wm_ref, wgk1_ref, wgk2_ref, bgk_ref, vng_ref, vnb_ref,
                            mixw_ref, mixb_ref,
                            q_ref, k_ref, a_ref, v_ref, og_ref, ob_ref, vn_ref):
    n = _rms(x_ref[...], g1_ref[...]).astype(BF16)
    q, k, v, log_a = _project_gla_inputs(n, wm_ref, wgk1_ref, wgk2_ref, bgk_ref)
    q_ref[...] = q
    k_ref[...] = k
    a_ref[...] = jnp.exp(log_a)
    v_ref[...] = v
    og_ref[...] = _gla_out_gate(n, wm_ref)
    ug, vn = _gmlp_inputs(n, wm_ref, vng_ref, vnb_ref)
    vn_ref[...] = vn
    ob_ref[...] = ug * (mixw_ref[...] * vn + mixb_ref[...])


def _mixer_sample_in(x2, wts):
    n_seq = x2.shape[0]
    consts = [wts['g1'], wts['w_main'], wts['w_gk1'], wts['w_gk2'], wts['b_gk'], wts['vn_g'], wts['vn_b'],
              wts['mix_w0'], wts['mix_b0']]
    widths = [GLA_DK_T, GLA_DK_T, GLA_DK_T, GLA_DV_T, D_MODEL, D_MODEL, D_GMLP]
    return pl.pallas_call(
        _mixer_sample_in_kernel,
        grid=(1,),
        in_specs=[_const_spec(x2.shape)] + [_const_spec(c.shape) for c in consts],
        out_specs=[pl.BlockSpec((n_seq, w), lambda i: (0, 0)) for w in widths],
        out_shape=[jax.ShapeDtypeStruct((n_seq, w), F32) for w in widths],
        compiler_params=_params(("arbitrary",)),
        name="mixer_sample_in",
    )(x2, *consts)


def _state_update_kernel(s_ref, q_ref, k_ref, a_ref, v_ref, so_ref, o_ref):
    for j in range(STATE_TOKENS):
        for h in range(GLA_HEADS):
            ks = slice(h * GLA_DK, (h + 1) * GLA_DK)
            vs = slice(h * GLA_DV, (h + 1) * GLA_DV)
            s_new = a_ref[0, ks, j:j + 1] * s_ref[j, h] + k_ref[0, ks, j:j + 1] * v_ref[j:j + 1, vs]
            so_ref[j, h] = s_new
            o_ref[j:j + 1, vs] = jnp.sum(q_ref[0, ks, j:j + 1] * s_new, axis=0, keepdims=True)


def _state_update(state, q, k, a, v):
    n_seq = state.shape[0]
    tb = STATE_TOKENS
    assert n_seq % tb == 0

    def cols(z):
        return z.reshape(n_seq // tb, tb, GLA_DK_T).transpose(0, 2, 1)

    col_spec = pl.BlockSpec((1, GLA_DK_T, tb), lambda i: (i, 0, 0))
    st_spec = pl.BlockSpec((tb, GLA_HEADS, GLA_DK, GLA_DV), lambda i: (i, 0, 0, 0))
    row_spec = pl.BlockSpec((tb, GLA_DV_T), lambda i: (i, 0))
    return pl.pallas_call(
        _state_update_kernel,
        grid=(n_seq // tb,),
        in_specs=[st_spec, col_spec, col_spec, col_spec, row_spec],
        out_specs=[st_spec, row_spec],
        out_shape=[jax.ShapeDtypeStruct(state.shape, F32), jax.ShapeDtypeStruct((n_seq, GLA_DV_T), F32)],
        compiler_params=_params(("arbitrary",)),
        name="state_update",
    )(state, cols(q), cols(k), cols(a), v)


def _mixer_sample_out_kernel(x_ref, o_ref, og_ref, ob_ref, glag_ref, wo_ref, h_ref):
    glag = glag_ref[...]
    parts = []
    for h in range(GLA_HEADS):
        vs = slice(h * GLA_DV, (h + 1) * GLA_DV)
        parts.append(_head_rms(o_ref[:, vs], glag))
    merged = jnp.concatenate(parts, axis=1) * og_ref[...] + ob_ref[...]
    h_ref[...] = x_ref[...] + _dot(merged.astype(BF16), wo_ref[...])


def _mixer_sample_out(x2, o, og, ob, wts):
    n_seq = x2.shape[0]
    consts = [x2, o, og, ob, wts['gla_g'], wts['w_o']]
    return pl.pallas_call(
        _mixer_sample_out_kernel,
        grid=(1,),
        in_specs=[_const_spec(c.shape) for c in consts],
        out_specs=pl.BlockSpec((n_seq, D_MODEL), lambda i: (0, 0)),
        out_shape=jax.ShapeDtypeStruct((n_seq, D_MODEL), F32),
        compiler_params=_params(("arbitrary",)),
        name="mixer_sample_out",
    )(*consts)


def _router_kernel(h_ref, g2_ref, wr_ref, br_ref, xp_ref, ti_ref, tw_ref, rk_ref, cnt_ref, cnt_scr, *, tm):
    i = pl.program_id(0)

    @pl.when(i == 0)
    def _():
        cnt_scr[...] = jnp.zeros_like(cnt_scr)

    hn = _rms(h_ref[...], g2_ref[...])
    hn_hi = hn.astype(BF16)
    hn_lo = (hn - hn_hi.astype(F32)).astype(BF16)
    w = wr_ref[...]
    w_hi = w.astype(BF16)
    w_lo = (w - w_hi.astype(F32)).astype(BF16)
    logits = _dot(hn_hi, w_hi) + _dot(hn_lo, w_hi) + _dot(hn_hi, w_lo) + br_ref[...]

    half = D_MODEL // 2
    hi_bits = lax.bitcast_convert_type(hn_hi[:, :half].astype(F32), U32)
    lo_bits = lax.bitcast_convert_type(hn_hi[:, half:].astype(F32), U32)
    xp_ref[...] = hi_bits | (lo_bits >> 16)

    lane = lax.broadcasted_iota(I32, (tm, LANES), 1)
    neg = jnp.float32(-jnp.inf)
    l = jnp.where(lane < N_EXPERTS, logits, neg)
    vals, idxs, hots = [], [], []
    for _ in range(TOP_K):
        m = jnp.max(l, axis=-1, keepdims=True)
        idx = jnp.min(jnp.where(l == m, lane, LANES), axis=-1, keepdims=True)
        hot = lane == idx
        l = jnp.where(hot, neg, l)
        vals.append(m)
        idxs.append(idx)
        hots.append(hot)
    exps = [jnp.exp(v - vals[0]) for v in vals]
    denom = exps[0] + exps[1] + exps[2] + exps[3]

    member = (hots[0] | hots[1] | hots[2] | hots[3]).astype(BF16)
    row = lax.broadcasted_iota(I32, (tm, tm), 0)
    col = lax.broadcasted_iota(I32, (tm, tm), 1)
    earlier = (row > col).astype(BF16)
    before = cnt_scr[...] + _dot(earlier, member)
    cnt_scr[...] = cnt_scr[...] + jnp.sum(member.astype(F32), axis=0, keepdims=True)
    cnt_ref[...] = cnt_scr[...].astype(I32)

    ti = jnp.zeros((tm, LANES), I32)
    tw = jnp.zeros((tm, LANES), F32)
    rk = jnp.zeros((tm, LANES), I32)
    for kk in range(TOP_K):
        sel = lane == kk
        rank = jnp.sum(jnp.where(hots[kk], before, 0.0), axis=-1, keepdims=True).astype(I32)
        ti = jnp.where(sel, idxs[kk], ti)
        tw = jnp.where(sel, exps[kk] / denom, tw)
        rk = jnp.where(sel, rank, rk)
    ti_ref[...] = ti
    tw_ref[...] = tw
    rk_ref[...] = rk


def _router_tile(t_all):
    for tm in (512, 384, 256, 128):
        if t_all % tm == 0:
            return tm
    raise ValueError(f"token count {t_all} must be a multiple of 128")


def _router(hbuf, wts):
    t_all = hbuf.shape[0]
    tm = _router_tile(t_all)
    consts = [wts['g2'], wts['w_router'], wts['b_router']]
    row = lambda w: pl.BlockSpec((tm, w), lambda i: (i, 0))
    return pl.pallas_call(
        functools.partial(_router_kernel, tm=tm),
        grid=(t_all // tm,),
        in_specs=[row(D_MODEL)] + [_const_spec(c.shape) for c in consts],
        out_specs=[row(D_MODEL // 2), row(LANES), row(LANES), row(LANES),
                   pl.BlockSpec((1, LANES), lambda i: (0, 0))],
        out_shape=[jax.ShapeDtypeStruct((t_all, D_MODEL // 2), U32),
                   jax.ShapeDtypeStruct((t_all, LANES), I32),
                   jax.ShapeDtypeStruct((t_all, LANES), F32),
                   jax.ShapeDtypeStruct((t_all, LANES), I32),
                   jax.ShapeDtypeStruct((1, LANES), I32)],
        scratch_shapes=[pltpu.VMEM((1, LANES), F32)],
        compiler_params=_params(("arbitrary",)),
        name="router",
    )(hbuf, *consts)


def _row_copy(src_hbm, row, dst, sem):
    return pltpu.make_async_copy(src_hbm.at[pl.ds(row, 1)], dst, sem)


def _experts_kernel(be_ref, nv_ref, rt_ref, xp_hbm, wg_ref, wu_ref, wd_ref, bg_ref, bu_ref, bd_ref,
                    out_ref, xbuf, sem, *, bm):
    del be_ref
    i = pl.program_id(0)
    n_valid = nv_ref[0]
    slot = i % 2
    unroll = 8

    def gather(blk, s, start):
        def body(r8, carry):
            for u in range(unroll):
                r = r8 * unroll + u
                cp = _row_copy(xp_hbm, rt_ref[blk * bm + r], xbuf.at[s, pl.ds(r, 1)], sem.at[s])
                if start:
                    cp.start()
                else:
                    cp.wait()
            return carry
        lax.fori_loop(0, bm // unroll, body, 0)

    @pl.when(i == 0)
    def _():
        gather(0, 0, True)

    @pl.when(i + 1 < n_valid)
    def _():
        gather(i + 1, 1 - slot, True)

    @pl.when(i < n_valid)
    def _():
        gather(i, slot, False)
        xu = xbuf[slot]
        first = lax.bitcast_convert_type(xu & jnp.uint32(0xFFFF0000), F32).astype(BF16)
        second = lax.bitcast_convert_type(xu << 16, F32).astype(BF16)
        x = jnp.concatenate([first, second], axis=1)
        g = jnp.minimum(_dot(x, wg_ref[0]) + bg_ref[0], SWIGLU_LIMIT)
        u = jnp.clip(_dot(x, wu_ref[0]) + bu_ref[0], -SWIGLU_LIMIT, SWIGLU_LIMIT)
        mid = (u + 1.0) * (g * _sigmoid(SWIGLU_ALPHA * g))
        out_ref[...] = _dot(mid.astype(BF16), wd_ref[0]) + bd_ref[0]

    @pl.when(i >= n_valid)
    def _():
        out_ref[...] = jnp.zeros_like(out_ref)


def _experts(block_e, n_valid, row_tok, xp, wts):
    bm = EXPERT_BLOCK
    n_blocks = block_e.shape[0]
    wspec = pl.BlockSpec((1, D_MODEL, D_FF), lambda i, be, nv, rt: (be[i], 0, 0))
    bspec = pl.BlockSpec((1, 1, D_FF), lambda i, be, nv, rt: (be[i], 0, 0))
    grid_spec = pltpu.PrefetchScalarGridSpec(
        num_scalar_prefetch=3,
        grid=(n_blocks,),
        in_specs=[pl.BlockSpec(memory_space=pl.ANY), wspec, wspec, wspec, bspec, bspec, bspec],
        out_specs=pl.BlockSpec((bm, D_MODEL), lambda i, be, nv, rt: (i, 0)),
        scratch_shapes=[pltpu.VMEM((2, bm, D_MODEL // 2), U32), pltpu.SemaphoreType.DMA((2,))],
    )
    return pl.pallas_call(
        functools.partial(_experts_kernel, bm=bm),
        grid_spec=grid_spec,
        out_shape=jax.ShapeDtypeStruct((n_blocks * bm, D_MODEL), F32),
        compiler_params=_params(("arbitrary",)),
        name="experts",
    )(block_e, n_valid, row_tok, xp, wts['w_gate'], wts['w_up'], wts['w_down'],
      wts['b_gate'], wts['b_up'], wts['b_down'])


def _combine_kernel(dest_ref, h_ref, tw_ref, p_ref, yb_hbm, g3_ref, wpg_ref, wpp_ref, gf_ref, out_ref,
                    ybuf, sem, *, tm, row0, n_tiles):
    i = pl.program_id(0)
    slot = i % 2

    def gather(tile, s, start):
        base = (row0 + tile * tm) * TOP_K

        def body(r, carry):
            for kk in range(TOP_K):
                cp = _row_copy(yb_hbm, dest_ref[base + r * TOP_K + kk], ybuf.at[s, kk, pl.ds(r, 1)], sem.at[s])
                if start:
                    cp.start()
                else:
                    cp.wait()
            return carry
        lax.fori_loop(0, tm, body, 0)

    @pl.when(i == 0)
    def _():
        gather(0, 0, True)

    @pl.when(i + 1 < n_tiles)
    def _():
        gather(i + 1, 1 - slot, True)

    gather(i, slot, False)
    tw = tw_ref[...]
    y = ybuf[slot, 0] * tw[:, 0:1]
    for kk in range(1, TOP_K):
        y = y + ybuf[slot, kk] * tw[:, kk:kk + 1]
    h2 = h_ref[...] + y
    gate = _sigmoid(_dot(_rms(h2, g3_ref[...]).astype(BF16), wpg_ref[...]))
    h3 = h2 + gate * _dot(p_ref[...].astype(BF16), wpp_ref[...])
    out_ref[...] = _rms(h3, gf_ref[...])


def _combine(dest, hbuf, tw, p2, yb, row0, wts):
    n_rows = p2.shape[0]
    tm = min(COMBINE_TILE, n_rows)
    assert n_rows % tm == 0 and row0 % tm == 0
    n_tiles = n_rows // tm
    off = row0 // tm
    consts = [wts['g3'], wts['w_ple_gate'], wts['w_ple_proj'], wts['g_final']]
    grid_spec = pltpu.PrefetchScalarGridSpec(
        num_scalar_prefetch=1,
        grid=(n_tiles,),
        in_specs=[pl.BlockSpec((tm, D_MODEL), lambda i, d: (i + off, 0)),
                  pl.BlockSpec((tm, LANES), lambda i, d: (i + off, 0)),
                  pl.BlockSpec((tm, D_PLE), lambda i, d: (i, 0)),
                  pl.BlockSpec(memory_space=pl.ANY)] + [_const_spec(c.shape) for c in consts],
        out_specs=pl.BlockSpec((tm, D_MODEL), lambda i, d: (i, 0)),
        scratch_shapes=[pltpu.VMEM((2, TOP_K, tm, D_MODEL), F32), pltpu.SemaphoreType.DMA((2,))],
    )
    return pl.pallas_call(
        functools.partial(_combine_kernel, tm=tm, row0=row0, n_tiles=n_tiles),
        grid_spec=grid_spec,
        out_shape=jax.ShapeDtypeStruct((n_rows, D_MODEL), F32),
        compiler_params=_params(("arbitrary",)),
        name="combine",
    )(dest, hbuf, tw, p2, yb, *consts)


def _prepare_weights(norm1_g, w_in, w_gk2, b_gk, gla_norm_g, v_norm_g, v_norm_b, w_sp, b_sp, w_o, norm2_g,
                     w_router, b_router, w_gate, b_gate, w_up, b_up, w_down, b_down, norm3_g, w_ple_gate,
                     w_ple_proj, final_g):
    o_gk = 2 * GLA_DK_T + 2 * GLA_DV_T
    w_main = jnp.concatenate([w_in[:, :o_gk], w_in[:, o_gk + GK_RANK:]], axis=1).astype(BF16)
    w_gk1 = jnp.pad(w_in[:, o_gk:o_gk + GK_RANK], ((0, 0), (0, LANES - GK_RANK))).astype(BF16)
    w_gk2p = jnp.pad(w_gk2, ((0, LANES - GK_RANK), (0, 0))).astype(BF16)
    row = lambda z: z.reshape(1, -1).astype(F32)
    return dict(
        g1=row(norm1_g), w_main=w_main, w_gk1=w_gk1, w_gk2=w_gk2p, b_gk=row(b_gk), gla_g=row(gla_norm_g),
        vn_g=row(v_norm_g), vn_b=row(v_norm_b), w_sp=w_sp, b_sp_t=b_sp.T,
        mix_w0=row(jnp.repeat(w_sp[:, 0, 0], MIX_DH)), mix_b0=row(jnp.repeat(b_sp[:, 0], MIX_DH)),
        w_o=w_o.astype(BF16), g2=row(norm2_g),
        w_router=jnp.pad(w_router, ((0, 0), (0, LANES - N_EXPERTS))),
        b_router=jnp.pad(row(b_router), ((0, 0), (0, LANES - N_EXPERTS))),
        w_gate=w_gate.astype(BF16), w_up=w_up.astype(BF16), w_down=w_down.astype(BF16),
        b_gate=b_gate.reshape(N_EXPERTS, 1, D_FF), b_up=b_up.reshape(N_EXPERTS, 1, D_FF),
        b_down=b_down.reshape(N_EXPERTS, 1, D_MODEL),
        g3=row(norm3_g), w_ple_gate=w_ple_gate.astype(BF16), w_ple_proj=w_ple_proj.astype(BF16),
        g_final=row(final_g))


def _dispatch_plan(top_i, rank, counts, n_blocks):
    bm = EXPERT_BLOCK
    t_all = top_i.shape[0]
    padded = (counts + bm - 1) // bm * bm
    pad_end = jnp.cumsum(padded)
    pad_start = pad_end - padded
    dest = (pad_start[top_i] + rank).reshape(-1)
    tok = jnp.repeat(jnp.arange(t_all, dtype=I32), TOP_K)
    row_tok = jnp.zeros((n_blocks * bm,), I32).at[dest].set(tok)
    n_valid = (pad_end[-1] // bm).astype(I32)
    blk = jnp.minimum(jnp.arange(n_blocks, dtype=I32), n_valid - 1)
    block_e = jnp.minimum(jnp.searchsorted(pad_end, blk * bm, side='right'), N_EXPERTS - 1).astype(I32)
    return dest.astype(I32), row_tok, block_e, n_valid.reshape(1)


def _layer(x_prompt, x_sample, state, p_prompt, p_sample, wts):
    n_p, len_p, _ = x_prompt.shape
    n_s, len_s, _ = x_sample.shape
    assert len_s == 1, "the sample group carries one new token per sequence"
    t_p = n_p * len_p
    t_all = t_p + n_s

    xs = x_sample.reshape(n_s, D_MODEL)
    q, k, a, v, og, ob, vn = _mixer_sample_in(xs, wts)
    st_sample, o = _state_update(state, q, k, a, v)
    h_sample = _mixer_sample_out(xs, o, og, ob, wts)
    hbuf, st_prompt = _mixer_prompt(x_prompt, h_sample, wts)

    xp, top_i, top_w, rank, counts = _router(hbuf, wts)
    n_blocks = (t_all * TOP_K + N_EXPERTS * (EXPERT_BLOCK - 1) + EXPERT_BLOCK - 1) // EXPERT_BLOCK
    dest, row_tok, block_e, n_valid = _dispatch_plan(top_i[:, :TOP_K], rank[:, :TOP_K], counts[0, :N_EXPERTS],
                                                     n_blocks)
    yb = _experts(block_e, n_valid, row_tok, xp, wts)
    y_prompt = _combine(dest, hbuf, top_w, p_prompt.reshape(t_p, D_PLE), yb, 0, wts)
    y_sample = _combine(dest, hbuf, top_w, p_sample.reshape(n_s, D_PLE), yb, t_p, wts)
    return (y_prompt.reshape(n_p, len_p, D_MODEL), y_sample.reshape(n_s, len_s, D_MODEL),
            st_prompt, st_sample, vn.reshape(n_s, len_s, D_GMLP))


def kernel(x_prompt, x_sample, state_gla, p_prompt, p_sample, norm1_g, w_in, w_gk2, b_gk, gla_norm_g, v_norm_g,
           v_norm_b, w_sp, b_sp, w_o, norm2_g, w_router, b_router, w_gate, b_gate, w_up, b_up, w_down, b_down,
           norm3_g, w_ple_gate, w_ple_proj, final_g):
    assert w_in.shape[0] == 1, "single-layer trunk"
    wts = _prepare_weights(norm1_g[0], w_in[0], w_gk2[0], b_gk[0], gla_norm_g[0], v_norm_g[0], v_norm_b[0],
                           w_sp[0], b_sp[0], w_o[0], norm2_g[0], w_router[0], b_router[0], w_gate[0], b_gate[0],
                           w_up[0], b_up[0], w_down[0], b_down[0], norm3_g[0], w_ple_gate[0], w_ple_proj[0],
                           final_g)
    y_p, y_s, st_p, st_s, vn_s = _layer(x_prompt, x_sample, state_gla[0], p_prompt[0], p_sample[0], wts)
    return (y_p, y_s, st_p[None], st_s[None].astype(state_gla.dtype), vn_s[None])
```

```python
import functools

import jax
import jax.numpy as jnp
from jax import lax
from jax.experimental import pallas as pl
from jax.experimental.pallas import tpu as pltpu

F32 = jnp.float32
BF16 = jnp.bfloat16
I32 = jnp.int32
U32 = jnp.uint32

D_MODEL = 1024
GLA_HEADS = 4
GLA_DK = 128
GLA_DV = 256
GLA_DK_T = GLA_HEADS * GLA_DK
GLA_DV_T = GLA_HEADS * GLA_DV
GK_RANK = 16
GATE_NORMALIZER = 16.0
GLA_CHUNK = 64
D_GMLP = 1024
MIX_HEADS = 4
MIX_DH = D_GMLP // MIX_HEADS
MIX_CHUNK = 128
N_EXPERTS = 32
TOP_K = 4
D_FF = 1024
SWIGLU_LIMIT = 7.0
SWIGLU_ALPHA = 1.702
D_PLE = 256
EPS = 1e-6

LANES = 128
VMEM_LIMIT_BYTES = 56 * 1024 * 1024

OFF_Q, OFF_K, OFF_V, OFF_R, OFF_U, OFF_VG, OFF_GA, OFF_GB, N_MAIN = (
    0, 512, 1024, 2048, 3072, 4096, 5120, 6144, 7168)

MIXER_TILE = 256
STATE_TOKENS = 8
EXPERT_BLOCK = 256
EXPERT_COLS = 256
COMBINE_TILE = 256
SLAB_ALIGN = 512


def _dot(a, b):
    return jnp.dot(a, b, preferred_element_type=F32)


def _dot_nt(a, b):
    return lax.dot_general(a, b, (((1,), (1,)), ((), ())), preferred_element_type=F32)


def _dot_tn(a, b):
    return lax.dot_general(a, b, (((0,), (0,)), ((), ())), preferred_element_type=F32)


def _rms(x, g):
    return x * lax.rsqrt(jnp.mean(x * x, axis=-1, keepdims=True) + EPS) * g


def _sigmoid(x):
    return 1.0 / (1.0 + jnp.exp(-x))


def _gelu(x):
    return 0.5 * x * (1.0 + lax.erf(x * (2.0 ** -0.5)))


def _log_sigmoid(x):
    return jnp.minimum(x, 0.0) - jnp.log1p(jnp.exp(-jnp.abs(x)))


def _split3(x):
    hi = x.astype(BF16)
    r1 = x - hi.astype(F32)
    mid = r1.astype(BF16)
    lo = (r1 - mid.astype(F32)).astype(BF16)
    return hi, mid, lo


def _const_spec(shape):
    nd = len(shape)
    return pl.BlockSpec(shape, lambda *_: (0,) * nd, pipeline_mode=pl.Buffered(1))


def _params(sem):
    return pltpu.CompilerParams(dimension_semantics=sem, vmem_limit_bytes=VMEM_LIMIT_BYTES)


def _project_gla_inputs(n, wm_ref, wgk1_ref, wgk2_ref, bgk_ref):
    q = _dot(n, wm_ref[:, OFF_Q:OFF_K]) * (GLA_DK ** -0.5)
    k = _dot(n, wm_ref[:, OFF_K:OFF_V])
    v = _dot(n, wm_ref[:, OFF_V:OFF_R])
    gk = _dot(n, wgk1_ref[...]).astype(BF16)
    log_a = _log_sigmoid(_dot(gk, wgk2_ref[...]) + bgk_ref[...]) * (1.0 / GATE_NORMALIZER)
    return q, k, v, log_a


def _gmlp_inputs(n, wm_ref, vng_ref, vnb_ref):
    u = _gelu(_dot(n, wm_ref[:, OFF_U:OFF_VG]))
    vg = _gelu(_dot(n, wm_ref[:, OFF_VG:OFF_GA]))
    mu = jnp.mean(vg, axis=-1, keepdims=True)
    vc = vg - mu
    var = jnp.mean(vc * vc, axis=-1, keepdims=True)
    vn = vc * lax.rsqrt(var + EPS) * vng_ref[...] + vnb_ref[...]
    ug = u * _sigmoid(_dot(n, wm_ref[:, OFF_GB:N_MAIN]))
    return ug, vn


def _gla_out_gate(n, wm_ref):
    r = _dot(n, wm_ref[:, OFF_R:OFF_U])
    ga = _dot(n, wm_ref[:, OFF_GA:OFF_GB])
    return r * _sigmoid(r) * _sigmoid(ga)


def _head_rms(o, g):
    return o * lax.rsqrt(jnp.mean(o * o, axis=-1, keepdims=True) + EPS) * g


def _mixer_prompt_kernel(x_ref, hs_ref, *refs, tm, n_seq, n_tiles):
    b = pl.program_id(0)
    t = pl.program_id(1)
    h_ref = refs[11]

    @pl.when(b < n_seq)
    def _():
        _mixer_prompt_tile(x_ref, *refs, tm=tm, n_tiles=n_tiles)

    @pl.when((b == n_seq) & (t == 0))
    def _():
        h_ref[0:hs_ref.shape[0], :] = hs_ref[...]


def _mixer_prompt_tile(x_ref, g1_ref, wm_ref, wgk1_ref, wgk2_ref, bgk_ref, glag_ref, vng_ref, vnb_ref,
                       wsp_ref, bspt_ref, wo_ref,
                       h_ref, st_ref,
                       n_scr, q_scr, k_scr, la_scr, v_scr, oa_scr, s_scr, *, tm, n_tiles):
    t = pl.program_id(1)

    @pl.when(t == 0)
    def _():
        s_scr[...] = jnp.zeros_like(s_scr)

    x = x_ref[0]
    n = _rms(x, g1_ref[...]).astype(BF16)
    n_scr[...] = n
    q, k, v, log_a = _project_gla_inputs(n, wm_ref, wgk1_ref, wgk2_ref, bgk_ref)
    q_scr[...] = q
    k_scr[...] = k
    v_scr[...] = v.astype(BF16)
    la_scr[...] = log_a

    row = lax.broadcasted_iota(I32, (GLA_CHUNK, GLA_CHUNK), 0)
    col = lax.broadcasted_iota(I32, (GLA_CHUNK, GLA_CHUNK), 1)
    causal = row >= col
    tri = causal.astype(BF16)
    glag = glag_ref[...]

    def chunk(c, carry):
        rows = pl.ds(pl.multiple_of(c * GLA_CHUNK, GLA_CHUNK), GLA_CHUNK)
        hi, mid, lo = _split3(la_scr[rows, :])
        b = _dot(tri, hi) + _dot(tri, mid) + _dot(tri, lo)
        b_last = b[GLA_CHUNK - 1:GLA_CHUNK, :]
        kk = k_scr[rows, :]
        qe = (q_scr[rows, :] * jnp.exp(b)).astype(BF16)
        ke = (kk * jnp.exp(-b)).astype(BF16)
        kd = (kk * jnp.exp(b_last - b)).astype(BF16)
        dec = jnp.exp(b_last)
        for h in range(GLA_HEADS):
            ks = slice(h * GLA_DK, (h + 1) * GLA_DK)
            vs = slice(h * GLA_DV, (h + 1) * GLA_DV)
            att = jnp.where(causal, _dot_nt(qe[:, ks], ke[:, ks]), 0.0).astype(BF16)
            vh = v_scr[rows, vs]
            s_t = s_scr[h]
            o = _dot(att, vh) + _dot_nt(qe[:, ks], s_t.astype(BF16))
            s_scr[h] = s_t * dec[:, ks] + _dot_tn(vh, kd[:, ks])
            oa_scr[rows, vs] = _head_rms(o, glag)
        return carry

    lax.fori_loop(0, tm // GLA_CHUNK, chunk, 0)

    n = n_scr[...]
    oa_scr[...] = oa_scr[...] * _gla_out_gate(n, wm_ref)
    ug, vn = _gmlp_inputs(n, wm_ref, vng_ref, vnb_ref)
    vn = vn.astype(BF16)
    mrow = lax.broadcasted_iota(I32, (MIX_CHUNK, MIX_CHUNK), 0)
    mcol = lax.broadcasted_iota(I32, (MIX_CHUNK, MIX_CHUNK), 1)
    for h in range(MIX_HEADS):
        w = jnp.where(mrow >= mcol, wsp_ref[h], 0.0).astype(BF16)
        cs = slice(h * MIX_DH, (h + 1) * MIX_DH)
        for j in range(tm // MIX_CHUNK):
            rs = slice(j * MIX_CHUNK, (j + 1) * MIX_CHUNK)
            mix = _dot(w, vn[rs, cs]) + bspt_ref[:, h:h + 1]
            oa_scr[rs, cs] = oa_scr[rs, cs] + ug[rs, cs] * mix

    h_ref[...] = x + _dot(oa_scr[...].astype(BF16), wo_ref[...])

    @pl.when(t == n_tiles - 1)
    def _():
        for h in range(GLA_HEADS):
            st_ref[0, h] = s_scr[h].T


def _mixer_prompt(x, h_sample, wts):
    n_seq, seq_len, _ = x.shape
    n_s = h_sample.shape[0]
    tm = min(MIXER_TILE, seq_len)
    assert seq_len % tm == 0 and tm % MIX_CHUNK == 0 and n_s <= tm
    n_tiles = seq_len // tm
    consts = [wts['g1'], wts['w_main'], wts['w_gk1'], wts['w_gk2'], wts['b_gk'], wts['gla_g'], wts['vn_g'],
              wts['vn_b'], wts['w_sp'], wts['b_sp_t'], wts['w_o']]
    kern = functools.partial(_mixer_prompt_kernel, tm=tm, n_seq=n_seq, n_tiles=n_tiles)
    last = n_seq - 1

    def prompt_tile(b, t):
        return jnp.minimum(b, last), jnp.where(b < n_seq, t, n_tiles - 1)

    def x_map(b, t):
        bb, tt = prompt_tile(b, t)
        return bb, tt, 0

    def h_map(b, t):
        return jnp.where(b < n_seq, b * n_tiles + t, n_seq * n_tiles), 0

    return pl.pallas_call(
        kern,
        grid=(n_seq + 1, n_tiles),
        in_specs=[pl.BlockSpec((1, tm, D_MODEL), x_map), _const_spec(h_sample.shape)]
        + [_const_spec(c.shape) for c in consts],
        out_specs=[pl.BlockSpec((tm, D_MODEL), h_map),
                   pl.BlockSpec((1, GLA_HEADS, GLA_DK, GLA_DV), lambda b, t: (jnp.minimum(b, last), 0, 0, 0))],
        out_shape=[jax.ShapeDtypeStruct((n_seq * seq_len + n_s, D_MODEL), F32),
                   jax.ShapeDtypeStruct((n_seq, GLA_HEADS, GLA_DK, GLA_DV), F32)],
        scratch_shapes=[pltpu.VMEM((tm, D_MODEL), BF16),
                        pltpu.VMEM((tm, GLA_DK_T), F32),
                        pltpu.VMEM((tm, GLA_DK_T), F32),
                        pltpu.VMEM((tm, GLA_DK_T), F32),
                        pltpu.VMEM((tm, GLA_DV_T), BF16),
                        pltpu.VMEM((tm, D_MODEL), F32),
                        pltpu.VMEM((GLA_HEADS, GLA_DV, GLA_DK), F32)],
        compiler_params=_params(("arbitrary", "arbitrary")),
        name="mixer_prompt",
    )(x, h_sample, *consts)


def _mixer_sample_in_kernel(x_ref, g1_ref, wm_ref, wgk1_ref, wgk2_ref, bgk_ref, vng_ref, vnb_ref,
                            mixw_ref, mixb_ref,
                            q_ref, k_ref, a_ref, v_ref, og_ref, ob_ref, vn_ref):
    n = _rms(x_ref[...], g1_ref[...]).astype(BF16)
    q, k, v, log_a = _project_gla_inputs(n, wm_ref, wgk1_ref, wgk2_ref, bgk_ref)
    q_ref[...] = q
    k_ref[...] = k
    a_ref[...] = jnp.exp(log_a)
    v_ref[...] = v
    og_ref[...] = _gla_out_gate(n, wm_ref)
    ug, vn = _gmlp_inputs(n, wm_ref, vng_ref, vnb_ref)
    vn_ref[...] = vn
    ob_ref[...] = ug * (mixw_ref[...] * vn + mixb_ref[...])


def _mixer_sample_in(x2, wts):
    n_seq = x2.shape[0]
    consts = [wts['g1'], wts['w_main'], wts['w_gk1'], wts['w_gk2'], wts['b_gk'], wts['vn_g'], wts['vn_b'],
              wts['mix_w0'], wts['mix_b0']]
    widths = [GLA_DK_T, GLA_DK_T, GLA_DK_T, GLA_DV_T, D_MODEL, D_MODEL, D_GMLP]
    return pl.pallas_call(
        _mixer_sample_in_kernel,
        grid=(1,),
        in_specs=[_const_spec(x2.shape)] + [_const_spec(c.shape) for c in consts],
        out_specs=[pl.BlockSpec((n_seq, w), lambda i: (0, 0)) for w in widths],
        out_shape=[jax.ShapeDtypeStruct((n_seq, w), F32) for w in widths],
        compiler_params=_params(("arbitrary",)),
        name="mixer_sample_in",
    )(x2, *consts)


def _state_update_kernel(s_ref, q_ref, k_ref, a_ref, v_ref, so_ref, o_ref):
    for j in range(STATE_TOKENS):
        for h in range(GLA_HEADS):
            ks = slice(h * GLA_DK, (h + 1) * GLA_DK)
            vs = slice(h * GLA_DV, (h + 1) * GLA_DV)
            s_new = a_ref[0, ks, j:j + 1] * s_ref[j, h] + k_ref[0, ks, j:j + 1] * v_ref[j:j + 1, vs]
            so_ref[j, h] = s_new
            o_ref[j:j + 1, vs] = jnp.sum(q_ref[0, ks, j:j + 1] * s_new, axis=0, keepdims=True)


def _state_update(state, q, k, a, v):
    n_seq = state.shape[0]
    tb = STATE_TOKENS
    assert n_seq % tb == 0

    def cols(z):
        return z.reshape(n_seq // tb, tb, GLA_DK_T).transpose(0, 2, 1)

    col_spec = pl.BlockSpec((1, GLA_DK_T, tb), lambda i: (i, 0, 0))
    st_spec = pl.BlockSpec((tb, GLA_HEADS, GLA_DK, GLA_DV), lambda i: (i, 0, 0, 0))
    row_spec = pl.BlockSpec((tb, GLA_DV_T), lambda i: (i, 0))
    return pl.pallas_call(
        _state_update_kernel,
        grid=(n_seq // tb,),
        in_specs=[st_spec, col_spec, col_spec, col_spec, row_spec],
        out_specs=[st_spec, row_spec],
        out_shape=[jax.ShapeDtypeStruct(state.shape, F32), jax.ShapeDtypeStruct((n_seq, GLA_DV_T), F32)],
        compiler_params=_params(("arbitrary",)),
        name="state_update",
    )(state, cols(q), cols(k), cols(a), v)


def _mixer_sample_out_kernel(x_ref, o_ref, og_ref, ob_ref, glag_ref, wo_ref, h_ref):
    glag = glag_ref[...]
    parts = []
    for h in range(GLA_HEADS):
        vs = slice(h * GLA_DV, (h + 1) * GLA_DV)
        parts.append(_head_rms(o_ref[:, vs], glag))
    merged = jnp.concatenate(parts, axis=1) * og_ref[...] + ob_ref[...]
    h_ref[...] = x_ref[...] + _dot(merged.astype(BF16), wo_ref[...])


def _mixer_sample_out(x2, o, og, ob, wts):
    n_seq = x2.shape[0]
    consts = [x2, o, og, ob, wts['gla_g'], wts['w_o']]
    return pl.pallas_call(
        _mixer_sample_out_kernel,
        grid=(1,),
        in_specs=[_const_spec(c.shape) for c in consts],
        out_specs=pl.BlockSpec((n_seq, D_MODEL), lambda i: (0, 0)),
        out_shape=jax.ShapeDtypeStruct((n_seq, D_MODEL), F32),
        compiler_params=_params(("arbitrary",)),
        name="mixer_sample_out",
    )(*consts)


def _router_kernel(h_ref, g2_ref, wr_ref, br_ref, xp_ref, ti_ref, tw_ref, rk_ref, cnt_ref, cnt_scr, *, tm):
    i = pl.program_id(0)

    @pl.when(i == 0)
    def _():
        cnt_scr[...] = jnp.zeros_like(cnt_scr)

    hn = _rms(h_ref[...], g2_ref[...])
    hn_hi = hn.astype(BF16)
    hn_lo = (hn - hn_hi.astype(F32)).astype(BF16)
    w = wr_ref[...]
    w_hi = w.astype(BF16)
    w_lo = (w - w_hi.astype(F32)).astype(BF16)
    logits = _dot(hn_hi, w_hi) + _dot(hn_lo, w_hi) + _dot(hn_hi, w_lo) + br_ref[...]

    xp_ref[...] = hn

    lane = lax.broadcasted_iota(I32, (tm, LANES), 1)
    neg = jnp.float32(-jnp.inf)
    l = jnp.where(lane < N_EXPERTS, logits, neg)
    vals, idxs, hots = [], [], []
    for _ in range(TOP_K):
        m = jnp.max(l, axis=-1, keepdims=True)
        idx = jnp.min(jnp.where(l == m, lane, LANES), axis=-1, keepdims=True)
        hot = lane == idx
        l = jnp.where(hot, neg, l)
        vals.append(m)
        idxs.append(idx)
        hots.append(hot)
    exps = [jnp.exp(v - vals[0]) for v in vals]
    denom = exps[0] + exps[1] + exps[2] + exps[3]

    member = (hots[0] | hots[1] | hots[2] | hots[3]).astype(BF16)
    row = lax.broadcasted_iota(I32, (tm, tm), 0)
    col = lax.broadcasted_iota(I32, (tm, tm), 1)
    earlier = (row > col).astype(BF16)
    before = cnt_scr[...] + _dot(earlier, member)
    cnt_scr[...] = cnt_scr[...] + jnp.sum(member.astype(F32), axis=0, keepdims=True)
    cnt_ref[...] = cnt_scr[...].astype(I32)

    ti = jnp.zeros((tm, LANES), I32)
    tw = jnp.zeros((tm, LANES), F32)
    rk = jnp.zeros((tm, LANES), I32)
    for kk in range(TOP_K):
        sel = lane == kk
        rank = jnp.sum(jnp.where(hots[kk], before, 0.0), axis=-1, keepdims=True).astype(I32)
        ti = jnp.where(sel, idxs[kk], ti)
        tw = jnp.where(sel, exps[kk] / denom, tw)
        rk = jnp.where(sel, rank, rk)
    ti_ref[...] = ti
    tw_ref[...] = tw
    rk_ref[...] = rk


def _router_tile(t_all):
    for tm in (512, 384, 256, 128):
        if t_all % tm == 0:
            return tm
    raise ValueError(f"token count {t_all} must be a multiple of 128")


def _router(hbuf, wts):
    t_all = hbuf.shape[0]
    tm = _router_tile(t_all)
    consts = [wts['g2'], wts['w_router'], wts['b_router']]
    row = lambda w: pl.BlockSpec((tm, w), lambda i: (i, 0))
    return pl.pallas_call(
        functools.partial(_router_kernel, tm=tm),
        grid=(t_all // tm,),
        in_specs=[row(D_MODEL)] + [_const_spec(c.shape) for c in consts],
        out_specs=[row(D_MODEL), row(LANES), row(LANES), row(LANES),
                   pl.BlockSpec((1, LANES), lambda i: (0, 0))],
        out_shape=[jax.ShapeDtypeStruct((t_all, D_MODEL), F32),
                   jax.ShapeDtypeStruct((t_all, LANES), I32),
                   jax.ShapeDtypeStruct((t_all, LANES), F32),
                   jax.ShapeDtypeStruct((t_all, LANES), I32),
                   jax.ShapeDtypeStruct((1, LANES), I32)],
        scratch_shapes=[pltpu.VMEM((1, LANES), F32)],
        compiler_params=_params(("arbitrary",)),
        name="router",
    )(hbuf, *consts)


def _row_copy(src_hbm, row, dst, sem):
    return pltpu.make_async_copy(src_hbm.at[pl.ds(row, 1)], dst, sem)


class _Layout:
    def __init__(self, t_all):
        bm = EXPERT_BLOCK
        self.t_all = t_all
        self.n_blocks = -(-(t_all * TOP_K + N_EXPERTS * (bm - 1)) // bm)
        self.slab = -(-t_all // SLAB_ALIGN) * SLAB_ALIGN
        self.dump0 = TOP_K * self.slab
        self.prime0 = self.dump0 + N_EXPERTS * bm
        self.y_rows = self.prime0 + bm
        self.table_rows = (self.n_blocks + 2) * bm


def _plan_kernel(dest_ref, lo_ref, hi_ref, base_ref, rows_ref, *, lay):
    bm = EXPERT_BLOCK

    def prime(r, c):
        rows_ref[r] = lay.prime0 + r
        return c
    lax.fori_loop(0, bm, prime, 0)

    def expert(e, c):
        base = base_ref[e]

        def pad(r, c2):
            rows_ref[bm + r] = base + r
            return c2
        lax.fori_loop(lo_ref[e], hi_ref[e], pad, 0)
        return c
    lax.fori_loop(0, N_EXPERTS, expert, 0)

    def tail(r, c):
        rows_ref[bm + r] = lay.dump0
        return c
    lax.fori_loop(hi_ref[N_EXPERTS - 1], lay.table_rows - bm, tail, 0)

    unroll = 8
    for k in range(TOP_K):
        def real(t8, c, k=k):
            for u in range(unroll):
                t = t8 * unroll + u
                rows_ref[bm + dest_ref[k * lay.t_all + t]] = k * lay.slab + t
            return c
        lax.fori_loop(0, lay.t_all // unroll, real, 0)


def _plan(dest_km, pad_lo, pad_hi, base, lay):
    grid_spec = pltpu.PrefetchScalarGridSpec(
        num_scalar_prefetch=4, grid=(1,), in_specs=[],
        out_specs=pl.BlockSpec(memory_space=pltpu.SMEM))
    return pl.pallas_call(
        functools.partial(_plan_kernel, lay=lay),
        grid_spec=grid_spec,
        out_shape=jax.ShapeDtypeStruct((lay.table_rows,), I32),
        compiler_params=_params(("arbitrary",)),
        name="plan",
    )(dest_km, pad_lo, pad_hi, base)


def _experts_kernel(be_ref, nv_ref, rows_ref, toks_ref, xp_hbm, wg_ref, wu_ref, wd_ref, bg_ref, bu_ref, bd_ref,
                    y_hbm, xbuf, obuf, sem_g, sem_s, *, lay):
    del be_ref
    bm = EXPERT_BLOCK
    i = pl.program_id(0)
    n_valid = nv_ref[0]
    slot = i % 2
    other = 1 - slot

    def gather_row(blk, s, r):
        return _row_copy(xp_hbm, toks_ref[(blk + 1) * bm + r], xbuf.at[s, pl.ds(r, 1)], sem_g.at[s])

    def scatter_row(blk, s, r):
        return pltpu.make_async_copy(obuf.at[s, pl.ds(r, 1)], y_hbm.at[pl.ds(rows_ref[(blk + 1) * bm + r], 1)],
                                     sem_s.at[s])

    def wait_gather(s):
        pltpu.make_async_copy(xp_hbm.at[pl.ds(0, bm)], xbuf.at[s], sem_g.at[s]).wait()

    def wait_scatter(s):
        pltpu.make_async_copy(obuf.at[s], y_hbm.at[pl.ds(0, bm)], sem_s.at[s]).wait()

    @pl.when(i == 0)
    def _():
        obuf[...] = jnp.zeros_like(obuf)
        fills = [(k * lay.slab + lay.t_all, lay.slab - lay.t_all) for k in range(TOP_K)]
        fills.append((lay.dump0, lay.y_rows - lay.dump0))
        copies = []
        for start, n in fills:
            for o in range(0, n, bm):
                m = min(bm, n - o)
                copies.append(pltpu.make_async_copy(obuf.at[0, pl.ds(0, m)], y_hbm.at[pl.ds(start + o, m)],
                                                    sem_s.at[0]))
        for cp in copies:
            cp.start()
        for cp in copies:
            cp.wait()

        def first(r, c):
            gather_row(0, 0, r).start()
            return c
        lax.fori_loop(0, bm, first, 0)

    @pl.when((i >= 1) & (i <= n_valid))
    def _():
        wait_scatter(slot)

    @pl.when(i < n_valid)
    def _():
        wait_gather(slot)
        x = xbuf[slot].astype(BF16)
        n_chunks = D_FF // EXPERT_COLS
        per_group = bm // (2 * n_chunks)

        def issue(group):
            for r in range(group * per_group, (group + 1) * per_group):
                gather_row(i + 1, other, r).start()
                scatter_row(i - 1, other, r).start()

        mids = []
        for j in range(n_chunks):
            cs = slice(j * EXPERT_COLS, (j + 1) * EXPERT_COLS)
            issue(2 * j)
            g = jnp.minimum(_dot(x, wg_ref[0, :, cs]) + bg_ref[0, :, cs], SWIGLU_LIMIT)
            issue(2 * j + 1)
            u = jnp.clip(_dot(x, wu_ref[0, :, cs]) + bu_ref[0, :, cs], -SWIGLU_LIMIT, SWIGLU_LIMIT)
            mids.append(((u + 1.0) * (g * _sigmoid(SWIGLU_ALPHA * g))).astype(BF16))
        mid = jnp.concatenate(mids, axis=1)
        obuf[slot] = _dot(mid, wd_ref[0]) + bd_ref[0]

    @pl.when(i == n_valid)
    def _():
        def last(r, c):
            scatter_row(i - 1, other, r).start()
            return c
        lax.fori_loop(0, bm, last, 0)
        wait_gather(slot)
        wait_scatter(other)


def _experts(block_e, n_valid, rows, toks, xp, wts, lay):
    bm = EXPERT_BLOCK
    wspec = pl.BlockSpec((1, D_MODEL, D_FF), lambda i, be, *_: (be[i], 0, 0))
    bspec = pl.BlockSpec((1, 1, D_FF), lambda i, be, *_: (be[i], 0, 0))
    grid_spec = pltpu.PrefetchScalarGridSpec(
        num_scalar_prefetch=4,
        grid=(lay.n_blocks + 1,),
        in_specs=[pl.BlockSpec(memory_space=pl.ANY), wspec, wspec, wspec, bspec, bspec, bspec],
        out_specs=pl.BlockSpec(memory_space=pl.ANY),
        scratch_shapes=[pltpu.VMEM((2, bm, D_MODEL), F32), pltpu.VMEM((2, bm, D_MODEL), F32),
                        pltpu.SemaphoreType.DMA((2,)), pltpu.SemaphoreType.DMA((2,))],
    )
    return pl.pallas_call(
        functools.partial(_experts_kernel, lay=lay),
        grid_spec=grid_spec,
        out_shape=jax.ShapeDtypeStruct((lay.y_rows, D_MODEL), F32),
        compiler_params=_params(("arbitrary",)),
        name="experts",
    )(block_e, n_valid, rows, toks, xp, wts['w_gate'], wts['w_up'], wts['w_down'],
      wts['b_gate'], wts['b_up'], wts['b_down'])


def _combine_kernel(h_ref, tw_ref, p_ref, y0_ref, y1_ref, y2_ref, y3_ref, g3_ref, wpg_ref, wpp_ref, gf_ref,
                    out_ref):
    tw = tw_ref[...]
    y = y0_ref[...] * tw[:, 0:1]
    for kk, y_ref in enumerate((y1_ref, y2_ref, y3_ref), start=1):
        y = y + y_ref[...] * tw[:, kk:kk + 1]
    h2 = h_ref[...] + y
    gate = _sigmoid(_dot(_rms(h2, g3_ref[...]).astype(BF16), wpg_ref[...]))
    h3 = h2 + gate * _dot(p_ref[...].astype(BF16), wpp_ref[...])
    out_ref[...] = _rms(h3, gf_ref[...])


def _combine(hbuf, tw, p2, y, row0, wts, lay):
    n_rows = p2.shape[0]
    tm = min(COMBINE_TILE, n_rows)
    assert n_rows % tm == 0 and row0 % tm == 0 and lay.slab % tm == 0
    off = row0 // tm
    consts = [wts['g3'], wts['w_ple_gate'], wts['w_ple_proj'], wts['g_final']]

    def slab_spec(k):
        return pl.BlockSpec((tm, D_MODEL), lambda i: (k * (lay.slab // tm) + off + i, 0))

    return pl.pallas_call(
        _combine_kernel,
        grid=(n_rows // tm,),
        in_specs=[pl.BlockSpec((tm, D_MODEL), lambda i: (i + off, 0)),
                  pl.BlockSpec((tm, LANES), lambda i: (i + off, 0)),
                  pl.BlockSpec((tm, D_PLE), lambda i: (i, 0))]
        + [slab_spec(k) for k in range(TOP_K)] + [_const_spec(c.shape) for c in consts],
        out_specs=pl.BlockSpec((tm, D_MODEL), lambda i: (i, 0)),
        out_shape=jax.ShapeDtypeStruct((n_rows, D_MODEL), F32),
        compiler_params=_params(("arbitrary",)),
        name="combine",
    )(hbuf, tw, p2, y, y, y, y, *consts)


def _prepare_weights(norm1_g, w_in, w_gk2, b_gk, gla_norm_g, v_norm_g, v_norm_b, w_sp, b_sp, w_o, norm2_g,
                     w_router, b_router, w_gate, b_gate, w_up, b_up, w_down, b_down, norm3_g, w_ple_gate,
                     w_ple_proj, final_g):
    o_gk = 2 * GLA_DK_T + 2 * GLA_DV_T
    w_main = jnp.concatenate([w_in[:, :o_gk], w_in[:, o_gk + GK_RANK:]], axis=1).astype(BF16)
    w_gk1 = jnp.pad(w_in[:, o_gk:o_gk + GK_RANK], ((0, 0), (0, LANES - GK_RANK))).astype(BF16)
    w_gk2p = jnp.pad(w_gk2, ((0, LANES - GK_RANK), (0, 0))).astype(BF16)
    row = lambda z: z.reshape(1, -1).astype(F32)
    return dict(
        g1=row(norm1_g), w_main=w_main, w_gk1=w_gk1, w_gk2=w_gk2p, b_gk=row(b_gk), gla_g=row(gla_norm_g),
        vn_g=row(v_norm_g), vn_b=row(v_norm_b), w_sp=w_sp, b_sp_t=b_sp.T,
        mix_w0=row(jnp.repeat(w_sp[:, 0, 0], MIX_DH)), mix_b0=row(jnp.repeat(b_sp[:, 0], MIX_DH)),
        w_o=w_o.astype(BF16), g2=row(norm2_g),
        w_router=jnp.pad(w_router, ((0, 0), (0, LANES - N_EXPERTS))),
        b_router=jnp.pad(row(b_router), ((0, 0), (0, LANES - N_EXPERTS))),
        w_gate=w_gate.astype(BF16), w_up=w_up.astype(BF16), w_down=w_down.astype(BF16),
        b_gate=b_gate.reshape(N_EXPERTS, 1, D_FF), b_up=b_up.reshape(N_EXPERTS, 1, D_FF),
        b_down=b_down.reshape(N_EXPERTS, 1, D_MODEL),
        g3=row(norm3_g), w_ple_gate=w_ple_gate.astype(BF16), w_ple_proj=w_ple_proj.astype(BF16),
        g_final=row(final_g))


def _dispatch_plan(top_i, rank, counts, lay):
    bm = EXPERT_BLOCK
    padded = (counts + bm - 1) // bm * bm
    pad_end = jnp.cumsum(padded)
    pad_start = pad_end - padded
    raw_end = jnp.cumsum(counts)
    experts = jnp.arange(N_EXPERTS, dtype=I32)
    start_of = jnp.sum(jnp.where(top_i[:, :, None] == experts, pad_start, 0), axis=-1)
    dest_km = (start_of + rank).T.reshape(-1).astype(I32)
    n_valid = (pad_end[-1] // bm).astype(I32)
    blk = jnp.minimum(jnp.arange(lay.n_blocks + 1, dtype=I32), n_valid - 1)
    block_e = jnp.sum(pad_end[None, :] <= (blk * bm)[:, None], axis=1).astype(I32)
    rows = _plan(dest_km, (pad_start + counts).astype(I32), pad_end.astype(I32),
                 (lay.dump0 - raw_end).astype(I32), lay)
    toks = jnp.where(rows < lay.dump0, rows % lay.slab, 0)
    return rows, toks, block_e, n_valid.reshape(1)


def _layer(x_prompt, x_sample, state, p_prompt, p_sample, wts):
    n_p, len_p, _ = x_prompt.shape
    n_s, len_s, _ = x_sample.shape
    assert len_s == 1, "the sample group carries one new token per sequence"
    t_p = n_p * len_p
    t_all = t_p + n_s

    xs = x_sample.reshape(n_s, D_MODEL)
    q, k, a, v, og, ob, vn = _mixer_sample_in(xs, wts)
    st_sample, o = _state_update(state, q, k, a, v)
    h_sample = _mixer_sample_out(xs, o, og, ob, wts)
    hbuf, st_prompt = _mixer_prompt(x_prompt, h_sample, wts)

    xp, top_i, top_w, rank, counts = _router(hbuf, wts)
    lay = _Layout(t_all)
    rows, toks, block_e, n_valid = _dispatch_plan(top_i[:, :TOP_K], rank[:, :TOP_K], counts[0, :N_EXPERTS], lay)
    y = _experts(block_e, n_valid, rows, toks, xp, wts, lay)
    y_prompt = _combine(hbuf, top_w, p_prompt.reshape(t_p, D_PLE), y, 0, wts, lay)
    y_sample = _combine(hbuf, top_w, p_sample.reshape(n_s, D_PLE), y, t_p, wts, lay)
    return (y_prompt.reshape(n_p, len_p, D_MODEL), y_sample.reshape(n_s, len_s, D_MODEL),
            st_prompt, st_sample, vn.reshape(n_s, len_s, D_GMLP))


def kernel(x_prompt, x_sample, state_gla, p_prompt, p_sample, norm1_g, w_in, w_gk2, b_gk, gla_norm_g, v_norm_g,
           v_norm_b, w_sp, b_sp, w_o, norm2_g, w_router, b_router, w_gate, b_gate, w_up, b_up, w_down, b_down,
           norm3_g, w_ple_gate, w_ple_proj, final_g):
    assert w_in.shape[0] == 1, "single-layer trunk"
    wts = _prepare_weights(norm1_g[0], w_in[0], w_gk2[0], b_gk[0], gla_norm_g[0], v_norm_g[0], v_norm_b[0],
                           w_sp[0], b_sp[0], w_o[0], norm2_g[0], w_router[0], b_router[0], w_gate[0], b_gate[0],
                           w_up[0], b_up[0], w_down[0], b_down[0], norm3_g[0], w_ple_gate[0], w_ple_proj[0],
                           final_g)
    y_p, y_s, st_p, st_s, vn_s = _layer(x_prompt, x_sample, state_gla[0], p_prompt[0], p_sample[0], wts)
    return (y_p, y_s, st_p[None], st_s[None].astype(state_gla.dtype), vn_s[None])
```

```python
import functools

import jax
import jax.numpy as jnp
from jax import lax
from jax.experimental import pallas as pl
from jax.experimental.pallas import tpu as pltpu

F32 = jnp.float32
BF16 = jnp.bfloat16
I32 = jnp.int32
U32 = jnp.uint32

D_MODEL = 1024
GLA_HEADS = 4
GLA_DK = 128
GLA_DV = 256
GLA_DK_T = GLA_HEADS * GLA_DK
GLA_DV_T = GLA_HEADS * GLA_DV
GK_RANK = 16
GATE_NORMALIZER = 16.0
GLA_CHUNK = 64
GLA_CHUNK_LOG2 = 6
D_GMLP = 1024
MIX_HEADS = 4
MIX_DH = D_GMLP // MIX_HEADS
MIX_CHUNK = 128
MIX_CHUNK_LOG2 = 7
assert 1 << GLA_CHUNK_LOG2 == GLA_CHUNK and 1 << MIX_CHUNK_LOG2 == MIX_CHUNK
N_EXPERTS = 32
TOP_K = 4
D_FF = 1024
SWIGLU_LIMIT = 7.0
SWIGLU_ALPHA = 1.702
D_PLE = 256
EPS = 1e-6

LANES = 128
ROW_SUBLANES = D_MODEL // LANES
VMEM_LIMIT_BYTES = 56 * 1024 * 1024

OFF_Q, OFF_K, OFF_V, OFF_R, OFF_U, OFF_VG, OFF_GA, OFF_GB, N_MAIN = (
    0, 512, 1024, 2048, 3072, 4096, 5120, 6144, 7168)

MIXER_TILE = 256
STATE_TOKENS = 8
EXPERT_BLOCK = 256
EXPERT_COLS = 256
COMBINE_TILE = 256
SLAB_ALIGN = 512


def _dot(a, b):
    return jnp.dot(a, b, preferred_element_type=F32)


def _dot_nt(a, b):
    return lax.dot_general(a, b, (((1,), (1,)), ((), ())), preferred_element_type=F32)


def _dot_tn(a, b):
    return lax.dot_general(a, b, (((0,), (0,)), ((), ())), preferred_element_type=F32)


def _rms(x, g):
    return x * lax.rsqrt(jnp.mean(x * x, axis=-1, keepdims=True) + EPS) * g


def _sigmoid(x):
    return 1.0 / (1.0 + jnp.exp(-x))


def _gelu(x):
    return 0.5 * x * (1.0 + lax.erf(x * (2.0 ** -0.5)))


def _log_sigmoid(x):
    return jnp.minimum(x, 0.0) - jnp.log1p(jnp.exp(-jnp.abs(x)))


def _split3(x):
    hi = x.astype(BF16)
    r1 = x - hi.astype(F32)
    mid = r1.astype(BF16)
    lo = (r1 - mid.astype(F32)).astype(BF16)
    return hi, mid, lo


def _const_spec(shape):
    nd = len(shape)
    return pl.BlockSpec(shape, lambda *_: (0,) * nd, pipeline_mode=pl.Buffered(1))


def _params(sem):
    return pltpu.CompilerParams(dimension_semantics=sem, vmem_limit_bytes=VMEM_LIMIT_BYTES)


def _project_gla_inputs(n, wm_ref, wgk1_ref, wgk2_ref, bgk_ref):
    q = _dot(n, wm_ref[:, OFF_Q:OFF_K]) * (GLA_DK ** -0.5)
    k = _dot(n, wm_ref[:, OFF_K:OFF_V])
    v = _dot(n, wm_ref[:, OFF_V:OFF_R])
    gk = _dot(n, wgk1_ref[...]).astype(BF16)
    log_a = _log_sigmoid(_dot(gk, wgk2_ref[...]) + bgk_ref[...]) * (1.0 / GATE_NORMALIZER)
    return q, k, v, log_a


def _gmlp_inputs(n, wm_ref, vng_ref, vnb_ref):
    u = _gelu(_dot(n, wm_ref[:, OFF_U:OFF_VG]))
    vg = _gelu(_dot(n, wm_ref[:, OFF_VG:OFF_GA]))
    mu = jnp.mean(vg, axis=-1, keepdims=True)
    vc = vg - mu
    var = jnp.mean(vc * vc, axis=-1, keepdims=True)
    vn = vc * lax.rsqrt(var + EPS) * vng_ref[...] + vnb_ref[...]
    ug = u * _sigmoid(_dot(n, wm_ref[:, OFF_GB:N_MAIN]))
    return ug, vn


def _gla_out_gate(n, wm_ref):
    r = _dot(n, wm_ref[:, OFF_R:OFF_U])
    ga = _dot(n, wm_ref[:, OFF_GA:OFF_GB])
    return r * _sigmoid(r) * _sigmoid(ga)


def _head_rms(o, g):
    return o * lax.rsqrt(jnp.mean(o * o, axis=-1, keepdims=True) + EPS) * g


def _mixer_prompt_kernel(x_ref, hs_ref, *refs, tm, n_seq, n_tiles):
    b = pl.program_id(0)
    t = pl.program_id(1)
    h_ref = refs[11]

    @pl.when(b < n_seq)
    def _():
        _mixer_prompt_tile(x_ref, *refs, tm=tm, n_tiles=n_tiles)

    @pl.when((b == n_seq) & (t == 0))
    def _():
        h_ref[0:hs_ref.shape[0], :] = hs_ref[...]


def _mixer_prompt_tile(x_ref, g1_ref, wm_ref, wgk1_ref, wgk2_ref, bgk_ref, glag_ref, vng_ref, vnb_ref,
                       wsp_ref, bspt_ref, wo_ref,
                       h_ref, st_ref,
                       n_scr, oa_scr, s_scr, *, tm, n_tiles):
    t = pl.program_id(1)

    @pl.when(t == 0)
    def _():
        s_scr[...] = jnp.zeros_like(s_scr)

    x = x_ref[0]
    n = _rms(x, g1_ref[...]).astype(BF16)
    n_scr[...] = n
    q, k, v, log_a = _project_gla_inputs(n, wm_ref, wgk1_ref, wgk2_ref, bgk_ref)
    v = v.astype(BF16)

    row = lax.broadcasted_iota(I32, (tm, tm), 0)
    col = lax.broadcasted_iota(I32, (tm, tm), 1)
    same_chunk = lax.shift_right_logical(row, GLA_CHUNK_LOG2) == lax.shift_right_logical(col, GLA_CHUNK_LOG2)
    causal = same_chunk & (row >= col)
    tri = causal.astype(BF16)
    blk = same_chunk.astype(BF16)
    hi, mid, lo = _split3(log_a)
    b = _dot(tri, hi) + _dot(tri, mid) + _dot(tri, lo)
    b_end = _dot(blk, hi) + _dot(blk, mid) + _dot(blk, lo)
    qe = (q * jnp.exp(b)).astype(BF16)
    ke = (k * jnp.exp(-b)).astype(BF16)
    kd = (k * jnp.exp(b_end - b)).astype(BF16)
    dec = jnp.exp(b_end)
    glag = glag_ref[...]
    for h in range(GLA_HEADS):
        ks = slice(h * GLA_DK, (h + 1) * GLA_DK)
        vs = slice(h * GLA_DV, (h + 1) * GLA_DV)
        vh = v[:, vs]
        att = jnp.where(causal, _dot_nt(qe[:, ks], ke[:, ks]), 0.0).astype(BF16)
        o_intra = _dot(att, vh)
        s_t = s_scr[h]
        o_inter = []
        for c in range(tm // GLA_CHUNK):
            rs = slice(c * GLA_CHUNK, (c + 1) * GLA_CHUNK)
            o_inter.append(_dot_nt(qe[rs, ks], s_t.astype(BF16)))
            s_t = s_t * dec[c * GLA_CHUNK:c * GLA_CHUNK + 1, ks] + _dot_tn(vh[rs], kd[rs, ks])
        s_scr[h] = s_t
        oa_scr[:, vs] = _head_rms(o_intra + jnp.concatenate(o_inter, axis=0), glag)

    n = n_scr[...]
    oa_scr[...] = oa_scr[...] * _gla_out_gate(n, wm_ref)
    ug, vn = _gmlp_inputs(n, wm_ref, vng_ref, vnb_ref)
    vn = vn.astype(BF16)
    reps = tm // MIX_CHUNK
    mix_mask = (lax.shift_right_logical(row, MIX_CHUNK_LOG2) == lax.shift_right_logical(col, MIX_CHUNK_LOG2)) & (
        row >= col)
    for h in range(MIX_HEADS):
        w = jnp.where(mix_mask, jnp.tile(wsp_ref[h], (reps, reps)), 0.0).astype(BF16)
        cs = slice(h * MIX_DH, (h + 1) * MIX_DH)
        mix = _dot(w, vn[:, cs]) + jnp.tile(bspt_ref[:, h:h + 1], (reps, 1))
        oa_scr[:, cs] = oa_scr[:, cs] + ug[:, cs] * mix

    h_ref[...] = x + _dot(oa_scr[...].astype(BF16), wo_ref[...])

    @pl.when(t == n_tiles - 1)
    def _():
        for h in range(GLA_HEADS):
            st_ref[0, h] = s_scr[h].T


def _mixer_prompt(x, h_sample, wts):
    n_seq, seq_len, _ = x.shape
    n_s = h_sample.shape[0]
    tm = min(MIXER_TILE, seq_len)
    assert seq_len % tm == 0 and tm % MIX_CHUNK == 0 and n_s <= tm
    n_tiles = seq_len // tm
    consts = [wts['g1'], wts['w_main'], wts['w_gk1'], wts['w_gk2'], wts['b_gk'], wts['gla_g'], wts['vn_g'],
              wts['vn_b'], wts['w_sp'], wts['b_sp_t'], wts['w_o']]
    kern = functools.partial(_mixer_prompt_kernel, tm=tm, n_seq=n_seq, n_tiles=n_tiles)
    last = n_seq - 1

    def prompt_tile(b, t):
        return jnp.minimum(b, last), jnp.where(b < n_seq, t, n_tiles - 1)

    def x_map(b, t):
        bb, tt = prompt_tile(b, t)
        return bb, tt, 0

    def h_map(b, t):
        return jnp.where(b < n_seq, b * n_tiles + t, n_seq * n_tiles), 0

    return pl.pallas_call(
        kern,
        grid=(n_seq + 1, n_tiles),
        in_specs=[pl.BlockSpec((1, tm, D_MODEL), x_map), _const_spec(h_sample.shape)]
        + [_const_spec(c.shape) for c in consts],
        out_specs=[pl.BlockSpec((tm, D_MODEL), h_map),
                   pl.BlockSpec((1, GLA_HEADS, GLA_DK, GLA_DV), lambda b, t: (jnp.minimum(b, last), 0, 0, 0))],
        out_shape=[jax.ShapeDtypeStruct((n_seq * seq_len + n_s, D_MODEL), F32),
                   jax.ShapeDtypeStruct((n_seq, GLA_HEADS, GLA_DK, GLA_DV), F32)],
        scratch_shapes=[pltpu.VMEM((tm, D_MODEL), BF16),
                        pltpu.VMEM((tm, D_MODEL), F32),
                        pltpu.VMEM((GLA_HEADS, GLA_DV, GLA_DK), F32)],
        compiler_params=_params(("arbitrary", "arbitrary")),
        name="mixer_prompt",
    )(x, h_sample, *consts)


def _mixer_sample_in_kernel(x_ref, g1_ref, wm_ref, wgk1_ref, wgk2_ref, bgk_ref, vng_ref, vnb_ref,
                            mixw_ref, mixb_ref,
                            q_ref, k_ref, a_ref, v_ref, og_ref, ob_ref, vn_ref):
    n = _rms(x_ref[...], g1_ref[...]).astype(BF16)
    q, k, v, log_a = _project_gla_inputs(n, wm_ref, wgk1_ref, wgk2_ref, bgk_ref)
    q_ref[...] = q
    k_ref[...] = k
    a_ref[...] = jnp.exp(log_a)
    v_ref[...] = v
    og_ref[...] = _gla_out_gate(n, wm_ref)
    ug, vn = _gmlp_inputs(n, wm_ref, vng_ref, vnb_ref)
    vn_ref[...] = vn
    ob_ref[...] = ug * (mixw_ref[...] * vn + mixb_ref[...])


def _mixer_sample_in(x2, wts):
    n_seq = x2.shape[0]
    consts = [wts['g1'], wts['w_main'], wts['w_gk1'], wts['w_gk2'], wts['b_gk'], wts['vn_g'], wts['vn_b'],
              wts['mix_w0'], wts['mix_b0']]
    widths = [GLA_DK_T, GLA_DK_T, GLA_DK_T, GLA_DV_T, D_MODEL, D_MODEL, D_GMLP]
    return pl.pallas_call(
        _mixer_sample_in_kernel,
        grid=(1,),
        in_specs=[_const_spec(x2.shape)] + [_const_spec(c.shape) for c in consts],
        out_specs=[pl.BlockSpec((n_seq, w), lambda i: (0, 0)) for w in widths],
        out_shape=[jax.ShapeDtypeStruct((n_seq, w), F32) for w in widths],
        compiler_params=_params(("arbitrary",)),
        name="mixer_sample_in",
    )(x2, *consts)


def _state_update_kernel(s_ref, q_ref, k_ref, a_ref, v_ref, so_ref, o_ref):
    for j in range(STATE_TOKENS):
        for h in range(GLA_HEADS):
            ks = slice(h * GLA_DK, (h + 1) * GLA_DK)
            vs = slice(h * GLA_DV, (h + 1) * GLA_DV)
            s_new = a_ref[0, ks, j:j + 1] * s_ref[j, h] + k_ref[0, ks, j:j + 1] * v_ref[j:j + 1, vs]
            so_ref[j, h] = s_new
            o_ref[j:j + 1, vs] = jnp.sum(q_ref[0, ks, j:j + 1] * s_new, axis=0, keepdims=True)


def _state_update(state, q, k, a, v):
    n_seq = state.shape[0]
    tb = STATE_TOKENS
    assert n_seq % tb == 0

    def cols(z):
        return z.reshape(n_seq // tb, tb, GLA_DK_T).transpose(0, 2, 1)

    col_spec = pl.BlockSpec((1, GLA_DK_T, tb), lambda i: (i, 0, 0))
    st_spec = pl.BlockSpec((tb, GLA_HEADS, GLA_DK, GLA_DV), lambda i: (i, 0, 0, 0))
    row_spec = pl.BlockSpec((tb, GLA_DV_T), lambda i: (i, 0))
    return pl.pallas_call(
        _state_update_kernel,
        grid=(n_seq // tb,),
        in_specs=[st_spec, col_spec, col_spec, col_spec, row_spec],
        out_specs=[st_spec, row_spec],
        out_shape=[jax.ShapeDtypeStruct(state.shape, F32), jax.ShapeDtypeStruct((n_seq, GLA_DV_T), F32)],
        compiler_params=_params(("arbitrary",)),
        name="state_update",
    )(state, cols(q), cols(k), cols(a), v)


def _mixer_sample_out_kernel(x_ref, o_ref, og_ref, ob_ref, glag_ref, wo_ref, h_ref):
    glag = glag_ref[...]
    parts = []
    for h in range(GLA_HEADS):
        vs = slice(h * GLA_DV, (h + 1) * GLA_DV)
        parts.append(_head_rms(o_ref[:, vs], glag))
    merged = jnp.concatenate(parts, axis=1) * og_ref[...] + ob_ref[...]
    h_ref[...] = x_ref[...] + _dot(merged.astype(BF16), wo_ref[...])


def _mixer_sample_out(x2, o, og, ob, wts):
    n_seq = x2.shape[0]
    consts = [x2, o, og, ob, wts['gla_g'], wts['w_o']]
    return pl.pallas_call(
        _mixer_sample_out_kernel,
        grid=(1,),
        in_specs=[_const_spec(c.shape) for c in consts],
        out_specs=pl.BlockSpec((n_seq, D_MODEL), lambda i: (0, 0)),
        out_shape=jax.ShapeDtypeStruct((n_seq, D_MODEL), F32),
        compiler_params=_params(("arbitrary",)),
        name="mixer_sample_out",
    )(*consts)


def _router_kernel(h_ref, g2_ref, wr_ref, br_ref, xp_ref, ti_ref, tw_ref, rk_ref, cnt_ref, cnt_scr, *, tm):
    i = pl.program_id(0)

    @pl.when(i == 0)
    def _():
        cnt_scr[...] = jnp.zeros_like(cnt_scr)

    hn = _rms(h_ref[...], g2_ref[...])
    hn_hi = hn.astype(BF16)
    hn_lo = (hn - hn_hi.astype(F32)).astype(BF16)
    w = wr_ref[...]
    w_hi = w.astype(BF16)
    w_lo = (w - w_hi.astype(F32)).astype(BF16)
    logits = _dot(hn_hi, w_hi) + _dot(hn_lo, w_hi) + _dot(hn_hi, w_lo) + br_ref[...]

    _rows_to_tiles(xp_ref, hn)

    lane = lax.broadcasted_iota(I32, (tm, LANES), 1)
    neg = jnp.float32(-jnp.inf)
    l = jnp.where(lane < N_EXPERTS, logits, neg)
    vals, idxs, hots = [], [], []
    for _ in range(TOP_K):
        m = jnp.max(l, axis=-1, keepdims=True)
        idx = jnp.min(jnp.where(l == m, lane, LANES), axis=-1, keepdims=True)
        hot = lane == idx
        l = jnp.where(hot, neg, l)
        vals.append(m)
        idxs.append(idx)
        hots.append(hot)
    exps = [jnp.exp(v - vals[0]) for v in vals]
    denom = exps[0] + exps[1] + exps[2] + exps[3]

    member = (hots[0] | hots[1] | hots[2] | hots[3]).astype(BF16)
    row = lax.broadcasted_iota(I32, (tm, tm), 0)
    col = lax.broadcasted_iota(I32, (tm, tm), 1)
    earlier = (row > col).astype(BF16)
    before = cnt_scr[...] + _dot(earlier, member)
    cnt_scr[...] = cnt_scr[...] + jnp.sum(member.astype(F32), axis=0, keepdims=True)
    cnt_ref[...] = cnt_scr[...].astype(I32)

    ti = jnp.zeros((tm, LANES), I32)
    tw = jnp.zeros((tm, LANES), F32)
    rk = jnp.zeros((tm, LANES), I32)
    for kk in range(TOP_K):
        sel = lane == kk
        rank = jnp.sum(jnp.where(hots[kk], before, 0.0), axis=-1, keepdims=True).astype(I32)
        ti = jnp.where(sel, idxs[kk], ti)
        tw = jnp.where(sel, exps[kk] / denom, tw)
        rk = jnp.where(sel, rank, rk)
    ti_ref[...] = ti
    tw_ref[...] = tw
    rk_ref[...] = rk


def _router_tile(t_all):
    for tm in (512, 384, 256, 128):
        if t_all % tm == 0:
            return tm
    raise ValueError(f"token count {t_all} must be a multiple of 128")


def _router(hbuf, wts):
    t_all = hbuf.shape[0]
    tm = _router_tile(t_all)
    consts = [wts['g2'], wts['w_router'], wts['b_router']]
    row = lambda w: pl.BlockSpec((tm, w), lambda i: (i, 0))
    return pl.pallas_call(
        functools.partial(_router_kernel, tm=tm),
        grid=(t_all // tm,),
        in_specs=[row(D_MODEL)] + [_const_spec(c.shape) for c in consts],
        out_specs=[pl.BlockSpec((tm * ROW_SUBLANES, LANES), lambda i: (i, 0)), row(LANES), row(LANES), row(LANES),
                   pl.BlockSpec((1, LANES), lambda i: (0, 0))],
        out_shape=[jax.ShapeDtypeStruct((t_all * ROW_SUBLANES, LANES), F32),
                   jax.ShapeDtypeStruct((t_all, LANES), I32),
                   jax.ShapeDtypeStruct((t_all, LANES), F32),
                   jax.ShapeDtypeStruct((t_all, LANES), I32),
                   jax.ShapeDtypeStruct((1, LANES), I32)],
        scratch_shapes=[pltpu.VMEM((1, LANES), F32)],
        compiler_params=_params(("arbitrary",)),
        name="router",
    )(hbuf, *consts)


def _row_tile(r):
    return pl.ds(r * ROW_SUBLANES, ROW_SUBLANES)


def _tile_of(ref, row):
    return ref.at[pl.ds(pl.multiple_of(row * ROW_SUBLANES, ROW_SUBLANES), ROW_SUBLANES)]


def _rows_from_tiles(ref, n_rows):
    return jnp.concatenate([ref[pl.ds(s, n_rows, stride=ROW_SUBLANES), :] for s in range(ROW_SUBLANES)], axis=1)


def _rows_to_tiles(ref, x):
    for s in range(ROW_SUBLANES):
        ref[pl.ds(s, x.shape[0], stride=ROW_SUBLANES), :] = x[:, s * LANES:(s + 1) * LANES]


class _Layout:
    def __init__(self, t_all):
        bm = EXPERT_BLOCK
        self.t_all = t_all
        self.n_blocks = -(-(t_all * TOP_K + N_EXPERTS * (bm - 1)) // bm)
        self.slab = -(-t_all // SLAB_ALIGN) * SLAB_ALIGN
        self.dump0 = TOP_K * self.slab
        self.prime0 = self.dump0 + N_EXPERTS * bm
        self.y_rows = self.prime0 + bm
        self.table_rows = (self.n_blocks + 2) * bm


def _plan_kernel(dest_ref, lo_ref, hi_ref, base_ref, rows_ref, *, lay):
    bm = EXPERT_BLOCK

    def prime(r, c):
        rows_ref[r] = lay.prime0 + r
        return c
    lax.fori_loop(0, bm, prime, 0)

    def expert(e, c):
        base = base_ref[e]

        def pad(r, c2):
            rows_ref[bm + r] = base + r
            return c2
        lax.fori_loop(lo_ref[e], hi_ref[e], pad, 0)
        return c
    lax.fori_loop(0, N_EXPERTS, expert, 0)

    def tail(r, c):
        rows_ref[bm + r] = lay.dump0
        return c
    lax.fori_loop(hi_ref[N_EXPERTS - 1], lay.table_rows - bm, tail, 0)

    unroll = 8
    for k in range(TOP_K):
        def real(t8, c, k=k):
            for u in range(unroll):
                t = t8 * unroll + u
                rows_ref[bm + dest_ref[k * lay.t_all + t]] = k * lay.slab + t
            return c
        lax.fori_loop(0, lay.t_all // unroll, real, 0)


def _plan(dest_km, pad_lo, pad_hi, base, lay):
    grid_spec = pltpu.PrefetchScalarGridSpec(
        num_scalar_prefetch=4, grid=(1,), in_specs=[],
        out_specs=pl.BlockSpec(memory_space=pltpu.SMEM))
    return pl.pallas_call(
        functools.partial(_plan_kernel, lay=lay),
        grid_spec=grid_spec,
        out_shape=jax.ShapeDtypeStruct((lay.table_rows,), I32),
        compiler_params=_params(("arbitrary",)),
        name="plan",
    )(dest_km, pad_lo, pad_hi, base)


def _experts_kernel(be_ref, nv_ref, rows_ref, toks_ref, xp_hbm, wg_ref, wu_ref, wd_ref, bg_ref, bu_ref, bd_ref,
                    y_hbm, xbuf, obuf, sem_g, sem_s, *, lay):
    del be_ref
    bm = EXPERT_BLOCK
    i = pl.program_id(0)
    n_valid = nv_ref[0]
    slot = i % 2
    other = 1 - slot

    def gather_row(blk, s, r):
        return pltpu.make_async_copy(_tile_of(xp_hbm, toks_ref[(blk + 1) * bm + r]), xbuf.at[s, _row_tile(r)],
                                     sem_g.at[s])

    def scatter_row(blk, s, r):
        return pltpu.make_async_copy(obuf.at[s, _row_tile(r)], _tile_of(y_hbm, rows_ref[(blk + 1) * bm + r]),
                                     sem_s.at[s])

    def wait_gather(s):
        pltpu.make_async_copy(xp_hbm.at[pl.ds(0, bm * ROW_SUBLANES)], xbuf.at[s], sem_g.at[s]).wait()

    def wait_scatter(s):
        pltpu.make_async_copy(obuf.at[s], y_hbm.at[pl.ds(0, bm * ROW_SUBLANES)], sem_s.at[s]).wait()

    @pl.when(i == 0)
    def _():
        obuf[...] = jnp.zeros_like(obuf)
        fills = [(k * lay.slab + lay.t_all, lay.slab - lay.t_all) for k in range(TOP_K)]
        fills.append((lay.dump0, lay.y_rows - lay.dump0))
        copies = []
        for start, n in fills:
            for o in range(0, n, bm):
                m = min(bm, n - o) * ROW_SUBLANES
                copies.append(pltpu.make_async_copy(obuf.at[0, pl.ds(0, m)],
                                                    y_hbm.at[pl.ds((start + o) * ROW_SUBLANES, m)], sem_s.at[0]))
        for cp in copies:
            cp.start()
        for cp in copies:
            cp.wait()

        def first(r, c):
            gather_row(0, 0, r).start()
            return c
        lax.fori_loop(0, bm, first, 0)

    @pl.when((i >= 1) & (i <= n_valid))
    def _():
        wait_scatter(slot)

    @pl.when(i < n_valid)
    def _():
        wait_gather(slot)
        x = _rows_from_tiles(xbuf.at[slot], bm).astype(BF16)
        n_chunks = D_FF // EXPERT_COLS
        per_group = bm // (2 * n_chunks)

        def issue(group):
            for r in range(group * per_group, (group + 1) * per_group):
                gather_row(i + 1, other, r).start()
                scatter_row(i - 1, other, r).start(priority=1)

        mids = []
        for j in range(n_chunks):
            cs = slice(j * EXPERT_COLS, (j + 1) * EXPERT_COLS)
            issue(2 * j)
            g = jnp.minimum(_dot(x, wg_ref[0, :, cs]) + bg_ref[0, :, cs], SWIGLU_LIMIT)
            issue(2 * j + 1)
            u = jnp.clip(_dot(x, wu_ref[0, :, cs]) + bu_ref[0, :, cs], -SWIGLU_LIMIT, SWIGLU_LIMIT)
            mids.append(((u + 1.0) * (g * _sigmoid(SWIGLU_ALPHA * g))).astype(BF16))
        mid = jnp.concatenate(mids, axis=1)
        _rows_to_tiles(obuf.at[slot], _dot(mid, wd_ref[0]) + bd_ref[0])

    @pl.when(i == n_valid)
    def _():
        def last(r, c):
            scatter_row(i - 1, other, r).start()
            return c
        lax.fori_loop(0, bm, last, 0)
        wait_gather(slot)
        wait_scatter(other)


def _experts(block_e, n_valid, rows, toks, xp, wts, lay):
    bm = EXPERT_BLOCK
    tile_rows = bm * ROW_SUBLANES
    wspec = pl.BlockSpec((1, D_MODEL, D_FF), lambda i, be, *_: (be[i], 0, 0))
    bspec = pl.BlockSpec((1, 1, D_FF), lambda i, be, *_: (be[i], 0, 0))
    grid_spec = pltpu.PrefetchScalarGridSpec(
        num_scalar_prefetch=4,
        grid=(lay.n_blocks + 1,),
        in_specs=[pl.BlockSpec(memory_space=pl.ANY), wspec, wspec, wspec, bspec, bspec, bspec],
        out_specs=pl.BlockSpec(memory_space=pl.ANY),
        scratch_shapes=[pltpu.VMEM((2, tile_rows, LANES), F32), pltpu.VMEM((2, tile_rows, LANES), F32),
                        pltpu.SemaphoreType.DMA((2,)), pltpu.SemaphoreType.DMA((2,))],
    )
    return pl.pallas_call(
        functools.partial(_experts_kernel, lay=lay),
        grid_spec=grid_spec,
        out_shape=jax.ShapeDtypeStruct((lay.y_rows * ROW_SUBLANES, LANES), F32),
        compiler_params=_params(("arbitrary",)),
        name="experts",
    )(block_e, n_valid, rows, toks, xp, wts['w_gate'], wts['w_up'], wts['w_down'],
      wts['b_gate'], wts['b_up'], wts['b_down'])


def _combine_kernel(h_ref, tw_ref, p_ref, y0_ref, y1_ref, y2_ref, y3_ref, g3_ref, wpg_ref, wpp_ref, gf_ref,
                    out_ref):
    tw = tw_ref[...]
    tm = tw.shape[0]
    y = _rows_from_tiles(y0_ref, tm) * tw[:, 0:1]
    for kk, y_ref in enumerate((y1_ref, y2_ref, y3_ref), start=1):
        y = y + _rows_from_tiles(y_ref, tm) * tw[:, kk:kk + 1]
    h2 = h_ref[...] + y
    gate = _sigmoid(_dot(_rms(h2, g3_ref[...]).astype(BF16), wpg_ref[...]))
    h3 = h2 + gate * _dot(p_ref[...].astype(BF16), wpp_ref[...])
    out_ref[...] = _rms(h3, gf_ref[...])


def _combine(hbuf, tw, p2, y, row0, wts, lay):
    n_rows = p2.shape[0]
    tm = min(COMBINE_TILE, n_rows)
    assert n_rows % tm == 0 and row0 % tm == 0 and lay.slab % tm == 0
    off = row0 // tm
    consts = [wts['g3'], wts['w_ple_gate'], wts['w_ple_proj'], wts['g_final']]

    def slab_spec(k):
        return pl.BlockSpec((tm * ROW_SUBLANES, LANES), lambda i: (k * (lay.slab // tm) + off + i, 0))

    return pl.pallas_call(
        _combine_kernel,
        grid=(n_rows // tm,),
        in_specs=[pl.BlockSpec((tm, D_MODEL), lambda i: (i + off, 0)),
                  pl.BlockSpec((tm, LANES), lambda i: (i + off, 0)),
                  pl.BlockSpec((tm, D_PLE), lambda i: (i, 0))]
        + [slab_spec(k) for k in range(TOP_K)] + [_const_spec(c.shape) for c in consts],
        out_specs=pl.BlockSpec((tm, D_MODEL), lambda i: (i, 0)),
        out_shape=jax.ShapeDtypeStruct((n_rows, D_MODEL), F32),
        compiler_params=_params(("arbitrary",)),
        name="combine",
    )(hbuf, tw, p2, y, y, y, y, *consts)


def _prepare_weights(norm1_g, w_in, w_gk2, b_gk, gla_norm_g, v_norm_g, v_norm_b, w_sp, b_sp, w_o, norm2_g,
                     w_router, b_router, w_gate, b_gate, w_up, b_up, w_down, b_down, norm3_g, w_ple_gate,
                     w_ple_proj, final_g):
    o_gk = 2 * GLA_DK_T + 2 * GLA_DV_T
    w_main = jnp.concatenate([w_in[:, :o_gk], w_in[:, o_gk + GK_RANK:]], axis=1).astype(BF16)
    w_gk1 = jnp.pad(w_in[:, o_gk:o_gk + GK_RANK], ((0, 0), (0, LANES - GK_RANK))).astype(BF16)
    w_gk2p = jnp.pad(w_gk2, ((0, LANES - GK_RANK), (0, 0))).astype(BF16)
    row = lambda z: z.reshape(1, -1).astype(F32)
    return dict(
        g1=row(norm1_g), w_main=w_main, w_gk1=w_gk1, w_gk2=w_gk2p, b_gk=row(b_gk), gla_g=row(gla_norm_g),
        vn_g=row(v_norm_g), vn_b=row(v_norm_b), w_sp=w_sp, b_sp_t=b_sp.T,
        mix_w0=row(jnp.repeat(w_sp[:, 0, 0], MIX_DH)), mix_b0=row(jnp.repeat(b_sp[:, 0], MIX_DH)),
        w_o=w_o.astype(BF16), g2=row(norm2_g),
        w_router=jnp.pad(w_router, ((0, 0), (0, LANES - N_EXPERTS))),
        b_router=jnp.pad(row(b_router), ((0, 0), (0, LANES - N_EXPERTS))),
        w_gate=w_gate.astype(BF16), w_up=w_up.astype(BF16), w_down=w_down.astype(BF16),
        b_gate=b_gate.reshape(N_EXPERTS, 1, D_FF), b_up=b_up.reshape(N_EXPERTS, 1, D_FF),
        b_down=b_down.reshape(N_EXPERTS, 1, D_MODEL),
        g3=row(norm3_g), w_ple_gate=w_ple_gate.astype(BF16), w_ple_proj=w_ple_proj.astype(BF16),
        g_final=row(final_g))


def _dispatch_plan(top_i, rank, counts, lay):
    bm = EXPERT_BLOCK
    padded = (counts + bm - 1) // bm * bm
    pad_end = jnp.cumsum(padded)
    pad_start = pad_end - padded
    raw_end = jnp.cumsum(counts)
    experts = jnp.arange(N_EXPERTS, dtype=I32)
    start_of = jnp.sum(jnp.where(top_i[:, :, None] == experts, pad_start, 0), axis=-1)
    dest_km = (start_of + rank).T.reshape(-1).astype(I32)
    n_valid = (pad_end[-1] // bm).astype(I32)
    blk = jnp.minimum(jnp.arange(lay.n_blocks + 1, dtype=I32), n_valid - 1)
    block_e = jnp.sum(pad_end[None, :] <= (blk * bm)[:, None], axis=1).astype(I32)
    rows = _plan(dest_km, (pad_start + counts).astype(I32), pad_end.astype(I32),
                 (lay.dump0 - raw_end).astype(I32), lay)
    toks = jnp.where(rows < lay.dump0, rows % lay.slab, 0)
    return rows, toks, block_e, n_valid.reshape(1)


def _layer(x_prompt, x_sample, state, p_prompt, p_sample, wts):
    n_p, len_p, _ = x_prompt.shape
    n_s, len_s, _ = x_sample.shape
    assert len_s == 1, "the sample group carries one new token per sequence"
    t_p = n_p * len_p
    t_all = t_p + n_s

    xs = x_sample.reshape(n_s, D_MODEL)
    q, k, a, v, og, ob, vn = _mixer_sample_in(xs, wts)
    st_sample, o = _state_update(state, q, k, a, v)
    h_sample = _mixer_sample_out(xs, o, og, ob, wts)
    hbuf, st_prompt = _mixer_prompt(x_prompt, h_sample, wts)

    xp, top_i, top_w, rank, counts = _router(hbuf, wts)
    lay = _Layout(t_all)
    rows, toks, block_e, n_valid = _dispatch_plan(top_i[:, :TOP_K], rank[:, :TOP_K], counts[0, :N_EXPERTS], lay)
    y = _experts(block_e, n_valid, rows, toks, xp, wts, lay)
    y_prompt = _combine(hbuf, top_w, p_prompt.reshape(t_p, D_PLE), y, 0, wts, lay)
    y_sample = _combine(hbuf, top_w, p_sample.reshape(n_s, D_PLE), y, t_p, wts, lay)
    return (y_prompt.reshape(n_p, len_p, D_MODEL), y_sample.reshape(n_s, len_s, D_MODEL),
            st_prompt, st_sample, vn.reshape(n_s, len_s, D_GMLP))


def kernel(x_prompt, x_sample, state_gla, p_prompt, p_sample, norm1_g, w_in, w_gk2, b_gk, gla_norm_g, v_norm_g,
           v_norm_b, w_sp, b_sp, w_o, norm2_g, w_router, b_router, w_gate, b_gate, w_up, b_up, w_down, b_down,
           norm3_g, w_ple_gate, w_ple_proj, final_g):
    assert w_in.shape[0] == 1, "single-layer trunk"
    wts = _prepare_weights(norm1_g[0], w_in[0], w_gk2[0], b_gk[0], gla_norm_g[0], v_norm_g[0], v_norm_b[0],
                           w_sp[0], b_sp[0], w_o[0], norm2_g[0], w_router[0], b_router[0], w_gate[0], b_gate[0],
                           w_up[0], b_up[0], w_down[0], b_down[0], norm3_g[0], w_ple_gate[0], w_ple_proj[0],
                           final_g)
    y_p, y_s, st_p, st_s, vn_s = _layer(x_prompt, x_sample, state_gla[0], p_prompt[0], p_sample[0], wts)
    return (y_p, y_s, st_p[None], st_s[None].astype(state_gla.dtype), vn_s[None])
```

```python
import functools

import jax
import jax.numpy as jnp
from jax import lax
from jax.experimental import pallas as pl
from jax.experimental.pallas import tpu as pltpu

F32 = jnp.float32
BF16 = jnp.bfloat16
I32 = jnp.int32
U32 = jnp.uint32

D_MODEL = 1024
GLA_HEADS = 4
GLA_DK = 128
GLA_DV = 256
GLA_DK_T = GLA_HEADS * GLA_DK
GLA_DV_T = GLA_HEADS * GLA_DV
GK_RANK = 16
GATE_NORMALIZER = 16.0
GLA_CHUNK = 64
GLA_CHUNK_LOG2 = 6
D_GMLP = 1024
MIX_HEADS = 4
MIX_DH = D_GMLP // MIX_HEADS
MIX_CHUNK = 128
MIX_CHUNK_LOG2 = 7
assert 1 << GLA_CHUNK_LOG2 == GLA_CHUNK and 1 << MIX_CHUNK_LOG2 == MIX_CHUNK
N_EXPERTS = 32
TOP_K = 4
D_FF = 1024
SWIGLU_LIMIT = 7.0
SWIGLU_ALPHA = 1.702
D_PLE = 256
EPS = 1e-6

LANES = 128
ROW_SUBLANES = D_MODEL // LANES
VMEM_LIMIT_BYTES = 56 * 1024 * 1024

OFF_Q, OFF_K, OFF_V, OFF_R, OFF_U, OFF_VG, OFF_GA, OFF_GB, N_MAIN = (
    0, 512, 1024, 2048, 3072, 4096, 5120, 6144, 7168)

MIXER_TILE = 256
STATE_TOKENS = 8
EXPERT_BLOCK = 256
EXPERT_COLS = 256
COMBINE_TILE = 256
SLAB_ALIGN = 512


def _dot(a, b):
    return jnp.dot(a, b, preferred_element_type=F32)


def _dot_nt(a, b):
    return lax.dot_general(a, b, (((1,), (1,)), ((), ())), preferred_element_type=F32)


def _dot_tn(a, b):
    return lax.dot_general(a, b, (((0,), (0,)), ((), ())), preferred_element_type=F32)


def _rms(x, g):
    return x * lax.rsqrt(jnp.mean(x * x, axis=-1, keepdims=True) + EPS) * g


def _sigmoid(x):
    return 1.0 / (1.0 + jnp.exp(-x))


def _gelu(x):
    return 0.5 * x * (1.0 + lax.erf(x * (2.0 ** -0.5)))


def _log_sigmoid(x):
    return jnp.minimum(x, 0.0) - jnp.log1p(jnp.exp(-jnp.abs(x)))


def _split3(x):
    hi = x.astype(BF16)
    r1 = x - hi.astype(F32)
    mid = r1.astype(BF16)
    lo = (r1 - mid.astype(F32)).astype(BF16)
    return hi, mid, lo


def _const_spec(shape):
    nd = len(shape)
    return pl.BlockSpec(shape, lambda *_: (0,) * nd, pipeline_mode=pl.Buffered(1))


def _params(sem):
    return pltpu.CompilerParams(dimension_semantics=sem, vmem_limit_bytes=VMEM_LIMIT_BYTES)


def _project_gla_inputs(n, wm_ref, wgk1_ref, wgk2_ref, bgk_ref):
    q = _dot(n, wm_ref[:, OFF_Q:OFF_K]) * (GLA_DK ** -0.5)
    k = _dot(n, wm_ref[:, OFF_K:OFF_V])
    v = _dot(n, wm_ref[:, OFF_V:OFF_R])
    gk = _dot(n, wgk1_ref[...]).astype(BF16)
    log_a = _log_sigmoid(_dot(gk, wgk2_ref[...]) + bgk_ref[...]) * (1.0 / GATE_NORMALIZER)
    return q, k, v, log_a


def _gmlp_inputs(n, wm_ref, vng_ref, vnb_ref):
    u = _gelu(_dot(n, wm_ref[:, OFF_U:OFF_VG]))
    vg = _gelu(_dot(n, wm_ref[:, OFF_VG:OFF_GA]))
    mu = jnp.mean(vg, axis=-1, keepdims=True)
    vc = vg - mu
    var = jnp.mean(vc * vc, axis=-1, keepdims=True)
    vn = vc * lax.rsqrt(var + EPS) * vng_ref[...] + vnb_ref[...]
    ug = u * _sigmoid(_dot(n, wm_ref[:, OFF_GB:N_MAIN]))
    return ug, vn


def _gla_out_gate(n, wm_ref):
    r = _dot(n, wm_ref[:, OFF_R:OFF_U])
    ga = _dot(n, wm_ref[:, OFF_GA:OFF_GB])
    return r * _sigmoid(r) * _sigmoid(ga)


def _head_rms(o, g):
    return o * lax.rsqrt(jnp.mean(o * o, axis=-1, keepdims=True) + EPS) * g


def _mixer_prompt_kernel(x_ref, hs_ref, *refs, tm, n_seq, n_tiles):
    b = pl.program_id(0)
    t = pl.program_id(1)
    h_ref = refs[11]

    @pl.when(b < n_seq)
    def _():
        _mixer_prompt_tile(x_ref, *refs, tm=tm, n_tiles=n_tiles)

    @pl.when((b == n_seq) & (t == 0))
    def _():
        h_ref[0:hs_ref.shape[0], :] = hs_ref[...]


def _mixer_prompt_tile(x_ref, g1_ref, wm_ref, wgk1_ref, wgk2_ref, bgk_ref, glag_ref, vng_ref, vnb_ref,
                       wsp_ref, bspt_ref, wo_ref,
                       h_ref, st_ref,
                       n_scr, oa_scr, s_scr, *, tm, n_tiles):
    t = pl.program_id(1)

    @pl.when(t == 0)
    def _():
        s_scr[...] = jnp.zeros_like(s_scr)

    x = x_ref[0]
    n = _rms(x, g1_ref[...]).astype(BF16)
    n_scr[...] = n
    q, k, v, log_a = _project_gla_inputs(n, wm_ref, wgk1_ref, wgk2_ref, bgk_ref)
    v = v.astype(BF16)

    row = lax.broadcasted_iota(I32, (tm, tm), 0)
    col = lax.broadcasted_iota(I32, (tm, tm), 1)
    same_chunk = lax.shift_right_logical(row, GLA_CHUNK_LOG2) == lax.shift_right_logical(col, GLA_CHUNK_LOG2)
    causal = same_chunk & (row >= col)
    tri = causal.astype(BF16)
    blk = same_chunk.astype(BF16)
    hi, mid, lo = _split3(log_a)
    b = _dot(tri, hi) + _dot(tri, mid) + _dot(tri, lo)
    b_end = _dot(blk, hi) + _dot(blk, mid) + _dot(blk, lo)
    qe = (q * jnp.exp(b)).astype(BF16)
    ke = (k * jnp.exp(-b)).astype(BF16)
    kd = (k * jnp.exp(b_end - b)).astype(BF16)
    dec = jnp.exp(b_end)
    glag = glag_ref[...]
    for h in range(GLA_HEADS):
        ks = slice(h * GLA_DK, (h + 1) * GLA_DK)
        vs = slice(h * GLA_DV, (h + 1) * GLA_DV)
        vh = v[:, vs]
        att = jnp.where(causal, _dot_nt(qe[:, ks], ke[:, ks]), 0.0).astype(BF16)
        o_intra = _dot(att, vh)
        s_t = s_scr[h]
        o_inter = []
        for c in range(tm // GLA_CHUNK):
            rs = slice(c * GLA_CHUNK, (c + 1) * GLA_CHUNK)
            o_inter.append(_dot_nt(qe[rs, ks], s_t.astype(BF16)))
            s_t = s_t * dec[c * GLA_CHUNK:c * GLA_CHUNK + 1, ks] + _dot_tn(vh[rs], kd[rs, ks])
        s_scr[h] = s_t
        oa_scr[:, vs] = _head_rms(o_intra + jnp.concatenate(o_inter, axis=0), glag)

    n = n_scr[...]
    oa_scr[...] = oa_scr[...] * _gla_out_gate(n, wm_ref)
    ug, vn = _gmlp_inputs(n, wm_ref, vng_ref, vnb_ref)
    vn = vn.astype(BF16)
    reps = tm // MIX_CHUNK
    mix_mask = (lax.shift_right_logical(row, MIX_CHUNK_LOG2) == lax.shift_right_logical(col, MIX_CHUNK_LOG2)) & (
        row >= col)
    for h in range(MIX_HEADS):
        w = jnp.where(mix_mask, jnp.tile(wsp_ref[h], (reps, reps)), 0.0).astype(BF16)
        cs = slice(h * MIX_DH, (h + 1) * MIX_DH)
        mix = _dot(w, vn[:, cs]) + jnp.tile(bspt_ref[:, h:h + 1], (reps, 1))
        oa_scr[:, cs] = oa_scr[:, cs] + ug[:, cs] * mix

    h_ref[...] = x + _dot(oa_scr[...].astype(BF16), wo_ref[...])

    @pl.when(t == n_tiles - 1)
    def _():
        for h in range(GLA_HEADS):
            st_ref[0, h] = s_scr[h].T


def _mixer_prompt(x, h_sample, wts):
    n_seq, seq_len, _ = x.shape
    n_s = h_sample.shape[0]
    tm = min(MIXER_TILE, seq_len)
    assert seq_len % tm == 0 and tm % MIX_CHUNK == 0 and n_s <= tm
    n_tiles = seq_len // tm
    consts = [wts['g1'], wts['w_main'], wts['w_gk1'], wts['w_gk2'], wts['b_gk'], wts['gla_g'], wts['vn_g'],
              wts['vn_b'], wts['w_sp'], wts['b_sp_t'], wts['w_o']]
    kern = functools.partial(_mixer_prompt_kernel, tm=tm, n_seq=n_seq, n_tiles=n_tiles)
    last = n_seq - 1

    def prompt_tile(b, t):
        return jnp.minimum(b, last), jnp.where(b < n_seq, t, n_tiles - 1)

    def x_map(b, t):
        bb, tt = prompt_tile(b, t)
        return bb, tt, 0

    def h_map(b, t):
        return jnp.where(b < n_seq, b * n_tiles + t, n_seq * n_tiles), 0

    return pl.pallas_call(
        kern,
        grid=(n_seq + 1, n_tiles),
        in_specs=[pl.BlockSpec((1, tm, D_MODEL), x_map), _const_spec(h_sample.shape)]
        + [_const_spec(c.shape) for c in consts],
        out_specs=[pl.BlockSpec((tm, D_MODEL), h_map),
                   pl.BlockSpec((1, GLA_HEADS, GLA_DK, GLA_DV), lambda b, t: (jnp.minimum(b, last), 0, 0, 0))],
        out_shape=[jax.ShapeDtypeStruct((n_seq * seq_len + n_s, D_MODEL), F32),
                   jax.ShapeDtypeStruct((n_seq, GLA_HEADS, GLA_DK, GLA_DV), F32)],
        scratch_shapes=[pltpu.VMEM((tm, D_MODEL), BF16),
                        pltpu.VMEM((tm, D_MODEL), F32),
                        pltpu.VMEM((GLA_HEADS, GLA_DV, GLA_DK), F32)],
        compiler_params=_params(("arbitrary", "arbitrary")),
        name="mixer_prompt",
    )(x, h_sample, *consts)


def _mixer_sample_in_kernel(x_ref, g1_ref, wm_ref, wgk1_ref, wgk2_ref, bgk_ref, vng_ref, vnb_ref,
                            mixw_ref, mixb_ref,
                            q_ref, k_ref, a_ref, v_ref, og_ref, ob_ref, vn_ref):
    n = _rms(x_ref[...], g1_ref[...]).astype(BF16)
    q, k, v, log_a = _project_gla_inputs(n, wm_ref, wgk1_ref, wgk2_ref, bgk_ref)
    q_ref[...] = q
    k_ref[...] = k
    a_ref[...] = jnp.exp(log_a)
    v_ref[...] = v
    og_ref[...] = _gla_out_gate(n, wm_ref)
    ug, vn = _gmlp_inputs(n, wm_ref, vng_ref, vnb_ref)
    vn_ref[...] = vn
    ob_ref[...] = ug * (mixw_ref[...] * vn + mixb_ref[...])


def _mixer_sample_in(x2, wts):
    n_seq = x2.shape[0]
    consts = [wts['g1'], wts['w_main'], wts['w_gk1'], wts['w_gk2'], wts['b_gk'], wts['vn_g'], wts['vn_b'],
              wts['mix_w0'], wts['mix_b0']]
    widths = [GLA_DK_T, GLA_DK_T, GLA_DK_T, GLA_DV_T, D_MODEL, D_MODEL, D_GMLP]
    return pl.pallas_call(
        _mixer_sample_in_kernel,
        grid=(1,),
        in_specs=[_const_spec(x2.shape)] + [_const_spec(c.shape) for c in consts],
        out_specs=[pl.BlockSpec((n_seq, w), lambda i: (0, 0)) for w in widths],
        out_shape=[jax.ShapeDtypeStruct((n_seq, w), F32) for w in widths],
        compiler_params=_params(("arbitrary",)),
        name="mixer_sample_in",
    )(x2, *consts)


def _state_update_kernel(s_ref, q_ref, k_ref, a_ref, v_ref, so_ref, o_ref):
    for j in range(STATE_TOKENS):
        for h in range(GLA_HEADS):
            ks = slice(h * GLA_DK, (h + 1) * GLA_DK)
            vs = slice(h * GLA_DV, (h + 1) * GLA_DV)
            s_new = a_ref[0, ks, j:j + 1] * s_ref[j, h] + k_ref[0, ks, j:j + 1] * v_ref[j:j + 1, vs]
            so_ref[j, h] = s_new
            o_ref[j:j + 1, vs] = jnp.sum(q_ref[0, ks, j:j + 1] * s_new, axis=0, keepdims=True)


def _state_update(state, q, k, a, v):
    n_seq = state.shape[0]
    tb = STATE_TOKENS
    assert n_seq % tb == 0

    def cols(z):
        return z.reshape(n_seq // tb, tb, GLA_DK_T).transpose(0, 2, 1)

    col_spec = pl.BlockSpec((1, GLA_DK_T, tb), lambda i: (i, 0, 0))
    st_spec = pl.BlockSpec((tb, GLA_HEADS, GLA_DK, GLA_DV), lambda i: (i, 0, 0, 0))
    row_spec = pl.BlockSpec((tb, GLA_DV_T), lambda i: (i, 0))
    return pl.pallas_call(
        _state_update_kernel,
        grid=(n_seq // tb,),
        in_specs=[st_spec, col_spec, col_spec, col_spec, row_spec],
        out_specs=[st_spec, row_spec],
        out_shape=[jax.ShapeDtypeStruct(state.shape, F32), jax.ShapeDtypeStruct((n_seq, GLA_DV_T), F32)],
        compiler_params=_params(("arbitrary",)),
        name="state_update",
    )(state, cols(q), cols(k), cols(a), v)


def _mixer_sample_out_kernel(x_ref, o_ref, og_ref, ob_ref, glag_ref, wo_ref, h_ref):
    glag = glag_ref[...]
    parts = []
    for h in range(GLA_HEADS):
        vs = slice(h * GLA_DV, (h + 1) * GLA_DV)
        parts.append(_head_rms(o_ref[:, vs], glag))
    merged = jnp.concatenate(parts, axis=1) * og_ref[...] + ob_ref[...]
    h_ref[...] = x_ref[...] + _dot(merged.astype(BF16), wo_ref[...])


def _mixer_sample_out(x2, o, og, ob, wts):
    n_seq = x2.shape[0]
    consts = [x2, o, og, ob, wts['gla_g'], wts['w_o']]
    return pl.pallas_call(
        _mixer_sample_out_kernel,
        grid=(1,),
        in_specs=[_const_spec(c.shape) for c in consts],
        out_specs=pl.BlockSpec((n_seq, D_MODEL), lambda i: (0, 0)),
        out_shape=jax.ShapeDtypeStruct((n_seq, D_MODEL), F32),
        compiler_params=_params(("arbitrary",)),
        name="mixer_sample_out",
    )(*consts)


def _router_kernel(h_ref, g2_ref, wr_ref, br_ref, xp_ref, ti_ref, tw_ref, rk_ref, cnt_ref, cnt_scr, *, tm):
    i = pl.program_id(0)

    @pl.when(i == 0)
    def _():
        cnt_scr[...] = jnp.zeros_like(cnt_scr)

    hn = _rms(h_ref[...], g2_ref[...])
    hn_hi = hn.astype(BF16)
    hn_lo = (hn - hn_hi.astype(F32)).astype(BF16)
    w = wr_ref[...]
    w_hi = w.astype(BF16)
    w_lo = (w - w_hi.astype(F32)).astype(BF16)
    logits = _dot(hn_hi, w_hi) + _dot(hn_lo, w_hi) + _dot(hn_hi, w_lo) + br_ref[...]

    _rows_to_tiles(xp_ref, hn)

    lane = lax.broadcasted_iota(I32, (tm, LANES), 1)
    neg = jnp.float32(-jnp.inf)
    l = jnp.where(lane < N_EXPERTS, logits, neg)
    vals, idxs, hots = [], [], []
    for _ in range(TOP_K):
        m = jnp.max(l, axis=-1, keepdims=True)
        idx = jnp.min(jnp.where(l == m, lane, LANES), axis=-1, keepdims=True)
        hot = lane == idx
        l = jnp.where(hot, neg, l)
        vals.append(m)
        idxs.append(idx)
        hots.append(hot)
    exps = [jnp.exp(v - vals[0]) for v in vals]
    denom = exps[0] + exps[1] + exps[2] + exps[3]

    member = (hots[0] | hots[1] | hots[2] | hots[3]).astype(BF16)
    row = lax.broadcasted_iota(I32, (tm, tm), 0)
    col = lax.broadcasted_iota(I32, (tm, tm), 1)
    earlier = (row > col).astype(BF16)
    before = cnt_scr[...] + _dot(earlier, member)
    cnt_scr[...] = cnt_scr[...] + jnp.sum(member.astype(F32), axis=0, keepdims=True)
    cnt_ref[...] = cnt_scr[...].astype(I32)

    ti = jnp.zeros((tm, LANES), I32)
    tw = jnp.zeros((tm, LANES), F32)
    rk = jnp.zeros((tm, LANES), I32)
    for kk in range(TOP_K):
        sel = lane == kk
        rank = jnp.sum(jnp.where(hots[kk], before, 0.0), axis=-1, keepdims=True).astype(I32)
        ti = jnp.where(sel, idxs[kk], ti)
        tw = jnp.where(sel, exps[kk] / denom, tw)
        rk = jnp.where(sel, rank, rk)
    ti_ref[...] = ti
    tw_ref[...] = tw
    rk_ref[...] = rk


def _router_tile(t_all):
    for tm in (512, 384, 256, 128):
        if t_all % tm == 0:
            return tm
    raise ValueError(f"token count {t_all} must be a multiple of 128")


def _router(hbuf, wts):
    t_all = hbuf.shape[0]
    tm = _router_tile(t_all)
    consts = [wts['g2'], wts['w_router'], wts['b_router']]
    row = lambda w: pl.BlockSpec((tm, w), lambda i: (i, 0))
    return pl.pallas_call(
        functools.partial(_router_kernel, tm=tm),
        grid=(t_all // tm,),
        in_specs=[row(D_MODEL)] + [_const_spec(c.shape) for c in consts],
        out_specs=[pl.BlockSpec((tm * ROW_SUBLANES, LANES), lambda i: (i, 0)), row(LANES), row(LANES), row(LANES),
                   pl.BlockSpec((1, LANES), lambda i: (0, 0))],
        out_shape=[jax.ShapeDtypeStruct((t_all * ROW_SUBLANES, LANES), F32),
                   jax.ShapeDtypeStruct((t_all, LANES), I32),
                   jax.ShapeDtypeStruct((t_all, LANES), F32),
                   jax.ShapeDtypeStruct((t_all, LANES), I32),
                   jax.ShapeDtypeStruct((1, LANES), I32)],
        scratch_shapes=[pltpu.VMEM((1, LANES), F32)],
        compiler_params=_params(("arbitrary",)),
        name="router",
    )(hbuf, *consts)


def _row_tile(r):
    start = r * ROW_SUBLANES
    if not isinstance(r, int):
        start = pl.multiple_of(start, ROW_SUBLANES)
    return pl.ds(start, ROW_SUBLANES)


def _tile_of(ref, row):
    return ref.at[_row_tile(row)]


def _rows_from_tiles(ref, n_rows):
    return jnp.concatenate([ref[pl.ds(s, n_rows, stride=ROW_SUBLANES), :] for s in range(ROW_SUBLANES)], axis=1)


def _rows_to_tiles(ref, x):
    for s in range(ROW_SUBLANES):
        ref[pl.ds(s, x.shape[0], stride=ROW_SUBLANES), :] = x[:, s * LANES:(s + 1) * LANES]


class _Layout:
    def __init__(self, t_all):
        bm = EXPERT_BLOCK
        self.t_all = t_all
        self.n_blocks = -(-(t_all * TOP_K + N_EXPERTS * (bm - 1)) // bm)
        self.slab = -(-t_all // SLAB_ALIGN) * SLAB_ALIGN
        self.dump0 = TOP_K * self.slab
        self.prime0 = self.dump0 + N_EXPERTS * bm
        self.y_rows = self.prime0 + bm
        self.sorted_rows = self.n_blocks * bm
        self.table_rows = (self.n_blocks + 1) * bm


def _dispatch_kernel(dest_ref, lo_ref, hi_ref, base_ref, xp_ref, xs_hbm, rows_ref, zero_scr, sem, *, lay, tm):
    bm = EXPERT_BLOCK
    i = pl.program_id(0)

    def fill(r, dump_row):
        rows_ref[bm + r] = dump_row
        return pltpu.make_async_copy(zero_scr, _tile_of(xs_hbm, r), sem.at[1])

    def fill_range(lo, hi, dump_of):
        def start(r, c):
            fill(r, dump_of(r)).start()
            return c
        lax.fori_loop(lo, hi, start, 0)

        def wait(r, c):
            pltpu.make_async_copy(zero_scr, _tile_of(xs_hbm, r), sem.at[1]).wait()
            return c
        lax.fori_loop(lo, hi, wait, 0)

    @pl.when(i == 0)
    def _():
        zero_scr[...] = jnp.zeros_like(zero_scr)

        def prime(r, c):
            rows_ref[r] = lay.prime0 + r
            return c
        lax.fori_loop(0, bm, prime, 0)

        def expert(e, c):
            base = base_ref[e]
            fill_range(lo_ref[e], hi_ref[e], lambda r: base + r)
            return c
        lax.fori_loop(0, N_EXPERTS, expert, 0)
        fill_range(hi_ref[N_EXPERTS - 1], lay.sorted_rows, lambda r: lay.dump0)

    unroll = 8
    t0 = i * tm

    def rows_of(t8, c):
        for u in range(unroll):
            t = t8 * unroll + u
            for k in range(TOP_K):
                d = dest_ref[k * lay.t_all + t0 + t]
                pltpu.make_async_copy(xp_ref.at[_row_tile(t)], _tile_of(xs_hbm, d), sem.at[0]).start()
                rows_ref[bm + d] = k * lay.slab + t0 + t
        return c
    lax.fori_loop(0, tm // unroll, rows_of, 0)
    for _ in range(TOP_K):
        pltpu.make_async_copy(xp_ref, xs_hbm.at[pl.ds(0, tm * ROW_SUBLANES)], sem.at[0]).wait()


def _dispatch(xp, dest_km, pad_lo, pad_hi, base, lay):
    tm = _router_tile(lay.t_all)
    grid_spec = pltpu.PrefetchScalarGridSpec(
        num_scalar_prefetch=4,
        grid=(lay.t_all // tm,),
        in_specs=[pl.BlockSpec((tm * ROW_SUBLANES, LANES), lambda i, *_: (i, 0))],
        out_specs=[pl.BlockSpec(memory_space=pl.ANY), pl.BlockSpec(memory_space=pltpu.SMEM)],
        scratch_shapes=[pltpu.VMEM((ROW_SUBLANES, LANES), F32), pltpu.SemaphoreType.DMA((2,))],
    )
    return pl.pallas_call(
        functools.partial(_dispatch_kernel, lay=lay, tm=tm),
        grid_spec=grid_spec,
        out_shape=[jax.ShapeDtypeStruct((lay.sorted_rows * ROW_SUBLANES, LANES), F32),
                   jax.ShapeDtypeStruct((lay.table_rows,), I32)],
        compiler_params=_params(("arbitrary",)),
        name="dispatch",
    )(dest_km, pad_lo, pad_hi, base, xp)


def _experts_kernel(be_ref, nv_ref, rows_ref, xs_ref, wg_ref, wu_ref, wd_ref, bg_ref, bu_ref, bd_ref,
                    y_hbm, obuf, sem_s, *, lay):
    del be_ref
    bm = EXPERT_BLOCK
    i = pl.program_id(0)
    n_valid = nv_ref[0]
    slot = i % 2
    other = 1 - slot

    def scatter_row(blk, s, r):
        return pltpu.make_async_copy(obuf.at[s, _row_tile(r)], _tile_of(y_hbm, rows_ref[(blk + 1) * bm + r]),
                                     sem_s.at[s])

    def wait_scatter(s):
        pltpu.make_async_copy(obuf.at[s], y_hbm.at[pl.ds(0, bm * ROW_SUBLANES)], sem_s.at[s]).wait()

    @pl.when(i == 0)
    def _():
        obuf[...] = jnp.zeros_like(obuf)
        fills = [(k * lay.slab + lay.t_all, lay.slab - lay.t_all) for k in range(TOP_K)]
        fills.append((lay.dump0, lay.y_rows - lay.dump0))
        copies = []
        for start, n in fills:
            for o in range(0, n, bm):
                m = min(bm, n - o) * ROW_SUBLANES
                copies.append(pltpu.make_async_copy(obuf.at[0, pl.ds(0, m)],
                                                    y_hbm.at[pl.ds((start + o) * ROW_SUBLANES, m)], sem_s.at[0]))
        for cp in copies:
            cp.start()
        for cp in copies:
            cp.wait()

    @pl.when((i >= 1) & (i <= n_valid))
    def _():
        wait_scatter(slot)

    @pl.when(i < n_valid)
    def _():
        x = _rows_from_tiles(xs_ref, bm).astype(BF16)
        n_chunks = D_FF // EXPERT_COLS
        per_group = bm // (2 * n_chunks)

        def issue(group):
            for r in range(group * per_group, (group + 1) * per_group):
                scatter_row(i - 1, other, r).start(priority=r % 2)

        mids = []
        for j in range(n_chunks):
            cs = slice(j * EXPERT_COLS, (j + 1) * EXPERT_COLS)
            issue(2 * j)
            g = jnp.minimum(_dot(x, wg_ref[0, :, cs]) + bg_ref[0, :, cs], SWIGLU_LIMIT)
            issue(2 * j + 1)
            u = jnp.clip(_dot(x, wu_ref[0, :, cs]) + bu_ref[0, :, cs], -SWIGLU_LIMIT, SWIGLU_LIMIT)
            mids.append(((u + 1.0) * (g * _sigmoid(SWIGLU_ALPHA * g))).astype(BF16))
        mid = jnp.concatenate(mids, axis=1)
        _rows_to_tiles(obuf.at[slot], _dot(mid, wd_ref[0]) + bd_ref[0])

    @pl.when(i == n_valid)
    def _():
        def last(r, c):
            scatter_row(i - 1, other, r).start()
            return c
        lax.fori_loop(0, bm, last, 0)
        wait_scatter(other)


def _experts(block_e, n_valid, rows, xs, wts, lay):
    bm = EXPERT_BLOCK
    tile_rows = bm * ROW_SUBLANES
    wspec = pl.BlockSpec((1, D_MODEL, D_FF), lambda i, be, *_: (be[i], 0, 0))
    bspec = pl.BlockSpec((1, 1, D_FF), lambda i, be, *_: (be[i], 0, 0))
    xspec = pl.BlockSpec((tile_rows, LANES), lambda i, be, nv, *_: (jnp.minimum(i, nv[0] - 1), 0))
    grid_spec = pltpu.PrefetchScalarGridSpec(
        num_scalar_prefetch=3,
        grid=(lay.n_blocks + 1,),
        in_specs=[xspec, wspec, wspec, wspec, bspec, bspec, bspec],
        out_specs=pl.BlockSpec(memory_space=pl.ANY),
        scratch_shapes=[pltpu.VMEM((2, tile_rows, LANES), F32), pltpu.SemaphoreType.DMA((2,))],
    )
    return pl.pallas_call(
        functools.partial(_experts_kernel, lay=lay),
        grid_spec=grid_spec,
        out_shape=jax.ShapeDtypeStruct((lay.y_rows * ROW_SUBLANES, LANES), F32),
        compiler_params=_params(("arbitrary",)),
        name="experts",
    )(block_e, n_valid, rows, xs, wts['w_gate'], wts['w_up'], wts['w_down'],
      wts['b_gate'], wts['b_up'], wts['b_down'])


def _combine_kernel(h_ref, tw_ref, p_ref, y0_ref, y1_ref, y2_ref, y3_ref, g3_ref, wpg_ref, wpp_ref, gf_ref,
                    out_ref):
    tw = tw_ref[...]
    tm = tw.shape[0]
    y = _rows_from_tiles(y0_ref, tm) * tw[:, 0:1]
    for kk, y_ref in enumerate((y1_ref, y2_ref, y3_ref), start=1):
        y = y + _rows_from_tiles(y_ref, tm) * tw[:, kk:kk + 1]
    h2 = h_ref[...] + y
    gate = _sigmoid(_dot(_rms(h2, g3_ref[...]).astype(BF16), wpg_ref[...]))
    h3 = h2 + gate * _dot(p_ref[...].astype(BF16), wpp_ref[...])
    out_ref[...] = _rms(h3, gf_ref[...])


def _combine(hbuf, tw, p2, y, row0, wts, lay):
    n_rows = p2.shape[0]
    tm = min(COMBINE_TILE, n_rows)
    assert n_rows % tm == 0 and row0 % tm == 0 and lay.slab % tm == 0
    off = row0 // tm
    consts = [wts['g3'], wts['w_ple_gate'], wts['w_ple_proj'], wts['g_final']]

    def slab_spec(k):
        return pl.BlockSpec((tm * ROW_SUBLANES, LANES), lambda i: (k * (lay.slab // tm) + off + i, 0))

    return pl.pallas_call(
        _combine_kernel,
        grid=(n_rows // tm,),
        in_specs=[pl.BlockSpec((tm, D_MODEL), lambda i: (i + off, 0)),
                  pl.BlockSpec((tm, LANES), lambda i: (i + off, 0)),
                  pl.BlockSpec((tm, D_PLE), lambda i: (i, 0))]
        + [slab_spec(k) for k in range(TOP_K)] + [_const_spec(c.shape) for c in consts],
        out_specs=pl.BlockSpec((tm, D_MODEL), lambda i: (i, 0)),
        out_shape=jax.ShapeDtypeStruct((n_rows, D_MODEL), F32),
        compiler_params=_params(("arbitrary",)),
        name="combine",
    )(hbuf, tw, p2, y, y, y, y, *consts)


def _prepare_weights(norm1_g, w_in, w_gk2, b_gk, gla_norm_g, v_norm_g, v_norm_b, w_sp, b_sp, w_o, norm2_g,
                     w_router, b_router, w_gate, b_gate, w_up, b_up, w_down, b_down, norm3_g, w_ple_gate,
                     w_ple_proj, final_g):
    o_gk = 2 * GLA_DK_T + 2 * GLA_DV_T
    w_main = jnp.concatenate([w_in[:, :o_gk], w_in[:, o_gk + GK_RANK:]], axis=1).astype(BF16)
    w_gk1 = jnp.pad(w_in[:, o_gk:o_gk + GK_RANK], ((0, 0), (0, LANES - GK_RANK))).astype(BF16)
    w_gk2p = jnp.pad(w_gk2, ((0, LANES - GK_RANK), (0, 0))).astype(BF16)
    row = lambda z: z.reshape(1, -1).astype(F32)
    return dict(
        g1=row(norm1_g), w_main=w_main, w_gk1=w_gk1, w_gk2=w_gk2p, b_gk=row(b_gk), gla_g=row(gla_norm_g),
        vn_g=row(v_norm_g), vn_b=row(v_norm_b), w_sp=w_sp, b_sp_t=b_sp.T,
        mix_w0=row(jnp.repeat(w_sp[:, 0, 0], MIX_DH)), mix_b0=row(jnp.repeat(b_sp[:, 0], MIX_DH)),
        w_o=w_o.astype(BF16), g2=row(norm2_g),
        w_router=jnp.pad(w_router, ((0, 0), (0, LANES - N_EXPERTS))),
        b_router=jnp.pad(row(b_router), ((0, 0), (0, LANES - N_EXPERTS))),
        w_gate=w_gate.astype(BF16), w_up=w_up.astype(BF16), w_down=w_down.astype(BF16),
        b_gate=b_gate.reshape(N_EXPERTS, 1, D_FF), b_up=b_up.reshape(N_EXPERTS, 1, D_FF),
        b_down=b_down.reshape(N_EXPERTS, 1, D_MODEL),
        g3=row(norm3_g), w_ple_gate=w_ple_gate.astype(BF16), w_ple_proj=w_ple_proj.astype(BF16),
        g_final=row(final_g))


def _dispatch_plan(xp, top_i, rank, counts, lay):
    bm = EXPERT_BLOCK
    padded = (counts + bm - 1) // bm * bm
    pad_end = jnp.cumsum(padded)
    pad_start = pad_end - padded
    raw_end = jnp.cumsum(counts)
    experts = jnp.arange(N_EXPERTS, dtype=I32)
    start_of = jnp.sum(jnp.where(top_i[:, :, None] == experts, pad_start, 0), axis=-1)
    dest_km = (start_of + rank).T.reshape(-1).astype(I32)
    n_valid = (pad_end[-1] // bm).astype(I32)
    blk = jnp.minimum(jnp.arange(lay.n_blocks + 1, dtype=I32), n_valid - 1)
    block_e = jnp.sum(pad_end[None, :] <= (blk * bm)[:, None], axis=1).astype(I32)
    xs, rows = _dispatch(xp, dest_km, (pad_start + counts).astype(I32), pad_end.astype(I32),
                         (lay.dump0 - raw_end).astype(I32), lay)
    return xs, rows, block_e, n_valid.reshape(1)


def _layer(x_prompt, x_sample, state, p_prompt, p_sample, wts):
    n_p, len_p, _ = x_prompt.shape
    n_s, len_s, _ = x_sample.shape
    assert len_s == 1, "the sample group carries one new token per sequence"
    t_p = n_p * len_p
    t_all = t_p + n_s

    xs = x_sample.reshape(n_s, D_MODEL)
    q, k, a, v, og, ob, vn = _mixer_sample_in(xs, wts)
    st_sample, o = _state_update(state, q, k, a, v)
    h_sample = _mixer_sample_out(xs, o, og, ob, wts)
    hbuf, st_prompt = _mixer_prompt(x_prompt, h_sample, wts)

    xp, top_i, top_w, rank, counts = _router(hbuf, wts)
    lay = _Layout(t_all)
    xs, rows, block_e, n_valid = _dispatch_plan(xp, top_i[:, :TOP_K], rank[:, :TOP_K], counts[0, :N_EXPERTS], lay)
    y = _experts(block_e, n_valid, rows, xs, wts, lay)
    y_prompt = _combine(hbuf, top_w, p_prompt.reshape(t_p, D_PLE), y, 0, wts, lay)
    y_sample = _combine(hbuf, top_w, p_sample.reshape(n_s, D_PLE), y, t_p, wts, lay)
    return (y_prompt.reshape(n_p, len_p, D_MODEL), y_sample.reshape(n_s, len_s, D_MODEL),
            st_prompt, st_sample, vn.reshape(n_s, len_s, D_GMLP))


def kernel(x_prompt, x_sample, state_gla, p_prompt, p_sample, norm1_g, w_in, w_gk2, b_gk, gla_norm_g, v_norm_g,
           v_norm_b, w_sp, b_sp, w_o, norm2_g, w_router, b_router, w_gate, b_gate, w_up, b_up, w_down, b_down,
           norm3_g, w_ple_gate, w_ple_proj, final_g):
    assert w_in.shape[0] == 1, "single-layer trunk"
    wts = _prepare_weights(norm1_g[0], w_in[0], w_gk2[0], b_gk[0], gla_norm_g[0], v_norm_g[0], v_norm_b[0],
                           w_sp[0], b_sp[0], w_o[0], norm2_g[0], w_router[0], b_router[0], w_gate[0], b_gate[0],
                           w_up[0], b_up[0], w_down[0], b_down[0], norm3_g[0], w_ple_gate[0], w_ple_proj[0],
                           final_g)
    y_p, y_s, st_p, st_s, vn_s = _layer(x_prompt, x_sample, state_gla[0], p_prompt[0], p_sample[0], wts)
    return (y_p, y_s, st_p[None], st_s[None].astype(state_gla.dtype), vn_s[None])
```

```python
import functools

import jax
import jax.numpy as jnp
from jax import lax
from jax.experimental import pallas as pl
from jax.experimental.pallas import tpu as pltpu

F32 = jnp.float32
BF16 = jnp.bfloat16
I32 = jnp.int32
U32 = jnp.uint32

D_MODEL = 1024
GLA_HEADS = 4
GLA_DK = 128
GLA_DV = 256
GLA_DK_T = GLA_HEADS * GLA_DK
GLA_DV_T = GLA_HEADS * GLA_DV
GK_RANK = 16
GATE_NORMALIZER = 16.0
GLA_CHUNK = 64
GLA_CHUNK_LOG2 = 6
D_GMLP = 1024
MIX_HEADS = 4
MIX_DH = D_GMLP // MIX_HEADS
MIX_CHUNK = 128
MIX_CHUNK_LOG2 = 7
assert 1 << GLA_CHUNK_LOG2 == GLA_CHUNK and 1 << MIX_CHUNK_LOG2 == MIX_CHUNK
N_EXPERTS = 32
TOP_K = 4
D_FF = 1024
SWIGLU_LIMIT = 7.0
SWIGLU_ALPHA = 1.702
D_PLE = 256
EPS = 1e-6

LANES = 128
ROW_SUBLANES = D_MODEL // LANES
VMEM_LIMIT_BYTES = 56 * 1024 * 1024

OFF_Q, OFF_K, OFF_V, OFF_R, OFF_U, OFF_VG, OFF_GA, OFF_GB, N_MAIN = (
    0, 512, 1024, 2048, 3072, 4096, 5120, 6144, 7168)

MIXER_TILE = 256
STATE_TOKENS = 8
EXPERT_BLOCK = 512
EXPERT_PARTS = 2
WEIGHT_CAST_ROWS = 256
COMBINE_TILE = 256
SLAB_ALIGN = 512


def _dot(a, b):
    return jnp.dot(a, b, preferred_element_type=F32)


def _dot_nt(a, b):
    return lax.dot_general(a, b, (((1,), (1,)), ((), ())), preferred_element_type=F32)


def _dot_tn(a, b):
    return lax.dot_general(a, b, (((0,), (0,)), ((), ())), preferred_element_type=F32)


def _rms(x, g):
    return x * lax.rsqrt(jnp.mean(x * x, axis=-1, keepdims=True) + EPS) * g


def _sigmoid(x):
    return 1.0 / (1.0 + jnp.exp(-x))


def _gelu(x):
    return 0.5 * x * (1.0 + lax.erf(x * (2.0 ** -0.5)))


def _log_sigmoid(x):
    return jnp.minimum(x, 0.0) - jnp.log1p(jnp.exp(-jnp.abs(x)))


def _split3(x):
    hi = x.astype(BF16)
    r1 = x - hi.astype(F32)
    mid = r1.astype(BF16)
    lo = (r1 - mid.astype(F32)).astype(BF16)
    return hi, mid, lo


def _const_spec(shape):
    nd = len(shape)
    return pl.BlockSpec(shape, lambda *_: (0,) * nd, pipeline_mode=pl.Buffered(1))


def _params(sem):
    return pltpu.CompilerParams(dimension_semantics=sem, vmem_limit_bytes=VMEM_LIMIT_BYTES)


def _project_gla_inputs(n, wm_ref, wgk1_ref, wgk2_ref, bgk_ref):
    q = _dot(n, wm_ref[:, OFF_Q:OFF_K]) * (GLA_DK ** -0.5)
    k = _dot(n, wm_ref[:, OFF_K:OFF_V])
    v = _dot(n, wm_ref[:, OFF_V:OFF_R])
    gk = _dot(n, wgk1_ref[...]).astype(BF16)
    log_a = _log_sigmoid(_dot(gk, wgk2_ref[...]) + bgk_ref[...]) * (1.0 / GATE_NORMALIZER)
    return q, k, v, log_a


def _gmlp_inputs(n, wm_ref, vng_ref, vnb_ref):
    u = _gelu(_dot(n, wm_ref[:, OFF_U:OFF_VG]))
    vg = _gelu(_dot(n, wm_ref[:, OFF_VG:OFF_GA]))
    mu = jnp.mean(vg, axis=-1, keepdims=True)
    vc = vg - mu
    var = jnp.mean(vc * vc, axis=-1, keepdims=True)
    vn = vc * lax.rsqrt(var + EPS) * vng_ref[...] + vnb_ref[...]
    ug = u * _sigmoid(_dot(n, wm_ref[:, OFF_GB:N_MAIN]))
    return ug, vn


def _gla_out_gate(n, wm_ref):
    r = _dot(n, wm_ref[:, OFF_R:OFF_U])
    ga = _dot(n, wm_ref[:, OFF_GA:OFF_GB])
    return r * _sigmoid(r) * _sigmoid(ga)


def _head_rms(o, g):
    return o * lax.rsqrt(jnp.mean(o * o, axis=-1, keepdims=True) + EPS) * g


def _mixer_prompt_kernel(x_ref, hs_ref, *refs, tm, n_seq, n_tiles):
    b = pl.program_id(0)
    t = pl.program_id(1)
    h_ref = refs[11]

    @pl.when(b < n_seq)
    def _():
        _mixer_prompt_tile(x_ref, *refs, tm=tm, n_tiles=n_tiles)

    @pl.when((b == n_seq) & (t == 0))
    def _():
        h_ref[0:hs_ref.shape[0], :] = hs_ref[...]


def _mixer_prompt_tile(x_ref, g1_ref, wm_ref, wgk1_ref, wgk2_ref, bgk_ref, glag_ref, vng_ref, vnb_ref,
                       wsp_ref, bspt_ref, wo_ref,
                       h_ref, st_ref,
                       n_scr, oa_scr, s_scr, *, tm, n_tiles):
    t = pl.program_id(1)

    @pl.when(t == 0)
    def _():
        s_scr[...] = jnp.zeros_like(s_scr)

    x = x_ref[0]
    n = _rms(x, g1_ref[...]).astype(BF16)
    n_scr[...] = n
    q, k, v, log_a = _project_gla_inputs(n, wm_ref, wgk1_ref, wgk2_ref, bgk_ref)
    v = v.astype(BF16)

    row = lax.broadcasted_iota(I32, (tm, tm), 0)
    col = lax.broadcasted_iota(I32, (tm, tm), 1)
    same_chunk = lax.shift_right_logical(row, GLA_CHUNK_LOG2) == lax.shift_right_logical(col, GLA_CHUNK_LOG2)
    causal = same_chunk & (row >= col)
    tri = causal.astype(BF16)
    blk = same_chunk.astype(BF16)
    hi, mid, lo = _split3(log_a)
    b = _dot(tri, hi) + _dot(tri, mid) + _dot(tri, lo)
    b_end = _dot(blk, hi) + _dot(blk, mid) + _dot(blk, lo)
    qe = (q * jnp.exp(b)).astype(BF16)
    ke = (k * jnp.exp(-b)).astype(BF16)
    kd = (k * jnp.exp(b_end - b)).astype(BF16)
    dec = jnp.exp(b_end)
    glag = glag_ref[...]
    for h in range(GLA_HEADS):
        ks = slice(h * GLA_DK, (h + 1) * GLA_DK)
        vs = slice(h * GLA_DV, (h + 1) * GLA_DV)
        vh = v[:, vs]
        att = jnp.where(causal, _dot_nt(qe[:, ks], ke[:, ks]), 0.0).astype(BF16)
        o_intra = _dot(att, vh)
        s_t = s_scr[h]
        o_inter = []
        for c in range(tm // GLA_CHUNK):
            rs = slice(c * GLA_CHUNK, (c + 1) * GLA_CHUNK)
            o_inter.append(_dot_nt(qe[rs, ks], s_t.astype(BF16)))
            s_t = s_t * dec[c * GLA_CHUNK:c * GLA_CHUNK + 1, ks] + _dot_tn(vh[rs], kd[rs, ks])
        s_scr[h] = s_t
        oa_scr[:, vs] = _head_rms(o_intra + jnp.concatenate(o_inter, axis=0), glag)

    n = n_scr[...]
    oa_scr[...] = oa_scr[...] * _gla_out_gate(n, wm_ref)
    ug, vn = _gmlp_inputs(n, wm_ref, vng_ref, vnb_ref)
    vn = vn.astype(BF16)
    reps = tm // MIX_CHUNK
    mix_mask = (lax.shift_right_logical(row, MIX_CHUNK_LOG2) == lax.shift_right_logical(col, MIX_CHUNK_LOG2)) & (
        row >= col)
    for h in range(MIX_HEADS):
        w = jnp.where(mix_mask, jnp.tile(wsp_ref[h], (reps, reps)), 0.0).astype(BF16)
        cs = slice(h * MIX_DH, (h + 1) * MIX_DH)
        mix = _dot(w, vn[:, cs]) + jnp.tile(bspt_ref[:, h:h + 1], (reps, 1))
        oa_scr[:, cs] = oa_scr[:, cs] + ug[:, cs] * mix

    h_ref[...] = x + _dot(oa_scr[...].astype(BF16), wo_ref[...])

    @pl.when(t == n_tiles - 1)
    def _():
        for h in range(GLA_HEADS):
            st_ref[0, h] = s_scr[h].T


def _mixer_prompt(x, h_sample, wts):
    n_seq, seq_len, _ = x.shape
    n_s = h_sample.shape[0]
    tm = min(MIXER_TILE, seq_len)
    assert seq_len % tm == 0 and tm % MIX_CHUNK == 0 and n_s <= tm
    n_tiles = seq_len // tm
    consts = [wts['g1'], wts['w_main'], wts['w_gk1'], wts['w_gk2'], wts['b_gk'], wts['gla_g'], wts['vn_g'],
              wts['vn_b'], wts['w_sp'], wts['b_sp_t'], wts['w_o']]
    kern = functools.partial(_mixer_prompt_kernel, tm=tm, n_seq=n_seq, n_tiles=n_tiles)
    last = n_seq - 1

    def prompt_tile(b, t):
        return jnp.minimum(b, last), jnp.where(b < n_seq, t, n_tiles - 1)

    def x_map(b, t):
        bb, tt = prompt_tile(b, t)
        return bb, tt, 0

    def h_map(b, t):
        return jnp.where(b < n_seq, b * n_tiles + t, n_seq * n_tiles), 0

    return pl.pallas_call(
        kern,
        grid=(n_seq + 1, n_tiles),
        in_specs=[pl.BlockSpec((1, tm, D_MODEL), x_map), _const_spec(h_sample.shape)]
        + [_const_spec(c.shape) for c in consts],
        out_specs=[pl.BlockSpec((tm, D_MODEL), h_map),
                   pl.BlockSpec((1, GLA_HEADS, GLA_DK, GLA_DV), lambda b, t: (jnp.minimum(b, last), 0, 0, 0))],
        out_shape=[jax.ShapeDtypeStruct((n_seq * seq_len + n_s, D_MODEL), F32),
                   jax.ShapeDtypeStruct((n_seq, GLA_HEADS, GLA_DK, GLA_DV), F32)],
        scratch_shapes=[pltpu.VMEM((tm, D_MODEL), BF16),
                        pltpu.VMEM((tm, D_MODEL), F32),
                        pltpu.VMEM((GLA_HEADS, GLA_DV, GLA_DK), F32)],
        compiler_params=_params(("arbitrary", "arbitrary")),
        name="mixer_prompt",
    )(x, h_sample, *consts)


def _mixer_sample_in_kernel(x_ref, g1_ref, wm_ref, wgk1_ref, wgk2_ref, bgk_ref, vng_ref, vnb_ref,
                            mixw_ref, mixb_ref,
                            q_ref, k_ref, a_ref, v_ref, og_ref, ob_ref, vn_ref):
    n = _rms(x_ref[...], g1_ref[...]).astype(BF16)
    q, k, v, log_a = _project_gla_inputs(n, wm_ref, wgk1_ref, wgk2_ref, bgk_ref)
    q_ref[...] = q
    k_ref[...] = k
    a_ref[...] = jnp.exp(log_a)
    v_ref[...] = v
    og_ref[...] = _gla_out_gate(n, wm_ref)
    ug, vn = _gmlp_inputs(n, wm_ref, vng_ref, vnb_ref)
    vn_ref[...] = vn
    ob_ref[...] = ug * (mixw_ref[...] * vn + mixb_ref[...])


def _mixer_sample_in(x2, wts):
    n_seq = x2.shape[0]
    consts = [wts['g1'], wts['w_main'], wts['w_gk1'], wts['w_gk2'], wts['b_gk'], wts['vn_g'], wts['vn_b'],
              wts['mix_w0'], wts['mix_b0']]
    widths = [GLA_DK_T, GLA_DK_T, GLA_DK_T, GLA_DV_T, D_MODEL, D_MODEL, D_GMLP]
    return pl.pallas_call(
        _mixer_sample_in_kernel,
        grid=(1,),
        in_specs=[_const_spec(x2.shape)] + [_const_spec(c.shape) for c in consts],
        out_specs=[pl.BlockSpec((n_seq, w), lambda i: (0, 0)) for w in widths],
        out_shape=[jax.ShapeDtypeStruct((n_seq, w), F32) for w in widths],
        compiler_params=_params(("arbitrary",)),
        name="mixer_sample_in",
    )(x2, *consts)


def _state_update_kernel(s_ref, q_ref, k_ref, a_ref, v_ref, so_ref, o_ref):
    for j in range(STATE_TOKENS):
        for h in range(GLA_HEADS):
            ks = slice(h * GLA_DK, (h + 1) * GLA_DK)
            vs = slice(h * GLA_DV, (h + 1) * GLA_DV)
            s_new = a_ref[0, ks, j:j + 1] * s_ref[j, h] + k_ref[0, ks, j:j + 1] * v_ref[j:j + 1, vs]
            so_ref[j, h] = s_new
            o_ref[j:j + 1, vs] = jnp.sum(q_ref[0, ks, j:j + 1] * s_new, axis=0, keepdims=True)


def _state_update(state, q, k, a, v):
    n_seq = state.shape[0]
    tb = STATE_TOKENS
    assert n_seq % tb == 0

    def cols(z):
        return z.reshape(n_seq // tb, tb, GLA_DK_T).transpose(0, 2, 1)

    col_spec = pl.BlockSpec((1, GLA_DK_T, tb), lambda i: (i, 0, 0))
    st_spec = pl.BlockSpec((tb, GLA_HEADS, GLA_DK, GLA_DV), lambda i: (i, 0, 0, 0))
    row_spec = pl.BlockSpec((tb, GLA_DV_T), lambda i: (i, 0))
    return pl.pallas_call(
        _state_update_kernel,
        grid=(n_seq // tb,),
        in_specs=[st_spec, col_spec, col_spec, col_spec, row_spec],
        out_specs=[st_spec, row_spec],
        out_shape=[jax.ShapeDtypeStruct(state.shape, F32), jax.ShapeDtypeStruct((n_seq, GLA_DV_T), F32)],
        compiler_params=_params(("arbitrary",)),
        name="state_update",
    )(state, cols(q), cols(k), cols(a), v)


def _mixer_sample_out_kernel(x_ref, o_ref, og_ref, ob_ref, glag_ref, wo_ref, h_ref):
    glag = glag_ref[...]
    parts = []
    for h in range(GLA_HEADS):
        vs = slice(h * GLA_DV, (h + 1) * GLA_DV)
        parts.append(_head_rms(o_ref[:, vs], glag))
    merged = jnp.concatenate(parts, axis=1) * og_ref[...] + ob_ref[...]
    h_ref[...] = x_ref[...] + _dot(merged.astype(BF16), wo_ref[...])


def _mixer_sample_out(x2, o, og, ob, wts):
    n_seq = x2.shape[0]
    consts = [x2, o, og, ob, wts['gla_g'], wts['w_o']]
    return pl.pallas_call(
        _mixer_sample_out_kernel,
        grid=(1,),
        in_specs=[_const_spec(c.shape) for c in consts],
        out_specs=pl.BlockSpec((n_seq, D_MODEL), lambda i: (0, 0)),
        out_shape=jax.ShapeDtypeStruct((n_seq, D_MODEL), F32),
        compiler_params=_params(("arbitrary",)),
        name="mixer_sample_out",
    )(*consts)


def _router_kernel(h_ref, g2_ref, wr_ref, br_ref, xp_ref, ti_ref, tw_ref, rk_ref, cnt_ref, cnt_scr, *, tm):
    i = pl.program_id(0)

    @pl.when(i == 0)
    def _():
        cnt_scr[...] = jnp.zeros_like(cnt_scr)

    hn = _rms(h_ref[...], g2_ref[...])
    hn_hi = hn.astype(BF16)
    hn_lo = (hn - hn_hi.astype(F32)).astype(BF16)
    w = wr_ref[...]
    w_hi = w.astype(BF16)
    w_lo = (w - w_hi.astype(F32)).astype(BF16)
    logits = _dot(hn_hi, w_hi) + _dot(hn_lo, w_hi) + _dot(hn_hi, w_lo) + br_ref[...]

    _rows_to_tiles(xp_ref, hn)

    lane = lax.broadcasted_iota(I32, (tm, LANES), 1)
    neg = jnp.float32(-jnp.inf)
    l = jnp.where(lane < N_EXPERTS, logits, neg)
    vals, idxs, hots = [], [], []
    for _ in range(TOP_K):
        m = jnp.max(l, axis=-1, keepdims=True)
        idx = jnp.min(jnp.where(l == m, lane, LANES), axis=-1, keepdims=True)
        hot = lane == idx
        l = jnp.where(hot, neg, l)
        vals.append(m)
        idxs.append(idx)
        hots.append(hot)
    exps = [jnp.exp(v - vals[0]) for v in vals]
    denom = exps[0] + exps[1] + exps[2] + exps[3]

    member = (hots[0] | hots[1] | hots[2] | hots[3]).astype(BF16)
    row = lax.broadcasted_iota(I32, (tm, tm), 0)
    col = lax.broadcasted_iota(I32, (tm, tm), 1)
    earlier = (row > col).astype(BF16)
    before = cnt_scr[...] + _dot(earlier, member)
    cnt_scr[...] = cnt_scr[...] + jnp.sum(member.astype(F32), axis=0, keepdims=True)
    cnt_ref[...] = cnt_scr[...].astype(I32)

    ti = jnp.zeros((tm, LANES), I32)
    tw = jnp.zeros((tm, LANES), F32)
    rk = jnp.zeros((tm, LANES), I32)
    for kk in range(TOP_K):
        sel = lane == kk
        rank = jnp.sum(jnp.where(hots[kk], before, 0.0), axis=-1, keepdims=True).astype(I32)
        ti = jnp.where(sel, idxs[kk], ti)
        tw = jnp.where(sel, exps[kk] / denom, tw)
        rk = jnp.where(sel, rank, rk)
    ti_ref[...] = ti
    tw_ref[...] = tw
    rk_ref[...] = rk


def _router_tile(t_all):
    for tm in (512, 384, 256, 128):
        if t_all % tm == 0:
            return tm
    raise ValueError(f"token count {t_all} must be a multiple of 128")


def _router(hbuf, wts):
    t_all = hbuf.shape[0]
    tm = _router_tile(t_all)
    consts = [wts['g2'], wts['w_router'], wts['b_router']]
    row = lambda w: pl.BlockSpec((tm, w), lambda i: (i, 0))
    return pl.pallas_call(
        functools.partial(_router_kernel, tm=tm),
        grid=(t_all // tm,),
        in_specs=[row(D_MODEL)] + [_const_spec(c.shape) for c in consts],
        out_specs=[pl.BlockSpec((tm * ROW_SUBLANES, LANES), lambda i: (i, 0)), row(LANES), row(LANES), row(LANES),
                   pl.BlockSpec((1, LANES), lambda i: (0, 0))],
        out_shape=[jax.ShapeDtypeStruct((t_all * ROW_SUBLANES, LANES), F32),
                   jax.ShapeDtypeStruct((t_all, LANES), I32),
                   jax.ShapeDtypeStruct((t_all, LANES), F32),
                   jax.ShapeDtypeStruct((t_all, LANES), I32),
                   jax.ShapeDtypeStruct((1, LANES), I32)],
        scratch_shapes=[pltpu.VMEM((1, LANES), F32)],
        compiler_params=_params(("arbitrary",)),
        name="router",
    )(hbuf, *consts)


def _row_tile(r):
    start = r * ROW_SUBLANES
    if not isinstance(r, int):
        start = pl.multiple_of(start, ROW_SUBLANES)
    return pl.ds(start, ROW_SUBLANES)


def _tile_of(ref, row):
    return ref.at[_row_tile(row)]


def _rows_from_tiles(ref, n_rows):
    return jnp.concatenate([ref[pl.ds(s, n_rows, stride=ROW_SUBLANES), :] for s in range(ROW_SUBLANES)], axis=1)


def _rows_to_tiles(ref, x):
    for s in range(ROW_SUBLANES):
        ref[pl.ds(s, x.shape[0], stride=ROW_SUBLANES), :] = x[:, s * LANES:(s + 1) * LANES]


class _Layout:
    def __init__(self, t_all):
        bm = EXPERT_BLOCK
        self.t_all = t_all
        self.n_blocks = -(-(t_all * TOP_K + N_EXPERTS * (bm - 1)) // bm)
        self.slab = -(-t_all // SLAB_ALIGN) * SLAB_ALIGN
        self.dump0 = TOP_K * self.slab
        self.prime0 = self.dump0 + N_EXPERTS * bm
        self.y_rows = self.prime0 + bm
        self.sorted_rows = self.n_blocks * bm
        self.table_rows = (self.n_blocks + 1) * bm


def _dispatch_kernel(dest_ref, lo_ref, hi_ref, base_ref, xp_ref, xs_hbm, rows_ref, zero_scr, sem, *, lay, tm):
    bm = EXPERT_BLOCK
    i = pl.program_id(0)

    def fill(r, dump_row):
        rows_ref[bm + r] = dump_row
        return pltpu.make_async_copy(zero_scr, _tile_of(xs_hbm, r), sem.at[1])

    def fill_range(lo, hi, dump_of):
        def start(r, c):
            fill(r, dump_of(r)).start()
            return c
        lax.fori_loop(lo, hi, start, 0)

        def wait(r, c):
            pltpu.make_async_copy(zero_scr, _tile_of(xs_hbm, r), sem.at[1]).wait()
            return c
        lax.fori_loop(lo, hi, wait, 0)

    @pl.when(i == 0)
    def _():
        zero_scr[...] = jnp.zeros_like(zero_scr)

        def prime(r, c):
            rows_ref[r] = lay.prime0 + r
            return c
        lax.fori_loop(0, bm, prime, 0)

        def expert(e, c):
            base = base_ref[e]
            fill_range(lo_ref[e], hi_ref[e], lambda r: base + r)
            return c
        lax.fori_loop(0, N_EXPERTS, expert, 0)
        fill_range(hi_ref[N_EXPERTS - 1], lay.sorted_rows, lambda r: lay.dump0)

    unroll = 8
    t0 = i * tm

    def rows_of(t8, c):
        for u in range(unroll):
            t = t8 * unroll + u
            for k in range(TOP_K):
                d = dest_ref[k * lay.t_all + t0 + t]
                pltpu.make_async_copy(xp_ref.at[_row_tile(t)], _tile_of(xs_hbm, d), sem.at[0]).start(
                    priority=k % 2)
                rows_ref[bm + d] = k * lay.slab + t0 + t
        return c
    lax.fori_loop(0, tm // unroll, rows_of, 0)
    for _ in range(TOP_K):
        pltpu.make_async_copy(xp_ref, xs_hbm.at[pl.ds(0, tm * ROW_SUBLANES)], sem.at[0]).wait()


def _dispatch(xp, dest_km, pad_lo, pad_hi, base, lay):
    tm = _router_tile(lay.t_all)
    grid_spec = pltpu.PrefetchScalarGridSpec(
        num_scalar_prefetch=4,
        grid=(lay.t_all // tm,),
        in_specs=[pl.BlockSpec((tm * ROW_SUBLANES, LANES), lambda i, *_: (i, 0))],
        out_specs=[pl.BlockSpec(memory_space=pl.ANY), pl.BlockSpec(memory_space=pltpu.SMEM)],
        scratch_shapes=[pltpu.VMEM((ROW_SUBLANES, LANES), F32), pltpu.SemaphoreType.DMA((2,))],
    )
    return pl.pallas_call(
        functools.partial(_dispatch_kernel, lay=lay, tm=tm),
        grid_spec=grid_spec,
        out_shape=[jax.ShapeDtypeStruct((lay.sorted_rows * ROW_SUBLANES, LANES), F32),
                   jax.ShapeDtypeStruct((lay.table_rows,), I32)],
        compiler_params=_params(("arbitrary",)),
        name="dispatch",
    )(dest_km, pad_lo, pad_hi, base, xp)


def _experts_kernel(be_ref, nv_ref, rows_ref, xs_ref, wg_ref, wu_ref, wd_ref, bg_ref, bu_ref, bd_ref,
                    y_hbm, obuf, wbf, sem_s, *, lay):
    bm = EXPERT_BLOCK
    i = pl.program_id(0)
    n_valid = nv_ref[0]
    slot = i % 2
    other = 1 - slot

    def scatter_row(blk, s, r):
        return pltpu.make_async_copy(obuf.at[s, _row_tile(r)], _tile_of(y_hbm, rows_ref[(blk + 1) * bm + r]),
                                     sem_s.at[s])

    def wait_scatter(s):
        pltpu.make_async_copy(obuf.at[s], y_hbm.at[pl.ds(0, bm * ROW_SUBLANES)], sem_s.at[s]).wait()

    @pl.when(i == 0)
    def _():
        obuf[...] = jnp.zeros_like(obuf)
        fills = [(k * lay.slab + lay.t_all, lay.slab - lay.t_all) for k in range(TOP_K)]
        fills.append((lay.dump0, lay.y_rows - lay.dump0))
        copies = []
        for start, n in fills:
            for o in range(0, n, bm):
                m = min(bm, n - o) * ROW_SUBLANES
                copies.append(pltpu.make_async_copy(obuf.at[0, pl.ds(0, m)],
                                                    y_hbm.at[pl.ds((start + o) * ROW_SUBLANES, m)], sem_s.at[0]))
        for cp in copies:
            cp.start()
        for cp in copies:
            cp.wait()

    @pl.when((i >= 1) & (i <= n_valid))
    def _():
        wait_scatter(slot)

    @pl.when(i < n_valid)
    def _():
        part = bm // EXPERT_PARTS
        per_group = bm // (2 * EXPERT_PARTS)

        @pl.when((i == 0) | (be_ref[i] != be_ref[jnp.maximum(i - 1, 0)]))
        def _():
            for w, w_ref in enumerate((wg_ref, wu_ref, wd_ref)):
                for c in range(0, D_MODEL, WEIGHT_CAST_ROWS):
                    wbf[w, c:c + WEIGHT_CAST_ROWS, :] = w_ref[0, c:c + WEIGHT_CAST_ROWS, :].astype(BF16)

        def issue(group):
            for r in range(group * per_group, (group + 1) * per_group):
                scatter_row(i - 1, other, r).start(priority=r % 2)

        for j in range(EXPERT_PARTS):
            tiles = pl.ds(j * part * ROW_SUBLANES, part * ROW_SUBLANES)
            x = _rows_from_tiles(xs_ref.at[tiles], part).astype(BF16)
            issue(2 * j)
            g = jnp.minimum(_dot(x, wbf[0]) + bg_ref[0], SWIGLU_LIMIT)
            u = jnp.clip(_dot(x, wbf[1]) + bu_ref[0], -SWIGLU_LIMIT, SWIGLU_LIMIT)
            mid = ((u + 1.0) * (g * _sigmoid(SWIGLU_ALPHA * g))).astype(BF16)
            issue(2 * j + 1)
            _rows_to_tiles(obuf.at[slot, tiles], _dot(mid, wbf[2]) + bd_ref[0])

    @pl.when(i == n_valid)
    def _():
        def last(r, c):
            scatter_row(i - 1, other, r).start()
            return c
        lax.fori_loop(0, bm, last, 0)
        wait_scatter(other)


def _experts(block_e, n_valid, rows, xs, wts, lay):
    bm = EXPERT_BLOCK
    tile_rows = bm * ROW_SUBLANES
    wspec = pl.BlockSpec((1, D_MODEL, D_FF), lambda i, be, *_: (be[i], 0, 0))
    bspec = pl.BlockSpec((1, 1, D_FF), lambda i, be, *_: (be[i], 0, 0))
    xspec = pl.BlockSpec((tile_rows, LANES), lambda i, be, nv, *_: (jnp.minimum(i, nv[0] - 1), 0))
    grid_spec = pltpu.PrefetchScalarGridSpec(
        num_scalar_prefetch=3,
        grid=(lay.n_blocks + 1,),
        in_specs=[xspec, wspec, wspec, wspec, bspec, bspec, bspec],
        out_specs=pl.BlockSpec(memory_space=pl.ANY),
        scratch_shapes=[pltpu.VMEM((2, tile_rows, LANES), F32), pltpu.VMEM((3, D_MODEL, D_FF), BF16),
                        pltpu.SemaphoreType.DMA((2,))],
    )
    return pl.pallas_call(
        functools.partial(_experts_kernel, lay=lay),
        grid_spec=grid_spec,
        out_shape=jax.ShapeDtypeStruct((lay.y_rows * ROW_SUBLANES, LANES), F32),
        compiler_params=_params(("arbitrary",)),
        name="experts",
    )(block_e, n_valid, rows, xs, wts['w_gate'], wts['w_up'], wts['w_down'],
      wts['b_gate'], wts['b_up'], wts['b_down'])


def _combine_kernel(h_ref, tw_ref, p_ref, y0_ref, y1_ref, y2_ref, y3_ref, g3_ref, wpg_ref, wpp_ref, gf_ref,
                    out_ref):
    tw = tw_ref[...]
    tm = tw.shape[0]
    y = _rows_from_tiles(y0_ref, tm) * tw[:, 0:1]
    for kk, y_ref in enumerate((y1_ref, y2_ref, y3_ref), start=1):
        y = y + _rows_from_tiles(y_ref, tm) * tw[:, kk:kk + 1]
    h2 = h_ref[...] + y
    gate = _sigmoid(_dot(_rms(h2, g3_ref[...]).astype(BF16), wpg_ref[...]))
    h3 = h2 + gate * _dot(p_ref[...].astype(BF16), wpp_ref[...])
    out_ref[...] = _rms(h3, gf_ref[...])


def _combine(hbuf, tw, p2, y, row0, wts, lay):
    n_rows = p2.shape[0]
    tm = min(COMBINE_TILE, n_rows)
    assert n_rows % tm == 0 and row0 % tm == 0 and lay.slab % tm == 0
    off = row0 // tm
    consts = [wts['g3'], wts['w_ple_gate'], wts['w_ple_proj'], wts['g_final']]

    def slab_spec(k):
        return pl.BlockSpec((tm * ROW_SUBLANES, LANES), lambda i: (k * (lay.slab // tm) + off + i, 0))

    return pl.pallas_call(
        _combine_kernel,
        grid=(n_rows // tm,),
        in_specs=[pl.BlockSpec((tm, D_MODEL), lambda i: (i + off, 0)),
                  pl.BlockSpec((tm, LANES), lambda i: (i + off, 0)),
                  pl.BlockSpec((tm, D_PLE), lambda i: (i, 0))]
        + [slab_spec(k) for k in range(TOP_K)] + [_const_spec(c.shape) for c in consts],
        out_specs=pl.BlockSpec((tm, D_MODEL), lambda i: (i, 0)),
        out_shape=jax.ShapeDtypeStruct((n_rows, D_MODEL), F32),
        compiler_params=_params(("arbitrary",)),
        name="combine",
    )(hbuf, tw, p2, y, y, y, y, *consts)


def _prepare_weights(norm1_g, w_in, w_gk2, b_gk, gla_norm_g, v_norm_g, v_norm_b, w_sp, b_sp, w_o, norm2_g,
                     w_router, b_router, w_gate, b_gate, w_up, b_up, w_down, b_down, norm3_g, w_ple_gate,
                     w_ple_proj, final_g):
    o_gk = 2 * GLA_DK_T + 2 * GLA_DV_T
    w_main = jnp.concatenate([w_in[:, :o_gk], w_in[:, o_gk + GK_RANK:]], axis=1).astype(BF16)
    w_gk1 = jnp.pad(w_in[:, o_gk:o_gk + GK_RANK], ((0, 0), (0, LANES - GK_RANK))).astype(BF16)
    w_gk2p = jnp.pad(w_gk2, ((0, LANES - GK_RANK), (0, 0))).astype(BF16)
    row = lambda z: z.reshape(1, -1).astype(F32)
    return dict(
        g1=row(norm1_g), w_main=w_main, w_gk1=w_gk1, w_gk2=w_gk2p, b_gk=row(b_gk), gla_g=row(gla_norm_g),
        vn_g=row(v_norm_g), vn_b=row(v_norm_b), w_sp=w_sp, b_sp_t=b_sp.T,
        mix_w0=row(jnp.repeat(w_sp[:, 0, 0], MIX_DH)), mix_b0=row(jnp.repeat(b_sp[:, 0], MIX_DH)),
        w_o=w_o.astype(BF16), g2=row(norm2_g),
        w_router=jnp.pad(w_router, ((0, 0), (0, LANES - N_EXPERTS))),
        b_router=jnp.pad(row(b_router), ((0, 0), (0, LANES - N_EXPERTS))),
        w_gate=w_gate, w_up=w_up, w_down=w_down,
        b_gate=b_gate.reshape(N_EXPERTS, 1, D_FF), b_up=b_up.reshape(N_EXPERTS, 1, D_FF),
        b_down=b_down.reshape(N_EXPERTS, 1, D_MODEL),
        g3=row(norm3_g), w_ple_gate=w_ple_gate.astype(BF16), w_ple_proj=w_ple_proj.astype(BF16),
        g_final=row(final_g))


def _dispatch_plan(xp, top_i, rank, counts, lay):
    bm = EXPERT_BLOCK
    padded = (counts + bm - 1) // bm * bm
    pad_end = jnp.cumsum(padded)
    pad_start = pad_end - padded
    raw_end = jnp.cumsum(counts)
    experts = jnp.arange(N_EXPERTS, dtype=I32)
    start_of = jnp.sum(jnp.where(top_i[:, :, None] == experts, pad_start, 0), axis=-1)
    dest_km = (start_of + rank).T.reshape(-1).astype(I32)
    n_valid = (pad_end[-1] // bm).astype(I32)
    blk = jnp.minimum(jnp.arange(lay.n_blocks + 1, dtype=I32), n_valid - 1)
    block_e = jnp.sum(pad_end[None, :] <= (blk * bm)[:, None], axis=1).astype(I32)
    xs, rows = _dispatch(xp, dest_km, (pad_start + counts).astype(I32), pad_end.astype(I32),
                         (lay.dump0 - raw_end).astype(I32), lay)
    return xs, rows, block_e, n_valid.reshape(1)


def _layer(x_prompt, x_sample, state, p_prompt, p_sample, wts):
    n_p, len_p, _ = x_prompt.shape
    n_s, len_s, _ = x_sample.shape
    assert len_s == 1, "the sample group carries one new token per sequence"
    t_p = n_p * len_p
    t_all = t_p + n_s

    xs = x_sample.reshape(n_s, D_MODEL)
    q, k, a, v, og, ob, vn = _mixer_sample_in(xs, wts)
    st_sample, o = _state_update(state, q, k, a, v)
    h_sample = _mixer_sample_out(xs, o, og, ob, wts)
    hbuf, st_prompt = _mixer_prompt(x_prompt, h_sample, wts)

    xp, top_i, top_w, rank, counts = _router(hbuf, wts)
    lay = _Layout(t_all)
    xs, rows, block_e, n_valid = _dispatch_plan(xp, top_i[:, :TOP_K], rank[:, :TOP_K], counts[0, :N_EXPERTS], lay)
    y = _experts(block_e, n_valid, rows, xs, wts, lay)
    y_prompt = _combine(hbuf, top_w, p_prompt.reshape(t_p, D_PLE), y, 0, wts, lay)
    y_sample = _combine(hbuf, top_w, p_sample.reshape(n_s, D_PLE), y, t_p, wts, lay)
    return (y_prompt.reshape(n_p, len_p, D_MODEL), y_sample.reshape(n_s, len_s, D_MODEL),
            st_prompt, st_sample, vn.reshape(n_s, len_s, D_GMLP))


def kernel(x_prompt, x_sample, state_gla, p_prompt, p_sample, norm1_g, w_in, w_gk2, b_gk, gla_norm_g, v_norm_g,
           v_norm_b, w_sp, b_sp, w_o, norm2_g, w_router, b_router, w_gate, b_gate, w_up, b_up, w_down, b_down,
           norm3_g, w_ple_gate, w_ple_proj, final_g):
    assert w_in.shape[0] == 1, "single-layer trunk"
    wts = _prepare_weights(norm1_g[0], w_in[0], w_gk2[0], b_gk[0], gla_norm_g[0], v_norm_g[0], v_norm_b[0],
                           w_sp[0], b_sp[0], w_o[0], norm2_g[0], w_router[0], b_router[0], w_gate[0], b_gate[0],
                           w_up[0], b_up[0], w_down[0], b_down[0], norm3_g[0], w_ple_gate[0], w_ple_proj[0],
                           final_g)
    y_p, y_s, st_p, st_s, vn_s = _layer(x_prompt, x_sample, state_gla[0], p_prompt[0], p_sample[0], wts)
    return (y_p, y_s, st_p[None], st_s[None].astype(state_gla.dtype), vn_s[None])
```

```python
import functools

import jax
import jax.numpy as jnp
from jax import lax
from jax.experimental import pallas as pl
from jax.experimental.pallas import tpu as pltpu

F32 = jnp.float32
BF16 = jnp.bfloat16
I32 = jnp.int32
U32 = jnp.uint32

D_MODEL = 1024
GLA_HEADS = 4
GLA_DK = 128
GLA_DV = 256
GLA_DK_T = GLA_HEADS * GLA_DK
GLA_DV_T = GLA_HEADS * GLA_DV
GK_RANK = 16
GATE_NORMALIZER = 16.0
GLA_CHUNK = 64
GLA_CHUNK_LOG2 = 6
D_GMLP = 1024
MIX_HEADS = 4
MIX_DH = D_GMLP // MIX_HEADS
MIX_CHUNK = 128
MIX_CHUNK_LOG2 = 7
assert 1 << GLA_CHUNK_LOG2 == GLA_CHUNK and 1 << MIX_CHUNK_LOG2 == MIX_CHUNK
N_EXPERTS = 32
TOP_K = 4
D_FF = 1024
SWIGLU_LIMIT = 7.0
SWIGLU_ALPHA = 1.702
D_PLE = 256
EPS = 1e-6

LANES = 128
ROW_SUBLANES = D_MODEL // LANES
VMEM_LIMIT_BYTES = 56 * 1024 * 1024

OFF_Q, OFF_K, OFF_V, OFF_R, OFF_U, OFF_VG, OFF_GA, OFF_GB, N_MAIN = (
    0, 512, 1024, 2048, 3072, 4096, 5120, 6144, 7168)

MIXER_TILE = 256
STATE_TOKENS = 8
EXPERT_BLOCK = 512
EXPERT_PARTS = 2
WEIGHT_CAST_ROWS = 256
STRIP_CHUNK = 8
STRIP_CHUNK_LOG2 = 3
ZERO_ROWS = 64
ZERO_ROWS_LOG2 = 6
assert 1 << STRIP_CHUNK_LOG2 == STRIP_CHUNK and 1 << ZERO_ROWS_LOG2 == ZERO_ROWS


def _dot(a, b):
    return jnp.dot(a, b, preferred_element_type=F32)


def _dot_nt(a, b):
    return lax.dot_general(a, b, (((1,), (1,)), ((), ())), preferred_element_type=F32)


def _dot_tn(a, b):
    return lax.dot_general(a, b, (((0,), (0,)), ((), ())), preferred_element_type=F32)


def _rms(x, g):
    return x * lax.rsqrt(jnp.mean(x * x, axis=-1, keepdims=True) + EPS) * g


def _sigmoid(x):
    return 1.0 / (1.0 + jnp.exp(-x))


def _gelu(x):
    return 0.5 * x * (1.0 + lax.erf(x * (2.0 ** -0.5)))


def _log_sigmoid(x):
    return jnp.minimum(x, 0.0) - jnp.log1p(jnp.exp(-jnp.abs(x)))


def _split3(x):
    hi = x.astype(BF16)
    r1 = x - hi.astype(F32)
    mid = r1.astype(BF16)
    lo = (r1 - mid.astype(F32)).astype(BF16)
    return hi, mid, lo


def _const_spec(shape):
    nd = len(shape)
    return pl.BlockSpec(shape, lambda *_: (0,) * nd, pipeline_mode=pl.Buffered(1))


def _params(sem):
    return pltpu.CompilerParams(dimension_semantics=sem, vmem_limit_bytes=VMEM_LIMIT_BYTES)


def _project_gla_inputs(n, wm_ref, wgk1_ref, wgk2_ref, bgk_ref):
    q = _dot(n, wm_ref[:, OFF_Q:OFF_K]) * (GLA_DK ** -0.5)
    k = _dot(n, wm_ref[:, OFF_K:OFF_V])
    v = _dot(n, wm_ref[:, OFF_V:OFF_R])
    gk = _dot(n, wgk1_ref[...]).astype(BF16)
    log_a = _log_sigmoid(_dot(gk, wgk2_ref[...]) + bgk_ref[...]) * (1.0 / GATE_NORMALIZER)
    return q, k, v, log_a


def _gmlp_inputs(n, wm_ref, vng_ref, vnb_ref):
    u = _gelu(_dot(n, wm_ref[:, OFF_U:OFF_VG]))
    vg = _gelu(_dot(n, wm_ref[:, OFF_VG:OFF_GA]))
    mu = jnp.mean(vg, axis=-1, keepdims=True)
    vc = vg - mu
    var = jnp.mean(vc * vc, axis=-1, keepdims=True)
    vn = vc * lax.rsqrt(var + EPS) * vng_ref[...] + vnb_ref[...]
    ug = u * _sigmoid(_dot(n, wm_ref[:, OFF_GB:N_MAIN]))
    return ug, vn


def _gla_out_gate(n, wm_ref):
    r = _dot(n, wm_ref[:, OFF_R:OFF_U])
    ga = _dot(n, wm_ref[:, OFF_GA:OFF_GB])
    return r * _sigmoid(r) * _sigmoid(ga)


def _head_rms(o, g):
    return o * lax.rsqrt(jnp.mean(o * o, axis=-1, keepdims=True) + EPS) * g


def _mixer_prompt_kernel(x_ref, hs_ref, *refs, tm, n_seq, n_tiles):
    b = pl.program_id(0)
    t = pl.program_id(1)
    h_ref = refs[11]

    @pl.when(b < n_seq)
    def _():
        _mixer_prompt_tile(x_ref, *refs, tm=tm, n_tiles=n_tiles)

    @pl.when((b == n_seq) & (t == 0))
    def _():
        h_ref[0:hs_ref.shape[0], :] = hs_ref[...]


def _mixer_prompt_tile(x_ref, g1_ref, wm_ref, wgk1_ref, wgk2_ref, bgk_ref, glag_ref, vng_ref, vnb_ref,
                       wsp_ref, bspt_ref, wo_ref,
                       h_ref, st_ref,
                       n_scr, oa_scr, s_scr, *, tm, n_tiles):
    t = pl.program_id(1)

    @pl.when(t == 0)
    def _():
        s_scr[...] = jnp.zeros_like(s_scr)

    x = x_ref[0]
    n = _rms(x, g1_ref[...]).astype(BF16)
    n_scr[...] = n
    q, k, v, log_a = _project_gla_inputs(n, wm_ref, wgk1_ref, wgk2_ref, bgk_ref)
    v = v.astype(BF16)

    row = lax.broadcasted_iota(I32, (tm, tm), 0)
    col = lax.broadcasted_iota(I32, (tm, tm), 1)
    same_chunk = lax.shift_right_logical(row, GLA_CHUNK_LOG2) == lax.shift_right_logical(col, GLA_CHUNK_LOG2)
    causal = same_chunk & (row >= col)
    tri = causal.astype(BF16)
    blk = same_chunk.astype(BF16)
    hi, mid, lo = _split3(log_a)
    b = _dot(tri, hi) + _dot(tri, mid) + _dot(tri, lo)
    b_end = _dot(blk, hi) + _dot(blk, mid) + _dot(blk, lo)
    qe = (q * jnp.exp(b)).astype(BF16)
    ke = (k * jnp.exp(-b)).astype(BF16)
    kd = (k * jnp.exp(b_end - b)).astype(BF16)
    dec = jnp.exp(b_end)
    glag = glag_ref[...]
    for h in range(GLA_HEADS):
        ks = slice(h * GLA_DK, (h + 1) * GLA_DK)
        vs = slice(h * GLA_DV, (h + 1) * GLA_DV)
        vh = v[:, vs]
        att = jnp.where(causal, _dot_nt(qe[:, ks], ke[:, ks]), 0.0).astype(BF16)
        o_intra = _dot(att, vh)
        s_t = s_scr[h]
        o_inter = []
        for c in range(tm // GLA_CHUNK):
            rs = slice(c * GLA_CHUNK, (c + 1) * GLA_CHUNK)
            o_inter.append(_dot_nt(qe[rs, ks], s_t.astype(BF16)))
            s_t = s_t * dec[c * GLA_CHUNK:c * GLA_CHUNK + 1, ks] + _dot_tn(vh[rs], kd[rs, ks])
        s_scr[h] = s_t
        oa_scr[:, vs] = _head_rms(o_intra + jnp.concatenate(o_inter, axis=0), glag)

    n = n_scr[...]
    oa_scr[...] = oa_scr[...] * _gla_out_gate(n, wm_ref)
    ug, vn = _gmlp_inputs(n, wm_ref, vng_ref, vnb_ref)
    vn = vn.astype(BF16)
    reps = tm // MIX_CHUNK
    mix_mask = (lax.shift_right_logical(row, MIX_CHUNK_LOG2) == lax.shift_right_logical(col, MIX_CHUNK_LOG2)) & (
        row >= col)
    for h in range(MIX_HEADS):
        w = jnp.where(mix_mask, jnp.tile(wsp_ref[h], (reps, reps)), 0.0).astype(BF16)
        cs = slice(h * MIX_DH, (h + 1) * MIX_DH)
        mix = _dot(w, vn[:, cs]) + jnp.tile(bspt_ref[:, h:h + 1], (reps, 1))
        oa_scr[:, cs] = oa_scr[:, cs] + ug[:, cs] * mix

    h_ref[...] = x + _dot(oa_scr[...].astype(BF16), wo_ref[...])

    @pl.when(t == n_tiles - 1)
    def _():
        for h in range(GLA_HEADS):
            st_ref[0, h] = s_scr[h].T


def _mixer_prompt(x, h_sample, wts):
    n_seq, seq_len, _ = x.shape
    n_s = h_sample.shape[0]
    tm = min(MIXER_TILE, seq_len)
    assert seq_len % tm == 0 and tm % MIX_CHUNK == 0 and n_s <= tm
    n_tiles = seq_len // tm
    consts = [wts['g1'], wts['w_main'], wts['w_gk1'], wts['w_gk2'], wts['b_gk'], wts['gla_g'], wts['vn_g'],
              wts['vn_b'], wts['w_sp'], wts['b_sp_t'], wts['w_o']]
    kern = functools.partial(_mixer_prompt_kernel, tm=tm, n_seq=n_seq, n_tiles=n_tiles)
    last = n_seq - 1

    def prompt_tile(b, t):
        return jnp.minimum(b, last), jnp.where(b < n_seq, t, n_tiles - 1)

    def x_map(b, t):
        bb, tt = prompt_tile(b, t)
        return bb, tt, 0

    def h_map(b, t):
        return jnp.where(b < n_seq, b * n_tiles + t, n_seq * n_tiles), 0

    return pl.pallas_call(
        kern,
        grid=(n_seq + 1, n_tiles),
        in_specs=[pl.BlockSpec((1, tm, D_MODEL), x_map), _const_spec(h_sample.shape)]
        + [_const_spec(c.shape) for c in consts],
        out_specs=[pl.BlockSpec((tm, D_MODEL), h_map),
                   pl.BlockSpec((1, GLA_HEADS, GLA_DK, GLA_DV), lambda b, t: (jnp.minimum(b, last), 0, 0, 0))],
        out_shape=[jax.ShapeDtypeStruct((n_seq * seq_len + n_s, D_MODEL), F32),
                   jax.ShapeDtypeStruct((n_seq, GLA_HEADS, GLA_DK, GLA_DV), F32)],
        scratch_shapes=[pltpu.VMEM((tm, D_MODEL), BF16),
                        pltpu.VMEM((tm, D_MODEL), F32),
                        pltpu.VMEM((GLA_HEADS, GLA_DV, GLA_DK), F32)],
        compiler_params=_params(("arbitrary", "arbitrary")),
        name="mixer_prompt",
    )(x, h_sample, *consts)


def _mixer_sample_in_kernel(x_ref, g1_ref, wm_ref, wgk1_ref, wgk2_ref, bgk_ref, vng_ref, vnb_ref,
                            mixw_ref, mixb_ref,
                            q_ref, k_ref, a_ref, v_ref, og_ref, ob_ref, vn_ref):
    n = _rms(x_ref[...], g1_ref[...]).astype(BF16)
    q, k, v, log_a = _project_gla_inputs(n, wm_ref, wgk1_ref, wgk2_ref, bgk_ref)
    q_ref[...] = q
    k_ref[...] = k
    a_ref[...] = jnp.exp(log_a)
    v_ref[...] = v
    og_ref[...] = _gla_out_gate(n, wm_ref)
    ug, vn = _gmlp_inputs(n, wm_ref, vng_ref, vnb_ref)
    vn_ref[...] = vn
    ob_ref[...] = ug * (mixw_ref[...] * vn + mixb_ref[...])


def _mixer_sample_in(x2, wts):
    n_seq = x2.shape[0]
    consts = [wts['g1'], wts['w_main'], wts['w_gk1'], wts['w_gk2'], wts['b_gk'], wts['vn_g'], wts['vn_b'],
              wts['mix_w0'], wts['mix_b0']]
    widths = [GLA_DK_T, GLA_DK_T, GLA_DK_T, GLA_DV_T, D_MODEL, D_MODEL, D_GMLP]
    return pl.pallas_call(
        _mixer_sample_in_kernel,
        grid=(1,),
        in_specs=[_const_spec(x2.shape)] + [_const_spec(c.shape) for c in consts],
        out_specs=[pl.BlockSpec((n_seq, w), lambda i: (0, 0)) for w in widths],
        out_shape=[jax.ShapeDtypeStruct((n_seq, w), F32) for w in widths],
        compiler_params=_params(("arbitrary",)),
        name="mixer_sample_in",
    )(x2, *consts)


def _state_update_kernel(s_ref, q_ref, k_ref, a_ref, v_ref, so_ref, o_ref):
    for j in range(STATE_TOKENS):
        for h in range(GLA_HEADS):
            ks = slice(h * GLA_DK, (h + 1) * GLA_DK)
            vs = slice(h * GLA_DV, (h + 1) * GLA_DV)
            s_new = a_ref[0, ks, j:j + 1] * s_ref[j, h] + k_ref[0, ks, j:j + 1] * v_ref[j:j + 1, vs]
            so_ref[j, h] = s_new
            o_ref[j:j + 1, vs] = jnp.sum(q_ref[0, ks, j:j + 1] * s_new, axis=0, keepdims=True)


def _state_update(state, q, k, a, v):
    n_seq = state.shape[0]
    tb = STATE_TOKENS
    assert n_seq % tb == 0

    def cols(z):
        return z.reshape(n_seq // tb, tb, GLA_DK_T).transpose(0, 2, 1)

    col_spec = pl.BlockSpec((1, GLA_DK_T, tb), lambda i: (i, 0, 0))
    st_spec = pl.BlockSpec((tb, GLA_HEADS, GLA_DK, GLA_DV), lambda i: (i, 0, 0, 0))
    row_spec = pl.BlockSpec((tb, GLA_DV_T), lambda i: (i, 0))
    return pl.pallas_call(
        _state_update_kernel,
        grid=(n_seq // tb,),
        in_specs=[st_spec, col_spec, col_spec, col_spec, row_spec],
        out_specs=[st_spec, row_spec],
        out_shape=[jax.ShapeDtypeStruct(state.shape, F32), jax.ShapeDtypeStruct((n_seq, GLA_DV_T), F32)],
        compiler_params=_params(("arbitrary",)),
        name="state_update",
    )(state, cols(q), cols(k), cols(a), v)


def _mixer_sample_out_kernel(x_ref, o_ref, og_ref, ob_ref, glag_ref, wo_ref, h_ref):
    glag = glag_ref[...]
    parts = []
    for h in range(GLA_HEADS):
        vs = slice(h * GLA_DV, (h + 1) * GLA_DV)
        parts.append(_head_rms(o_ref[:, vs], glag))
    merged = jnp.concatenate(parts, axis=1) * og_ref[...] + ob_ref[...]
    h_ref[...] = x_ref[...] + _dot(merged.astype(BF16), wo_ref[...])


def _mixer_sample_out(x2, o, og, ob, wts):
    n_seq = x2.shape[0]
    consts = [x2, o, og, ob, wts['gla_g'], wts['w_o']]
    return pl.pallas_call(
        _mixer_sample_out_kernel,
        grid=(1,),
        in_specs=[_const_spec(c.shape) for c in consts],
        out_specs=pl.BlockSpec((n_seq, D_MODEL), lambda i: (0, 0)),
        out_shape=jax.ShapeDtypeStruct((n_seq, D_MODEL), F32),
        compiler_params=_params(("arbitrary",)),
        name="mixer_sample_out",
    )(*consts)


def _router_kernel(h_ref, g2_ref, wr_ref, br_ref, xp_ref, ti_ref, tw_ref, rk_ref, cb_ref, cnt_ref, cnt_scr, *, tm):
    i = pl.program_id(0)

    @pl.when(i == 0)
    def _():
        cnt_scr[...] = jnp.zeros_like(cnt_scr)

    cb_ref[0] = cnt_scr[...].astype(I32)

    hn = _rms(h_ref[...], g2_ref[...])
    hn_hi = hn.astype(BF16)
    hn_lo = (hn - hn_hi.astype(F32)).astype(BF16)
    w = wr_ref[...]
    w_hi = w.astype(BF16)
    w_lo = (w - w_hi.astype(F32)).astype(BF16)
    logits = _dot(hn_hi, w_hi) + _dot(hn_lo, w_hi) + _dot(hn_hi, w_lo) + br_ref[...]

    _rows_to_tiles(xp_ref, hn)

    lane = lax.broadcasted_iota(I32, (tm, LANES), 1)
    neg = jnp.float32(-jnp.inf)
    l = jnp.where(lane < N_EXPERTS, logits, neg)
    vals, idxs, hots = [], [], []
    for _ in range(TOP_K):
        m = jnp.max(l, axis=-1, keepdims=True)
        idx = jnp.min(jnp.where(l == m, lane, LANES), axis=-1, keepdims=True)
        hot = lane == idx
        l = jnp.where(hot, neg, l)
        vals.append(m)
        idxs.append(idx)
        hots.append(hot)
    exps = [jnp.exp(v - vals[0]) for v in vals]
    denom = exps[0] + exps[1] + exps[2] + exps[3]

    member = (hots[0] | hots[1] | hots[2] | hots[3]).astype(BF16)
    row = lax.broadcasted_iota(I32, (tm, tm), 0)
    col = lax.broadcasted_iota(I32, (tm, tm), 1)
    earlier = (row > col).astype(BF16)
    before = cnt_scr[...] + _dot(earlier, member)
    cnt_scr[...] = cnt_scr[...] + jnp.sum(member.astype(F32), axis=0, keepdims=True)
    cnt_ref[...] = cnt_scr[...].astype(I32)

    ti = jnp.zeros((tm, LANES), I32)
    tw = jnp.zeros((tm, LANES), F32)
    rk = jnp.zeros((tm, LANES), I32)
    for kk in range(TOP_K):
        sel = lane == kk
        rank = jnp.sum(jnp.where(hots[kk], before, 0.0), axis=-1, keepdims=True).astype(I32)
        ti = jnp.where(sel, idxs[kk], ti)
        tw = jnp.where(sel, exps[kk] / denom, tw)
        rk = jnp.where(sel, rank, rk)
    ti_ref[...] = ti
    tw_ref[...] = tw
    rk_ref[...] = rk


def _router_tile(t_all):
    for tm in (512, 384, 256, 128):
        if t_all % tm == 0:
            return tm
    raise ValueError(f"token count {t_all} must be a multiple of 128")


def _router(hbuf, wts):
    t_all = hbuf.shape[0]
    tm = _router_tile(t_all)
    consts = [wts['g2'], wts['w_router'], wts['b_router']]
    row = lambda w: pl.BlockSpec((tm, w), lambda i: (i, 0))
    return pl.pallas_call(
        functools.partial(_router_kernel, tm=tm),
        grid=(t_all // tm,),
        in_specs=[row(D_MODEL)] + [_const_spec(c.shape) for c in consts],
        out_specs=[pl.BlockSpec((tm * ROW_SUBLANES, LANES), lambda i: (i, 0)), row(LANES), row(LANES), row(LANES),
                   pl.BlockSpec((1, 1, LANES), lambda i: (i, 0, 0)),
                   pl.BlockSpec((1, LANES), lambda i: (0, 0))],
        out_shape=[jax.ShapeDtypeStruct((t_all * ROW_SUBLANES, LANES), F32),
                   jax.ShapeDtypeStruct((t_all, LANES), I32),
                   jax.ShapeDtypeStruct((t_all, LANES), F32),
                   jax.ShapeDtypeStruct((t_all, LANES), I32),
                   jax.ShapeDtypeStruct((t_all // tm, 1, LANES), I32),
                   jax.ShapeDtypeStruct((1, LANES), I32)],
        scratch_shapes=[pltpu.VMEM((1, LANES), F32)],
        compiler_params=_params(("arbitrary",)),
        name="router",
    )(hbuf, *consts)


def _row_tile(r):
    start = r * ROW_SUBLANES
    if not isinstance(r, int):
        start = pl.multiple_of(start, ROW_SUBLANES)
    return pl.ds(start, ROW_SUBLANES)


def _tile_of(ref, row):
    return ref.at[_row_tile(row)]


def _rows_from_tiles(ref, n_rows):
    return jnp.concatenate([ref[pl.ds(s, n_rows, stride=ROW_SUBLANES), :] for s in range(ROW_SUBLANES)], axis=1)


def _rows_to_tiles(ref, x):
    for s in range(ROW_SUBLANES):
        ref[pl.ds(s, x.shape[0], stride=ROW_SUBLANES), :] = x[:, s * LANES:(s + 1) * LANES]


def _tiles(row, n_rows):
    return pl.ds(pl.multiple_of(row * ROW_SUBLANES, ROW_SUBLANES), n_rows * ROW_SUBLANES)


class _Layout:
    def __init__(self, t_all):
        bm = EXPERT_BLOCK
        self.t_all = t_all
        self.tile = _router_tile(t_all)
        self.n_tiles = t_all // self.tile
        self.n_blocks = -(-(t_all * TOP_K + N_EXPERTS * (bm - 1)) // bm)
        self.sorted_rows = self.n_blocks * bm


def _strip_copies(src_ref, src_row, dst_ref, dst_row, n, sem):
    n_chunks = lax.shift_right_logical(n, STRIP_CHUNK_LOG2)

    def chunk(j, c):
        o = j * STRIP_CHUNK
        pltpu.make_async_copy(src_ref.at[_tiles(src_row + o, STRIP_CHUNK)], dst_ref.at[_tiles(dst_row + o, STRIP_CHUNK)],
                              sem).start()
        return c
    lax.fori_loop(0, n_chunks, chunk, 0)

    def single(j, c):
        o = n_chunks * STRIP_CHUNK + j
        pltpu.make_async_copy(src_ref.at[_tiles(src_row + o, 1)], dst_ref.at[_tiles(dst_row + o, 1)], sem).start()
        return c
    lax.fori_loop(0, n - n_chunks * STRIP_CHUNK, single, 0)


def _wait_rows(hbm_ref, vmem_ref, n_rows, sem):
    pltpu.make_async_copy(hbm_ref.at[pl.ds(0, n_rows * ROW_SUBLANES)], vmem_ref.at[pl.ds(0, n_rows * ROW_SUBLANES)],
                          sem).wait()


def _dispatch_kernel(soff_ref, scnt_ref, sgs_ref, lo_ref, hi_ref, xp_ref, lpr_ref, xs_hbm, loc, zero_scr, sem,
                     *, lay):
    tm = lay.tile
    n_pairs = TOP_K * tm
    g = pl.program_id(0)
    slot = g % 2

    def zero_fill(lo, hi, wait):
        def copies(first, count, size):
            def body(j, c):
                cp = pltpu.make_async_copy(zero_scr.at[pl.ds(0, size * ROW_SUBLANES)],
                                           xs_hbm.at[_tiles(first + j * size, size)], sem.at[2])
                if wait:
                    cp.wait()
                else:
                    cp.start()
                return c
            lax.fori_loop(0, count, body, 0)
        n = hi - lo
        n_big = lax.shift_right_logical(n, ZERO_ROWS_LOG2)
        copies(lo, n_big, ZERO_ROWS)
        done = n_big * ZERO_ROWS
        n_mid = lax.shift_right_logical(n - done, STRIP_CHUNK_LOG2)
        copies(lo + done, n_mid, STRIP_CHUNK)
        done = done + n_mid * STRIP_CHUNK
        copies(lo + done, n - done, 1)

    @pl.when(g == 0)
    def _():
        zero_scr[...] = jnp.zeros_like(zero_scr)
        for wait in (False, True):
            def expert(e, c, wait=wait):
                zero_fill(lo_ref[e], hi_ref[e], wait)
                return c
            lax.fori_loop(0, N_EXPERTS, expert, 0)
            zero_fill(hi_ref[N_EXPERTS - 1], lay.sorted_rows, wait)

    @pl.when(g >= 2)
    def _():
        _wait_rows(xs_hbm, loc.at[slot], n_pairs, sem.at[slot])

    x = _rows_from_tiles(xp_ref, tm).astype(BF16)
    pos = lax.broadcasted_iota(I32, (n_pairs, tm), 0)
    lpr = lpr_ref[0]
    hit = pos == lpr[0:1, :]
    for k in range(1, TOP_K):
        hit = hit | (pos == lpr[k:k + 1, :])
    _rows_to_tiles(loc.at[slot], _dot(hit.astype(BF16), x))

    def strip(e, c):
        s = g * N_EXPERTS + e
        _strip_copies(loc.at[slot], soff_ref[s], xs_hbm, sgs_ref[s], scnt_ref[s], sem.at[slot])
        return c
    lax.fori_loop(0, N_EXPERTS, strip, 0)

    @pl.when(g == lay.n_tiles - 1)
    def _():
        _wait_rows(xs_hbm, loc.at[slot], n_pairs, sem.at[slot])
        if lay.n_tiles > 1:
            _wait_rows(xs_hbm, loc.at[1 - slot], n_pairs, sem.at[1 - slot])


def _dispatch(xp, lp_rows, soff, scnt, sgs, pad_lo, pad_hi, lay):
    tm = lay.tile
    grid_spec = pltpu.PrefetchScalarGridSpec(
        num_scalar_prefetch=5,
        grid=(lay.n_tiles,),
        in_specs=[pl.BlockSpec((tm * ROW_SUBLANES, LANES), lambda g, *_: (g, 0)),
                  pl.BlockSpec((1, ROW_SUBLANES, tm), lambda g, *_: (g, 0, 0))],
        out_specs=pl.BlockSpec(memory_space=pl.ANY),
        scratch_shapes=[pltpu.VMEM((2, TOP_K * tm * ROW_SUBLANES, LANES), F32),
                        pltpu.VMEM((ZERO_ROWS * ROW_SUBLANES, LANES), F32),
                        pltpu.SemaphoreType.DMA((3,))],
    )
    return pl.pallas_call(
        functools.partial(_dispatch_kernel, lay=lay),
        grid_spec=grid_spec,
        out_shape=jax.ShapeDtypeStruct((lay.sorted_rows * ROW_SUBLANES, LANES), F32),
        compiler_params=_params(("arbitrary",)),
        name="dispatch",
    )(soff, scnt, sgs, pad_lo, pad_hi, xp, lp_rows)


def _experts_kernel(be_ref, nv_ref, xs_ref, wg_ref, wu_ref, wd_ref, bg_ref, bu_ref, bd_ref, ys_ref, wbf):
    bm = EXPERT_BLOCK
    i = pl.program_id(0)
    n_valid = nv_ref[0]

    @pl.when(i < n_valid)
    def _():
        @pl.when((i == 0) | (be_ref[i] != be_ref[jnp.maximum(i - 1, 0)]))
        def _():
            for w, w_ref in enumerate((wg_ref, wu_ref, wd_ref)):
                for c in range(0, D_MODEL, WEIGHT_CAST_ROWS):
                    wbf[w, c:c + WEIGHT_CAST_ROWS, :] = w_ref[0, c:c + WEIGHT_CAST_ROWS, :].astype(BF16)

        part = bm // EXPERT_PARTS
        for j in range(EXPERT_PARTS):
            tiles = pl.ds(j * part * ROW_SUBLANES, part * ROW_SUBLANES)
            x = _rows_from_tiles(xs_ref.at[tiles], part).astype(BF16)
            g = jnp.minimum(_dot(x, wbf[0]) + bg_ref[0], SWIGLU_LIMIT)
            u = jnp.clip(_dot(x, wbf[1]) + bu_ref[0], -SWIGLU_LIMIT, SWIGLU_LIMIT)
            mid = ((u + 1.0) * (g * _sigmoid(SWIGLU_ALPHA * g))).astype(BF16)
            _rows_to_tiles(ys_ref.at[tiles], _dot(mid, wbf[2]) + bd_ref[0])

    @pl.when(i >= n_valid)
    def _():
        ys_ref[...] = jnp.zeros_like(ys_ref)


def _experts(block_e, n_valid, xs, wts, lay):
    bm = EXPERT_BLOCK
    tile_rows = bm * ROW_SUBLANES
    wspec = pl.BlockSpec((1, D_MODEL, D_FF), lambda i, be, nv: (be[i], 0, 0))
    bspec = pl.BlockSpec((1, 1, D_FF), lambda i, be, nv: (be[i], 0, 0))
    grid_spec = pltpu.PrefetchScalarGridSpec(
        num_scalar_prefetch=2,
        grid=(lay.n_blocks,),
        in_specs=[pl.BlockSpec((tile_rows, LANES), lambda i, be, nv: (jnp.minimum(i, nv[0] - 1), 0)),
                  wspec, wspec, wspec, bspec, bspec, bspec],
        out_specs=pl.BlockSpec((tile_rows, LANES), lambda i, be, nv: (i, 0)),
        scratch_shapes=[pltpu.VMEM((3, D_MODEL, D_FF), BF16)],
    )
    return pl.pallas_call(
        functools.partial(_experts_kernel),
        grid_spec=grid_spec,
        out_shape=jax.ShapeDtypeStruct((lay.sorted_rows * ROW_SUBLANES, LANES), F32),
        compiler_params=_params(("arbitrary",)),
        name="experts",
    )(block_e, n_valid, xs, wts['w_gate'], wts['w_up'], wts['w_down'], wts['b_gate'], wts['b_up'], wts['b_down'])


def _combine_kernel(soff_ref, scnt_ref, sgs_ref, h_ref, tw_ref, lpc_ref, p_ref, ys_hbm, g3_ref, wpg_ref, wpp_ref,
                    gf_ref, yp_ref, ysm_ref, loc, sem, *, lay, n_sample):
    tm = lay.tile
    n_pairs = TOP_K * tm
    g = pl.program_id(0)
    slot = g % 2

    def fetch(tile, s):
        def strip(e, c):
            i = tile * N_EXPERTS + e
            _strip_copies(ys_hbm, sgs_ref[i], loc.at[s], soff_ref[i], scnt_ref[i], sem.at[s])
            return c
        lax.fori_loop(0, N_EXPERTS, strip, 0)

    @pl.when(g == 0)
    def _():
        fetch(0, 0)

    @pl.when(g + 1 < lay.n_tiles)
    def _():
        fetch(g + 1, 1 - slot)

    _wait_rows(ys_hbm, loc.at[slot], n_pairs, sem.at[slot])
    y_rows = _rows_from_tiles(loc.at[slot], n_pairs).astype(BF16)
    tw = tw_ref[...]
    lpc = lpc_ref[...]
    pos = lax.broadcasted_iota(I32, (tm, n_pairs), 1)
    weight = jnp.zeros((tm, n_pairs), F32)
    for k in range(TOP_K):
        weight = jnp.where(pos == lpc[:, k:k + 1], tw[:, k:k + 1], weight)
    w_hi = weight.astype(BF16)
    w_lo = (weight - w_hi.astype(F32)).astype(BF16)
    y = _dot(w_hi, y_rows) + _dot(w_lo, y_rows)
    h2 = h_ref[...] + y
    gate = _sigmoid(_dot(_rms(h2, g3_ref[...]).astype(BF16), wpg_ref[...]))
    h3 = h2 + gate * _dot(p_ref[...].astype(BF16), wpp_ref[...])
    out = _rms(h3, gf_ref[...])
    yp_ref[...] = out

    @pl.when(g == lay.n_tiles - 1)
    def _():
        ysm_ref[...] = out[tm - n_sample:, :]


def _combine(hbuf, tw, lp_cols, p_all, ys, soff, scnt, sgs, n_prompt, wts, lay):
    tm = lay.tile
    n_sample = lay.t_all - n_prompt
    assert 0 < n_sample <= tm and (lay.n_tiles - 1) * tm < n_prompt
    consts = [wts['g3'], wts['w_ple_gate'], wts['w_ple_proj'], wts['g_final']]
    row = lambda w: pl.BlockSpec((tm, w), lambda g, *_: (g, 0))
    grid_spec = pltpu.PrefetchScalarGridSpec(
        num_scalar_prefetch=3,
        grid=(lay.n_tiles,),
        in_specs=[row(D_MODEL), row(LANES), row(LANES), row(D_PLE), pl.BlockSpec(memory_space=pl.ANY)]
        + [pl.BlockSpec(c.shape, lambda g, *_: (0, 0), pipeline_mode=pl.Buffered(1)) for c in consts],
        out_specs=[row(D_MODEL), pl.BlockSpec((n_sample, D_MODEL), lambda g, *_: (0, 0))],
        scratch_shapes=[pltpu.VMEM((2, TOP_K * tm * ROW_SUBLANES, LANES), F32), pltpu.SemaphoreType.DMA((2,))],
    )
    return pl.pallas_call(
        functools.partial(_combine_kernel, lay=lay, n_sample=n_sample),
        grid_spec=grid_spec,
        out_shape=[jax.ShapeDtypeStruct((n_prompt, D_MODEL), F32), jax.ShapeDtypeStruct((n_sample, D_MODEL), F32)],
        compiler_params=_params(("arbitrary",)),
        name="combine",
    )(soff, scnt, sgs, hbuf, tw, lp_cols, p_all, ys, *consts)


def _prepare_weights(norm1_g, w_in, w_gk2, b_gk, gla_norm_g, v_norm_g, v_norm_b, w_sp, b_sp, w_o, norm2_g,
                     w_router, b_router, w_gate, b_gate, w_up, b_up, w_down, b_down, norm3_g, w_ple_gate,
                     w_ple_proj, final_g):
    o_gk = 2 * GLA_DK_T + 2 * GLA_DV_T
    w_main = jnp.concatenate([w_in[:, :o_gk], w_in[:, o_gk + GK_RANK:]], axis=1).astype(BF16)
    w_gk1 = jnp.pad(w_in[:, o_gk:o_gk + GK_RANK], ((0, 0), (0, LANES - GK_RANK))).astype(BF16)
    w_gk2p = jnp.pad(w_gk2, ((0, LANES - GK_RANK), (0, 0))).astype(BF16)
    row = lambda z: z.reshape(1, -1).astype(F32)
    return dict(
        g1=row(norm1_g), w_main=w_main, w_gk1=w_gk1, w_gk2=w_gk2p, b_gk=row(b_gk), gla_g=row(gla_norm_g),
        vn_g=row(v_norm_g), vn_b=row(v_norm_b), w_sp=w_sp, b_sp_t=b_sp.T,
        mix_w0=row(jnp.repeat(w_sp[:, 0, 0], MIX_DH)), mix_b0=row(jnp.repeat(b_sp[:, 0], MIX_DH)),
        w_o=w_o.astype(BF16), g2=row(norm2_g),
        w_router=jnp.pad(w_router, ((0, 0), (0, LANES - N_EXPERTS))),
        b_router=jnp.pad(row(b_router), ((0, 0), (0, LANES - N_EXPERTS))),
        w_gate=w_gate, w_up=w_up, w_down=w_down,
        b_gate=b_gate.reshape(N_EXPERTS, 1, D_FF), b_up=b_up.reshape(N_EXPERTS, 1, D_FF),
        b_down=b_down.reshape(N_EXPERTS, 1, D_MODEL),
        g3=row(norm3_g), w_ple_gate=w_ple_gate.astype(BF16), w_ple_proj=w_ple_proj.astype(BF16),
        g_final=row(final_g))


def _routing_tables(top_i, rank, counts, counts_before, lay):
    bm = EXPERT_BLOCK
    tm = lay.tile
    padded = (counts + bm - 1) // bm * bm
    pad_end = jnp.cumsum(padded)
    pad_start = pad_end - padded
    tile_counts = jnp.concatenate([counts_before[1:], counts[None]], axis=0) - counts_before
    local_off = jnp.cumsum(tile_counts, axis=1) - tile_counts
    global_start = pad_start[None, :] + counts_before
    experts = jnp.arange(N_EXPERTS, dtype=I32)
    shift = jnp.repeat(local_off - counts_before, tm, axis=0)
    lp = jnp.sum(jnp.where(top_i[:, :, None] == experts, shift[:, None, :], 0), axis=-1) + rank
    lp = lp.astype(I32)
    lp_cols = jnp.pad(lp, ((0, 0), (0, LANES - TOP_K)))
    lp_rows = jnp.pad(lp.reshape(lay.n_tiles, tm, TOP_K).transpose(0, 2, 1), ((0, 0), (0, ROW_SUBLANES - TOP_K), (0, 0)))
    n_valid = (pad_end[-1] // bm).astype(I32)
    blk = jnp.minimum(jnp.arange(lay.n_blocks, dtype=I32), n_valid - 1)
    block_e = jnp.sum(pad_end[None, :] <= (blk * bm)[:, None], axis=1).astype(I32)
    flat = lambda z: z.reshape(-1).astype(I32)
    return dict(lp_cols=lp_cols, lp_rows=lp_rows, soff=flat(local_off), scnt=flat(tile_counts), sgs=flat(global_start),
                pad_lo=(pad_start + counts).astype(I32), pad_hi=pad_end.astype(I32), block_e=block_e,
                n_valid=n_valid.reshape(1))


def _layer(x_prompt, x_sample, state, p_prompt, p_sample, wts):
    n_p, len_p, _ = x_prompt.shape
    n_s, len_s, _ = x_sample.shape
    assert len_s == 1, "the sample group carries one new token per sequence"
    t_p = n_p * len_p
    t_all = t_p + n_s

    xs = x_sample.reshape(n_s, D_MODEL)
    q, k, a, v, og, ob, vn = _mixer_sample_in(xs, wts)
    st_sample, o = _state_update(state, q, k, a, v)
    h_sample = _mixer_sample_out(xs, o, og, ob, wts)
    hbuf, st_prompt = _mixer_prompt(x_prompt, h_sample, wts)

    lay = _Layout(t_all)
    xp, top_i, top_w, rank, counts_before, counts = _router(hbuf, wts)
    tab = _routing_tables(top_i[:, :TOP_K], rank[:, :TOP_K], counts[0, :N_EXPERTS],
                          counts_before[:, 0, :N_EXPERTS], lay)
    x_sorted = _dispatch(xp, tab['lp_rows'], tab['soff'], tab['scnt'], tab['sgs'], tab['pad_lo'], tab['pad_hi'], lay)
    y_sorted = _experts(tab['block_e'], tab['n_valid'], x_sorted, wts, lay)
    p_all = jnp.concatenate([p_prompt.reshape(t_p, D_PLE), p_sample.reshape(n_s, D_PLE)], axis=0)
    y_prompt, y_sample = _combine(hbuf, top_w, tab['lp_cols'], p_all, y_sorted, tab['soff'], tab['scnt'], tab['sgs'],
                                  t_p, wts, lay)
    return (y_prompt.reshape(n_p, len_p, D_MODEL), y_sample.reshape(n_s, len_s, D_MODEL),
            st_prompt, st_sample, vn.reshape(n_s, len_s, D_GMLP))


def kernel(x_prompt, x_sample, state_gla, p_prompt, p_sample, norm1_g, w_in, w_gk2, b_gk, gla_norm_g, v_norm_g,
           v_norm_b, w_sp, b_sp, w_o, norm2_g, w_router, b_router, w_gate, b_gate, w_up, b_up, w_down, b_down,
           norm3_g, w_ple_gate, w_ple_proj, final_g):
    assert w_in.shape[0] == 1, "single-layer trunk"
    wts = _prepare_weights(norm1_g[0], w_in[0], w_gk2[0], b_gk[0], gla_norm_g[0], v_norm_g[0], v_norm_b[0],
                           w_sp[0], b_sp[0], w_o[0], norm2_g[0], w_router[0], b_router[0], w_gate[0], b_gate[0],
                           w_up[0], b_up[0], w_down[0], b_down[0], norm3_g[0], w_ple_gate[0], w_ple_proj[0],
                           final_g)
    y_p, y_s, st_p, st_s, vn_s = _layer(x_prompt, x_sample, state_gla[0], p_prompt[0], p_sample[0], wts)
    return (y_p, y_s, st_p[None], st_s[None].astype(state_gla.dtype), vn_s[None])
```

```python
import functools

import jax
import jax.numpy as jnp
from jax import lax
from jax.experimental import pallas as pl
from jax.experimental.pallas import tpu as pltpu

F32 = jnp.float32
BF16 = jnp.bfloat16
I32 = jnp.int32
U32 = jnp.uint32

D_MODEL = 1024
GLA_HEADS = 4
GLA_DK = 128
GLA_DV = 256
GLA_DK_T = GLA_HEADS * GLA_DK
GLA_DV_T = GLA_HEADS * GLA_DV
GK_RANK = 16
GATE_NORMALIZER = 16.0
GLA_CHUNK = 64
GLA_CHUNK_LOG2 = 6
D_GMLP = 1024
MIX_HEADS = 4
MIX_DH = D_GMLP // MIX_HEADS
MIX_CHUNK = 128
MIX_CHUNK_LOG2 = 7
assert 1 << GLA_CHUNK_LOG2 == GLA_CHUNK and 1 << MIX_CHUNK_LOG2 == MIX_CHUNK
N_EXPERTS = 32
TOP_K = 4
D_FF = 1024
SWIGLU_LIMIT = 7.0
SWIGLU_ALPHA = 1.702
D_PLE = 256
EPS = 1e-6

LANES = 128
ROW_SUBLANES = D_MODEL // LANES
VMEM_LIMIT_BYTES = 56 * 1024 * 1024

OFF_Q, OFF_K, OFF_V, OFF_R, OFF_U, OFF_VG, OFF_GA, OFF_GB, N_MAIN = (
    0, 512, 1024, 2048, 3072, 4096, 5120, 6144, 7168)

MIXER_TILE = 512
MIXER_PARTS = 2
STATE_TOKENS = 8
EXPERT_BLOCK = 512
EXPERT_PARTS = 2
WEIGHT_CAST_ROWS = 256
STRIP_CHUNK = 8
STRIP_CHUNK_LOG2 = 3
ZERO_ROWS = 64
ZERO_ROWS_LOG2 = 6
assert 1 << STRIP_CHUNK_LOG2 == STRIP_CHUNK and 1 << ZERO_ROWS_LOG2 == ZERO_ROWS


def _dot(a, b):
    return jnp.dot(a, b, preferred_element_type=F32)


def _dot_nt(a, b):
    return lax.dot_general(a, b, (((1,), (1,)), ((), ())), preferred_element_type=F32)


def _dot_tn(a, b):
    return lax.dot_general(a, b, (((0,), (0,)), ((), ())), preferred_element_type=F32)


def _rms(x, g):
    return x * lax.rsqrt(jnp.mean(x * x, axis=-1, keepdims=True) + EPS) * g


def _sigmoid(x):
    return 1.0 / (1.0 + jnp.exp(-x))


def _gelu(x):
    return 0.5 * x * (1.0 + lax.erf(x * (2.0 ** -0.5)))


def _log_sigmoid(x):
    return jnp.minimum(x, 0.0) - jnp.log1p(jnp.exp(-jnp.abs(x)))


def _split3(x):
    hi = x.astype(BF16)
    r1 = x - hi.astype(F32)
    mid = r1.astype(BF16)
    lo = (r1 - mid.astype(F32)).astype(BF16)
    return hi, mid, lo


def _const_spec(shape):
    nd = len(shape)
    return pl.BlockSpec(shape, lambda *_: (0,) * nd, pipeline_mode=pl.Buffered(1))


def _params(sem):
    return pltpu.CompilerParams(dimension_semantics=sem, vmem_limit_bytes=VMEM_LIMIT_BYTES)


def _project_gla_inputs(n, wm_ref, wgk1_ref, wgk2_ref, bgk_ref):
    q = _dot(n, wm_ref[:, OFF_Q:OFF_K]) * (GLA_DK ** -0.5)
    k = _dot(n, wm_ref[:, OFF_K:OFF_V])
    v = _dot(n, wm_ref[:, OFF_V:OFF_R])
    gk = _dot(n, wgk1_ref[...]).astype(BF16)
    log_a = _log_sigmoid(_dot(gk, wgk2_ref[...]) + bgk_ref[...]) * (1.0 / GATE_NORMALIZER)
    return q, k, v, log_a


def _gmlp_inputs(n, wm_ref, vng_ref, vnb_ref):
    u = _gelu(_dot(n, wm_ref[:, OFF_U:OFF_VG]))
    vg = _gelu(_dot(n, wm_ref[:, OFF_VG:OFF_GA]))
    mu = jnp.mean(vg, axis=-1, keepdims=True)
    vc = vg - mu
    var = jnp.mean(vc * vc, axis=-1, keepdims=True)
    vn = vc * lax.rsqrt(var + EPS) * vng_ref[...] + vnb_ref[...]
    ug = u * _sigmoid(_dot(n, wm_ref[:, OFF_GB:N_MAIN]))
    return ug, vn


def _gla_out_gate(n, wm_ref):
    r = _dot(n, wm_ref[:, OFF_R:OFF_U])
    ga = _dot(n, wm_ref[:, OFF_GA:OFF_GB])
    return r * _sigmoid(r) * _sigmoid(ga)


def _head_rms(o, g):
    return o * lax.rsqrt(jnp.mean(o * o, axis=-1, keepdims=True) + EPS) * g


def _mixer_prompt_kernel(x_ref, hs_ref, *refs, tm, n_seq, n_tiles):
    b = pl.program_id(0)
    t = pl.program_id(1)
    h_ref = refs[11]

    @pl.when(b < n_seq)
    def _():
        _mixer_prompt_tile(x_ref, *refs, tm=tm, n_tiles=n_tiles)

    @pl.when((b == n_seq) & (t == 0))
    def _():
        h_ref[0:hs_ref.shape[0], :] = hs_ref[...]


def _mixer_prompt_tile(x_ref, g1_ref, wm_ref, wgk1_ref, wgk2_ref, bgk_ref, glag_ref, vng_ref, vnb_ref,
                       wsp_ref, bspt_ref, wo_ref,
                       h_ref, st_ref,
                       n_scr, oa_scr, s_scr, *, tm, n_tiles):
    t = pl.program_id(1)

    @pl.when(t == 0)
    def _():
        s_scr[...] = jnp.zeros_like(s_scr)

    pm = tm // MIXER_PARTS
    for part in range(MIXER_PARTS):
        rows = pl.ds(part * pm, pm)
        _mixer_prompt_part(x_ref.at[0, rows], g1_ref, wm_ref, wgk1_ref, wgk2_ref, bgk_ref, glag_ref, vng_ref, vnb_ref,
                           wsp_ref, bspt_ref, wo_ref, h_ref.at[rows], n_scr.at[rows], oa_scr.at[rows], s_scr, tm=pm)

    @pl.when(t == n_tiles - 1)
    def _():
        for h in range(GLA_HEADS):
            st_ref[0, h] = s_scr[h].T


def _mixer_prompt_part(x_ref, g1_ref, wm_ref, wgk1_ref, wgk2_ref, bgk_ref, glag_ref, vng_ref, vnb_ref,
                       wsp_ref, bspt_ref, wo_ref, h_ref, n_scr, oa_scr, s_scr, *, tm):
    x = x_ref[...]
    n = _rms(x, g1_ref[...]).astype(BF16)
    n_scr[...] = n
    q, k, v, log_a = _project_gla_inputs(n, wm_ref, wgk1_ref, wgk2_ref, bgk_ref)
    v = v.astype(BF16)

    row = lax.broadcasted_iota(I32, (tm, tm), 0)
    col = lax.broadcasted_iota(I32, (tm, tm), 1)
    same_chunk = lax.shift_right_logical(row, GLA_CHUNK_LOG2) == lax.shift_right_logical(col, GLA_CHUNK_LOG2)
    causal = same_chunk & (row >= col)
    tri = causal.astype(BF16)
    blk = same_chunk.astype(BF16)
    hi, mid, lo = _split3(log_a)
    b = _dot(tri, hi) + _dot(tri, mid) + _dot(tri, lo)
    b_end = _dot(blk, hi) + _dot(blk, mid) + _dot(blk, lo)
    qe = (q * jnp.exp(b)).astype(BF16)
    ke = (k * jnp.exp(-b)).astype(BF16)
    kd = (k * jnp.exp(b_end - b)).astype(BF16)
    dec = jnp.exp(b_end)
    glag = glag_ref[...]
    for h in range(GLA_HEADS):
        ks = slice(h * GLA_DK, (h + 1) * GLA_DK)
        vs = slice(h * GLA_DV, (h + 1) * GLA_DV)
        vh = v[:, vs]
        att = jnp.where(causal, _dot_nt(qe[:, ks], ke[:, ks]), 0.0).astype(BF16)
        o_intra = _dot(att, vh)
        s_t = s_scr[h]
        o_inter = []
        for c in range(tm // GLA_CHUNK):
            rs = slice(c * GLA_CHUNK, (c + 1) * GLA_CHUNK)
            o_inter.append(_dot_nt(qe[rs, ks], s_t.astype(BF16)))
            s_t = s_t * dec[c * GLA_CHUNK:c * GLA_CHUNK + 1, ks] + _dot_tn(vh[rs], kd[rs, ks])
        s_scr[h] = s_t
        oa_scr[:, vs] = _head_rms(o_intra + jnp.concatenate(o_inter, axis=0), glag)

    n = n_scr[...]
    oa_scr[...] = oa_scr[...] * _gla_out_gate(n, wm_ref)
    ug, vn = _gmlp_inputs(n, wm_ref, vng_ref, vnb_ref)
    vn = vn.astype(BF16)
    reps = tm // MIX_CHUNK
    mix_mask = (lax.shift_right_logical(row, MIX_CHUNK_LOG2) == lax.shift_right_logical(col, MIX_CHUNK_LOG2)) & (
        row >= col)
    for h in range(MIX_HEADS):
        w = jnp.where(mix_mask, jnp.tile(wsp_ref[h], (reps, reps)), 0.0).astype(BF16)
        cs = slice(h * MIX_DH, (h + 1) * MIX_DH)
        mix = _dot(w, vn[:, cs]) + jnp.tile(bspt_ref[:, h:h + 1], (reps, 1))
        oa_scr[:, cs] = oa_scr[:, cs] + ug[:, cs] * mix

    h_ref[...] = x + _dot(oa_scr[...].astype(BF16), wo_ref[...])


def _mixer_prompt(x, h_sample, wts):
    n_seq, seq_len, _ = x.shape
    n_s = h_sample.shape[0]
    tm = min(MIXER_TILE, seq_len)
    assert seq_len % tm == 0 and tm % MIX_CHUNK == 0 and n_s <= tm
    n_tiles = seq_len // tm
    consts = [wts['g1'], wts['w_main'], wts['w_gk1'], wts['w_gk2'], wts['b_gk'], wts['gla_g'], wts['vn_g'],
              wts['vn_b'], wts['w_sp'], wts['b_sp_t'], wts['w_o']]
    kern = functools.partial(_mixer_prompt_kernel, tm=tm, n_seq=n_seq, n_tiles=n_tiles)
    last = n_seq - 1

    def prompt_tile(b, t):
        return jnp.minimum(b, last), jnp.where(b < n_seq, t, n_tiles - 1)

    def x_map(b, t):
        bb, tt = prompt_tile(b, t)
        return bb, tt, 0

    def h_map(b, t):
        return jnp.where(b < n_seq, b * n_tiles + t, n_seq * n_tiles), 0

    return pl.pallas_call(
        kern,
        grid=(n_seq + 1, n_tiles),
        in_specs=[pl.BlockSpec((1, tm, D_MODEL), x_map), _const_spec(h_sample.shape)]
        + [_const_spec(c.shape) for c in consts],
        out_specs=[pl.BlockSpec((tm, D_MODEL), h_map),
                   pl.BlockSpec((1, GLA_HEADS, GLA_DK, GLA_DV), lambda b, t: (jnp.minimum(b, last), 0, 0, 0))],
        out_shape=[jax.ShapeDtypeStruct((n_seq * seq_len + n_s, D_MODEL), F32),
                   jax.ShapeDtypeStruct((n_seq, GLA_HEADS, GLA_DK, GLA_DV), F32)],
        scratch_shapes=[pltpu.VMEM((tm, D_MODEL), BF16),
                        pltpu.VMEM((tm, D_MODEL), F32),
                        pltpu.VMEM((GLA_HEADS, GLA_DV, GLA_DK), F32)],
        compiler_params=_params(("arbitrary", "arbitrary")),
        name="mixer_prompt",
    )(x, h_sample, *consts)


def _mixer_sample_in_kernel(x_ref, g1_ref, wm_ref, wgk1_ref, wgk2_ref, bgk_ref, vng_ref, vnb_ref,
                            mixw_ref, mixb_ref,
                            q_ref, k_ref, a_ref, v_ref, og_ref, ob_ref, vn_ref):
    n = _rms(x_ref[...], g1_ref[...]).astype(BF16)
    q, k, v, log_a = _project_gla_inputs(n, wm_ref, wgk1_ref, wgk2_ref, bgk_ref)
    q_ref[...] = q
    k_ref[...] = k
    a_ref[...] = jnp.exp(log_a)
    v_ref[...] = v
    og_ref[...] = _gla_out_gate(n, wm_ref)
    ug, vn = _gmlp_inputs(n, wm_ref, vng_ref, vnb_ref)
    vn_ref[...] = vn
    ob_ref[...] = ug * (mixw_ref[...] * vn + mixb_ref[...])


def _mixer_sample_in(x2, wts):
    n_seq = x2.shape[0]
    consts = [wts['g1'], wts['w_main'], wts['w_gk1'], wts['w_gk2'], wts['b_gk'], wts['vn_g'], wts['vn_b'],
              wts['mix_w0'], wts['mix_b0']]
    widths = [GLA_DK_T, GLA_DK_T, GLA_DK_T, GLA_DV_T, D_MODEL, D_MODEL, D_GMLP]
    return pl.pallas_call(
        _mixer_sample_in_kernel,
        grid=(1,),
        in_specs=[_const_spec(x2.shape)] + [_const_spec(c.shape) for c in consts],
        out_specs=[pl.BlockSpec((n_seq, w), lambda i: (0, 0)) for w in widths],
        out_shape=[jax.ShapeDtypeStruct((n_seq, w), F32) for w in widths],
        compiler_params=_params(("arbitrary",)),
        name="mixer_sample_in",
    )(x2, *consts)


def _state_update_kernel(s_ref, q_ref, k_ref, a_ref, v_ref, so_ref, o_ref):
    for j in range(STATE_TOKENS):
        for h in range(GLA_HEADS):
            ks = slice(h * GLA_DK, (h + 1) * GLA_DK)
            vs = slice(h * GLA_DV, (h + 1) * GLA_DV)
            s_new = a_ref[0, ks, j:j + 1] * s_ref[j, h] + k_ref[0, ks, j:j + 1] * v_ref[j:j + 1, vs]
            so_ref[j, h] = s_new
            o_ref[j:j + 1, vs] = jnp.sum(q_ref[0, ks, j:j + 1] * s_new, axis=0, keepdims=True)


def _state_update(state, q, k, a, v):
    n_seq = state.shape[0]
    tb = STATE_TOKENS
    assert n_seq % tb == 0

    def cols(z):
        return z.reshape(n_seq // tb, tb, GLA_DK_T).transpose(0, 2, 1)

    col_spec = pl.BlockSpec((1, GLA_DK_T, tb), lambda i: (i, 0, 0))
    st_spec = pl.BlockSpec((tb, GLA_HEADS, GLA_DK, GLA_DV), lambda i: (i, 0, 0, 0))
    row_spec = pl.BlockSpec((tb, GLA_DV_T), lambda i: (i, 0))
    return pl.pallas_call(
        _state_update_kernel,
        grid=(n_seq // tb,),
        in_specs=[st_spec, col_spec, col_spec, col_spec, row_spec],
        out_specs=[st_spec, row_spec],
        out_shape=[jax.ShapeDtypeStruct(state.shape, F32), jax.ShapeDtypeStruct((n_seq, GLA_DV_T), F32)],
        compiler_params=_params(("arbitrary",)),
        name="state_update",
    )(state, cols(q), cols(k), cols(a), v)


def _mixer_sample_out_kernel(x_ref, o_ref, og_ref, ob_ref, glag_ref, wo_ref, h_ref):
    glag = glag_ref[...]
    parts = []
    for h in range(GLA_HEADS):
        vs = slice(h * GLA_DV, (h + 1) * GLA_DV)
        parts.append(_head_rms(o_ref[:, vs], glag))
    merged = jnp.concatenate(parts, axis=1) * og_ref[...] + ob_ref[...]
    h_ref[...] = x_ref[...] + _dot(merged.astype(BF16), wo_ref[...])


def _mixer_sample_out(x2, o, og, ob, wts):
    n_seq = x2.shape[0]
    consts = [x2, o, og, ob, wts['gla_g'], wts['w_o']]
    return pl.pallas_call(
        _mixer_sample_out_kernel,
        grid=(1,),
        in_specs=[_const_spec(c.shape) for c in consts],
        out_specs=pl.BlockSpec((n_seq, D_MODEL), lambda i: (0, 0)),
        out_shape=jax.ShapeDtypeStruct((n_seq, D_MODEL), F32),
        compiler_params=_params(("arbitrary",)),
        name="mixer_sample_out",
    )(*consts)


def _router_kernel(h_ref, g2_ref, wr_ref, br_ref, xp_ref, ti_ref, tw_ref, rk_ref, cb_ref, cnt_ref, cnt_scr, *, tm):
    i = pl.program_id(0)

    @pl.when(i == 0)
    def _():
        cnt_scr[...] = jnp.zeros_like(cnt_scr)

    cb_ref[0] = cnt_scr[...].astype(I32)

    hn = _rms(h_ref[...], g2_ref[...])
    hn_hi = hn.astype(BF16)
    hn_lo = (hn - hn_hi.astype(F32)).astype(BF16)
    w = wr_ref[...]
    w_hi = w.astype(BF16)
    w_lo = (w - w_hi.astype(F32)).astype(BF16)
    logits = _dot(hn_hi, w_hi) + _dot(hn_lo, w_hi) + _dot(hn_hi, w_lo) + br_ref[...]

    _rows_to_tiles(xp_ref, hn)

    lane = lax.broadcasted_iota(I32, (tm, LANES), 1)
    neg = jnp.float32(-jnp.inf)
    l = jnp.where(lane < N_EXPERTS, logits, neg)
    vals, idxs, hots = [], [], []
    for _ in range(TOP_K):
        m = jnp.max(l, axis=-1, keepdims=True)
        idx = jnp.min(jnp.where(l == m, lane, LANES), axis=-1, keepdims=True)
        hot = lane == idx
        l = jnp.where(hot, neg, l)
        vals.append(m)
        idxs.append(idx)
        hots.append(hot)
    exps = [jnp.exp(v - vals[0]) for v in vals]
    denom = exps[0] + exps[1] + exps[2] + exps[3]

    member = (hots[0] | hots[1] | hots[2] | hots[3]).astype(BF16)
    row = lax.broadcasted_iota(I32, (tm, tm), 0)
    col = lax.broadcasted_iota(I32, (tm, tm), 1)
    earlier = (row > col).astype(BF16)
    before = cnt_scr[...] + _dot(earlier, member)
    cnt_scr[...] = cnt_scr[...] + jnp.sum(member.astype(F32), axis=0, keepdims=True)
    cnt_ref[...] = cnt_scr[...].astype(I32)

    ti = jnp.zeros((tm, LANES), I32)
    tw = jnp.zeros((tm, LANES), F32)
    rk = jnp.zeros((tm, LANES), I32)
    for kk in range(TOP_K):
        sel = lane == kk
        rank = jnp.sum(jnp.where(hots[kk], before, 0.0), axis=-1, keepdims=True).astype(I32)
        ti = jnp.where(sel, idxs[kk], ti)
        tw = jnp.where(sel, exps[kk] / denom, tw)
        rk = jnp.where(sel, rank, rk)
    ti_ref[...] = ti
    tw_ref[...] = tw
    rk_ref[...] = rk


def _router_tile(t_all):
    for tm in (512, 384, 256, 128):
        if t_all % tm == 0:
            return tm
    raise ValueError(f"token count {t_all} must be a multiple of 128")


def _router(hbuf, wts):
    t_all = hbuf.shape[0]
    tm = _router_tile(t_all)
    consts = [wts['g2'], wts['w_router'], wts['b_router']]
    row = lambda w: pl.BlockSpec((tm, w), lambda i: (i, 0))
    return pl.pallas_call(
        functools.partial(_router_kernel, tm=tm),
        grid=(t_all // tm,),
        in_specs=[row(D_MODEL)] + [_const_spec(c.shape) for c in consts],
        out_specs=[pl.BlockSpec((tm * ROW_SUBLANES, LANES), lambda i: (i, 0)), row(LANES), row(LANES), row(LANES),
                   pl.BlockSpec((1, 1, LANES), lambda i: (i, 0, 0)),
                   pl.BlockSpec((1, LANES), lambda i: (0, 0))],
        out_shape=[jax.ShapeDtypeStruct((t_all * ROW_SUBLANES, LANES), F32),
                   jax.ShapeDtypeStruct((t_all, LANES), I32),
                   jax.ShapeDtypeStruct((t_all, LANES), F32),
                   jax.ShapeDtypeStruct((t_all, LANES), I32),
                   jax.ShapeDtypeStruct((t_all // tm, 1, LANES), I32),
                   jax.ShapeDtypeStruct((1, LANES), I32)],
        scratch_shapes=[pltpu.VMEM((1, LANES), F32)],
        compiler_params=_params(("arbitrary",)),
        name="router",
    )(hbuf, *consts)


def _row_tile(r):
    start = r * ROW_SUBLANES
    if not isinstance(r, int):
        start = pl.multiple_of(start, ROW_SUBLANES)
    return pl.ds(start, ROW_SUBLANES)


def _tile_of(ref, row):
    return ref.at[_row_tile(row)]


def _rows_from_tiles(ref, n_rows):
    return jnp.concatenate([ref[pl.ds(s, n_rows, stride=ROW_SUBLANES), :] for s in range(ROW_SUBLANES)], axis=1)


def _rows_to_tiles(ref, x):
    for s in range(ROW_SUBLANES):
        ref[pl.ds(s, x.shape[0], stride=ROW_SUBLANES), :] = x[:, s * LANES:(s + 1) * LANES]


def _tiles(row, n_rows):
    return pl.ds(pl.multiple_of(row * ROW_SUBLANES, ROW_SUBLANES), n_rows * ROW_SUBLANES)


class _Layout:
    def __init__(self, t_all):
        bm = EXPERT_BLOCK
        self.t_all = t_all
        self.tile = _router_tile(t_all)
        self.n_tiles = t_all // self.tile
        self.n_blocks = -(-(t_all * TOP_K + N_EXPERTS * (bm - 1)) // bm)
        self.sorted_rows = self.n_blocks * bm


def _strip_copies(src_ref, src_row, dst_ref, dst_row, n, sem):
    n_chunks = lax.shift_right_logical(n, STRIP_CHUNK_LOG2)

    def chunk(j, c):
        o = j * STRIP_CHUNK
        pltpu.make_async_copy(src_ref.at[_tiles(src_row + o, STRIP_CHUNK)], dst_ref.at[_tiles(dst_row + o, STRIP_CHUNK)],
                              sem).start()
        return c
    lax.fori_loop(0, n_chunks, chunk, 0)

    def single(j, c):
        o = n_chunks * STRIP_CHUNK + j
        pltpu.make_async_copy(src_ref.at[_tiles(src_row + o, 1)], dst_ref.at[_tiles(dst_row + o, 1)], sem).start()
        return c
    lax.fori_loop(0, n - n_chunks * STRIP_CHUNK, single, 0)


def _wait_rows(hbm_ref, vmem_ref, n_rows, sem):
    pltpu.make_async_copy(hbm_ref.at[pl.ds(0, n_rows * ROW_SUBLANES)], vmem_ref.at[pl.ds(0, n_rows * ROW_SUBLANES)],
                          sem).wait()


def _dispatch_kernel(soff_ref, scnt_ref, sgs_ref, lo_ref, hi_ref, xp_ref, lpr_ref, xs_hbm, loc, zero_scr, sem,
                     *, lay):
    tm = lay.tile
    n_pairs = TOP_K * tm
    g = pl.program_id(0)
    slot = g % 2

    def zero_fill(lo, hi, wait):
        def copies(first, count, size):
            def body(j, c):
                cp = pltpu.make_async_copy(zero_scr.at[pl.ds(0, size * ROW_SUBLANES)],
                                           xs_hbm.at[_tiles(first + j * size, size)], sem.at[2])
                if wait:
                    cp.wait()
                else:
                    cp.start()
                return c
            lax.fori_loop(0, count, body, 0)
        n = hi - lo
        n_big = lax.shift_right_logical(n, ZERO_ROWS_LOG2)
        copies(lo, n_big, ZERO_ROWS)
        done = n_big * ZERO_ROWS
        n_mid = lax.shift_right_logical(n - done, STRIP_CHUNK_LOG2)
        copies(lo + done, n_mid, STRIP_CHUNK)
        done = done + n_mid * STRIP_CHUNK
        copies(lo + done, n - done, 1)

    @pl.when(g == 0)
    def _():
        zero_scr[...] = jnp.zeros_like(zero_scr)
        for wait in (False, True):
            def expert(e, c, wait=wait):
                zero_fill(lo_ref[e], hi_ref[e], wait)
                return c
            lax.fori_loop(0, N_EXPERTS, expert, 0)
            zero_fill(hi_ref[N_EXPERTS - 1], lay.sorted_rows, wait)

    @pl.when(g >= 2)
    def _():
        _wait_rows(xs_hbm, loc.at[slot], n_pairs, sem.at[slot])

    x = _rows_from_tiles(xp_ref, tm).astype(BF16)
    pos = lax.broadcasted_iota(I32, (n_pairs, tm), 0)
    lpr = lpr_ref[0]
    hit = pos == lpr[0:1, :]
    for k in range(1, TOP_K):
        hit = hit | (pos == lpr[k:k + 1, :])
    _rows_to_tiles(loc.at[slot], _dot(hit.astype(BF16), x))

    def strip(e, c):
        s = g * N_EXPERTS + e
        _strip_copies(loc.at[slot], soff_ref[s], xs_hbm, sgs_ref[s], scnt_ref[s], sem.at[slot])
        return c
    lax.fori_loop(0, N_EXPERTS, strip, 0)

    @pl.when(g == lay.n_tiles - 1)
    def _():
        _wait_rows(xs_hbm, loc.at[slot], n_pairs, sem.at[slot])
        if lay.n_tiles > 1:
            _wait_rows(xs_hbm, loc.at[1 - slot], n_pairs, sem.at[1 - slot])


def _dispatch(xp, lp_rows, soff, scnt, sgs, pad_lo, pad_hi, lay):
    tm = lay.tile
    grid_spec = pltpu.PrefetchScalarGridSpec(
        num_scalar_prefetch=5,
        grid=(lay.n_tiles,),
        in_specs=[pl.BlockSpec((tm * ROW_SUBLANES, LANES), lambda g, *_: (g, 0)),
                  pl.BlockSpec((1, ROW_SUBLANES, tm), lambda g, *_: (g, 0, 0))],
        out_specs=pl.BlockSpec(memory_space=pl.ANY),
        scratch_shapes=[pltpu.VMEM((2, TOP_K * tm * ROW_SUBLANES, LANES), F32),
                        pltpu.VMEM((ZERO_ROWS * ROW_SUBLANES, LANES), F32),
                        pltpu.SemaphoreType.DMA((3,))],
    )
    return pl.pallas_call(
        functools.partial(_dispatch_kernel, lay=lay),
        grid_spec=grid_spec,
        out_shape=jax.ShapeDtypeStruct((lay.sorted_rows * ROW_SUBLANES, LANES), F32),
        compiler_params=_params(("arbitrary",)),
        name="dispatch",
    )(soff, scnt, sgs, pad_lo, pad_hi, xp, lp_rows)


def _experts_kernel(be_ref, nv_ref, xs_ref, wg_ref, wu_ref, wd_ref, bg_ref, bu_ref, bd_ref, ys_ref, wbf):
    bm = EXPERT_BLOCK
    i = pl.program_id(0)
    n_valid = nv_ref[0]

    @pl.when(i < n_valid)
    def _():
        @pl.when((i == 0) | (be_ref[i] != be_ref[jnp.maximum(i - 1, 0)]))
        def _():
            for w, w_ref in enumerate((wg_ref, wu_ref, wd_ref)):
                for c in range(0, D_MODEL, WEIGHT_CAST_ROWS):
                    wbf[w, c:c + WEIGHT_CAST_ROWS, :] = w_ref[0, c:c + WEIGHT_CAST_ROWS, :].astype(BF16)

        part = bm // EXPERT_PARTS
        for j in range(EXPERT_PARTS):
            tiles = pl.ds(j * part * ROW_SUBLANES, part * ROW_SUBLANES)
            x = _rows_from_tiles(xs_ref.at[tiles], part).astype(BF16)
            g = jnp.minimum(_dot(x, wbf[0]) + bg_ref[0], SWIGLU_LIMIT)
            u = jnp.clip(_dot(x, wbf[1]) + bu_ref[0], -SWIGLU_LIMIT, SWIGLU_LIMIT)
            mid = ((u + 1.0) * (g * _sigmoid(SWIGLU_ALPHA * g))).astype(BF16)
            _rows_to_tiles(ys_ref.at[tiles], _dot(mid, wbf[2]) + bd_ref[0])

    @pl.when(i >= n_valid)
    def _():
        ys_ref[...] = jnp.zeros_like(ys_ref)


def _experts(block_e, n_valid, xs, wts, lay):
    bm = EXPERT_BLOCK
    tile_rows = bm * ROW_SUBLANES
    wspec = pl.BlockSpec((1, D_MODEL, D_FF), lambda i, be, nv: (be[i], 0, 0))
    bspec = pl.BlockSpec((1, 1, D_FF), lambda i, be, nv: (be[i], 0, 0))
    grid_spec = pltpu.PrefetchScalarGridSpec(
        num_scalar_prefetch=2,
        grid=(lay.n_blocks,),
        in_specs=[pl.BlockSpec((tile_rows, LANES), lambda i, be, nv: (jnp.minimum(i, nv[0] - 1), 0)),
                  wspec, wspec, wspec, bspec, bspec, bspec],
        out_specs=pl.BlockSpec((tile_rows, LANES), lambda i, be, nv: (i, 0)),
        scratch_shapes=[pltpu.VMEM((3, D_MODEL, D_FF), BF16)],
    )
    return pl.pallas_call(
        functools.partial(_experts_kernel),
        grid_spec=grid_spec,
        out_shape=jax.ShapeDtypeStruct((lay.sorted_rows * ROW_SUBLANES, LANES), F32),
        compiler_params=_params(("arbitrary",)),
        name="experts",
    )(block_e, n_valid, xs, wts['w_gate'], wts['w_up'], wts['w_down'], wts['b_gate'], wts['b_up'], wts['b_down'])


def _combine_kernel(soff_ref, scnt_ref, sgs_ref, h_ref, tw_ref, lpc_ref, p_ref, ys_hbm, g3_ref, wpg_ref, wpp_ref,
                    gf_ref, yp_ref, ysm_ref, loc, sem, *, lay, n_sample):
    tm = lay.tile
    n_pairs = TOP_K * tm
    g = pl.program_id(0)
    slot = g % 2

    def fetch(tile, s):
        def strip(e, c):
            i = tile * N_EXPERTS + e
            _strip_copies(ys_hbm, sgs_ref[i], loc.at[s], soff_ref[i], scnt_ref[i], sem.at[s])
            return c
        lax.fori_loop(0, N_EXPERTS, strip, 0)

    @pl.when(g == 0)
    def _():
        fetch(0, 0)

    @pl.when(g + 1 < lay.n_tiles)
    def _():
        fetch(g + 1, 1 - slot)

    _wait_rows(ys_hbm, loc.at[slot], n_pairs, sem.at[slot])
    y_rows = _rows_from_tiles(loc.at[slot], n_pairs).astype(BF16)
    tw = tw_ref[...]
    lpc = lpc_ref[...]
    pos = lax.broadcasted_iota(I32, (tm, n_pairs), 1)
    weight = jnp.zeros((tm, n_pairs), F32)
    for k in range(TOP_K):
        weight = jnp.where(pos == lpc[:, k:k + 1], tw[:, k:k + 1], weight)
    y = _dot(weight.astype(BF16), y_rows)
    h2 = h_ref[...] + y
    gate = _sigmoid(_dot(_rms(h2, g3_ref[...]).astype(BF16), wpg_ref[...]))
    h3 = h2 + gate * _dot(p_ref[...].astype(BF16), wpp_ref[...])
    out = _rms(h3, gf_ref[...])
    yp_ref[...] = out

    @pl.when(g == lay.n_tiles - 1)
    def _():
        ysm_ref[...] = out[tm - n_sample:, :]


def _combine(hbuf, tw, lp_cols, p_all, ys, soff, scnt, sgs, n_prompt, wts, lay):
    tm = lay.tile
    n_sample = lay.t_all - n_prompt
    assert 0 < n_sample <= tm and (lay.n_tiles - 1) * tm < n_prompt
    consts = [wts['g3'], wts['w_ple_gate'], wts['w_ple_proj'], wts['g_final']]
    row = lambda w: pl.BlockSpec((tm, w), lambda g, *_: (g, 0))
    grid_spec = pltpu.PrefetchScalarGridSpec(
        num_scalar_prefetch=3,
        grid=(lay.n_tiles,),
        in_specs=[row(D_MODEL), row(LANES), row(LANES), row(D_PLE), pl.BlockSpec(memory_space=pl.ANY)]
        + [pl.BlockSpec(c.shape, lambda g, *_: (0, 0), pipeline_mode=pl.Buffered(1)) for c in consts],
        out_specs=[row(D_MODEL), pl.BlockSpec((n_sample, D_MODEL), lambda g, *_: (0, 0))],
        scratch_shapes=[pltpu.VMEM((2, TOP_K * tm * ROW_SUBLANES, LANES), F32), pltpu.SemaphoreType.DMA((2,))],
    )
    return pl.pallas_call(
        functools.partial(_combine_kernel, lay=lay, n_sample=n_sample),
        grid_spec=grid_spec,
        out_shape=[jax.ShapeDtypeStruct((n_prompt, D_MODEL), F32), jax.ShapeDtypeStruct((n_sample, D_MODEL), F32)],
        compiler_params=_params(("arbitrary",)),
        name="combine",
    )(soff, scnt, sgs, hbuf, tw, lp_cols, p_all, ys, *consts)


def _prepare_weights(norm1_g, w_in, w_gk2, b_gk, gla_norm_g, v_norm_g, v_norm_b, w_sp, b_sp, w_o, norm2_g,
                     w_router, b_router, w_gate, b_gate, w_up, b_up, w_down, b_down, norm3_g, w_ple_gate,
                     w_ple_proj, final_g):
    o_gk = 2 * GLA_DK_T + 2 * GLA_DV_T
    w_main = jnp.concatenate([w_in[:, :o_gk], w_in[:, o_gk + GK_RANK:]], axis=1).astype(BF16)
    w_gk1 = jnp.pad(w_in[:, o_gk:o_gk + GK_RANK], ((0, 0), (0, LANES - GK_RANK))).astype(BF16)
    w_gk2p = jnp.pad(w_gk2, ((0, LANES - GK_RANK), (0, 0))).astype(BF16)
    row = lambda z: z.reshape(1, -1).astype(F32)
    return dict(
        g1=row(norm1_g), w_main=w_main, w_gk1=w_gk1, w_gk2=w_gk2p, b_gk=row(b_gk), gla_g=row(gla_norm_g),
        vn_g=row(v_norm_g), vn_b=row(v_norm_b), w_sp=w_sp, b_sp_t=b_sp.T,
        mix_w0=row(jnp.repeat(w_sp[:, 0, 0], MIX_DH)), mix_b0=row(jnp.repeat(b_sp[:, 0], MIX_DH)),
        w_o=w_o.astype(BF16), g2=row(norm2_g),
        w_router=jnp.pad(w_router, ((0, 0), (0, LANES - N_EXPERTS))),
        b_router=jnp.pad(row(b_router), ((0, 0), (0, LANES - N_EXPERTS))),
        w_gate=w_gate, w_up=w_up, w_down=w_down,
        b_gate=b_gate.reshape(N_EXPERTS, 1, D_FF), b_up=b_up.reshape(N_EXPERTS, 1, D_FF),
        b_down=b_down.reshape(N_EXPERTS, 1, D_MODEL),
        g3=row(norm3_g), w_ple_gate=w_ple_gate.astype(BF16), w_ple_proj=w_ple_proj.astype(BF16),
        g_final=row(final_g))


def _routing_tables(top_i, rank, counts, counts_before, lay):
    bm = EXPERT_BLOCK
    tm = lay.tile
    padded = (counts + bm - 1) // bm * bm
    pad_end = jnp.cumsum(padded)
    pad_start = pad_end - padded
    tile_counts = jnp.concatenate([counts_before[1:], counts[None]], axis=0) - counts_before
    local_off = jnp.cumsum(tile_counts, axis=1) - tile_counts
    global_start = pad_start[None, :] + counts_before
    experts = jnp.arange(N_EXPERTS, dtype=I32)
    shift = jnp.repeat(local_off - counts_before, tm, axis=0)
    lp = jnp.sum(jnp.where(top_i[:, :, None] == experts, shift[:, None, :], 0), axis=-1) + rank
    lp = lp.astype(I32)
    lp_cols = jnp.pad(lp, ((0, 0), (0, LANES - TOP_K)))
    lp_rows = jnp.pad(lp.reshape(lay.n_tiles, tm, TOP_K).transpose(0, 2, 1), ((0, 0), (0, ROW_SUBLANES - TOP_K), (0, 0)))
    n_valid = (pad_end[-1] // bm).astype(I32)
    blk = jnp.minimum(jnp.arange(lay.n_blocks, dtype=I32), n_valid - 1)
    block_e = jnp.sum(pad_end[None, :] <= (blk * bm)[:, None], axis=1).astype(I32)
    flat = lambda z: z.reshape(-1).astype(I32)
    return dict(lp_cols=lp_cols, lp_rows=lp_rows, soff=flat(local_off), scnt=flat(tile_counts), sgs=flat(global_start),
                pad_lo=(pad_start + counts).astype(I32), pad_hi=pad_end.astype(I32), block_e=block_e,
                n_valid=n_valid.reshape(1))


def _layer(x_prompt, x_sample, state, p_prompt, p_sample, wts):
    n_p, len_p, _ = x_prompt.shape
    n_s, len_s, _ = x_sample.shape
    assert len_s == 1, "the sample group carries one new token per sequence"
    t_p = n_p * len_p
    t_all = t_p + n_s

    xs = x_sample.reshape(n_s, D_MODEL)
    q, k, a, v, og, ob, vn = _mixer_sample_in(xs, wts)
    st_sample, o = _state_update(state, q, k, a, v)
    h_sample = _mixer_sample_out(xs, o, og, ob, wts)
    hbuf, st_prompt = _mixer_prompt(x_prompt, h_sample, wts)

    lay = _Layout(t_all)
    xp, top_i, top_w, rank, counts_before, counts = _router(hbuf, wts)
    tab = _routing_tables(top_i[:, :TOP_K], rank[:, :TOP_K], counts[0, :N_EXPERTS],
                          counts_before[:, 0, :N_EXPERTS], lay)
    x_sorted = _dispatch(xp, tab['lp_rows'], tab['soff'], tab['scnt'], tab['sgs'], tab['pad_lo'], tab['pad_hi'], lay)
    y_sorted = _experts(tab['block_e'], tab['n_valid'], x_sorted, wts, lay)
    p_all = jnp.concatenate([p_prompt.reshape(t_p, D_PLE), p_sample.reshape(n_s, D_PLE)], axis=0)
    y_prompt, y_sample = _combine(hbuf, top_w, tab['lp_cols'], p_all, y_sorted, tab['soff'], tab['scnt'], tab['sgs'],
                                  t_p, wts, lay)
    return (y_prompt.reshape(n_p, len_p, D_MODEL), y_sample.reshape(n_s, len_s, D_MODEL),
            st_prompt, st_sample, vn.reshape(n_s, len_s, D_GMLP))


def kernel(x_prompt, x_sample, state_gla, p_prompt, p_sample, norm1_g, w_in, w_gk2, b_gk, gla_norm_g, v_norm_g,
           v_norm_b, w_sp, b_sp, w_o, norm2_g, w_router, b_router, w_gate, b_gate, w_up, b_up, w_down, b_down,
           norm3_g, w_ple_gate, w_ple_proj, final_g):
    assert w_in.shape[0] == 1, "single-layer trunk"
    wts = _prepare_weights(norm1_g[0], w_in[0], w_gk2[0], b_gk[0], gla_norm_g[0], v_norm_g[0], v_norm_b[0],
                           w_sp[0], b_sp[0], w_o[0], norm2_g[0], w_router[0], b_router[0], w_gate[0], b_gate[0],
                           w_up[0], b_up[0], w_down[0], b_down[0], norm3_g[0], w_ple_gate[0], w_ple_proj[0],
                           final_g)
    y_p, y_s, st_p, st_s, vn_s = _layer(x_prompt, x_sample, state_gla[0], p_prompt[0], p_sample[0], wts)
    return (y_p, y_s, st_p[None], st_s[None].astype(state_gla.dtype), vn_s[None])
```

```python
import functools

import jax
import jax.numpy as jnp
from jax import lax
from jax.experimental import pallas as pl
from jax.experimental.pallas import tpu as pltpu

F32 = jnp.float32
BF16 = jnp.bfloat16
I32 = jnp.int32
U32 = jnp.uint32

D_MODEL = 1024
GLA_HEADS = 4
GLA_DK = 128
GLA_DV = 256
GLA_DK_T = GLA_HEADS * GLA_DK
GLA_DV_T = GLA_HEADS * GLA_DV
GK_RANK = 16
GATE_NORMALIZER = 16.0
GLA_CHUNK = 64
GLA_CHUNK_LOG2 = 6
D_GMLP = 1024
MIX_HEADS = 4
MIX_DH = D_GMLP // MIX_HEADS
MIX_CHUNK = 128
MIX_CHUNK_LOG2 = 7
assert 1 << GLA_CHUNK_LOG2 == GLA_CHUNK and 1 << MIX_CHUNK_LOG2 == MIX_CHUNK
N_EXPERTS = 32
TOP_K = 4
D_FF = 1024
SWIGLU_LIMIT = 7.0
SWIGLU_ALPHA = 1.702
D_PLE = 256
EPS = 1e-6

LANES = 128
ROW_SUBLANES = D_MODEL // LANES
VMEM_LIMIT_BYTES = 56 * 1024 * 1024

OFF_Q, OFF_K, OFF_V, OFF_R, OFF_U, OFF_VG, OFF_GA, OFF_GB, N_MAIN = (
    0, 512, 1024, 2048, 3072, 4096, 5120, 6144, 7168)

MIXER_TILE = 512
MIXER_PARTS = 2
STATE_TOKENS = 8
EXPERT_BLOCK = 512
EXPERT_PARTS = 2
WEIGHT_CAST_ROWS = 256
STRIP_CHUNK = 8
STRIP_CHUNK_LOG2 = 3
ZERO_ROWS = 64
ZERO_ROWS_LOG2 = 6
assert 1 << STRIP_CHUNK_LOG2 == STRIP_CHUNK and 1 << ZERO_ROWS_LOG2 == ZERO_ROWS


def _dot(a, b):
    return jnp.dot(a, b, preferred_element_type=F32)


def _dot_nt(a, b):
    return lax.dot_general(a, b, (((1,), (1,)), ((), ())), preferred_element_type=F32)


def _dot_tn(a, b):
    return lax.dot_general(a, b, (((0,), (0,)), ((), ())), preferred_element_type=F32)


def _rms(x, g):
    return x * lax.rsqrt(jnp.mean(x * x, axis=-1, keepdims=True) + EPS) * g


def _sigmoid(x):
    return 1.0 / (1.0 + jnp.exp(-x))


def _gelu(x):
    return 0.5 * x * (1.0 + lax.erf(x * (2.0 ** -0.5)))


def _log_sigmoid(x):
    return jnp.minimum(x, 0.0) - jnp.log1p(jnp.exp(-jnp.abs(x)))


def _split3(x):
    hi = x.astype(BF16)
    r1 = x - hi.astype(F32)
    mid = r1.astype(BF16)
    lo = (r1 - mid.astype(F32)).astype(BF16)
    return hi, mid, lo


def _const_spec(shape):
    nd = len(shape)
    return pl.BlockSpec(shape, lambda *_: (0,) * nd, pipeline_mode=pl.Buffered(1))


def _params(sem):
    return pltpu.CompilerParams(dimension_semantics=sem, vmem_limit_bytes=VMEM_LIMIT_BYTES)


def _main_cols(wa_ref, wb_ref, lo, hi):
    if hi <= OFF_U:
        return wa_ref[:, lo:hi]
    return wb_ref[:, lo - OFF_U:hi - OFF_U]


def _project_gla_inputs(n, wm, wgk1_ref, wgk2_ref, bgk_ref):
    q = _dot(n, wm(OFF_Q, OFF_K)) * (GLA_DK ** -0.5)
    k = _dot(n, wm(OFF_K, OFF_V))
    v = _dot(n, wm(OFF_V, OFF_R))
    gk = _dot(n, wgk1_ref[...]).astype(BF16)
    log_a = _log_sigmoid(_dot(gk, wgk2_ref[...]) + bgk_ref[...]) * (1.0 / GATE_NORMALIZER)
    return q, k, v, log_a


def _gmlp_inputs(n, wm, vng_ref, vnb_ref):
    u = _gelu(_dot(n, wm(OFF_U, OFF_VG)))
    vg = _gelu(_dot(n, wm(OFF_VG, OFF_GA)))
    mu = jnp.mean(vg, axis=-1, keepdims=True)
    vc = vg - mu
    var = jnp.mean(vc * vc, axis=-1, keepdims=True)
    vn = vc * lax.rsqrt(var + EPS) * vng_ref[...] + vnb_ref[...]
    ug = u * _sigmoid(_dot(n, wm(OFF_GB, N_MAIN)))
    return ug, vn


def _gla_out_gate(n, wm):
    r = _dot(n, wm(OFF_R, OFF_U))
    ga = _dot(n, wm(OFF_GA, OFF_GB))
    return r * _sigmoid(r) * _sigmoid(ga)


def _head_rms(o, g):
    return o * lax.rsqrt(jnp.mean(o * o, axis=-1, keepdims=True) + EPS) * g


def _mixer_prompt_kernel(x_ref, hs_ref, *refs, tm, n_seq, n_tiles):
    b = pl.program_id(0)
    t = pl.program_id(1)
    h_ref = refs[12]

    @pl.when(b < n_seq)
    def _():
        _mixer_prompt_tile(x_ref, *refs, tm=tm, n_tiles=n_tiles)

    @pl.when((b == n_seq) & (t == 0))
    def _():
        h_ref[0:hs_ref.shape[0], :] = hs_ref[...]


def _mixer_prompt_tile(x_ref, g1_ref, wa_ref, wb_ref, wgk1_ref, wgk2_ref, bgk_ref, glag_ref, vng_ref, vnb_ref,
                       wsp_ref, bspt_ref, wo_ref,
                       h_ref, st_ref,
                       n_scr, oa_scr, s_scr, *, tm, n_tiles):
    t = pl.program_id(1)

    @pl.when(t == 0)
    def _():
        s_scr[...] = jnp.zeros_like(s_scr)

    wm = functools.partial(_main_cols, wa_ref, wb_ref)
    pm = tm // MIXER_PARTS
    for part in range(MIXER_PARTS):
        rows = pl.ds(part * pm, pm)
        _mixer_prompt_part(x_ref.at[0, rows], g1_ref, wm, wgk1_ref, wgk2_ref, bgk_ref, glag_ref, vng_ref, vnb_ref,
                           wsp_ref, bspt_ref, wo_ref, h_ref.at[rows], n_scr.at[rows], oa_scr.at[rows], s_scr, tm=pm)

    @pl.when(t == n_tiles - 1)
    def _():
        for h in range(GLA_HEADS):
            st_ref[0, h] = s_scr[h].T


def _mixer_prompt_part(x_ref, g1_ref, wm, wgk1_ref, wgk2_ref, bgk_ref, glag_ref, vng_ref, vnb_ref,
                       wsp_ref, bspt_ref, wo_ref, h_ref, n_scr, oa_scr, s_scr, *, tm):
    x = x_ref[...]
    n = _rms(x, g1_ref[...]).astype(BF16)
    n_scr[...] = n
    q, k, v, log_a = _project_gla_inputs(n, wm, wgk1_ref, wgk2_ref, bgk_ref)
    v = v.astype(BF16)

    row = lax.broadcasted_iota(I32, (tm, tm), 0)
    col = lax.broadcasted_iota(I32, (tm, tm), 1)
    same_chunk = lax.shift_right_logical(row, GLA_CHUNK_LOG2) == lax.shift_right_logical(col, GLA_CHUNK_LOG2)
    causal = same_chunk & (row >= col)
    tri = causal.astype(BF16)
    blk = same_chunk.astype(BF16)
    hi, mid, lo = _split3(log_a)
    b = _dot(tri, hi) + _dot(tri, mid) + _dot(tri, lo)
    b_end = _dot(blk, hi) + _dot(blk, mid) + _dot(blk, lo)
    qe = (q * jnp.exp(b)).astype(BF16)
    ke = (k * jnp.exp(-b)).astype(BF16)
    kd = (k * jnp.exp(b_end - b)).astype(BF16)
    dec = jnp.exp(b_end)
    glag = glag_ref[...]
    for h in range(GLA_HEADS):
        ks = slice(h * GLA_DK, (h + 1) * GLA_DK)
        vs = slice(h * GLA_DV, (h + 1) * GLA_DV)
        vh = v[:, vs]
        att = jnp.where(causal, _dot_nt(qe[:, ks], ke[:, ks]), 0.0).astype(BF16)
        o_intra = _dot(att, vh)
        s_t = s_scr[h]
        o_inter = []
        for c in range(tm // GLA_CHUNK):
            rs = slice(c * GLA_CHUNK, (c + 1) * GLA_CHUNK)
            o_inter.append(_dot_nt(qe[rs, ks], s_t.astype(BF16)))
            s_t = s_t * dec[c * GLA_CHUNK:c * GLA_CHUNK + 1, ks] + _dot_tn(vh[rs], kd[rs, ks])
        s_scr[h] = s_t
        oa_scr[:, vs] = _head_rms(o_intra + jnp.concatenate(o_inter, axis=0), glag)

    n = n_scr[...]
    oa_scr[...] = oa_scr[...] * _gla_out_gate(n, wm)
    ug, vn = _gmlp_inputs(n, wm, vng_ref, vnb_ref)
    vn = vn.astype(BF16)
    reps = tm // MIX_CHUNK
    mix_mask = (lax.shift_right_logical(row, MIX_CHUNK_LOG2) == lax.shift_right_logical(col, MIX_CHUNK_LOG2)) & (
        row >= col)
    for h in range(MIX_HEADS):
        w = jnp.where(mix_mask, jnp.tile(wsp_ref[h], (reps, reps)), 0.0).astype(BF16)
        cs = slice(h * MIX_DH, (h + 1) * MIX_DH)
        mix = _dot(w, vn[:, cs]) + jnp.tile(bspt_ref[:, h:h + 1], (reps, 1))
        oa_scr[:, cs] = oa_scr[:, cs] + ug[:, cs] * mix

    h_ref[...] = x + _dot(oa_scr[...].astype(BF16), wo_ref[...])


def _mixer_prompt(x, h_sample, wts):
    n_seq, seq_len, _ = x.shape
    n_s = h_sample.shape[0]
    tm = min(MIXER_TILE, seq_len)
    assert seq_len % tm == 0 and tm % MIX_CHUNK == 0 and n_s <= tm
    n_tiles = seq_len // tm
    consts = [wts['g1'], wts['w_a'], wts['w_b'], wts['w_gk1'], wts['w_gk2'], wts['b_gk'], wts['gla_g'], wts['vn_g'],
              wts['vn_b'], wts['w_sp'], wts['b_sp_t'], wts['w_o']]
    kern = functools.partial(_mixer_prompt_kernel, tm=tm, n_seq=n_seq, n_tiles=n_tiles)
    last = n_seq - 1

    def prompt_tile(b, t):
        return jnp.minimum(b, last), jnp.where(b < n_seq, t, n_tiles - 1)

    def x_map(b, t):
        bb, tt = prompt_tile(b, t)
        return bb, tt, 0

    def h_map(b, t):
        return jnp.where(b < n_seq, b * n_tiles + t, n_seq * n_tiles), 0

    return pl.pallas_call(
        kern,
        grid=(n_seq + 1, n_tiles),
        in_specs=[pl.BlockSpec((1, tm, D_MODEL), x_map), _const_spec(h_sample.shape)]
        + [_const_spec(c.shape) for c in consts],
        out_specs=[pl.BlockSpec((tm, D_MODEL), h_map),
                   pl.BlockSpec((1, GLA_HEADS, GLA_DK, GLA_DV), lambda b, t: (jnp.minimum(b, last), 0, 0, 0))],
        out_shape=[jax.ShapeDtypeStruct((n_seq * seq_len + n_s, D_MODEL), F32),
                   jax.ShapeDtypeStruct((n_seq, GLA_HEADS, GLA_DK, GLA_DV), F32)],
        scratch_shapes=[pltpu.VMEM((tm, D_MODEL), BF16),
                        pltpu.VMEM((tm, D_MODEL), F32),
                        pltpu.VMEM((GLA_HEADS, GLA_DV, GLA_DK), F32)],
        compiler_params=_params(("arbitrary", "arbitrary")),
        name="mixer_prompt",
    )(x, h_sample, *consts)


def _mixer_sample_in_kernel(x_ref, g1_ref, wa_ref, wb_ref, wgk1_ref, wgk2_ref, bgk_ref, vng_ref, vnb_ref,
                            mixw_ref, mixb_ref,
                            q_ref, k_ref, a_ref, v_ref, og_ref, ob_ref, vn_ref):
    wm = functools.partial(_main_cols, wa_ref, wb_ref)
    n = _rms(x_ref[...], g1_ref[...]).astype(BF16)
    q, k, v, log_a = _project_gla_inputs(n, wm, wgk1_ref, wgk2_ref, bgk_ref)
    q_ref[...] = q
    k_ref[...] = k
    a_ref[...] = jnp.exp(log_a)
    v_ref[...] = v
    og_ref[...] = _gla_out_gate(n, wm)
    ug, vn = _gmlp_inputs(n, wm, vng_ref, vnb_ref)
    vn_ref[...] = vn
    ob_ref[...] = ug * (mixw_ref[...] * vn + mixb_ref[...])


def _mixer_sample_in(x2, wts):
    n_seq = x2.shape[0]
    consts = [wts['g1'], wts['w_a'], wts['w_b'], wts['w_gk1'], wts['w_gk2'], wts['b_gk'], wts['vn_g'], wts['vn_b'],
              wts['mix_w0'], wts['mix_b0']]
    widths = [GLA_DK_T, GLA_DK_T, GLA_DK_T, GLA_DV_T, D_MODEL, D_MODEL, D_GMLP]
    return pl.pallas_call(
        _mixer_sample_in_kernel,
        grid=(1,),
        in_specs=[_const_spec(x2.shape)] + [_const_spec(c.shape) for c in consts],
        out_specs=[pl.BlockSpec((n_seq, w), lambda i: (0, 0)) for w in widths],
        out_shape=[jax.ShapeDtypeStruct((n_seq, w), F32) for w in widths],
        compiler_params=_params(("arbitrary",)),
        name="mixer_sample_in",
    )(x2, *consts)


def _state_update_kernel(s_ref, q_ref, k_ref, a_ref, v_ref, so_ref, o_ref):
    for j in range(STATE_TOKENS):
        for h in range(GLA_HEADS):
            ks = slice(h * GLA_DK, (h + 1) * GLA_DK)
            vs = slice(h * GLA_DV, (h + 1) * GLA_DV)
            s_new = a_ref[0, ks, j:j + 1] * s_ref[j, h] + k_ref[0, ks, j:j + 1] * v_ref[j:j + 1, vs]
            so_ref[j, h] = s_new
            o_ref[j:j + 1, vs] = jnp.sum(q_ref[0, ks, j:j + 1] * s_new, axis=0, keepdims=True)


def _state_update(state, q, k, a, v):
    n_seq = state.shape[0]
    tb = STATE_TOKENS
    assert n_seq % tb == 0

    def cols(z):
        return z.reshape(n_seq // tb, tb, GLA_DK_T).transpose(0, 2, 1)

    col_spec = pl.BlockSpec((1, GLA_DK_T, tb), lambda i: (i, 0, 0))
    st_spec = pl.BlockSpec((tb, GLA_HEADS, GLA_DK, GLA_DV), lambda i: (i, 0, 0, 0))
    row_spec = pl.BlockSpec((tb, GLA_DV_T), lambda i: (i, 0))
    return pl.pallas_call(
        _state_update_kernel,
        grid=(n_seq // tb,),
        in_specs=[st_spec, col_spec, col_spec, col_spec, row_spec],
        out_specs=[st_spec, row_spec],
        out_shape=[jax.ShapeDtypeStruct(state.shape, F32), jax.ShapeDtypeStruct((n_seq, GLA_DV_T), F32)],
        compiler_params=_params(("arbitrary",)),
        name="state_update",
    )(state, cols(q), cols(k), cols(a), v)


def _mixer_sample_out_kernel(x_ref, o_ref, og_ref, ob_ref, glag_ref, wo_ref, h_ref):
    glag = glag_ref[...]
    parts = []
    for h in range(GLA_HEADS):
        vs = slice(h * GLA_DV, (h + 1) * GLA_DV)
        parts.append(_head_rms(o_ref[:, vs], glag))
    merged = jnp.concatenate(parts, axis=1) * og_ref[...] + ob_ref[...]
    h_ref[...] = x_ref[...] + _dot(merged.astype(BF16), wo_ref[...])


def _mixer_sample_out(x2, o, og, ob, wts):
    n_seq = x2.shape[0]
    consts = [x2, o, og, ob, wts['gla_g'], wts['w_o']]
    return pl.pallas_call(
        _mixer_sample_out_kernel,
        grid=(1,),
        in_specs=[_const_spec(c.shape) for c in consts],
        out_specs=pl.BlockSpec((n_seq, D_MODEL), lambda i: (0, 0)),
        out_shape=jax.ShapeDtypeStruct((n_seq, D_MODEL), F32),
        compiler_params=_params(("arbitrary",)),
        name="mixer_sample_out",
    )(*consts)


def _router_kernel(h_ref, g2_ref, wr_ref, br_ref, xp_ref, tw_ref, lpc_ref, lpr_ref, cb_ref, tc_ref, lo_ref, cnt_ref, cnt_scr,
                   *, tm):
    i = pl.program_id(0)

    @pl.when(i == 0)
    def _():
        cnt_scr[...] = jnp.zeros_like(cnt_scr)

    cb_ref[0] = cnt_scr[...].astype(I32)

    hn = _rms(h_ref[...], g2_ref[...])
    hn_hi = hn.astype(BF16)
    hn_lo = (hn - hn_hi.astype(F32)).astype(BF16)
    w = wr_ref[...]
    w_hi = w.astype(BF16)
    w_lo = (w - w_hi.astype(F32)).astype(BF16)
    logits = _dot(hn_hi, w_hi) + _dot(hn_lo, w_hi) + _dot(hn_hi, w_lo) + br_ref[...]

    _rows_to_tiles(xp_ref, hn)

    lane = lax.broadcasted_iota(I32, (tm, LANES), 1)
    neg = jnp.float32(-jnp.inf)
    l = jnp.where(lane < N_EXPERTS, logits, neg)
    vals, hots = [], []
    for _ in range(TOP_K):
        m = jnp.max(l, axis=-1, keepdims=True)
        idx = jnp.min(jnp.where(l == m, lane, LANES), axis=-1, keepdims=True)
        hot = lane == idx
        l = jnp.where(hot, neg, l)
        vals.append(m)
        hots.append(hot)
    exps = [jnp.exp(v - vals[0]) for v in vals]
    denom = exps[0] + exps[1] + exps[2] + exps[3]

    member = (hots[0] | hots[1] | hots[2] | hots[3]).astype(BF16)
    row = lax.broadcasted_iota(I32, (tm, tm), 0)
    col = lax.broadcasted_iota(I32, (tm, tm), 1)
    in_tile = _dot((row > col).astype(BF16), member)
    tile_cnt = jnp.sum(member.astype(F32), axis=0, keepdims=True)
    e_row = lax.broadcasted_iota(I32, (LANES, LANES), 0)
    e_col = lax.broadcasted_iota(I32, (LANES, LANES), 1)
    lower = (e_row < e_col).astype(BF16)
    c_hi, c_mid, c_lo = _split3(jnp.broadcast_to(tile_cnt, (ROW_SUBLANES, LANES)))
    local_off = (_dot(c_hi, lower) + _dot(c_mid, lower) + _dot(c_lo, lower))[0:1, :]
    tc_ref[0] = tile_cnt.astype(I32)
    lo_ref[0] = local_off.astype(I32)
    cnt_scr[...] = cnt_scr[...] + tile_cnt
    cnt_ref[...] = cnt_scr[...].astype(I32)

    place = local_off + in_tile
    tw = jnp.zeros((tm, LANES), F32)
    lp = jnp.zeros((tm, LANES), F32)
    for kk in range(TOP_K):
        sel = lane == kk
        tw = jnp.where(sel, exps[kk] / denom, tw)
        lp = jnp.where(sel, jnp.sum(jnp.where(hots[kk], place, 0.0), axis=-1, keepdims=True), lp)
    tw_ref[...] = tw
    lpc_ref[...] = lp.astype(I32)
    lpr_ref[0] = lp.T[0:ROW_SUBLANES, :].astype(I32)


def _router_tile(t_all):
    for tm in (512, 384, 256, 128):
        if t_all % tm == 0:
            return tm
    raise ValueError(f"token count {t_all} must be a multiple of 128")


def _router(hbuf, wts):
    t_all = hbuf.shape[0]
    tm = _router_tile(t_all)
    consts = [wts['g2'], wts['w_router'], wts['b_router']]
    row = lambda w: pl.BlockSpec((tm, w), lambda i: (i, 0))
    n_tiles = t_all // tm
    per_tile = pl.BlockSpec((1, 1, LANES), lambda i: (i, 0, 0))
    tile_vec = jax.ShapeDtypeStruct((n_tiles, 1, LANES), I32)
    return pl.pallas_call(
        functools.partial(_router_kernel, tm=tm),
        grid=(t_all // tm,),
        in_specs=[row(D_MODEL)] + [_const_spec(c.shape) for c in consts],
        out_specs=[pl.BlockSpec((tm * ROW_SUBLANES, LANES), lambda i: (i, 0)), row(LANES), row(LANES),
                   pl.BlockSpec((1, ROW_SUBLANES, tm), lambda i: (i, 0, 0)),
                   per_tile, per_tile, per_tile, pl.BlockSpec((1, LANES), lambda i: (0, 0))],
        out_shape=[jax.ShapeDtypeStruct((t_all * ROW_SUBLANES, LANES), F32),
                   jax.ShapeDtypeStruct((t_all, LANES), F32),
                   jax.ShapeDtypeStruct((t_all, LANES), I32),
                   jax.ShapeDtypeStruct((n_tiles, ROW_SUBLANES, tm), I32),
                   tile_vec, tile_vec, tile_vec, jax.ShapeDtypeStruct((1, LANES), I32)],
        scratch_shapes=[pltpu.VMEM((1, LANES), F32)],
        compiler_params=_params(("arbitrary",)),
        name="router",
    )(hbuf, *consts)


def _row_tile(r):
    start = r * ROW_SUBLANES
    if not isinstance(r, int):
        start = pl.multiple_of(start, ROW_SUBLANES)
    return pl.ds(start, ROW_SUBLANES)


def _tile_of(ref, row):
    return ref.at[_row_tile(row)]


def _rows_from_tiles(ref, n_rows):
    return jnp.concatenate([ref[pl.ds(s, n_rows, stride=ROW_SUBLANES), :] for s in range(ROW_SUBLANES)], axis=1)


def _rows_to_tiles(ref, x):
    for s in range(ROW_SUBLANES):
        ref[pl.ds(s, x.shape[0], stride=ROW_SUBLANES), :] = x[:, s * LANES:(s + 1) * LANES]


def _tiles(row, n_rows):
    return pl.ds(pl.multiple_of(row * ROW_SUBLANES, ROW_SUBLANES), n_rows * ROW_SUBLANES)


class _Layout:
    def __init__(self, t_all):
        bm = EXPERT_BLOCK
        self.t_all = t_all
        self.tile = _router_tile(t_all)
        self.n_tiles = t_all // self.tile
        self.n_blocks = -(-(t_all * TOP_K + N_EXPERTS * (bm - 1)) // bm)
        self.sorted_rows = self.n_blocks * bm


def _strip_copies(src_ref, src_row, dst_ref, dst_row, n, sem):
    n_chunks = lax.shift_right_logical(n, STRIP_CHUNK_LOG2)

    def chunk(j, c):
        o = j * STRIP_CHUNK
        pltpu.make_async_copy(src_ref.at[_tiles(src_row + o, STRIP_CHUNK)], dst_ref.at[_tiles(dst_row + o, STRIP_CHUNK)],
                              sem).start()
        return c
    lax.fori_loop(0, n_chunks, chunk, 0)

    def single(j, c):
        o = n_chunks * STRIP_CHUNK + j
        pltpu.make_async_copy(src_ref.at[_tiles(src_row + o, 1)], dst_ref.at[_tiles(dst_row + o, 1)], sem).start()
        return c
    lax.fori_loop(0, n - n_chunks * STRIP_CHUNK, single, 0)


def _wait_rows(hbm_ref, vmem_ref, n_rows, sem):
    pltpu.make_async_copy(hbm_ref.at[pl.ds(0, n_rows * ROW_SUBLANES)], vmem_ref.at[pl.ds(0, n_rows * ROW_SUBLANES)],
                          sem).wait()


def _dispatch_kernel(soff_ref, scnt_ref, sgs_ref, lo_ref, hi_ref, xp_ref, lpr_ref, xs_hbm, loc, zero_scr, sem,
                     *, lay):
    tm = lay.tile
    n_pairs = TOP_K * tm
    g = pl.program_id(0)
    slot = g % 2

    def zero_fill(lo, hi, wait):
        def copies(first, count, size):
            def body(j, c):
                cp = pltpu.make_async_copy(zero_scr.at[pl.ds(0, size * ROW_SUBLANES)],
                                           xs_hbm.at[_tiles(first + j * size, size)], sem.at[2])
                if wait:
                    cp.wait()
                else:
                    cp.start()
                return c
            lax.fori_loop(0, count, body, 0)
        n = hi - lo
        n_big = lax.shift_right_logical(n, ZERO_ROWS_LOG2)
        copies(lo, n_big, ZERO_ROWS)
        done = n_big * ZERO_ROWS
        n_mid = lax.shift_right_logical(n - done, STRIP_CHUNK_LOG2)
        copies(lo + done, n_mid, STRIP_CHUNK)
        done = done + n_mid * STRIP_CHUNK
        copies(lo + done, n - done, 1)

    @pl.when(g == 0)
    def _():
        zero_scr[...] = jnp.zeros_like(zero_scr)
        for wait in (False, True):
            def expert(e, c, wait=wait):
                zero_fill(lo_ref[e], hi_ref[e], wait)
                return c
            lax.fori_loop(0, N_EXPERTS, expert, 0)
            zero_fill(hi_ref[N_EXPERTS - 1], lay.sorted_rows, wait)

    @pl.when(g >= 2)
    def _():
        _wait_rows(xs_hbm, loc.at[slot], n_pairs, sem.at[slot])

    x = _rows_from_tiles(xp_ref, tm).astype(BF16)
    pos = lax.broadcasted_iota(I32, (n_pairs, tm), 0)
    lpr = lpr_ref[0]
    hit = pos == lpr[0:1, :]
    for k in range(1, TOP_K):
        hit = hit | (pos == lpr[k:k + 1, :])
    _rows_to_tiles(loc.at[slot], _dot(hit.astype(BF16), x))

    def strip(e, c):
        s = g * N_EXPERTS + e
        _strip_copies(loc.at[slot], soff_ref[s], xs_hbm, sgs_ref[s], scnt_ref[s], sem.at[slot])
        return c
    lax.fori_loop(0, N_EXPERTS, strip, 0)

    @pl.when(g == lay.n_tiles - 1)
    def _():
        _wait_rows(xs_hbm, loc.at[slot], n_pairs, sem.at[slot])
        if lay.n_tiles > 1:
            _wait_rows(xs_hbm, loc.at[1 - slot], n_pairs, sem.at[1 - slot])


def _dispatch(xp, lp_rows, soff, scnt, sgs, pad_lo, pad_hi, lay):
    tm = lay.tile
    grid_spec = pltpu.PrefetchScalarGridSpec(
        num_scalar_prefetch=5,
        grid=(lay.n_tiles,),
        in_specs=[pl.BlockSpec((tm * ROW_SUBLANES, LANES), lambda g, *_: (g, 0)),
                  pl.BlockSpec((1, ROW_SUBLANES, tm), lambda g, *_: (g, 0, 0))],
        out_specs=pl.BlockSpec(memory_space=pl.ANY),
        scratch_shapes=[pltpu.VMEM((2, TOP_K * tm * ROW_SUBLANES, LANES), F32),
                        pltpu.VMEM((ZERO_ROWS * ROW_SUBLANES, LANES), F32),
                        pltpu.SemaphoreType.DMA((3,))],
    )
    return pl.pallas_call(
        functools.partial(_dispatch_kernel, lay=lay),
        grid_spec=grid_spec,
        out_shape=jax.ShapeDtypeStruct((lay.sorted_rows * ROW_SUBLANES, LANES), F32),
        compiler_params=_params(("arbitrary",)),
        name="dispatch",
    )(soff, scnt, sgs, pad_lo, pad_hi, xp, lp_rows)


def _experts_kernel(be_ref, nv_ref, xs_ref, wg_ref, wu_ref, wd_ref, bg_ref, bu_ref, bd_ref, ys_ref, wbf):
    bm = EXPERT_BLOCK
    i = pl.program_id(0)
    n_valid = nv_ref[0]

    @pl.when(i < n_valid)
    def _():
        @pl.when((i == 0) | (be_ref[i] != be_ref[jnp.maximum(i - 1, 0)]))
        def _():
            for w, w_ref in enumerate((wg_ref, wu_ref, wd_ref)):
                for c in range(0, D_MODEL, WEIGHT_CAST_ROWS):
                    wbf[w, c:c + WEIGHT_CAST_ROWS, :] = w_ref[0, c:c + WEIGHT_CAST_ROWS, :].astype(BF16)

        part = bm // EXPERT_PARTS
        for j in range(EXPERT_PARTS):
            tiles = pl.ds(j * part * ROW_SUBLANES, part * ROW_SUBLANES)
            x = _rows_from_tiles(xs_ref.at[tiles], part).astype(BF16)
            g = jnp.minimum(_dot(x, wbf[0]) + bg_ref[0], SWIGLU_LIMIT)
            u = jnp.clip(_dot(x, wbf[1]) + bu_ref[0], -SWIGLU_LIMIT, SWIGLU_LIMIT)
            mid = ((u + 1.0) * (g * _sigmoid(SWIGLU_ALPHA * g))).astype(BF16)
            _rows_to_tiles(ys_ref.at[tiles], _dot(mid, wbf[2]) + bd_ref[0])

    @pl.when(i >= n_valid)
    def _():
        ys_ref[...] = jnp.zeros_like(ys_ref)


def _experts(block_e, n_valid, xs, wts, lay):
    bm = EXPERT_BLOCK
    tile_rows = bm * ROW_SUBLANES
    wspec = pl.BlockSpec((1, D_MODEL, D_FF), lambda i, be, nv: (be[i], 0, 0))
    bspec = pl.BlockSpec((1, 1, D_FF), lambda i, be, nv: (be[i], 0, 0))
    grid_spec = pltpu.PrefetchScalarGridSpec(
        num_scalar_prefetch=2,
        grid=(lay.n_blocks,),
        in_specs=[pl.BlockSpec((tile_rows, LANES), lambda i, be, nv: (jnp.minimum(i, nv[0] - 1), 0)),
                  wspec, wspec, wspec, bspec, bspec, bspec],
        out_specs=pl.BlockSpec((tile_rows, LANES), lambda i, be, nv: (i, 0)),
        scratch_shapes=[pltpu.VMEM((3, D_MODEL, D_FF), BF16)],
    )
    return pl.pallas_call(
        functools.partial(_experts_kernel),
        grid_spec=grid_spec,
        out_shape=jax.ShapeDtypeStruct((lay.sorted_rows * ROW_SUBLANES, LANES), F32),
        compiler_params=_params(("arbitrary",)),
        name="experts",
    )(block_e, n_valid, xs, wts['w_gate'], wts['w_up'], wts['w_down'], wts['b_gate'], wts['b_up'], wts['b_down'])


def _combine_kernel(soff_ref, scnt_ref, sgs_ref, h_ref, tw_ref, lpc_ref, pp_ref, ps_ref, ys_hbm, g3_ref, wpg_ref,
                    wpp_ref, gf_ref, yp_ref, ysm_ref, loc, sem, *, lay, n_sample):
    tm = lay.tile
    n_pairs = TOP_K * tm
    g = pl.program_id(0)
    slot = g % 2

    def fetch(tile, s):
        def strip(e, c):
            i = tile * N_EXPERTS + e
            _strip_copies(ys_hbm, sgs_ref[i], loc.at[s], soff_ref[i], scnt_ref[i], sem.at[s])
            return c
        lax.fori_loop(0, N_EXPERTS, strip, 0)

    @pl.when(g == 0)
    def _():
        fetch(0, 0)

    @pl.when(g + 1 < lay.n_tiles)
    def _():
        fetch(g + 1, 1 - slot)

    _wait_rows(ys_hbm, loc.at[slot], n_pairs, sem.at[slot])
    y_rows = _rows_from_tiles(loc.at[slot], n_pairs).astype(BF16)
    tw = tw_ref[...]
    lpc = lpc_ref[...]
    pos = lax.broadcasted_iota(I32, (tm, n_pairs), 1)
    weight = jnp.zeros((tm, n_pairs), F32)
    for k in range(TOP_K):
        weight = jnp.where(pos == lpc[:, k:k + 1], tw[:, k:k + 1], weight)
    y = _dot(weight.astype(BF16), y_rows)
    h2 = h_ref[...] + y
    gate = _sigmoid(_dot(_rms(h2, g3_ref[...]).astype(BF16), wpg_ref[...]))
    p = pp_ref[...]
    p = jnp.where(g == lay.n_tiles - 1, jnp.concatenate([p[:tm - n_sample], ps_ref[...]], axis=0), p)
    h3 = h2 + gate * _dot(p.astype(BF16), wpp_ref[...])
    out = _rms(h3, gf_ref[...])
    yp_ref[...] = out

    @pl.when(g == lay.n_tiles - 1)
    def _():
        ysm_ref[...] = out[tm - n_sample:, :]


def _combine(hbuf, tw, lp_cols, p_prompt, p_sample, ys, soff, scnt, sgs, wts, lay):
    tm = lay.tile
    n_prompt, n_sample = p_prompt.shape[0], p_sample.shape[0]
    assert 0 < n_sample <= tm and (lay.n_tiles - 1) * tm < n_prompt
    consts = [wts['g3'], wts['w_ple_gate'], wts['w_ple_proj'], wts['g_final']]
    row = lambda w: pl.BlockSpec((tm, w), lambda g, *_: (g, 0))
    grid_spec = pltpu.PrefetchScalarGridSpec(
        num_scalar_prefetch=3,
        grid=(lay.n_tiles,),
        in_specs=[row(D_MODEL), row(LANES), row(LANES), row(D_PLE),
                  pl.BlockSpec((n_sample, D_PLE), lambda g, *_: (0, 0)), pl.BlockSpec(memory_space=pl.ANY)]
        + [pl.BlockSpec(c.shape, lambda g, *_: (0, 0), pipeline_mode=pl.Buffered(1)) for c in consts],
        out_specs=[row(D_MODEL), pl.BlockSpec((n_sample, D_MODEL), lambda g, *_: (0, 0))],
        scratch_shapes=[pltpu.VMEM((2, TOP_K * tm * ROW_SUBLANES, LANES), F32), pltpu.SemaphoreType.DMA((2,))],
    )
    return pl.pallas_call(
        functools.partial(_combine_kernel, lay=lay, n_sample=n_sample),
        grid_spec=grid_spec,
        out_shape=[jax.ShapeDtypeStruct((n_prompt, D_MODEL), F32), jax.ShapeDtypeStruct((n_sample, D_MODEL), F32)],
        compiler_params=_params(("arbitrary",)),
        name="combine",
    )(soff, scnt, sgs, hbuf, tw, lp_cols, p_prompt, p_sample, ys, *consts)


def _prepare_weights(norm1_g, w_in, w_gk2, b_gk, gla_norm_g, v_norm_g, v_norm_b, w_sp, b_sp, w_o, norm2_g,
                     w_router, b_router, w_gate, b_gate, w_up, b_up, w_down, b_down, norm3_g, w_ple_gate,
                     w_ple_proj, final_g):
    o_gk = 2 * GLA_DK_T + 2 * GLA_DV_T
    w_a = w_in[:, :o_gk].astype(BF16)
    w_b = w_in[:, o_gk + GK_RANK:].astype(BF16)
    w_gk1 = jnp.pad(w_in[:, o_gk:o_gk + GK_RANK], ((0, 0), (0, LANES - GK_RANK))).astype(BF16)
    w_gk2p = jnp.pad(w_gk2, ((0, LANES - GK_RANK), (0, 0))).astype(BF16)
    row = lambda z: z.reshape(1, -1).astype(F32)
    return dict(
        g1=row(norm1_g), w_a=w_a, w_b=w_b, w_gk1=w_gk1, w_gk2=w_gk2p, b_gk=row(b_gk), gla_g=row(gla_norm_g),
        vn_g=row(v_norm_g), vn_b=row(v_norm_b), w_sp=w_sp, b_sp_t=b_sp.T,
        mix_w0=row(jnp.repeat(w_sp[:, 0, 0], MIX_DH)), mix_b0=row(jnp.repeat(b_sp[:, 0], MIX_DH)),
        w_o=w_o.astype(BF16), g2=row(norm2_g),
        w_router=jnp.pad(w_router, ((0, 0), (0, LANES - N_EXPERTS))),
        b_router=jnp.pad(row(b_router), ((0, 0), (0, LANES - N_EXPERTS))),
        w_gate=w_gate, w_up=w_up, w_down=w_down,
        b_gate=b_gate.reshape(N_EXPERTS, 1, D_FF), b_up=b_up.reshape(N_EXPERTS, 1, D_FF),
        b_down=b_down.reshape(N_EXPERTS, 1, D_MODEL),
        g3=row(norm3_g), w_ple_gate=w_ple_gate.astype(BF16), w_ple_proj=w_ple_proj.astype(BF16),
        g_final=row(final_g))


def _routing_tables(counts, counts_before, tile_counts, local_off, lay):
    bm = EXPERT_BLOCK
    padded = (counts + bm - 1) // bm * bm
    pad_end = jnp.cumsum(padded)
    pad_start = pad_end - padded
    global_start = pad_start[None, :] + counts_before
    n_valid = (pad_end[-1] // bm).astype(I32)
    blk = jnp.minimum(jnp.arange(lay.n_blocks, dtype=I32), n_valid - 1)
    block_e = jnp.sum(pad_end[None, :] <= (blk * bm)[:, None], axis=1).astype(I32)
    flat = lambda z: z.reshape(-1).astype(I32)
    return dict(soff=flat(local_off), scnt=flat(tile_counts), sgs=flat(global_start),
                pad_lo=(pad_start + counts).astype(I32), pad_hi=pad_end.astype(I32), block_e=block_e,
                n_valid=n_valid.reshape(1))


def _layer(x_prompt, x_sample, state, p_prompt, p_sample, wts):
    n_p, len_p, _ = x_prompt.shape
    n_s, len_s, _ = x_sample.shape
    assert len_s == 1, "the sample group carries one new token per sequence"
    t_p = n_p * len_p
    t_all = t_p + n_s

    xs = x_sample.reshape(n_s, D_MODEL)
    q, k, a, v, og, ob, vn = _mixer_sample_in(xs, wts)
    st_sample, o = _state_update(state, q, k, a, v)
    h_sample = _mixer_sample_out(xs, o, og, ob, wts)
    hbuf, st_prompt = _mixer_prompt(x_prompt, h_sample, wts)

    lay = _Layout(t_all)
    xp, top_w, lp_cols, lp_rows, counts_before, tile_counts, local_off, counts = _router(hbuf, wts)
    ne = N_EXPERTS
    tab = _routing_tables(counts[0, :ne], counts_before[:, 0, :ne], tile_counts[:, 0, :ne], local_off[:, 0, :ne], lay)
    x_sorted = _dispatch(xp, lp_rows, tab['soff'], tab['scnt'], tab['sgs'], tab['pad_lo'], tab['pad_hi'], lay)
    y_sorted = _experts(tab['block_e'], tab['n_valid'], x_sorted, wts, lay)
    y_prompt, y_sample = _combine(hbuf, top_w, lp_cols, p_prompt.reshape(t_p, D_PLE), p_sample.reshape(n_s, D_PLE),
                                  y_sorted, tab['soff'], tab['scnt'], tab['sgs'], wts, lay)
    return (y_prompt.reshape(n_p, len_p, D_MODEL), y_sample.reshape(n_s, len_s, D_MODEL),
            st_prompt, st_sample, vn.reshape(n_s, len_s, D_GMLP))


def kernel(x_prompt, x_sample, state_gla, p_prompt, p_sample, norm1_g, w_in, w_gk2, b_gk, gla_norm_g, v_norm_g,
           v_norm_b, w_sp, b_sp, w_o, norm2_g, w_router, b_router, w_gate, b_gate, w_up, b_up, w_down, b_down,
           norm3_g, w_ple_gate, w_ple_proj, final_g):
    assert w_in.shape[0] == 1, "single-layer trunk"
    wts = _prepare_weights(norm1_g[0], w_in[0], w_gk2[0], b_gk[0], gla_norm_g[0], v_norm_g[0], v_norm_b[0],
                           w_sp[0], b_sp[0], w_o[0], norm2_g[0], w_router[0], b_router[0], w_gate[0], b_gate[0],
                           w_up[0], b_up[0], w_down[0], b_down[0], norm3_g[0], w_ple_gate[0], w_ple_proj[0],
                           final_g)
    y_p, y_s, st_p, st_s, vn_s = _layer(x_prompt, x_sample, state_gla[0], p_prompt[0], p_sample[0], wts)
    return (y_p, y_s, st_p[None], st_s[None].astype(state_gla.dtype), vn_s[None])
```

```python
import functools

import jax
import jax.numpy as jnp
from jax import lax
from jax.experimental import pallas as pl
from jax.experimental.pallas import tpu as pltpu

F32 = jnp.float32
BF16 = jnp.bfloat16
I32 = jnp.int32
U32 = jnp.uint32

D_MODEL = 1024
GLA_HEADS = 4
GLA_DK = 128
GLA_DV = 256
GLA_DK_T = GLA_HEADS * GLA_DK
GLA_DV_T = GLA_HEADS * GLA_DV
GK_RANK = 16
GATE_NORMALIZER = 16.0
GLA_CHUNK = 64
GLA_CHUNK_LOG2 = 6
D_GMLP = 1024
MIX_HEADS = 4
MIX_DH = D_GMLP // MIX_HEADS
MIX_CHUNK = 128
MIX_CHUNK_LOG2 = 7
assert 1 << GLA_CHUNK_LOG2 == GLA_CHUNK and 1 << MIX_CHUNK_LOG2 == MIX_CHUNK
N_EXPERTS = 32
TOP_K = 4
D_FF = 1024
SWIGLU_LIMIT = 7.0
SWIGLU_ALPHA = 1.702
D_PLE = 256
EPS = 1e-6

LANES = 128
ROW_SUBLANES = D_MODEL // LANES
VMEM_LIMIT_BYTES = 56 * 1024 * 1024

OFF_Q, OFF_K, OFF_V, OFF_R, OFF_U, OFF_VG, OFF_GA, OFF_GB, N_MAIN = (
    0, 512, 1024, 2048, 3072, 4096, 5120, 6144, 7168)

MIXER_TILE = 512
MIXER_PARTS = 2
STATE_TOKENS = 8
EXPERT_BLOCK = 512
EXPERT_PARTS = 2
WEIGHT_CAST_ROWS = 256
STRIP_CHUNK = 16
STRIP_CHUNK_LOG2 = 4
ZERO_ROWS = 64
ZERO_ROWS_LOG2 = 6
assert 1 << STRIP_CHUNK_LOG2 == STRIP_CHUNK and 1 << ZERO_ROWS_LOG2 == ZERO_ROWS


def _dot(a, b):
    return jnp.dot(a, b, preferred_element_type=F32)


def _dot_nt(a, b):
    return lax.dot_general(a, b, (((1,), (1,)), ((), ())), preferred_element_type=F32)


def _dot_tn(a, b):
    return lax.dot_general(a, b, (((0,), (0,)), ((), ())), preferred_element_type=F32)


def _rms(x, g):
    return x * lax.rsqrt(jnp.mean(x * x, axis=-1, keepdims=True) + EPS) * g


def _sigmoid(x):
    return 1.0 / (1.0 + jnp.exp(-x))


def _gelu(x):
    return 0.5 * x * (1.0 + lax.erf(x * (2.0 ** -0.5)))


def _log_sigmoid(x):
    return jnp.minimum(x, 0.0) - jnp.log1p(jnp.exp(-jnp.abs(x)))


def _split3(x):
    hi = x.astype(BF16)
    r1 = x - hi.astype(F32)
    mid = r1.astype(BF16)
    lo = (r1 - mid.astype(F32)).astype(BF16)
    return hi, mid, lo


def _const_spec(shape):
    nd = len(shape)
    return pl.BlockSpec(shape, lambda *_: (0,) * nd, pipeline_mode=pl.Buffered(1))


def _params(sem):
    return pltpu.CompilerParams(dimension_semantics=sem, vmem_limit_bytes=VMEM_LIMIT_BYTES)


def _main_cols(wa_ref, wb_ref, lo, hi):
    if hi <= OFF_U:
        return wa_ref[:, lo:hi]
    return wb_ref[:, lo - OFF_U:hi - OFF_U]


def _project_gla_inputs(n, wm, wgk1_ref, wgk2_ref, bgk_ref):
    q = _dot(n, wm(OFF_Q, OFF_K)) * (GLA_DK ** -0.5)
    k = _dot(n, wm(OFF_K, OFF_V))
    v = _dot(n, wm(OFF_V, OFF_R))
    gk = _dot(n, wgk1_ref[...]).astype(BF16)
    log_a = _log_sigmoid(_dot(gk, wgk2_ref[...]) + bgk_ref[...]) * (1.0 / GATE_NORMALIZER)
    return q, k, v, log_a


def _gmlp_inputs(n, wm, vng_ref, vnb_ref):
    u = _gelu(_dot(n, wm(OFF_U, OFF_VG)))
    vg = _gelu(_dot(n, wm(OFF_VG, OFF_GA)))
    mu = jnp.mean(vg, axis=-1, keepdims=True)
    vc = vg - mu
    var = jnp.mean(vc * vc, axis=-1, keepdims=True)
    vn = vc * lax.rsqrt(var + EPS) * vng_ref[...] + vnb_ref[...]
    ug = u * _sigmoid(_dot(n, wm(OFF_GB, N_MAIN)))
    return ug, vn


def _gla_out_gate(n, wm):
    r = _dot(n, wm(OFF_R, OFF_U))
    ga = _dot(n, wm(OFF_GA, OFF_GB))
    return r * _sigmoid(r) * _sigmoid(ga)


def _head_rms(o, g):
    return o * lax.rsqrt(jnp.mean(o * o, axis=-1, keepdims=True) + EPS) * g


def _mixer_prompt_kernel(x_ref, hs_ref, *refs, tm, n_seq, n_tiles):
    b = pl.program_id(0)
    t = pl.program_id(1)
    h_ref = refs[12]

    @pl.when(b < n_seq)
    def _():
        _mixer_prompt_tile(x_ref, *refs, tm=tm, n_tiles=n_tiles)

    @pl.when((b == n_seq) & (t == 0))
    def _():
        h_ref[0:hs_ref.shape[0], :] = hs_ref[...]


def _mixer_prompt_tile(x_ref, g1_ref, wa_ref, wb_ref, wgk1_ref, wgk2_ref, bgk_ref, glag_ref, vng_ref, vnb_ref,
                       wsp_ref, bspt_ref, wo_ref,
                       h_ref, st_ref,
                       n_scr, oa_scr, s_scr, *, tm, n_tiles):
    t = pl.program_id(1)

    @pl.when(t == 0)
    def _():
        s_scr[...] = jnp.zeros_like(s_scr)

    wm = functools.partial(_main_cols, wa_ref, wb_ref)
    pm = tm // MIXER_PARTS
    for part in range(MIXER_PARTS):
        rows = pl.ds(part * pm, pm)
        _mixer_prompt_part(x_ref.at[0, rows], g1_ref, wm, wgk1_ref, wgk2_ref, bgk_ref, glag_ref, vng_ref, vnb_ref,
                           wsp_ref, bspt_ref, wo_ref, h_ref.at[rows], n_scr.at[rows], oa_scr.at[rows], s_scr, tm=pm)

    @pl.when(t == n_tiles - 1)
    def _():
        for h in range(GLA_HEADS):
            st_ref[0, h] = s_scr[h].T


def _mixer_prompt_part(x_ref, g1_ref, wm, wgk1_ref, wgk2_ref, bgk_ref, glag_ref, vng_ref, vnb_ref,
                       wsp_ref, bspt_ref, wo_ref, h_ref, n_scr, oa_scr, s_scr, *, tm):
    x = x_ref[...]
    n = _rms(x, g1_ref[...]).astype(BF16)
    n_scr[...] = n
    q, k, v, log_a = _project_gla_inputs(n, wm, wgk1_ref, wgk2_ref, bgk_ref)
    v = v.astype(BF16)

    row = lax.broadcasted_iota(I32, (tm, tm), 0)
    col = lax.broadcasted_iota(I32, (tm, tm), 1)
    same_chunk = lax.shift_right_logical(row, GLA_CHUNK_LOG2) == lax.shift_right_logical(col, GLA_CHUNK_LOG2)
    causal = same_chunk & (row >= col)
    tri = causal.astype(BF16)
    blk = same_chunk.astype(BF16)
    hi, mid, lo = _split3(log_a)
    b = _dot(tri, hi) + _dot(tri, mid) + _dot(tri, lo)
    b_end = _dot(blk, hi) + _dot(blk, mid) + _dot(blk, lo)
    qe = (q * jnp.exp(b)).astype(BF16)
    ke = (k * jnp.exp(-b)).astype(BF16)
    kd = (k * jnp.exp(b_end - b)).astype(BF16)
    dec = jnp.exp(b_end)
    glag = glag_ref[...]
    for h in range(GLA_HEADS):
        ks = slice(h * GLA_DK, (h + 1) * GLA_DK)
        vs = slice(h * GLA_DV, (h + 1) * GLA_DV)
        vh = v[:, vs]
        att = jnp.where(causal, _dot_nt(qe[:, ks], ke[:, ks]), 0.0).astype(BF16)
        o_intra = _dot(att, vh)
        s_t = s_scr[h]
        o_inter = []
        for c in range(tm // GLA_CHUNK):
            rs = slice(c * GLA_CHUNK, (c + 1) * GLA_CHUNK)
            o_inter.append(_dot_nt(qe[rs, ks], s_t.astype(BF16)))
            s_t = s_t * dec[c * GLA_CHUNK:c * GLA_CHUNK + 1, ks] + _dot_tn(vh[rs], kd[rs, ks])
        s_scr[h] = s_t
        oa_scr[:, vs] = _head_rms(o_intra + jnp.concatenate(o_inter, axis=0), glag)

    n = n_scr[...]
    oa_scr[...] = oa_scr[...] * _gla_out_gate(n, wm)
    ug, vn = _gmlp_inputs(n, wm, vng_ref, vnb_ref)
    vn = vn.astype(BF16)
    reps = tm // MIX_CHUNK
    mix_mask = (lax.shift_right_logical(row, MIX_CHUNK_LOG2) == lax.shift_right_logical(col, MIX_CHUNK_LOG2)) & (
        row >= col)
    for h in range(MIX_HEADS):
        w = jnp.where(mix_mask, jnp.tile(wsp_ref[h], (reps, reps)), 0.0).astype(BF16)
        cs = slice(h * MIX_DH, (h + 1) * MIX_DH)
        mix = _dot(w, vn[:, cs]) + jnp.tile(bspt_ref[:, h:h + 1], (reps, 1))
        oa_scr[:, cs] = oa_scr[:, cs] + ug[:, cs] * mix

    h_ref[...] = x + _dot(oa_scr[...].astype(BF16), wo_ref[...])


def _mixer_prompt(x, h_sample, wts):
    n_seq, seq_len, _ = x.shape
    n_s = h_sample.shape[0]
    tm = min(MIXER_TILE, seq_len)
    assert seq_len % tm == 0 and tm % MIX_CHUNK == 0 and n_s <= tm
    n_tiles = seq_len // tm
    consts = [wts['g1'], wts['w_a'], wts['w_b'], wts['w_gk1'], wts['w_gk2'], wts['b_gk'], wts['gla_g'], wts['vn_g'],
              wts['vn_b'], wts['w_sp'], wts['b_sp_t'], wts['w_o']]
    kern = functools.partial(_mixer_prompt_kernel, tm=tm, n_seq=n_seq, n_tiles=n_tiles)
    last = n_seq - 1

    def prompt_tile(b, t):
        return jnp.minimum(b, last), jnp.where(b < n_seq, t, n_tiles - 1)

    def x_map(b, t):
        bb, tt = prompt_tile(b, t)
        return bb, tt, 0

    def h_map(b, t):
        return jnp.where(b < n_seq, b * n_tiles + t, n_seq * n_tiles), 0

    return pl.pallas_call(
        kern,
        grid=(n_seq + 1, n_tiles),
        in_specs=[pl.BlockSpec((1, tm, D_MODEL), x_map), _const_spec(h_sample.shape)]
        + [_const_spec(c.shape) for c in consts],
        out_specs=[pl.BlockSpec((tm, D_MODEL), h_map),
                   pl.BlockSpec((1, GLA_HEADS, GLA_DK, GLA_DV), lambda b, t: (jnp.minimum(b, last), 0, 0, 0))],
        out_shape=[jax.ShapeDtypeStruct((n_seq * seq_len + n_s, D_MODEL), F32),
                   jax.ShapeDtypeStruct((n_seq, GLA_HEADS, GLA_DK, GLA_DV), F32)],
        scratch_shapes=[pltpu.VMEM((tm, D_MODEL), BF16),
                        pltpu.VMEM((tm, D_MODEL), F32),
                        pltpu.VMEM((GLA_HEADS, GLA_DV, GLA_DK), F32)],
        compiler_params=_params(("arbitrary", "arbitrary")),
        name="mixer_prompt",
    )(x, h_sample, *consts)


def _mixer_sample_in_kernel(x_ref, g1_ref, wa_ref, wb_ref, wgk1_ref, wgk2_ref, bgk_ref, vng_ref, vnb_ref,
                            mixw_ref, mixb_ref,
                            q_ref, k_ref, a_ref, v_ref, og_ref, ob_ref, vn_ref):
    wm = functools.partial(_main_cols, wa_ref, wb_ref)
    n = _rms(x_ref[...], g1_ref[...]).astype(BF16)
    q, k, v, log_a = _project_gla_inputs(n, wm, wgk1_ref, wgk2_ref, bgk_ref)
    q_ref[...] = q
    k_ref[...] = k
    a_ref[...] = jnp.exp(log_a)
    v_ref[...] = v
    og_ref[...] = _gla_out_gate(n, wm)
    ug, vn = _gmlp_inputs(n, wm, vng_ref, vnb_ref)
    vn_ref[...] = vn
    ob_ref[...] = ug * (mixw_ref[...] * vn + mixb_ref[...])


def _mixer_sample_in(x2, wts):
    n_seq = x2.shape[0]
    consts = [wts['g1'], wts['w_a'], wts['w_b'], wts['w_gk1'], wts['w_gk2'], wts['b_gk'], wts['vn_g'], wts['vn_b'],
              wts['mix_w0'], wts['mix_b0']]
    widths = [GLA_DK_T, GLA_DK_T, GLA_DK_T, GLA_DV_T, D_MODEL, D_MODEL, D_GMLP]
    return pl.pallas_call(
        _mixer_sample_in_kernel,
        grid=(1,),
        in_specs=[_const_spec(x2.shape)] + [_const_spec(c.shape) for c in consts],
        out_specs=[pl.BlockSpec((n_seq, w), lambda i: (0, 0)) for w in widths],
        out_shape=[jax.ShapeDtypeStruct((n_seq, w), F32) for w in widths],
        compiler_params=_params(("arbitrary",)),
        name="mixer_sample_in",
    )(x2, *consts)


def _state_update_kernel(s_ref, q_ref, k_ref, a_ref, v_ref, so_ref, o_ref):
    for j in range(STATE_TOKENS):
        for h in range(GLA_HEADS):
            ks = slice(h * GLA_DK, (h + 1) * GLA_DK)
            vs = slice(h * GLA_DV, (h + 1) * GLA_DV)
            s_new = a_ref[0, ks, j:j + 1] * s_ref[j, h] + k_ref[0, ks, j:j + 1] * v_ref[j:j + 1, vs]
            so_ref[j, h] = s_new
            o_ref[j:j + 1, vs] = jnp.sum(q_ref[0, ks, j:j + 1] * s_new, axis=0, keepdims=True)


def _state_update(state, q, k, a, v):
    n_seq = state.shape[0]
    tb = STATE_TOKENS
    assert n_seq % tb == 0

    def cols(z):
        return z.reshape(n_seq // tb, tb, GLA_DK_T).transpose(0, 2, 1)

    col_spec = pl.BlockSpec((1, GLA_DK_T, tb), lambda i: (i, 0, 0))
    st_spec = pl.BlockSpec((tb, GLA_HEADS, GLA_DK, GLA_DV), lambda i: (i, 0, 0, 0))
    row_spec = pl.BlockSpec((tb, GLA_DV_T), lambda i: (i, 0))
    return pl.pallas_call(
        _state_update_kernel,
        grid=(n_seq // tb,),
        in_specs=[st_spec, col_spec, col_spec, col_spec, row_spec],
        out_specs=[st_spec, row_spec],
        out_shape=[jax.ShapeDtypeStruct(state.shape, F32), jax.ShapeDtypeStruct((n_seq, GLA_DV_T), F32)],
        compiler_params=_params(("arbitrary",)),
        name="state_update",
    )(state, cols(q), cols(k), cols(a), v)


def _mixer_sample_out_kernel(x_ref, o_ref, og_ref, ob_ref, glag_ref, wo_ref, h_ref):
    glag = glag_ref[...]
    parts = []
    for h in range(GLA_HEADS):
        vs = slice(h * GLA_DV, (h + 1) * GLA_DV)
        parts.append(_head_rms(o_ref[:, vs], glag))
    merged = jnp.concatenate(parts, axis=1) * og_ref[...] + ob_ref[...]
    h_ref[...] = x_ref[...] + _dot(merged.astype(BF16), wo_ref[...])


def _mixer_sample_out(x2, o, og, ob, wts):
    n_seq = x2.shape[0]
    consts = [x2, o, og, ob, wts['gla_g'], wts['w_o']]
    return pl.pallas_call(
        _mixer_sample_out_kernel,
        grid=(1,),
        in_specs=[_const_spec(c.shape) for c in consts],
        out_specs=pl.BlockSpec((n_seq, D_MODEL), lambda i: (0, 0)),
        out_shape=jax.ShapeDtypeStruct((n_seq, D_MODEL), F32),
        compiler_params=_params(("arbitrary",)),
        name="mixer_sample_out",
    )(*consts)


def _router_kernel(h_ref, g2_ref, wr_ref, br_ref, xp_ref, tw_ref, lpc_ref, lpr_ref, cb_ref, tc_ref, lo_ref, cnt_ref, cnt_scr,
                   *, tm):
    i = pl.program_id(0)

    @pl.when(i == 0)
    def _():
        cnt_scr[...] = jnp.zeros_like(cnt_scr)

    cb_ref[0] = cnt_scr[...].astype(I32)

    hn = _rms(h_ref[...], g2_ref[...])
    hn_hi = hn.astype(BF16)
    hn_lo = (hn - hn_hi.astype(F32)).astype(BF16)
    w = wr_ref[...]
    w_hi = w.astype(BF16)
    w_lo = (w - w_hi.astype(F32)).astype(BF16)
    logits = _dot(hn_hi, w_hi) + _dot(hn_lo, w_hi) + _dot(hn_hi, w_lo) + br_ref[...]

    _rows_to_tiles(xp_ref, hn)

    lane = lax.broadcasted_iota(I32, (tm, LANES), 1)
    neg = jnp.float32(-jnp.inf)
    l = jnp.where(lane < N_EXPERTS, logits, neg)
    vals, hots = [], []
    for _ in range(TOP_K):
        m = jnp.max(l, axis=-1, keepdims=True)
        idx = jnp.min(jnp.where(l == m, lane, LANES), axis=-1, keepdims=True)
        hot = lane == idx
        l = jnp.where(hot, neg, l)
        vals.append(m)
        hots.append(hot)
    exps = [jnp.exp(v - vals[0]) for v in vals]
    denom = exps[0] + exps[1] + exps[2] + exps[3]

    member = (hots[0] | hots[1] | hots[2] | hots[3]).astype(BF16)
    row = lax.broadcasted_iota(I32, (tm, tm), 0)
    col = lax.broadcasted_iota(I32, (tm, tm), 1)
    in_tile = _dot((row > col).astype(BF16), member)
    tile_cnt = jnp.sum(member.astype(F32), axis=0, keepdims=True)
    e_row = lax.broadcasted_iota(I32, (LANES, LANES), 0)
    e_col = lax.broadcasted_iota(I32, (LANES, LANES), 1)
    lower = (e_row < e_col).astype(BF16)
    c_hi, c_mid, c_lo = _split3(jnp.broadcast_to(tile_cnt, (ROW_SUBLANES, LANES)))
    local_off = (_dot(c_hi, lower) + _dot(c_mid, lower) + _dot(c_lo, lower))[0:1, :]
    tc_ref[0] = tile_cnt.astype(I32)
    lo_ref[0] = local_off.astype(I32)
    cnt_scr[...] = cnt_scr[...] + tile_cnt
    cnt_ref[...] = cnt_scr[...].astype(I32)

    place = local_off + in_tile
    tw = jnp.zeros((tm, LANES), F32)
    lp = jnp.zeros((tm, LANES), F32)
    for kk in range(TOP_K):
        sel = lane == kk
        tw = jnp.where(sel, exps[kk] / denom, tw)
        lp = jnp.where(sel, jnp.sum(jnp.where(hots[kk], place, 0.0), axis=-1, keepdims=True), lp)
    tw_ref[...] = tw
    lpc_ref[...] = lp.astype(I32)
    lpr_ref[0] = lp.T[0:ROW_SUBLANES, :].astype(I32)


def _router_tile(t_all):
    for tm in (512, 384, 256, 128):
        if t_all % tm == 0:
            return tm
    raise ValueError(f"token count {t_all} must be a multiple of 128")


def _router(hbuf, wts):
    t_all = hbuf.shape[0]
    tm = _router_tile(t_all)
    consts = [wts['g2'], wts['w_router'], wts['b_router']]
    row = lambda w: pl.BlockSpec((tm, w), lambda i: (i, 0))
    n_tiles = t_all // tm
    per_tile = pl.BlockSpec((1, 1, LANES), lambda i: (i, 0, 0))
    tile_vec = jax.ShapeDtypeStruct((n_tiles, 1, LANES), I32)
    return pl.pallas_call(
        functools.partial(_router_kernel, tm=tm),
        grid=(t_all // tm,),
        in_specs=[row(D_MODEL)] + [_const_spec(c.shape) for c in consts],
        out_specs=[pl.BlockSpec((tm * ROW_SUBLANES, LANES), lambda i: (i, 0)), row(LANES), row(LANES),
                   pl.BlockSpec((1, ROW_SUBLANES, tm), lambda i: (i, 0, 0)),
                   per_tile, per_tile, per_tile, pl.BlockSpec((1, LANES), lambda i: (0, 0))],
        out_shape=[jax.ShapeDtypeStruct((t_all * ROW_SUBLANES, LANES), F32),
                   jax.ShapeDtypeStruct((t_all, LANES), F32),
                   jax.ShapeDtypeStruct((t_all, LANES), I32),
                   jax.ShapeDtypeStruct((n_tiles, ROW_SUBLANES, tm), I32),
                   tile_vec, tile_vec, tile_vec, jax.ShapeDtypeStruct((1, LANES), I32)],
        scratch_shapes=[pltpu.VMEM((1, LANES), F32)],
        compiler_params=_params(("arbitrary",)),
        name="router",
    )(hbuf, *consts)


def _row_tile(r):
    start = r * ROW_SUBLANES
    if not isinstance(r, int):
        start = pl.multiple_of(start, ROW_SUBLANES)
    return pl.ds(start, ROW_SUBLANES)


def _tile_of(ref, row):
    return ref.at[_row_tile(row)]


def _rows_from_tiles(ref, n_rows):
    return jnp.concatenate([ref[pl.ds(s, n_rows, stride=ROW_SUBLANES), :] for s in range(ROW_SUBLANES)], axis=1)


def _rows_to_tiles(ref, x):
    for s in range(ROW_SUBLANES):
        ref[pl.ds(s, x.shape[0], stride=ROW_SUBLANES), :] = x[:, s * LANES:(s + 1) * LANES]


def _tiles(row, n_rows):
    return pl.ds(pl.multiple_of(row * ROW_SUBLANES, ROW_SUBLANES), n_rows * ROW_SUBLANES)


class _Layout:
    def __init__(self, t_all):
        bm = EXPERT_BLOCK
        self.t_all = t_all
        self.tile = _router_tile(t_all)
        self.n_tiles = t_all // self.tile
        self.n_blocks = -(-(t_all * TOP_K + N_EXPERTS * (bm - 1)) // bm)
        self.sorted_rows = self.n_blocks * bm


def _strip_copies(src_ref, src_row, dst_ref, dst_row, n, sem):
    def copy(offset, size):
        pltpu.make_async_copy(src_ref.at[_tiles(src_row + offset, size)], dst_ref.at[_tiles(dst_row + offset, size)],
                              sem).start()

    n_chunks = lax.shift_right_logical(n, STRIP_CHUNK_LOG2)

    def chunk(j, c):
        copy(j * STRIP_CHUNK, STRIP_CHUNK)
        return c
    lax.fori_loop(0, n_chunks, chunk, 0)

    done = n_chunks * STRIP_CHUNK
    size = STRIP_CHUNK // 2
    while size >= 1:
        part = n & size

        @pl.when(part != 0)
        def _(done=done, size=size):
            copy(done, size)
        done = done + part
        size //= 2


def _wait_rows(hbm_ref, vmem_ref, n_rows, sem):
    pltpu.make_async_copy(hbm_ref.at[pl.ds(0, n_rows * ROW_SUBLANES)], vmem_ref.at[pl.ds(0, n_rows * ROW_SUBLANES)],
                          sem).wait()


def _dispatch_kernel(soff_ref, scnt_ref, sgs_ref, lo_ref, hi_ref, xp_ref, lpr_ref, xs_hbm, loc, zero_scr, sem,
                     *, lay):
    tm = lay.tile
    n_pairs = TOP_K * tm
    g = pl.program_id(0)
    slot = g % 2

    def zero_fill(lo, hi, wait):
        def copies(first, count, size):
            def body(j, c):
                cp = pltpu.make_async_copy(zero_scr.at[pl.ds(0, size * ROW_SUBLANES)],
                                           xs_hbm.at[_tiles(first + j * size, size)], sem.at[2])
                if wait:
                    cp.wait()
                else:
                    cp.start()
                return c
            lax.fori_loop(0, count, body, 0)
        n = hi - lo
        n_big = lax.shift_right_logical(n, ZERO_ROWS_LOG2)
        copies(lo, n_big, ZERO_ROWS)
        done = n_big * ZERO_ROWS
        n_mid = lax.shift_right_logical(n - done, STRIP_CHUNK_LOG2)
        copies(lo + done, n_mid, STRIP_CHUNK)
        done = done + n_mid * STRIP_CHUNK
        copies(lo + done, n - done, 1)

    @pl.when(g == 0)
    def _():
        zero_scr[...] = jnp.zeros_like(zero_scr)
        for wait in (False, True):
            def expert(e, c, wait=wait):
                zero_fill(lo_ref[e], hi_ref[e], wait)
                return c
            lax.fori_loop(0, N_EXPERTS, expert, 0)
            zero_fill(hi_ref[N_EXPERTS - 1], lay.sorted_rows, wait)

    @pl.when(g >= 2)
    def _():
        _wait_rows(xs_hbm, loc.at[slot], n_pairs, sem.at[slot])

    x = _rows_from_tiles(xp_ref, tm).astype(BF16)
    pos = lax.broadcasted_iota(I32, (n_pairs, tm), 0)
    lpr = lpr_ref[0]
    hit = pos == lpr[0:1, :]
    for k in range(1, TOP_K):
        hit = hit | (pos == lpr[k:k + 1, :])
    _rows_to_tiles(loc.at[slot], _dot(hit.astype(BF16), x))

    def strip(e, c):
        s = g * N_EXPERTS + e
        _strip_copies(loc.at[slot], soff_ref[s], xs_hbm, sgs_ref[s], scnt_ref[s], sem.at[slot])
        return c
    lax.fori_loop(0, N_EXPERTS, strip, 0)

    @pl.when(g == lay.n_tiles - 1)
    def _():
        _wait_rows(xs_hbm, loc.at[slot], n_pairs, sem.at[slot])
        if lay.n_tiles > 1:
            _wait_rows(xs_hbm, loc.at[1 - slot], n_pairs, sem.at[1 - slot])


def _dispatch(xp, lp_rows, soff, scnt, sgs, pad_lo, pad_hi, lay):
    tm = lay.tile
    grid_spec = pltpu.PrefetchScalarGridSpec(
        num_scalar_prefetch=5,
        grid=(lay.n_tiles,),
        in_specs=[pl.BlockSpec((tm * ROW_SUBLANES, LANES), lambda g, *_: (g, 0)),
                  pl.BlockSpec((1, ROW_SUBLANES, tm), lambda g, *_: (g, 0, 0))],
        out_specs=pl.BlockSpec(memory_space=pl.ANY),
        scratch_shapes=[pltpu.VMEM((2, TOP_K * tm * ROW_SUBLANES, LANES), F32),
                        pltpu.VMEM((ZERO_ROWS * ROW_SUBLANES, LANES), F32),
                        pltpu.SemaphoreType.DMA((3,))],
    )
    return pl.pallas_call(
        functools.partial(_dispatch_kernel, lay=lay),
        grid_spec=grid_spec,
        out_shape=jax.ShapeDtypeStruct((lay.sorted_rows * ROW_SUBLANES, LANES), F32),
        compiler_params=_params(("arbitrary",)),
        name="dispatch",
    )(soff, scnt, sgs, pad_lo, pad_hi, xp, lp_rows)


def _experts_kernel(be_ref, nv_ref, xs_ref, wg_ref, wu_ref, wd_ref, bg_ref, bu_ref, bd_ref, ys_ref, wbf):
    bm = EXPERT_BLOCK
    i = pl.program_id(0)
    n_valid = nv_ref[0]

    @pl.when(i < n_valid)
    def _():
        @pl.when((i == 0) | (be_ref[i] != be_ref[jnp.maximum(i - 1, 0)]))
        def _():
            for w, w_ref in enumerate((wg_ref, wu_ref, wd_ref)):
                for c in range(0, D_MODEL, WEIGHT_CAST_ROWS):
                    wbf[w, c:c + WEIGHT_CAST_ROWS, :] = w_ref[0, c:c + WEIGHT_CAST_ROWS, :].astype(BF16)

        part = bm // EXPERT_PARTS
        for j in range(EXPERT_PARTS):
            tiles = pl.ds(j * part * ROW_SUBLANES, part * ROW_SUBLANES)
            x = _rows_from_tiles(xs_ref.at[tiles], part).astype(BF16)
            g = jnp.minimum(_dot(x, wbf[0]) + bg_ref[0], SWIGLU_LIMIT)
            u = jnp.clip(_dot(x, wbf[1]) + bu_ref[0], -SWIGLU_LIMIT, SWIGLU_LIMIT)
            mid = ((u + 1.0) * (g * _sigmoid(SWIGLU_ALPHA * g))).astype(BF16)
            _rows_to_tiles(ys_ref.at[tiles], _dot(mid, wbf[2]) + bd_ref[0])

    @pl.when(i >= n_valid)
    def _():
        ys_ref[...] = jnp.zeros_like(ys_ref)


def _experts(block_e, n_valid, xs, wts, lay):
    bm = EXPERT_BLOCK
    tile_rows = bm * ROW_SUBLANES
    wspec = pl.BlockSpec((1, D_MODEL, D_FF), lambda i, be, nv: (be[i], 0, 0))
    bspec = pl.BlockSpec((1, 1, D_FF), lambda i, be, nv: (be[i], 0, 0))
    grid_spec = pltpu.PrefetchScalarGridSpec(
        num_scalar_prefetch=2,
        grid=(lay.n_blocks,),
        in_specs=[pl.BlockSpec((tile_rows, LANES), lambda i, be, nv: (jnp.minimum(i, nv[0] - 1), 0)),
                  wspec, wspec, wspec, bspec, bspec, bspec],
        out_specs=pl.BlockSpec((tile_rows, LANES), lambda i, be, nv: (i, 0)),
        scratch_shapes=[pltpu.VMEM((3, D_MODEL, D_FF), BF16)],
    )
    return pl.pallas_call(
        functools.partial(_experts_kernel),
        grid_spec=grid_spec,
        out_shape=jax.ShapeDtypeStruct((lay.sorted_rows * ROW_SUBLANES, LANES), F32),
        compiler_params=_params(("arbitrary",)),
        name="experts",
    )(block_e, n_valid, xs, wts['w_gate'], wts['w_up'], wts['w_down'], wts['b_gate'], wts['b_up'], wts['b_down'])


def _combine_kernel(soff_ref, scnt_ref, sgs_ref, h_ref, tw_ref, lpc_ref, pp_ref, ps_ref, ys_hbm, g3_ref, wpg_ref,
                    wpp_ref, gf_ref, yp_ref, ysm_ref, loc, sem, *, lay, n_sample):
    tm = lay.tile
    n_pairs = TOP_K * tm
    g = pl.program_id(0)
    slot = g % 2

    def fetch(tile, s):
        def strip(e, c):
            i = tile * N_EXPERTS + e
            _strip_copies(ys_hbm, sgs_ref[i], loc.at[s], soff_ref[i], scnt_ref[i], sem.at[s])
            return c
        lax.fori_loop(0, N_EXPERTS, strip, 0)

    @pl.when(g == 0)
    def _():
        fetch(0, 0)

    @pl.when(g + 1 < lay.n_tiles)
    def _():
        fetch(g + 1, 1 - slot)

    _wait_rows(ys_hbm, loc.at[slot], n_pairs, sem.at[slot])
    y_rows = _rows_from_tiles(loc.at[slot], n_pairs).astype(BF16)
    tw = tw_ref[...]
    lpc = lpc_ref[...]
    pos = lax.broadcasted_iota(I32, (tm, n_pairs), 1)
    weight = jnp.zeros((tm, n_pairs), F32)
    for k in range(TOP_K):
        weight = jnp.where(pos == lpc[:, k:k + 1], tw[:, k:k + 1], weight)
    y = _dot(weight.astype(BF16), y_rows)
    h2 = h_ref[...] + y
    gate = _sigmoid(_dot(_rms(h2, g3_ref[...]).astype(BF16), wpg_ref[...]))
    p = pp_ref[...]
    p = jnp.where(g == lay.n_tiles - 1, jnp.concatenate([p[:tm - n_sample], ps_ref[...]], axis=0), p)
    h3 = h2 + gate * _dot(p.astype(BF16), wpp_ref[...])
    out = _rms(h3, gf_ref[...])
    yp_ref[...] = out

    @pl.when(g == lay.n_tiles - 1)
    def _():
        ysm_ref[...] = out[tm - n_sample:, :]


def _combine(hbuf, tw, lp_cols, p_prompt, p_sample, ys, soff, scnt, sgs, wts, lay):
    tm = lay.tile
    n_prompt, n_sample = p_prompt.shape[0], p_sample.shape[0]
    assert 0 < n_sample <= tm and (lay.n_tiles - 1) * tm < n_prompt
    consts = [wts['g3'], wts['w_ple_gate'], wts['w_ple_proj'], wts['g_final']]
    row = lambda w: pl.BlockSpec((tm, w), lambda g, *_: (g, 0))
    grid_spec = pltpu.PrefetchScalarGridSpec(
        num_scalar_prefetch=3,
        grid=(lay.n_tiles,),
        in_specs=[row(D_MODEL), row(LANES), row(LANES), row(D_PLE),
                  pl.BlockSpec((n_sample, D_PLE), lambda g, *_: (0, 0)), pl.BlockSpec(memory_space=pl.ANY)]
        + [pl.BlockSpec(c.shape, lambda g, *_: (0, 0), pipeline_mode=pl.Buffered(1)) for c in consts],
        out_specs=[row(D_MODEL), pl.BlockSpec((n_sample, D_MODEL), lambda g, *_: (0, 0))],
        scratch_shapes=[pltpu.VMEM((2, TOP_K * tm * ROW_SUBLANES, LANES), F32), pltpu.SemaphoreType.DMA((2,))],
    )
    return pl.pallas_call(
        functools.partial(_combine_kernel, lay=lay, n_sample=n_sample),
        grid_spec=grid_spec,
        out_shape=[jax.ShapeDtypeStruct((n_prompt, D_MODEL), F32), jax.ShapeDtypeStruct((n_sample, D_MODEL), F32)],
        compiler_params=_params(("arbitrary",)),
        name="combine",
    )(soff, scnt, sgs, hbuf, tw, lp_cols, p_prompt, p_sample, ys, *consts)


def _prepare_weights(norm1_g, w_in, w_gk2, b_gk, gla_norm_g, v_norm_g, v_norm_b, w_sp, b_sp, w_o, norm2_g,
                     w_router, b_router, w_gate, b_gate, w_up, b_up, w_down, b_down, norm3_g, w_ple_gate,
                     w_ple_proj, final_g):
    o_gk = 2 * GLA_DK_T + 2 * GLA_DV_T
    w_a = w_in[:, :o_gk].astype(BF16)
    w_b = w_in[:, o_gk + GK_RANK:].astype(BF16)
    w_gk1 = jnp.pad(w_in[:, o_gk:o_gk + GK_RANK], ((0, 0), (0, LANES - GK_RANK))).astype(BF16)
    w_gk2p = jnp.pad(w_gk2, ((0, LANES - GK_RANK), (0, 0))).astype(BF16)
    row = lambda z: z.reshape(1, -1).astype(F32)
    return dict(
        g1=row(norm1_g), w_a=w_a, w_b=w_b, w_gk1=w_gk1, w_gk2=w_gk2p, b_gk=row(b_gk), gla_g=row(gla_norm_g),
        vn_g=row(v_norm_g), vn_b=row(v_norm_b), w_sp=w_sp, b_sp_t=b_sp.T,
        mix_w0=row(jnp.repeat(w_sp[:, 0, 0], MIX_DH)), mix_b0=row(jnp.repeat(b_sp[:, 0], MIX_DH)),
        w_o=w_o.astype(BF16), g2=row(norm2_g),
        w_router=jnp.pad(w_router, ((0, 0), (0, LANES - N_EXPERTS))),
        b_router=jnp.pad(row(b_router), ((0, 0), (0, LANES - N_EXPERTS))),
        w_gate=w_gate, w_up=w_up, w_down=w_down,
        b_gate=b_gate.reshape(N_EXPERTS, 1, D_FF), b_up=b_up.reshape(N_EXPERTS, 1, D_FF),
        b_down=b_down.reshape(N_EXPERTS, 1, D_MODEL),
        g3=row(norm3_g), w_ple_gate=w_ple_gate.astype(BF16), w_ple_proj=w_ple_proj.astype(BF16),
        g_final=row(final_g))


def _routing_tables(counts, counts_before, tile_counts, local_off, lay):
    bm = EXPERT_BLOCK
    padded = (counts + bm - 1) // bm * bm
    pad_end = jnp.cumsum(padded)
    pad_start = pad_end - padded
    global_start = pad_start[None, :] + counts_before
    n_valid = (pad_end[-1] // bm).astype(I32)
    blk = jnp.minimum(jnp.arange(lay.n_blocks, dtype=I32), n_valid - 1)
    block_e = jnp.sum(pad_end[None, :] <= (blk * bm)[:, None], axis=1).astype(I32)
    flat = lambda z: z.reshape(-1).astype(I32)
    return dict(soff=flat(local_off), scnt=flat(tile_counts), sgs=flat(global_start),
                pad_lo=(pad_start + counts).astype(I32), pad_hi=pad_end.astype(I32), block_e=block_e,
                n_valid=n_valid.reshape(1))


def _layer(x_prompt, x_sample, state, p_prompt, p_sample, wts):
    n_p, len_p, _ = x_prompt.shape
    n_s, len_s, _ = x_sample.shape
    assert len_s == 1, "the sample group carries one new token per sequence"
    t_p = n_p * len_p
    t_all = t_p + n_s

    xs = x_sample.reshape(n_s, D_MODEL)
    q, k, a, v, og, ob, vn = _mixer_sample_in(xs, wts)
    st_sample, o = _state_update(state, q, k, a, v)
    h_sample = _mixer_sample_out(xs, o, og, ob, wts)
    hbuf, st_prompt = _mixer_prompt(x_prompt, h_sample, wts)

    lay = _Layout(t_all)
    xp, top_w, lp_cols, lp_rows, counts_before, tile_counts, local_off, counts = _router(hbuf, wts)
    ne = N_EXPERTS
    tab = _routing_tables(counts[0, :ne], counts_before[:, 0, :ne], tile_counts[:, 0, :ne], local_off[:, 0, :ne], lay)
    x_sorted = _dispatch(xp, lp_rows, tab['soff'], tab['scnt'], tab['sgs'], tab['pad_lo'], tab['pad_hi'], lay)
    y_sorted = _experts(tab['block_e'], tab['n_valid'], x_sorted, wts, lay)
    y_prompt, y_sample = _combine(hbuf, top_w, lp_cols, p_prompt.reshape(t_p, D_PLE), p_sample.reshape(n_s, D_PLE),
                                  y_sorted, tab['soff'], tab['scnt'], tab['sgs'], wts, lay)
    return (y_prompt.reshape(n_p, len_p, D_MODEL), y_sample.reshape(n_s, len_s, D_MODEL),
            st_prompt, st_sample, vn.reshape(n_s, len_s, D_GMLP))


def kernel(x_prompt, x_sample, state_gla, p_prompt, p_sample, norm1_g, w_in, w_gk2, b_gk, gla_norm_g, v_norm_g,
           v_norm_b, w_sp, b_sp, w_o, norm2_g, w_router, b_router, w_gate, b_gate, w_up, b_up, w_down, b_down,
           norm3_g, w_ple_gate, w_ple_proj, final_g):
    assert w_in.shape[0] == 1, "single-layer trunk"
    wts = _prepare_weights(norm1_g[0], w_in[0], w_gk2[0], b_gk[0], gla_norm_g[0], v_norm_g[0], v_norm_b[0],
                           w_sp[0], b_sp[0], w_o[0], norm2_g[0], w_router[0], b_router[0], w_gate[0], b_gate[0],
                           w_up[0], b_up[0], w_down[0], b_down[0], norm3_g[0], w_ple_gate[0], w_ple_proj[0],
                           final_g)
    y_p, y_s, st_p, st_s, vn_s = _layer(x_prompt, x_sample, state_gla[0], p_prompt[0], p_sample[0], wts)
    return (y_p, y_s, st_p[None], st_s[None].astype(state_gla.dtype), vn_s[None])
```

```python
import functools

import jax
import jax.numpy as jnp
from jax import lax
from jax.experimental import pallas as pl
from jax.experimental.pallas import tpu as pltpu

F32 = jnp.float32
BF16 = jnp.bfloat16
I32 = jnp.int32
U32 = jnp.uint32

D_MODEL = 1024
GLA_HEADS = 4
GLA_DK = 128
GLA_DV = 256
GLA_DK_T = GLA_HEADS * GLA_DK
GLA_DV_T = GLA_HEADS * GLA_DV
GK_RANK = 16
GATE_NORMALIZER = 16.0
GLA_CHUNK = 64
GLA_CHUNK_LOG2 = 6
D_GMLP = 1024
MIX_HEADS = 4
MIX_DH = D_GMLP // MIX_HEADS
MIX_CHUNK = 128
MIX_CHUNK_LOG2 = 7
assert 1 << GLA_CHUNK_LOG2 == GLA_CHUNK and 1 << MIX_CHUNK_LOG2 == MIX_CHUNK
N_EXPERTS = 32
TOP_K = 4
D_FF = 1024
SWIGLU_LIMIT = 7.0
SWIGLU_ALPHA = 1.702
D_PLE = 256
EPS = 1e-6

LANES = 128
ROW_SUBLANES = D_MODEL // LANES
VMEM_LIMIT_BYTES = 56 * 1024 * 1024

OFF_Q, OFF_K, OFF_V, OFF_R, OFF_U, OFF_VG, OFF_GA, OFF_GB, N_MAIN = (
    0, 512, 1024, 2048, 3072, 4096, 5120, 6144, 7168)

MIXER_TILE = 512
MIXER_PARTS = 2
STATE_TOKENS = 16
EXPERT_BLOCK = 512
EXPERT_PARTS = 2
WEIGHT_CAST_ROWS = 256
STRIP_CHUNK = 16
STRIP_CHUNK_LOG2 = 4
ZERO_ROWS = 64
ZERO_ROWS_LOG2 = 6
assert 1 << STRIP_CHUNK_LOG2 == STRIP_CHUNK and 1 << ZERO_ROWS_LOG2 == ZERO_ROWS


def _dot(a, b):
    return jnp.dot(a, b, preferred_element_type=F32)


def _dot_nt(a, b):
    return lax.dot_general(a, b, (((1,), (1,)), ((), ())), preferred_element_type=F32)


def _dot_tn(a, b):
    return lax.dot_general(a, b, (((0,), (0,)), ((), ())), preferred_element_type=F32)


def _rms(x, g):
    return x * lax.rsqrt(jnp.mean(x * x, axis=-1, keepdims=True) + EPS) * g


def _sigmoid(x):
    return 1.0 / (1.0 + jnp.exp(-x))


def _gelu(x):
    return 0.5 * x * (1.0 + lax.erf(x * (2.0 ** -0.5)))


def _log_sigmoid(x):
    return jnp.minimum(x, 0.0) - jnp.log1p(jnp.exp(-jnp.abs(x)))


def _split3(x):
    hi = x.astype(BF16)
    r1 = x - hi.astype(F32)
    mid = r1.astype(BF16)
    lo = (r1 - mid.astype(F32)).astype(BF16)
    return hi, mid, lo


def _const_spec(shape):
    nd = len(shape)
    return pl.BlockSpec(shape, lambda *_: (0,) * nd, pipeline_mode=pl.Buffered(1))


def _params(sem):
    return pltpu.CompilerParams(dimension_semantics=sem, vmem_limit_bytes=VMEM_LIMIT_BYTES)


def _main_cols(wa_ref, wb_ref, lo, hi):
    if hi <= OFF_U:
        return wa_ref[:, lo:hi]
    return wb_ref[:, lo - OFF_U:hi - OFF_U]


def _project_gla_inputs(n, wm, wgk1_ref, wgk2_ref, bgk_ref):
    q = _dot(n, wm(OFF_Q, OFF_K)) * (GLA_DK ** -0.5)
    k = _dot(n, wm(OFF_K, OFF_V))
    v = _dot(n, wm(OFF_V, OFF_R))
    gk = _dot(n, wgk1_ref[...]).astype(BF16)
    log_a = _log_sigmoid(_dot(gk, wgk2_ref[...]) + bgk_ref[...]) * (1.0 / GATE_NORMALIZER)
    return q, k, v, log_a


def _gmlp_inputs(n, wm, vng_ref, vnb_ref):
    u = _gelu(_dot(n, wm(OFF_U, OFF_VG)))
    vg = _gelu(_dot(n, wm(OFF_VG, OFF_GA)))
    mu = jnp.mean(vg, axis=-1, keepdims=True)
    vc = vg - mu
    var = jnp.mean(vc * vc, axis=-1, keepdims=True)
    vn = vc * lax.rsqrt(var + EPS) * vng_ref[...] + vnb_ref[...]
    ug = u * _sigmoid(_dot(n, wm(OFF_GB, N_MAIN)))
    return ug, vn


def _gla_out_gate(n, wm):
    r = _dot(n, wm(OFF_R, OFF_U))
    ga = _dot(n, wm(OFF_GA, OFF_GB))
    return r * _sigmoid(r) * _sigmoid(ga)


def _head_rms(o, g):
    return o * lax.rsqrt(jnp.mean(o * o, axis=-1, keepdims=True) + EPS) * g


def _mixer_prompt_kernel(x_ref, hs_ref, *refs, tm, n_seq, n_tiles):
    b = pl.program_id(0)
    t = pl.program_id(1)
    h_ref = refs[12]

    @pl.when(b < n_seq)
    def _():
        _mixer_prompt_tile(x_ref, *refs, tm=tm, n_tiles=n_tiles)

    @pl.when((b == n_seq) & (t == 0))
    def _():
        h_ref[0:hs_ref.shape[0], :] = hs_ref[...]


def _mixer_prompt_tile(x_ref, g1_ref, wa_ref, wb_ref, wgk1_ref, wgk2_ref, bgk_ref, glag_ref, vng_ref, vnb_ref,
                       wsp_ref, bspt_ref, wo_ref,
                       h_ref, st_ref,
                       n_scr, oa_scr, s_scr, *, tm, n_tiles):
    t = pl.program_id(1)

    @pl.when(t == 0)
    def _():
        s_scr[...] = jnp.zeros_like(s_scr)

    wm = functools.partial(_main_cols, wa_ref, wb_ref)
    pm = tm // MIXER_PARTS
    for part in range(MIXER_PARTS):
        rows = pl.ds(part * pm, pm)
        _mixer_prompt_part(x_ref.at[0, rows], g1_ref, wm, wgk1_ref, wgk2_ref, bgk_ref, glag_ref, vng_ref, vnb_ref,
                           wsp_ref, bspt_ref, wo_ref, h_ref.at[rows], n_scr.at[rows], oa_scr.at[rows], s_scr, tm=pm)

    @pl.when(t == n_tiles - 1)
    def _():
        for h in range(GLA_HEADS):
            st_ref[0, h] = s_scr[h].T


def _mixer_prompt_part(x_ref, g1_ref, wm, wgk1_ref, wgk2_ref, bgk_ref, glag_ref, vng_ref, vnb_ref,
                       wsp_ref, bspt_ref, wo_ref, h_ref, n_scr, oa_scr, s_scr, *, tm):
    x = x_ref[...]
    n = _rms(x, g1_ref[...]).astype(BF16)
    n_scr[...] = n
    q, k, v, log_a = _project_gla_inputs(n, wm, wgk1_ref, wgk2_ref, bgk_ref)
    v = v.astype(BF16)

    row = lax.broadcasted_iota(I32, (tm, tm), 0)
    col = lax.broadcasted_iota(I32, (tm, tm), 1)
    same_chunk = lax.shift_right_logical(row, GLA_CHUNK_LOG2) == lax.shift_right_logical(col, GLA_CHUNK_LOG2)
    causal = same_chunk & (row >= col)
    tri = causal.astype(BF16)
    blk = same_chunk.astype(BF16)
    hi, mid, lo = _split3(log_a)
    b = _dot(tri, hi) + _dot(tri, mid) + _dot(tri, lo)
    b_end = _dot(blk, hi) + _dot(blk, mid) + _dot(blk, lo)
    qe = (q * jnp.exp(b)).astype(BF16)
    ke = (k * jnp.exp(-b)).astype(BF16)
    kd = (k * jnp.exp(b_end - b)).astype(BF16)
    dec = jnp.exp(b_end)
    glag = glag_ref[...]
    for h in range(GLA_HEADS):
        ks = slice(h * GLA_DK, (h + 1) * GLA_DK)
        vs = slice(h * GLA_DV, (h + 1) * GLA_DV)
        vh = v[:, vs]
        att = jnp.where(causal, _dot_nt(qe[:, ks], ke[:, ks]), 0.0).astype(BF16)
        o_intra = _dot(att, vh)
        s_t = s_scr[h]
        o_inter = []
        for c in range(tm // GLA_CHUNK):
            rs = slice(c * GLA_CHUNK, (c + 1) * GLA_CHUNK)
            o_inter.append(_dot_nt(qe[rs, ks], s_t.astype(BF16)))
            s_t = s_t * dec[c * GLA_CHUNK:c * GLA_CHUNK + 1, ks] + _dot_tn(vh[rs], kd[rs, ks])
        s_scr[h] = s_t
        oa_scr[:, vs] = _head_rms(o_intra + jnp.concatenate(o_inter, axis=0), glag)

    n = n_scr[...]
    oa_scr[...] = oa_scr[...] * _gla_out_gate(n, wm)
    ug, vn = _gmlp_inputs(n, wm, vng_ref, vnb_ref)
    vn = vn.astype(BF16)
    reps = tm // MIX_CHUNK
    mix_mask = (lax.shift_right_logical(row, MIX_CHUNK_LOG2) == lax.shift_right_logical(col, MIX_CHUNK_LOG2)) & (
        row >= col)
    for h in range(MIX_HEADS):
        w = jnp.where(mix_mask, jnp.tile(wsp_ref[h], (reps, reps)), 0.0).astype(BF16)
        cs = slice(h * MIX_DH, (h + 1) * MIX_DH)
        mix = _dot(w, vn[:, cs]) + jnp.tile(bspt_ref[:, h:h + 1], (reps, 1))
        oa_scr[:, cs] = oa_scr[:, cs] + ug[:, cs] * mix

    h_ref[...] = x + _dot(oa_scr[...].astype(BF16), wo_ref[...])


def _mixer_prompt(x, h_sample, wts):
    n_seq, seq_len, _ = x.shape
    n_s = h_sample.shape[0]
    tm = min(MIXER_TILE, seq_len)
    assert seq_len % tm == 0 and tm % MIX_CHUNK == 0 and n_s <= tm
    n_tiles = seq_len // tm
    consts = [wts['g1'], wts['w_a'], wts['w_b'], wts['w_gk1'], wts['w_gk2'], wts['b_gk'], wts['gla_g'], wts['vn_g'],
              wts['vn_b'], wts['w_sp'], wts['b_sp_t'], wts['w_o']]
    kern = functools.partial(_mixer_prompt_kernel, tm=tm, n_seq=n_seq, n_tiles=n_tiles)
    last = n_seq - 1

    def prompt_tile(b, t):
        return jnp.minimum(b, last), jnp.where(b < n_seq, t, n_tiles - 1)

    def x_map(b, t):
        bb, tt = prompt_tile(b, t)
        return bb, tt, 0

    def h_map(b, t):
        return jnp.where(b < n_seq, b * n_tiles + t, n_seq * n_tiles), 0

    return pl.pallas_call(
        kern,
        grid=(n_seq + 1, n_tiles),
        in_specs=[pl.BlockSpec((1, tm, D_MODEL), x_map), _const_spec(h_sample.shape)]
        + [_const_spec(c.shape) for c in consts],
        out_specs=[pl.BlockSpec((tm, D_MODEL), h_map),
                   pl.BlockSpec((1, GLA_HEADS, GLA_DK, GLA_DV), lambda b, t: (jnp.minimum(b, last), 0, 0, 0))],
        out_shape=[jax.ShapeDtypeStruct((n_seq * seq_len + n_s, D_MODEL), F32),
                   jax.ShapeDtypeStruct((n_seq, GLA_HEADS, GLA_DK, GLA_DV), F32)],
        scratch_shapes=[pltpu.VMEM((tm, D_MODEL), BF16),
                        pltpu.VMEM((tm, D_MODEL), F32),
                        pltpu.VMEM((GLA_HEADS, GLA_DV, GLA_DK), F32)],
        compiler_params=_params(("arbitrary", "arbitrary")),
        name="mixer_prompt",
    )(x, h_sample, *consts)


def _mixer_sample_in_kernel(x_ref, g1_ref, wa_ref, wb_ref, wgk1_ref, wgk2_ref, bgk_ref, vng_ref, vnb_ref,
                            mixw_ref, mixb_ref,
                            q_ref, k_ref, a_ref, v_ref, og_ref, ob_ref, vn_ref):
    wm = functools.partial(_main_cols, wa_ref, wb_ref)
    n = _rms(x_ref[...], g1_ref[...]).astype(BF16)
    q, k, v, log_a = _project_gla_inputs(n, wm, wgk1_ref, wgk2_ref, bgk_ref)
    q_ref[...] = q
    k_ref[...] = k
    a_ref[...] = jnp.exp(log_a)
    v_ref[...] = v
    og_ref[...] = _gla_out_gate(n, wm)
    ug, vn = _gmlp_inputs(n, wm, vng_ref, vnb_ref)
    vn_ref[...] = vn
    ob_ref[...] = ug * (mixw_ref[...] * vn + mixb_ref[...])


def _mixer_sample_in(x2, wts):
    n_seq = x2.shape[0]
    consts = [wts['g1'], wts['w_a'], wts['w_b'], wts['w_gk1'], wts['w_gk2'], wts['b_gk'], wts['vn_g'], wts['vn_b'],
              wts['mix_w0'], wts['mix_b0']]
    widths = [GLA_DK_T, GLA_DK_T, GLA_DK_T, GLA_DV_T, D_MODEL, D_MODEL, D_GMLP]
    return pl.pallas_call(
        _mixer_sample_in_kernel,
        grid=(1,),
        in_specs=[_const_spec(x2.shape)] + [_const_spec(c.shape) for c in consts],
        out_specs=[pl.BlockSpec((n_seq, w), lambda i: (0, 0)) for w in widths],
        out_shape=[jax.ShapeDtypeStruct((n_seq, w), F32) for w in widths],
        compiler_params=_params(("arbitrary",)),
        name="mixer_sample_in",
    )(x2, *consts)


def _state_update_kernel(s_ref, q_ref, k_ref, a_ref, v_ref, so_ref, o_ref):
    for j in range(STATE_TOKENS):
        for h in range(GLA_HEADS):
            ks = slice(h * GLA_DK, (h + 1) * GLA_DK)
            vs = slice(h * GLA_DV, (h + 1) * GLA_DV)
            s_new = a_ref[0, ks, j:j + 1] * s_ref[j, h] + k_ref[0, ks, j:j + 1] * v_ref[j:j + 1, vs]
            so_ref[j, h] = s_new
            o_ref[j:j + 1, vs] = jnp.sum(q_ref[0, ks, j:j + 1] * s_new, axis=0, keepdims=True)


def _state_update(state, q, k, a, v):
    n_seq = state.shape[0]
    tb = STATE_TOKENS
    assert n_seq % tb == 0

    def cols(z):
        return z.reshape(n_seq // tb, tb, GLA_DK_T).transpose(0, 2, 1)

    col_spec = pl.BlockSpec((1, GLA_DK_T, tb), lambda i: (i, 0, 0))
    st_spec = pl.BlockSpec((tb, GLA_HEADS, GLA_DK, GLA_DV), lambda i: (i, 0, 0, 0))
    row_spec = pl.BlockSpec((tb, GLA_DV_T), lambda i: (i, 0))
    return pl.pallas_call(
        _state_update_kernel,
        grid=(n_seq // tb,),
        in_specs=[st_spec, col_spec, col_spec, col_spec, row_spec],
        out_specs=[st_spec, row_spec],
        out_shape=[jax.ShapeDtypeStruct(state.shape, F32), jax.ShapeDtypeStruct((n_seq, GLA_DV_T), F32)],
        compiler_params=_params(("arbitrary",)),
        name="state_update",
    )(state, cols(q), cols(k), cols(a), v)


def _mixer_sample_out_kernel(x_ref, o_ref, og_ref, ob_ref, glag_ref, wo_ref, h_ref):
    glag = glag_ref[...]
    parts = []
    for h in range(GLA_HEADS):
        vs = slice(h * GLA_DV, (h + 1) * GLA_DV)
        parts.append(_head_rms(o_ref[:, vs], glag))
    merged = jnp.concatenate(parts, axis=1) * og_ref[...] + ob_ref[...]
    h_ref[...] = x_ref[...] + _dot(merged.astype(BF16), wo_ref[...])


def _mixer_sample_out(x2, o, og, ob, wts):
    n_seq = x2.shape[0]
    consts = [x2, o, og, ob, wts['gla_g'], wts['w_o']]
    return pl.pallas_call(
        _mixer_sample_out_kernel,
        grid=(1,),
        in_specs=[_const_spec(c.shape) for c in consts],
        out_specs=pl.BlockSpec((n_seq, D_MODEL), lambda i: (0, 0)),
        out_shape=jax.ShapeDtypeStruct((n_seq, D_MODEL), F32),
        compiler_params=_params(("arbitrary",)),
        name="mixer_sample_out",
    )(*consts)


def _router_kernel(h_ref, g2_ref, wr_ref, br_ref, xp_ref, tw_ref, lpc_ref, lpr_ref, cb_ref, tc_ref, lo_ref, cnt_ref, cnt_scr,
                   *, tm):
    i = pl.program_id(0)

    @pl.when(i == 0)
    def _():
        cnt_scr[...] = jnp.zeros_like(cnt_scr)

    cb_ref[0] = cnt_scr[...].astype(I32)

    hn = _rms(h_ref[...], g2_ref[...])
    hn_hi = hn.astype(BF16)
    hn_lo = (hn - hn_hi.astype(F32)).astype(BF16)
    w = wr_ref[...]
    w_hi = w.astype(BF16)
    w_lo = (w - w_hi.astype(F32)).astype(BF16)
    logits = _dot(hn_hi, w_hi) + _dot(hn_lo, w_hi) + _dot(hn_hi, w_lo) + br_ref[...]

    _rows_to_tiles(xp_ref, hn)

    lane = lax.broadcasted_iota(I32, (tm, LANES), 1)
    neg = jnp.float32(-jnp.inf)
    l = jnp.where(lane < N_EXPERTS, logits, neg)
    vals, hots = [], []
    for _ in range(TOP_K):
        m = jnp.max(l, axis=-1, keepdims=True)
        idx = jnp.min(jnp.where(l == m, lane, LANES), axis=-1, keepdims=True)
        hot = lane == idx
        l = jnp.where(hot, neg, l)
        vals.append(m)
        hots.append(hot)
    exps = [jnp.exp(v - vals[0]) for v in vals]
    denom = exps[0] + exps[1] + exps[2] + exps[3]

    member = (hots[0] | hots[1] | hots[2] | hots[3]).astype(BF16)
    row = lax.broadcasted_iota(I32, (tm, tm), 0)
    col = lax.broadcasted_iota(I32, (tm, tm), 1)
    in_tile = _dot((row > col).astype(BF16), member)
    tile_cnt = jnp.sum(member.astype(F32), axis=0, keepdims=True)
    e_row = lax.broadcasted_iota(I32, (LANES, LANES), 0)
    e_col = lax.broadcasted_iota(I32, (LANES, LANES), 1)
    lower = (e_row < e_col).astype(BF16)
    c_hi, c_mid, c_lo = _split3(jnp.broadcast_to(tile_cnt, (ROW_SUBLANES, LANES)))
    local_off = (_dot(c_hi, lower) + _dot(c_mid, lower) + _dot(c_lo, lower))[0:1, :]
    tc_ref[0] = tile_cnt.astype(I32)
    lo_ref[0] = local_off.astype(I32)
    cnt_scr[...] = cnt_scr[...] + tile_cnt
    cnt_ref[...] = cnt_scr[...].astype(I32)

    place = local_off + in_tile
    tw = jnp.zeros((tm, LANES), F32)
    lp = jnp.zeros((tm, LANES), F32)
    for kk in range(TOP_K):
        sel = lane == kk
        tw = jnp.where(sel, exps[kk] / denom, tw)
        lp = jnp.where(sel, jnp.sum(jnp.where(hots[kk], place, 0.0), axis=-1, keepdims=True), lp)
    tw_ref[...] = tw
    lpc_ref[...] = lp.astype(I32)
    lpr_ref[0] = lp.T[0:ROW_SUBLANES, :].astype(I32)


def _router_tile(t_all):
    for tm in (512, 384, 256, 128):
        if t_all % tm == 0:
            return tm
    raise ValueError(f"token count {t_all} must be a multiple of 128")


def _router(hbuf, wts):
    t_all = hbuf.shape[0]
    tm = _router_tile(t_all)
    consts = [wts['g2'], wts['w_router'], wts['b_router']]
    row = lambda w: pl.BlockSpec((tm, w), lambda i: (i, 0))
    n_tiles = t_all // tm
    per_tile = pl.BlockSpec((1, 1, LANES), lambda i: (i, 0, 0))
    tile_vec = jax.ShapeDtypeStruct((n_tiles, 1, LANES), I32)
    return pl.pallas_call(
        functools.partial(_router_kernel, tm=tm),
        grid=(t_all // tm,),
        in_specs=[row(D_MODEL)] + [_const_spec(c.shape) for c in consts],
        out_specs=[pl.BlockSpec((tm * ROW_SUBLANES, LANES), lambda i: (i, 0)), row(LANES), row(LANES),
                   pl.BlockSpec((1, ROW_SUBLANES, tm), lambda i: (i, 0, 0)),
                   per_tile, per_tile, per_tile, pl.BlockSpec((1, LANES), lambda i: (0, 0))],
        out_shape=[jax.ShapeDtypeStruct((t_all * ROW_SUBLANES, LANES), F32),
                   jax.ShapeDtypeStruct((t_all, LANES), F32),
                   jax.ShapeDtypeStruct((t_all, LANES), I32),
                   jax.ShapeDtypeStruct((n_tiles, ROW_SUBLANES, tm), I32),
                   tile_vec, tile_vec, tile_vec, jax.ShapeDtypeStruct((1, LANES), I32)],
        scratch_shapes=[pltpu.VMEM((1, LANES), F32)],
        compiler_params=_params(("arbitrary",)),
        name="router",
    )(hbuf, *consts)


def _row_tile(r):
    start = r * ROW_SUBLANES
    if not isinstance(r, int):
        start = pl.multiple_of(start, ROW_SUBLANES)
    return pl.ds(start, ROW_SUBLANES)


def _tile_of(ref, row):
    return ref.at[_row_tile(row)]


def _rows_from_tiles(ref, n_rows):
    return jnp.concatenate([ref[pl.ds(s, n_rows, stride=ROW_SUBLANES), :] for s in range(ROW_SUBLANES)], axis=1)


def _rows_to_tiles(ref, x):
    for s in range(ROW_SUBLANES):
        ref[pl.ds(s, x.shape[0], stride=ROW_SUBLANES), :] = x[:, s * LANES:(s + 1) * LANES]


def _tiles(row, n_rows):
    return pl.ds(pl.multiple_of(row * ROW_SUBLANES, ROW_SUBLANES), n_rows * ROW_SUBLANES)


class _Layout:
    def __init__(self, t_all):
        bm = EXPERT_BLOCK
        self.t_all = t_all
        self.tile = _router_tile(t_all)
        self.n_tiles = t_all // self.tile
        self.n_blocks = -(-(t_all * TOP_K + N_EXPERTS * (bm - 1)) // bm)
        self.sorted_rows = self.n_blocks * bm


def _strip_copies(src_ref, src_row, dst_ref, dst_row, n, sem):
    def copy(offset, size):
        pltpu.make_async_copy(src_ref.at[_tiles(src_row + offset, size)], dst_ref.at[_tiles(dst_row + offset, size)],
                              sem).start()

    n_chunks = lax.shift_right_logical(n, STRIP_CHUNK_LOG2)

    def chunk(j, c):
        copy(j * STRIP_CHUNK, STRIP_CHUNK)
        return c
    lax.fori_loop(0, n_chunks, chunk, 0)

    done = n_chunks * STRIP_CHUNK
    size = STRIP_CHUNK // 2
    while size >= 1:
        part = n & size

        @pl.when(part != 0)
        def _(done=done, size=size):
            copy(done, size)
        done = done + part
        size //= 2


def _wait_rows(hbm_ref, vmem_ref, n_rows, sem):
    pltpu.make_async_copy(hbm_ref.at[pl.ds(0, n_rows * ROW_SUBLANES)], vmem_ref.at[pl.ds(0, n_rows * ROW_SUBLANES)],
                          sem).wait()


def _dispatch_kernel(soff_ref, scnt_ref, sgs_ref, lo_ref, hi_ref, xp_ref, lpr_ref, xs_hbm, loc, zero_scr, sem,
                     *, lay):
    tm = lay.tile
    n_pairs = TOP_K * tm
    g = pl.program_id(0)
    slot = g % 2

    def zero_fill(lo, hi, wait):
        def copies(first, count, size):
            def body(j, c):
                cp = pltpu.make_async_copy(zero_scr.at[pl.ds(0, size * ROW_SUBLANES)],
                                           xs_hbm.at[_tiles(first + j * size, size)], sem.at[2])
                if wait:
                    cp.wait()
                else:
                    cp.start()
                return c
            lax.fori_loop(0, count, body, 0)
        n = hi - lo
        n_big = lax.shift_right_logical(n, ZERO_ROWS_LOG2)
        copies(lo, n_big, ZERO_ROWS)
        done = n_big * ZERO_ROWS
        n_mid = lax.shift_right_logical(n - done, STRIP_CHUNK_LOG2)
        copies(lo + done, n_mid, STRIP_CHUNK)
        done = done + n_mid * STRIP_CHUNK
        copies(lo + done, n - done, 1)

    @pl.when(g == 0)
    def _():
        zero_scr[...] = jnp.zeros_like(zero_scr)
        for wait in (False, True):
            def expert(e, c, wait=wait):
                zero_fill(lo_ref[e], hi_ref[e], wait)
                return c
            lax.fori_loop(0, N_EXPERTS, expert, 0)
            zero_fill(hi_ref[N_EXPERTS - 1], lay.sorted_rows, wait)

    @pl.when(g >= 2)
    def _():
        _wait_rows(xs_hbm, loc.at[slot], n_pairs, sem.at[slot])

    x = _rows_from_tiles(xp_ref, tm).astype(BF16)
    pos = lax.broadcasted_iota(I32, (n_pairs, tm), 0)
    lpr = lpr_ref[0]
    hit = pos == lpr[0:1, :]
    for k in range(1, TOP_K):
        hit = hit | (pos == lpr[k:k + 1, :])
    _rows_to_tiles(loc.at[slot], _dot(hit.astype(BF16), x))

    def strip(e, c):
        s = g * N_EXPERTS + e
        _strip_copies(loc.at[slot], soff_ref[s], xs_hbm, sgs_ref[s], scnt_ref[s], sem.at[slot])
        return c
    lax.fori_loop(0, N_EXPERTS, strip, 0)

    @pl.when(g == lay.n_tiles - 1)
    def _():
        _wait_rows(xs_hbm, loc.at[slot], n_pairs, sem.at[slot])
        if lay.n_tiles > 1:
            _wait_rows(xs_hbm, loc.at[1 - slot], n_pairs, sem.at[1 - slot])


def _dispatch(xp, lp_rows, soff, scnt, sgs, pad_lo, pad_hi, lay):
    tm = lay.tile
    grid_spec = pltpu.PrefetchScalarGridSpec(
        num_scalar_prefetch=5,
        grid=(lay.n_tiles,),
        in_specs=[pl.BlockSpec((tm * ROW_SUBLANES, LANES), lambda g, *_: (g, 0)),
                  pl.BlockSpec((1, ROW_SUBLANES, tm), lambda g, *_: (g, 0, 0))],
        out_specs=pl.BlockSpec(memory_space=pl.ANY),
        scratch_shapes=[pltpu.VMEM((2, TOP_K * tm * ROW_SUBLANES, LANES), F32),
                        pltpu.VMEM((ZERO_ROWS * ROW_SUBLANES, LANES), F32),
                        pltpu.SemaphoreType.DMA((3,))],
    )
    return pl.pallas_call(
        functools.partial(_dispatch_kernel, lay=lay),
        grid_spec=grid_spec,
        out_shape=jax.ShapeDtypeStruct((lay.sorted_rows * ROW_SUBLANES, LANES), F32),
        compiler_params=_params(("arbitrary",)),
        name="dispatch",
    )(soff, scnt, sgs, pad_lo, pad_hi, xp, lp_rows)


def _experts_kernel(be_ref, nv_ref, br_ref, xs_ref, wg_ref, wu_ref, wd_ref, bg_ref, bu_ref, bd_ref, ys_ref, wbf):
    bm = EXPERT_BLOCK
    i = pl.program_id(0)
    n_valid = nv_ref[0]

    @pl.when(i < n_valid)
    def _():
        @pl.when((i == 0) | (be_ref[i] != be_ref[jnp.maximum(i - 1, 0)]))
        def _():
            for w, w_ref in enumerate((wg_ref, wu_ref, wd_ref)):
                for c in range(0, D_MODEL, WEIGHT_CAST_ROWS):
                    wbf[w, c:c + WEIGHT_CAST_ROWS, :] = w_ref[0, c:c + WEIGHT_CAST_ROWS, :].astype(BF16)

        part = bm // EXPERT_PARTS

        def run(n_parts):
            for j in range(EXPERT_PARTS):
                tiles = pl.ds(j * part * ROW_SUBLANES, part * ROW_SUBLANES)
                if j >= n_parts:
                    ys_ref[tiles, :] = jnp.zeros((part * ROW_SUBLANES, LANES), F32)
                    continue
                x = _rows_from_tiles(xs_ref.at[tiles], part).astype(BF16)
                g = jnp.minimum(_dot(x, wbf[0]) + bg_ref[0], SWIGLU_LIMIT)
                u = jnp.clip(_dot(x, wbf[1]) + bu_ref[0], -SWIGLU_LIMIT, SWIGLU_LIMIT)
                mid = ((u + 1.0) * (g * _sigmoid(SWIGLU_ALPHA * g))).astype(BF16)
                _rows_to_tiles(ys_ref.at[tiles], _dot(mid, wbf[2]) + bd_ref[0])

        n_parts = (br_ref[i] + (part - 1)) // part
        for n in range(1, EXPERT_PARTS + 1):
            @pl.when(n_parts == n)
            def _(n=n):
                run(n)

    @pl.when(i >= n_valid)
    def _():
        ys_ref[...] = jnp.zeros_like(ys_ref)


def _experts(block_e, n_valid, block_rows, xs, wts, lay):
    bm = EXPERT_BLOCK
    tile_rows = bm * ROW_SUBLANES
    wspec = pl.BlockSpec((1, D_MODEL, D_FF), lambda i, be, nv, br: (be[i], 0, 0))
    bspec = pl.BlockSpec((1, 1, D_FF), lambda i, be, nv, br: (be[i], 0, 0))
    grid_spec = pltpu.PrefetchScalarGridSpec(
        num_scalar_prefetch=3,
        grid=(lay.n_blocks,),
        in_specs=[pl.BlockSpec((tile_rows, LANES), lambda i, be, nv, br: (jnp.minimum(i, nv[0] - 1), 0)),
                  wspec, wspec, wspec, bspec, bspec, bspec],
        out_specs=pl.BlockSpec((tile_rows, LANES), lambda i, be, nv, br: (i, 0)),
        scratch_shapes=[pltpu.VMEM((3, D_MODEL, D_FF), BF16)],
    )
    return pl.pallas_call(
        functools.partial(_experts_kernel),
        grid_spec=grid_spec,
        out_shape=jax.ShapeDtypeStruct((lay.sorted_rows * ROW_SUBLANES, LANES), F32),
        compiler_params=_params(("arbitrary",)),
        name="experts",
    )(block_e, n_valid, block_rows, xs, wts['w_gate'], wts['w_up'], wts['w_down'], wts['b_gate'], wts['b_up'],
      wts['b_down'])


def _combine_kernel(soff_ref, scnt_ref, sgs_ref, h_ref, tw_ref, lpc_ref, pp_ref, ps_ref, ys_hbm, g3_ref, wpg_ref,
                    wpp_ref, gf_ref, yp_ref, ysm_ref, loc, sem, *, lay, n_sample):
    tm = lay.tile
    n_pairs = TOP_K * tm
    g = pl.program_id(0)
    slot = g % 2

    def fetch(tile, s):
        def strip(e, c):
            i = tile * N_EXPERTS + e
            _strip_copies(ys_hbm, sgs_ref[i], loc.at[s], soff_ref[i], scnt_ref[i], sem.at[s])
            return c
        lax.fori_loop(0, N_EXPERTS, strip, 0)

    @pl.when(g == 0)
    def _():
        fetch(0, 0)

    @pl.when(g + 1 < lay.n_tiles)
    def _():
        fetch(g + 1, 1 - slot)

    _wait_rows(ys_hbm, loc.at[slot], n_pairs, sem.at[slot])
    y_rows = _rows_from_tiles(loc.at[slot], n_pairs).astype(BF16)
    tw = tw_ref[...]
    lpc = lpc_ref[...]
    pos = lax.broadcasted_iota(I32, (tm, n_pairs), 1)
    weight = jnp.zeros((tm, n_pairs), F32)
    for k in range(TOP_K):
        weight = jnp.where(pos == lpc[:, k:k + 1], tw[:, k:k + 1], weight)
    y = _dot(weight.astype(BF16), y_rows)
    h2 = h_ref[...] + y
    gate = _sigmoid(_dot(_rms(h2, g3_ref[...]).astype(BF16), wpg_ref[...]))
    p = pp_ref[...]
    p = jnp.where(g == lay.n_tiles - 1, jnp.concatenate([p[:tm - n_sample], ps_ref[...]], axis=0), p)
    h3 = h2 + gate * _dot(p.astype(BF16), wpp_ref[...])
    out = _rms(h3, gf_ref[...])
    yp_ref[...] = out

    @pl.when(g == lay.n_tiles - 1)
    def _():
        ysm_ref[...] = out[tm - n_sample:, :]


def _combine(hbuf, tw, lp_cols, p_prompt, p_sample, ys, soff, scnt, sgs, wts, lay):
    tm = lay.tile
    n_prompt, n_sample = p_prompt.shape[0], p_sample.shape[0]
    assert 0 < n_sample <= tm and (lay.n_tiles - 1) * tm < n_prompt
    consts = [wts['g3'], wts['w_ple_gate'], wts['w_ple_proj'], wts['g_final']]
    row = lambda w: pl.BlockSpec((tm, w), lambda g, *_: (g, 0))
    grid_spec = pltpu.PrefetchScalarGridSpec(
        num_scalar_prefetch=3,
        grid=(lay.n_tiles,),
        in_specs=[row(D_MODEL), row(LANES), row(LANES), row(D_PLE),
                  pl.BlockSpec((n_sample, D_PLE), lambda g, *_: (0, 0)), pl.BlockSpec(memory_space=pl.ANY)]
        + [pl.BlockSpec(c.shape, lambda g, *_: (0, 0), pipeline_mode=pl.Buffered(1)) for c in consts],
        out_specs=[row(D_MODEL), pl.BlockSpec((n_sample, D_MODEL), lambda g, *_: (0, 0))],
        scratch_shapes=[pltpu.VMEM((2, TOP_K * tm * ROW_SUBLANES, LANES), F32), pltpu.SemaphoreType.DMA((2,))],
    )
    return pl.pallas_call(
        functools.partial(_combine_kernel, lay=lay, n_sample=n_sample),
        grid_spec=grid_spec,
        out_shape=[jax.ShapeDtypeStruct((n_prompt, D_MODEL), F32), jax.ShapeDtypeStruct((n_sample, D_MODEL), F32)],
        compiler_params=_params(("arbitrary",)),
        name="combine",
    )(soff, scnt, sgs, hbuf, tw, lp_cols, p_prompt, p_sample, ys, *consts)


def _prepare_weights(norm1_g, w_in, w_gk2, b_gk, gla_norm_g, v_norm_g, v_norm_b, w_sp, b_sp, w_o, norm2_g,
                     w_router, b_router, w_gate, b_gate, w_up, b_up, w_down, b_down, norm3_g, w_ple_gate,
                     w_ple_proj, final_g):
    o_gk = 2 * GLA_DK_T + 2 * GLA_DV_T
    w_a = w_in[:, :o_gk].astype(BF16)
    w_b = w_in[:, o_gk + GK_RANK:].astype(BF16)
    w_gk1 = jnp.pad(w_in[:, o_gk:o_gk + GK_RANK], ((0, 0), (0, LANES - GK_RANK))).astype(BF16)
    w_gk2p = jnp.pad(w_gk2, ((0, LANES - GK_RANK), (0, 0))).astype(BF16)
    row = lambda z: z.reshape(1, -1).astype(F32)
    return dict(
        g1=row(norm1_g), w_a=w_a, w_b=w_b, w_gk1=w_gk1, w_gk2=w_gk2p, b_gk=row(b_gk), gla_g=row(gla_norm_g),
        vn_g=row(v_norm_g), vn_b=row(v_norm_b), w_sp=w_sp, b_sp_t=b_sp.T,
        mix_w0=row(jnp.repeat(w_sp[:, 0, 0], MIX_DH)), mix_b0=row(jnp.repeat(b_sp[:, 0], MIX_DH)),
        w_o=w_o.astype(BF16), g2=row(norm2_g),
        w_router=jnp.pad(w_router, ((0, 0), (0, LANES - N_EXPERTS))),
        b_router=jnp.pad(row(b_router), ((0, 0), (0, LANES - N_EXPERTS))),
        w_gate=w_gate, w_up=w_up, w_down=w_down,
        b_gate=b_gate.reshape(N_EXPERTS, 1, D_FF), b_up=b_up.reshape(N_EXPERTS, 1, D_FF),
        b_down=b_down.reshape(N_EXPERTS, 1, D_MODEL),
        g3=row(norm3_g), w_ple_gate=w_ple_gate.astype(BF16), w_ple_proj=w_ple_proj.astype(BF16),
        g_final=row(final_g))


def _routing_tables(counts, counts_before, tile_counts, local_off, lay):
    bm = EXPERT_BLOCK
    padded = (counts + bm - 1) // bm * bm
    pad_end = jnp.cumsum(padded)
    pad_start = pad_end - padded
    global_start = pad_start[None, :] + counts_before
    n_valid = (pad_end[-1] // bm).astype(I32)
    blk = jnp.minimum(jnp.arange(lay.n_blocks, dtype=I32), n_valid - 1)
    block_e = jnp.sum(pad_end[None, :] <= (blk * bm)[:, None], axis=1).astype(I32)
    pad_lo = pad_start + counts
    token_end = jnp.sum(jnp.where(block_e[:, None] == jnp.arange(N_EXPERTS, dtype=I32), pad_lo[None, :], 0), axis=1)
    block_rows = jnp.clip(token_end - blk * bm, 1, bm).astype(I32)
    flat = lambda z: z.reshape(-1).astype(I32)
    return dict(soff=flat(local_off), scnt=flat(tile_counts), sgs=flat(global_start),
                pad_lo=pad_lo.astype(I32), pad_hi=pad_end.astype(I32), block_e=block_e, block_rows=block_rows,
                n_valid=n_valid.reshape(1))


def _layer(x_prompt, x_sample, state, p_prompt, p_sample, wts):
    n_p, len_p, _ = x_prompt.shape
    n_s, len_s, _ = x_sample.shape
    assert len_s == 1, "the sample group carries one new token per sequence"
    t_p = n_p * len_p
    t_all = t_p + n_s

    xs = x_sample.reshape(n_s, D_MODEL)
    q, k, a, v, og, ob, vn = _mixer_sample_in(xs, wts)
    st_sample, o = _state_update(state, q, k, a, v)
    h_sample = _mixer_sample_out(xs, o, og, ob, wts)
    hbuf, st_prompt = _mixer_prompt(x_prompt, h_sample, wts)

    lay = _Layout(t_all)
    xp, top_w, lp_cols, lp_rows, counts_before, tile_counts, local_off, counts = _router(hbuf, wts)
    ne = N_EXPERTS
    tab = _routing_tables(counts[0, :ne], counts_before[:, 0, :ne], tile_counts[:, 0, :ne], local_off[:, 0, :ne], lay)
    x_sorted = _dispatch(xp, lp_rows, tab['soff'], tab['scnt'], tab['sgs'], tab['pad_lo'], tab['pad_hi'], lay)
    y_sorted = _experts(tab['block_e'], tab['n_valid'], tab['block_rows'], x_sorted, wts, lay)
    y_prompt, y_sample = _combine(hbuf, top_w, lp_cols, p_prompt.reshape(t_p, D_PLE), p_sample.reshape(n_s, D_PLE),
                                  y_sorted, tab['soff'], tab['scnt'], tab['sgs'], wts, lay)
    return (y_prompt.reshape(n_p, len_p, D_MODEL), y_sample.reshape(n_s, len_s, D_MODEL),
            st_prompt, st_sample, vn.reshape(n_s, len_s, D_GMLP))


def kernel(x_prompt, x_sample, state_gla, p_prompt, p_sample, norm1_g, w_in, w_gk2, b_gk, gla_norm_g, v_norm_g,
           v_norm_b, w_sp, b_sp, w_o, norm2_g, w_router, b_router, w_gate, b_gate, w_up, b_up, w_down, b_down,
           norm3_g, w_ple_gate, w_ple_proj, final_g):
    assert w_in.shape[0] == 1, "single-layer trunk"
    wts = _prepare_weights(norm1_g[0], w_in[0], w_gk2[0], b_gk[0], gla_norm_g[0], v_norm_g[0], v_norm_b[0],
                           w_sp[0], b_sp[0], w_o[0], norm2_g[0], w_router[0], b_router[0], w_gate[0], b_gate[0],
                           w_up[0], b_up[0], w_down[0], b_down[0], norm3_g[0], w_ple_gate[0], w_ple_proj[0],
                           final_g)
    y_p, y_s, st_p, st_s, vn_s = _layer(x_prompt, x_sample, state_gla[0], p_prompt[0], p_sample[0], wts)
    return (y_p, y_s, st_p[None], st_s[None].astype(state_gla.dtype), vn_s[None])
```

```python
import functools

import jax
import jax.numpy as jnp
from jax import lax
from jax.experimental import pallas as pl
from jax.experimental.pallas import tpu as pltpu

F32 = jnp.float32
BF16 = jnp.bfloat16
I32 = jnp.int32
U32 = jnp.uint32

D_MODEL = 1024
GLA_HEADS = 4
GLA_DK = 128
GLA_DV = 256
GLA_DK_T = GLA_HEADS * GLA_DK
GLA_DV_T = GLA_HEADS * GLA_DV
GK_RANK = 16
GATE_NORMALIZER = 16.0
GLA_CHUNK = 64
GLA_CHUNK_LOG2 = 6
D_GMLP = 1024
MIX_HEADS = 4
MIX_DH = D_GMLP // MIX_HEADS
MIX_CHUNK = 128
MIX_CHUNK_LOG2 = 7
assert 1 << GLA_CHUNK_LOG2 == GLA_CHUNK and 1 << MIX_CHUNK_LOG2 == MIX_CHUNK
N_EXPERTS = 32
TOP_K = 4
D_FF = 1024
SWIGLU_LIMIT = 7.0
SWIGLU_ALPHA = 1.702
D_PLE = 256
EPS = 1e-6

LANES = 128
ROW_SUBLANES = D_MODEL // LANES
VMEM_LIMIT_BYTES = 56 * 1024 * 1024

OFF_Q, OFF_K, OFF_V, OFF_R, OFF_U, OFF_VG, OFF_GA, OFF_GB, N_MAIN = (
    0, 512, 1024, 2048, 3072, 4096, 5120, 6144, 7168)

MIXER_TILE = 512
MIXER_PARTS = 2
STATE_TOKENS = 8
EXPERT_BLOCK = 768
EXPERT_PARTS = 3
WEIGHT_CAST_ROWS = 256
STRIP_CHUNK = 16
STRIP_CHUNK_LOG2 = 4
ZERO_ROWS = 64
ZERO_ROWS_LOG2 = 6
assert 1 << STRIP_CHUNK_LOG2 == STRIP_CHUNK and 1 << ZERO_ROWS_LOG2 == ZERO_ROWS


def _dot(a, b):
    return jnp.dot(a, b, preferred_element_type=F32)


def _dot_nt(a, b):
    return lax.dot_general(a, b, (((1,), (1,)), ((), ())), preferred_element_type=F32)


def _dot_tn(a, b):
    return lax.dot_general(a, b, (((0,), (0,)), ((), ())), preferred_element_type=F32)


def _rms(x, g):
    return x * lax.rsqrt(jnp.mean(x * x, axis=-1, keepdims=True) + EPS) * g


def _sigmoid(x):
    return 1.0 / (1.0 + jnp.exp(-x))


def _gelu(x):
    return 0.5 * x * (1.0 + lax.erf(x * (2.0 ** -0.5)))


def _log_sigmoid(x):
    return jnp.minimum(x, 0.0) - jnp.log1p(jnp.exp(-jnp.abs(x)))


def _split3(x):
    hi = x.astype(BF16)
    r1 = x - hi.astype(F32)
    mid = r1.astype(BF16)
    lo = (r1 - mid.astype(F32)).astype(BF16)
    return hi, mid, lo


def _const_spec(shape):
    nd = len(shape)
    return pl.BlockSpec(shape, lambda *_: (0,) * nd, pipeline_mode=pl.Buffered(1))


def _params(sem):
    return pltpu.CompilerParams(dimension_semantics=sem, vmem_limit_bytes=VMEM_LIMIT_BYTES)


def _main_cols(wa_ref, wb_ref, lo, hi):
    if hi <= OFF_U:
        return wa_ref[:, lo:hi]
    return wb_ref[:, lo - OFF_U:hi - OFF_U]


def _project_gla_inputs(n, wm, wgk1_ref, wgk2_ref, bgk_ref):
    q = _dot(n, wm(OFF_Q, OFF_K)) * (GLA_DK ** -0.5)
    k = _dot(n, wm(OFF_K, OFF_V))
    v = _dot(n, wm(OFF_V, OFF_R))
    gk = _dot(n, wgk1_ref[...]).astype(BF16)
    log_a = _log_sigmoid(_dot(gk, wgk2_ref[...]) + bgk_ref[...]) * (1.0 / GATE_NORMALIZER)
    return q, k, v, log_a


def _gmlp_inputs(n, wm, vng_ref, vnb_ref):
    u = _gelu(_dot(n, wm(OFF_U, OFF_VG)))
    vg = _gelu(_dot(n, wm(OFF_VG, OFF_GA)))
    mu = jnp.mean(vg, axis=-1, keepdims=True)
    vc = vg - mu
    var = jnp.mean(vc * vc, axis=-1, keepdims=True)
    vn = vc * lax.rsqrt(var + EPS) * vng_ref[...] + vnb_ref[...]
    ug = u * _sigmoid(_dot(n, wm(OFF_GB, N_MAIN)))
    return ug, vn


def _gla_out_gate(n, wm):
    r = _dot(n, wm(OFF_R, OFF_U))
    ga = _dot(n, wm(OFF_GA, OFF_GB))
    return r * _sigmoid(r) * _sigmoid(ga)


def _head_rms(o, g):
    return o * lax.rsqrt(jnp.mean(o * o, axis=-1, keepdims=True) + EPS) * g


def _mixer_prompt_kernel(x_ref, hs_ref, *refs, tm, n_seq, n_tiles):
    b = pl.program_id(0)
    t = pl.program_id(1)
    h_ref = refs[12]

    @pl.when(b < n_seq)
    def _():
        _mixer_prompt_tile(x_ref, *refs, tm=tm, n_tiles=n_tiles)

    @pl.when((b == n_seq) & (t == 0))
    def _():
        h_ref[0:hs_ref.shape[0], :] = hs_ref[...]


def _mixer_prompt_tile(x_ref, g1_ref, wa_ref, wb_ref, wgk1_ref, wgk2_ref, bgk_ref, glag_ref, vng_ref, vnb_ref,
                       wsp_ref, bspt_ref, wo_ref,
                       h_ref, st_ref,
                       n_scr, oa_scr, s_scr, *, tm, n_tiles):
    t = pl.program_id(1)

    @pl.when(t == 0)
    def _():
        s_scr[...] = jnp.zeros_like(s_scr)

    wm = functools.partial(_main_cols, wa_ref, wb_ref)
    pm = tm // MIXER_PARTS
    for part in range(MIXER_PARTS):
        rows = pl.ds(part * pm, pm)
        _mixer_prompt_part(x_ref.at[0, rows], g1_ref, wm, wgk1_ref, wgk2_ref, bgk_ref, glag_ref, vng_ref, vnb_ref,
                           wsp_ref, bspt_ref, wo_ref, h_ref.at[rows], n_scr.at[rows], oa_scr.at[rows], s_scr, tm=pm)

    @pl.when(t == n_tiles - 1)
    def _():
        for h in range(GLA_HEADS):
            st_ref[0, h] = s_scr[h].T


def _mixer_prompt_part(x_ref, g1_ref, wm, wgk1_ref, wgk2_ref, bgk_ref, glag_ref, vng_ref, vnb_ref,
                       wsp_ref, bspt_ref, wo_ref, h_ref, n_scr, oa_scr, s_scr, *, tm):
    x = x_ref[...]
    n = _rms(x, g1_ref[...]).astype(BF16)
    n_scr[...] = n
    q, k, v, log_a = _project_gla_inputs(n, wm, wgk1_ref, wgk2_ref, bgk_ref)
    v = v.astype(BF16)

    row = lax.broadcasted_iota(I32, (tm, tm), 0)
    col = lax.broadcasted_iota(I32, (tm, tm), 1)
    same_chunk = lax.shift_right_logical(row, GLA_CHUNK_LOG2) == lax.shift_right_logical(col, GLA_CHUNK_LOG2)
    causal = same_chunk & (row >= col)
    tri = causal.astype(BF16)
    blk = same_chunk.astype(BF16)
    hi, mid, lo = _split3(log_a)
    b = _dot(tri, hi) + _dot(tri, mid) + _dot(tri, lo)
    b_end = _dot(blk, hi) + _dot(blk, mid) + _dot(blk, lo)
    qe = (q * jnp.exp(b)).astype(BF16)
    ke = (k * jnp.exp(-b)).astype(BF16)
    kd = (k * jnp.exp(b_end - b)).astype(BF16)
    dec = jnp.exp(b_end)
    glag = glag_ref[...]
    for h in range(GLA_HEADS):
        ks = slice(h * GLA_DK, (h + 1) * GLA_DK)
        vs = slice(h * GLA_DV, (h + 1) * GLA_DV)
        vh = v[:, vs]
        att = jnp.where(causal, _dot_nt(qe[:, ks], ke[:, ks]), 0.0).astype(BF16)
        o_intra = _dot(att, vh)
        s_t = s_scr[h]
        o_inter = []
        for c in range(tm // GLA_CHUNK):
            rs = slice(c * GLA_CHUNK, (c + 1) * GLA_CHUNK)
            o_inter.append(_dot_nt(qe[rs, ks], s_t.astype(BF16)))
            s_t = s_t * dec[c * GLA_CHUNK:c * GLA_CHUNK + 1, ks] + _dot_tn(vh[rs], kd[rs, ks])
        s_scr[h] = s_t
        oa_scr[:, vs] = _head_rms(o_intra + jnp.concatenate(o_inter, axis=0), glag)

    n = n_scr[...]
    oa_scr[...] = oa_scr[...] * _gla_out_gate(n, wm)
    ug, vn = _gmlp_inputs(n, wm, vng_ref, vnb_ref)
    vn = vn.astype(BF16)
    reps = tm // MIX_CHUNK
    mix_mask = (lax.shift_right_logical(row, MIX_CHUNK_LOG2) == lax.shift_right_logical(col, MIX_CHUNK_LOG2)) & (
        row >= col)
    for h in range(MIX_HEADS):
        w = jnp.where(mix_mask, jnp.tile(wsp_ref[h], (reps, reps)), 0.0).astype(BF16)
        cs = slice(h * MIX_DH, (h + 1) * MIX_DH)
        mix = _dot(w, vn[:, cs]) + jnp.tile(bspt_ref[:, h:h + 1], (reps, 1))
        oa_scr[:, cs] = oa_scr[:, cs] + ug[:, cs] * mix

    h_ref[...] = x + _dot(oa_scr[...].astype(BF16), wo_ref[...])


def _mixer_prompt(x, h_sample, wts):
    n_seq, seq_len, _ = x.shape
    n_s = h_sample.shape[0]
    tm = min(MIXER_TILE, seq_len)
    assert seq_len % tm == 0 and tm % MIX_CHUNK == 0 and n_s <= tm
    n_tiles = seq_len // tm
    consts = [wts['g1'], wts['w_a'], wts['w_b'], wts['w_gk1'], wts['w_gk2'], wts['b_gk'], wts['gla_g'], wts['vn_g'],
              wts['vn_b'], wts['w_sp'], wts['b_sp_t'], wts['w_o']]
    kern = functools.partial(_mixer_prompt_kernel, tm=tm, n_seq=n_seq, n_tiles=n_tiles)
    last = n_seq - 1

    def prompt_tile(b, t):
        return jnp.minimum(b, last), jnp.where(b < n_seq, t, n_tiles - 1)

    def x_map(b, t):
        bb, tt = prompt_tile(b, t)
        return bb, tt, 0

    def h_map(b, t):
        return jnp.where(b < n_seq, b * n_tiles + t, n_seq * n_tiles), 0

    return pl.pallas_call(
        kern,
        grid=(n_seq + 1, n_tiles),
        in_specs=[pl.BlockSpec((1, tm, D_MODEL), x_map), _const_spec(h_sample.shape)]
        + [_const_spec(c.shape) for c in consts],
        out_specs=[pl.BlockSpec((tm, D_MODEL), h_map),
                   pl.BlockSpec((1, GLA_HEADS, GLA_DK, GLA_DV), lambda b, t: (jnp.minimum(b, last), 0, 0, 0))],
        out_shape=[jax.ShapeDtypeStruct((n_seq * seq_len + n_s, D_MODEL), F32),
                   jax.ShapeDtypeStruct((n_seq, GLA_HEADS, GLA_DK, GLA_DV), F32)],
        scratch_shapes=[pltpu.VMEM((tm, D_MODEL), BF16),
                        pltpu.VMEM((tm, D_MODEL), F32),
                        pltpu.VMEM((GLA_HEADS, GLA_DV, GLA_DK), F32)],
        compiler_params=_params(("arbitrary", "arbitrary")),
        name="mixer_prompt",
    )(x, h_sample, *consts)


def _mixer_sample_in_kernel(x_ref, g1_ref, wa_ref, wb_ref, wgk1_ref, wgk2_ref, bgk_ref, vng_ref, vnb_ref,
                            mixw_ref, mixb_ref,
                            q_ref, k_ref, a_ref, v_ref, og_ref, ob_ref, vn_ref):
    wm = functools.partial(_main_cols, wa_ref, wb_ref)
    n = _rms(x_ref[...], g1_ref[...]).astype(BF16)
    q, k, v, log_a = _project_gla_inputs(n, wm, wgk1_ref, wgk2_ref, bgk_ref)
    q_ref[...] = q
    k_ref[...] = k
    a_ref[...] = jnp.exp(log_a)
    v_ref[...] = v
    og_ref[...] = _gla_out_gate(n, wm)
    ug, vn = _gmlp_inputs(n, wm, vng_ref, vnb_ref)
    vn_ref[...] = vn
    ob_ref[...] = ug * (mixw_ref[...] * vn + mixb_ref[...])


def _mixer_sample_in(x2, wts):
    n_seq = x2.shape[0]
    consts = [wts['g1'], wts['w_a'], wts['w_b'], wts['w_gk1'], wts['w_gk2'], wts['b_gk'], wts['vn_g'], wts['vn_b'],
              wts['mix_w0'], wts['mix_b0']]
    widths = [GLA_DK_T, GLA_DK_T, GLA_DK_T, GLA_DV_T, D_MODEL, D_MODEL, D_GMLP]
    return pl.pallas_call(
        _mixer_sample_in_kernel,
        grid=(1,),
        in_specs=[_const_spec(x2.shape)] + [_const_spec(c.shape) for c in consts],
        out_specs=[pl.BlockSpec((n_seq, w), lambda i: (0, 0)) for w in widths],
        out_shape=[jax.ShapeDtypeStruct((n_seq, w), F32) for w in widths],
        compiler_params=_params(("arbitrary",)),
        name="mixer_sample_in",
    )(x2, *consts)


def _state_update_kernel(s_ref, q_ref, k_ref, a_ref, v_ref, so_ref, o_ref):
    for j in range(STATE_TOKENS):
        for h in range(GLA_HEADS):
            ks = slice(h * GLA_DK, (h + 1) * GLA_DK)
            vs = slice(h * GLA_DV, (h + 1) * GLA_DV)
            s_new = a_ref[0, ks, j:j + 1] * s_ref[j, h] + k_ref[0, ks, j:j + 1] * v_ref[j:j + 1, vs]
            so_ref[j, h] = s_new
            o_ref[j:j + 1, vs] = jnp.sum(q_ref[0, ks, j:j + 1] * s_new, axis=0, keepdims=True)


def _state_update(state, q, k, a, v):
    n_seq = state.shape[0]
    tb = STATE_TOKENS
    assert n_seq % tb == 0

    def cols(z):
        return z.reshape(n_seq // tb, tb, GLA_DK_T).transpose(0, 2, 1)

    col_spec = pl.BlockSpec((1, GLA_DK_T, tb), lambda i: (i, 0, 0))
    st_spec = pl.BlockSpec((tb, GLA_HEADS, GLA_DK, GLA_DV), lambda i: (i, 0, 0, 0))
    row_spec = pl.BlockSpec((tb, GLA_DV_T), lambda i: (i, 0))
    return pl.pallas_call(
        _state_update_kernel,
        grid=(n_seq // tb,),
        in_specs=[st_spec, col_spec, col_spec, col_spec, row_spec],
        out_specs=[st_spec, row_spec],
        out_shape=[jax.ShapeDtypeStruct(state.shape, F32), jax.ShapeDtypeStruct((n_seq, GLA_DV_T), F32)],
        compiler_params=_params(("arbitrary",)),
        name="state_update",
    )(state, cols(q), cols(k), cols(a), v)


def _mixer_sample_out_kernel(x_ref, o_ref, og_ref, ob_ref, glag_ref, wo_ref, h_ref):
    glag = glag_ref[...]
    parts = []
    for h in range(GLA_HEADS):
        vs = slice(h * GLA_DV, (h + 1) * GLA_DV)
        parts.append(_head_rms(o_ref[:, vs], glag))
    merged = jnp.concatenate(parts, axis=1) * og_ref[...] + ob_ref[...]
    h_ref[...] = x_ref[...] + _dot(merged.astype(BF16), wo_ref[...])


def _mixer_sample_out(x2, o, og, ob, wts):
    n_seq = x2.shape[0]
    consts = [x2, o, og, ob, wts['gla_g'], wts['w_o']]
    return pl.pallas_call(
        _mixer_sample_out_kernel,
        grid=(1,),
        in_specs=[_const_spec(c.shape) for c in consts],
        out_specs=pl.BlockSpec((n_seq, D_MODEL), lambda i: (0, 0)),
        out_shape=jax.ShapeDtypeStruct((n_seq, D_MODEL), F32),
        compiler_params=_params(("arbitrary",)),
        name="mixer_sample_out",
    )(*consts)


def _router_kernel(h_ref, g2_ref, wr_ref, br_ref, xp_ref, tw_ref, lpc_ref, lpr_ref, cb_ref, tc_ref, lo_ref, cnt_ref, cnt_scr,
                   *, tm):
    i = pl.program_id(0)

    @pl.when(i == 0)
    def _():
        cnt_scr[...] = jnp.zeros_like(cnt_scr)

    cb_ref[0] = cnt_scr[...].astype(I32)

    hn = _rms(h_ref[...], g2_ref[...])
    hn_hi = hn.astype(BF16)
    hn_lo = (hn - hn_hi.astype(F32)).astype(BF16)
    w = wr_ref[...]
    w_hi = w.astype(BF16)
    w_lo = (w - w_hi.astype(F32)).astype(BF16)
    logits = _dot(hn_hi, w_hi) + _dot(hn_lo, w_hi) + _dot(hn_hi, w_lo) + br_ref[...]

    _rows_to_tiles(xp_ref, hn)

    lane = lax.broadcasted_iota(I32, (tm, LANES), 1)
    neg = jnp.float32(-jnp.inf)
    l = jnp.where(lane < N_EXPERTS, logits, neg)
    vals, hots = [], []
    for _ in range(TOP_K):
        m = jnp.max(l, axis=-1, keepdims=True)
        idx = jnp.min(jnp.where(l == m, lane, LANES), axis=-1, keepdims=True)
        hot = lane == idx
        l = jnp.where(hot, neg, l)
        vals.append(m)
        hots.append(hot)
    exps = [jnp.exp(v - vals[0]) for v in vals]
    denom = exps[0] + exps[1] + exps[2] + exps[3]

    member = (hots[0] | hots[1] | hots[2] | hots[3]).astype(BF16)
    row = lax.broadcasted_iota(I32, (tm, tm), 0)
    col = lax.broadcasted_iota(I32, (tm, tm), 1)
    in_tile = _dot((row > col).astype(BF16), member)
    tile_cnt = jnp.sum(member.astype(F32), axis=0, keepdims=True)
    e_row = lax.broadcasted_iota(I32, (LANES, LANES), 0)
    e_col = lax.broadcasted_iota(I32, (LANES, LANES), 1)
    lower = (e_row < e_col).astype(BF16)
    c_hi, c_mid, c_lo = _split3(jnp.broadcast_to(tile_cnt, (ROW_SUBLANES, LANES)))
    local_off = (_dot(c_hi, lower) + _dot(c_mid, lower) + _dot(c_lo, lower))[0:1, :]
    tc_ref[0] = tile_cnt.astype(I32)
    lo_ref[0] = local_off.astype(I32)
    cnt_scr[...] = cnt_scr[...] + tile_cnt
    cnt_ref[...] = cnt_scr[...].astype(I32)

    place = local_off + in_tile
    tw = jnp.zeros((tm, LANES), F32)
    lp = jnp.zeros((tm, LANES), F32)
    for kk in range(TOP_K):
        sel = lane == kk
        tw = jnp.where(sel, exps[kk] / denom, tw)
        lp = jnp.where(sel, jnp.sum(jnp.where(hots[kk], place, 0.0), axis=-1, keepdims=True), lp)
    tw_ref[...] = tw
    lpc_ref[...] = lp.astype(I32)
    lpr_ref[0] = lp.T[0:ROW_SUBLANES, :].astype(I32)


def _router_tile(t_all):
    for tm in (512, 384, 256, 128):
        if t_all % tm == 0:
            return tm
    raise ValueError(f"token count {t_all} must be a multiple of 128")


def _router(hbuf, wts):
    t_all = hbuf.shape[0]
    tm = _router_tile(t_all)
    consts = [wts['g2'], wts['w_router'], wts['b_router']]
    row = lambda w: pl.BlockSpec((tm, w), lambda i: (i, 0))
    n_tiles = t_all // tm
    per_tile = pl.BlockSpec((1, 1, LANES), lambda i: (i, 0, 0))
    tile_vec = jax.ShapeDtypeStruct((n_tiles, 1, LANES), I32)
    return pl.pallas_call(
        functools.partial(_router_kernel, tm=tm),
        grid=(t_all // tm,),
        in_specs=[row(D_MODEL)] + [_const_spec(c.shape) for c in consts],
        out_specs=[pl.BlockSpec((tm * ROW_SUBLANES, LANES), lambda i: (i, 0)), row(LANES), row(LANES),
                   pl.BlockSpec((1, ROW_SUBLANES, tm), lambda i: (i, 0, 0)),
                   per_tile, per_tile, per_tile, pl.BlockSpec((1, LANES), lambda i: (0, 0))],
        out_shape=[jax.ShapeDtypeStruct((t_all * ROW_SUBLANES, LANES), F32),
                   jax.ShapeDtypeStruct((t_all, LANES), F32),
                   jax.ShapeDtypeStruct((t_all, LANES), I32),
                   jax.ShapeDtypeStruct((n_tiles, ROW_SUBLANES, tm), I32),
                   tile_vec, tile_vec, tile_vec, jax.ShapeDtypeStruct((1, LANES), I32)],
        scratch_shapes=[pltpu.VMEM((1, LANES), F32)],
        compiler_params=_params(("arbitrary",)),
        name="router",
    )(hbuf, *consts)


def _row_tile(r):
    start = r * ROW_SUBLANES
    if not isinstance(r, int):
        start = pl.multiple_of(start, ROW_SUBLANES)
    return pl.ds(start, ROW_SUBLANES)


def _tile_of(ref, row):
    return ref.at[_row_tile(row)]


def _rows_from_tiles(ref, n_rows):
    return jnp.concatenate([ref[pl.ds(s, n_rows, stride=ROW_SUBLANES), :] for s in range(ROW_SUBLANES)], axis=1)


def _rows_to_tiles(ref, x):
    for s in range(ROW_SUBLANES):
        ref[pl.ds(s, x.shape[0], stride=ROW_SUBLANES), :] = x[:, s * LANES:(s + 1) * LANES]


def _tiles(row, n_rows):
    return pl.ds(pl.multiple_of(row * ROW_SUBLANES, ROW_SUBLANES), n_rows * ROW_SUBLANES)


class _Layout:
    def __init__(self, t_all):
        bm = EXPERT_BLOCK
        self.t_all = t_all
        self.tile = _router_tile(t_all)
        self.n_tiles = t_all // self.tile
        self.n_blocks = -(-(t_all * TOP_K + N_EXPERTS * (bm - 1)) // bm)
        self.sorted_rows = self.n_blocks * bm


def _strip_copies(src_ref, src_row, dst_ref, dst_row, n, sem):
    def copy(offset, size):
        pltpu.make_async_copy(src_ref.at[_tiles(src_row + offset, size)], dst_ref.at[_tiles(dst_row + offset, size)],
                              sem).start()

    n_chunks = lax.shift_right_logical(n, STRIP_CHUNK_LOG2)

    def chunk(j, c):
        copy(j * STRIP_CHUNK, STRIP_CHUNK)
        return c
    lax.fori_loop(0, n_chunks, chunk, 0)

    done = n_chunks * STRIP_CHUNK
    size = STRIP_CHUNK // 2
    while size >= 1:
        part = n & size

        @pl.when(part != 0)
        def _(done=done, size=size):
            copy(done, size)
        done = done + part
        size //= 2


def _wait_rows(hbm_ref, vmem_ref, n_rows, sem):
    pltpu.make_async_copy(hbm_ref.at[pl.ds(0, n_rows * ROW_SUBLANES)], vmem_ref.at[pl.ds(0, n_rows * ROW_SUBLANES)],
                          sem).wait()


def _dispatch_kernel(soff_ref, scnt_ref, sgs_ref, lo_ref, hi_ref, xp_ref, lpr_ref, xs_hbm, loc, zero_scr, sem,
                     *, lay):
    tm = lay.tile
    n_pairs = TOP_K * tm
    g = pl.program_id(0)
    slot = g % 2

    def zero_fill(lo, hi, wait):
        def copies(first, count, size):
            def body(j, c):
                cp = pltpu.make_async_copy(zero_scr.at[pl.ds(0, size * ROW_SUBLANES)],
                                           xs_hbm.at[_tiles(first + j * size, size)], sem.at[2])
                if wait:
                    cp.wait()
                else:
                    cp.start()
                return c
            lax.fori_loop(0, count, body, 0)
        n = hi - lo
        n_big = lax.shift_right_logical(n, ZERO_ROWS_LOG2)
        copies(lo, n_big, ZERO_ROWS)
        done = n_big * ZERO_ROWS
        n_mid = lax.shift_right_logical(n - done, STRIP_CHUNK_LOG2)
        copies(lo + done, n_mid, STRIP_CHUNK)
        done = done + n_mid * STRIP_CHUNK
        copies(lo + done, n - done, 1)

    @pl.when(g == 0)
    def _():
        zero_scr[...] = jnp.zeros_like(zero_scr)
        for wait in (False, True):
            def expert(e, c, wait=wait):
                zero_fill(lo_ref[e], hi_ref[e], wait)
                return c
            lax.fori_loop(0, N_EXPERTS, expert, 0)
            zero_fill(hi_ref[N_EXPERTS - 1], lay.sorted_rows, wait)

    @pl.when(g >= 2)
    def _():
        _wait_rows(xs_hbm, loc.at[slot], n_pairs, sem.at[slot])

    x = _rows_from_tiles(xp_ref, tm).astype(BF16)
    pos = lax.broadcasted_iota(I32, (n_pairs, tm), 0)
    lpr = lpr_ref[0]
    hit = pos == lpr[0:1, :]
    for k in range(1, TOP_K):
        hit = hit | (pos == lpr[k:k + 1, :])
    _rows_to_tiles(loc.at[slot], _dot(hit.astype(BF16), x))

    def strip(e, c):
        s = g * N_EXPERTS + e
        _strip_copies(loc.at[slot], soff_ref[s], xs_hbm, sgs_ref[s], scnt_ref[s], sem.at[slot])
        return c
    lax.fori_loop(0, N_EXPERTS, strip, 0)

    @pl.when(g == lay.n_tiles - 1)
    def _():
        _wait_rows(xs_hbm, loc.at[slot], n_pairs, sem.at[slot])
        if lay.n_tiles > 1:
            _wait_rows(xs_hbm, loc.at[1 - slot], n_pairs, sem.at[1 - slot])


def _dispatch(xp, lp_rows, soff, scnt, sgs, pad_lo, pad_hi, lay):
    tm = lay.tile
    grid_spec = pltpu.PrefetchScalarGridSpec(
        num_scalar_prefetch=5,
        grid=(lay.n_tiles,),
        in_specs=[pl.BlockSpec((tm * ROW_SUBLANES, LANES), lambda g, *_: (g, 0)),
                  pl.BlockSpec((1, ROW_SUBLANES, tm), lambda g, *_: (g, 0, 0))],
        out_specs=pl.BlockSpec(memory_space=pl.ANY),
        scratch_shapes=[pltpu.VMEM((2, TOP_K * tm * ROW_SUBLANES, LANES), F32),
                        pltpu.VMEM((ZERO_ROWS * ROW_SUBLANES, LANES), F32),
                        pltpu.SemaphoreType.DMA((3,))],
    )
    return pl.pallas_call(
        functools.partial(_dispatch_kernel, lay=lay),
        grid_spec=grid_spec,
        out_shape=jax.ShapeDtypeStruct((lay.sorted_rows * ROW_SUBLANES, LANES), F32),
        compiler_params=_params(("arbitrary",)),
        name="dispatch",
    )(soff, scnt, sgs, pad_lo, pad_hi, xp, lp_rows)


def _experts_kernel(be_ref, nv_ref, xs_ref, wg_ref, wu_ref, wd_ref, bg_ref, bu_ref, bd_ref, ys_ref, wbf):
    bm = EXPERT_BLOCK
    i = pl.program_id(0)
    n_valid = nv_ref[0]

    @pl.when(i < n_valid)
    def _():
        @pl.when((i == 0) | (be_ref[i] != be_ref[jnp.maximum(i - 1, 0)]))
        def _():
            for w, w_ref in enumerate((wg_ref, wu_ref, wd_ref)):
                for c in range(0, D_MODEL, WEIGHT_CAST_ROWS):
                    wbf[w, c:c + WEIGHT_CAST_ROWS, :] = w_ref[0, c:c + WEIGHT_CAST_ROWS, :].astype(BF16)

        part = bm // EXPERT_PARTS
        for j in range(EXPERT_PARTS):
            tiles = pl.ds(j * part * ROW_SUBLANES, part * ROW_SUBLANES)
            x = _rows_from_tiles(xs_ref.at[tiles], part).astype(BF16)
            g = jnp.minimum(_dot(x, wbf[0]) + bg_ref[0], SWIGLU_LIMIT)
            u = jnp.clip(_dot(x, wbf[1]) + bu_ref[0], -SWIGLU_LIMIT, SWIGLU_LIMIT)
            mid = ((u + 1.0) * (g * _sigmoid(SWIGLU_ALPHA * g))).astype(BF16)
            _rows_to_tiles(ys_ref.at[tiles], _dot(mid, wbf[2]) + bd_ref[0])

    @pl.when(i >= n_valid)
    def _():
        ys_ref[...] = jnp.zeros_like(ys_ref)


def _experts(block_e, n_valid, xs, wts, lay):
    bm = EXPERT_BLOCK
    tile_rows = bm * ROW_SUBLANES
    wspec = pl.BlockSpec((1, D_MODEL, D_FF), lambda i, be, nv: (be[i], 0, 0))
    bspec = pl.BlockSpec((1, 1, D_FF), lambda i, be, nv: (be[i], 0, 0))
    grid_spec = pltpu.PrefetchScalarGridSpec(
        num_scalar_prefetch=2,
        grid=(lay.n_blocks,),
        in_specs=[pl.BlockSpec((tile_rows, LANES), lambda i, be, nv: (jnp.minimum(i, nv[0] - 1), 0)),
                  wspec, wspec, wspec, bspec, bspec, bspec],
        out_specs=pl.BlockSpec((tile_rows, LANES), lambda i, be, nv: (i, 0)),
        scratch_shapes=[pltpu.VMEM((3, D_MODEL, D_FF), BF16)],
    )
    return pl.pallas_call(
        functools.partial(_experts_kernel),
        grid_spec=grid_spec,
        out_shape=jax.ShapeDtypeStruct((lay.sorted_rows * ROW_SUBLANES, LANES), F32),
        compiler_params=_params(("arbitrary",)),
        name="experts",
    )(block_e, n_valid, xs, wts['w_gate'], wts['w_up'], wts['w_down'], wts['b_gate'], wts['b_up'], wts['b_down'])


def _combine_kernel(soff_ref, scnt_ref, sgs_ref, h_ref, tw_ref, lpc_ref, pp_ref, ps_ref, ys_hbm, g3_ref, wpg_ref,
                    wpp_ref, gf_ref, yp_ref, ysm_ref, loc, sem, *, lay, n_sample):
    tm = lay.tile
    n_pairs = TOP_K * tm
    g = pl.program_id(0)
    slot = g % 2

    def fetch(tile, s):
        def strip(e, c):
            i = tile * N_EXPERTS + e
            _strip_copies(ys_hbm, sgs_ref[i], loc.at[s], soff_ref[i], scnt_ref[i], sem.at[s])
            return c
        lax.fori_loop(0, N_EXPERTS, strip, 0)

    @pl.when(g == 0)
    def _():
        fetch(0, 0)

    @pl.when(g + 1 < lay.n_tiles)
    def _():
        fetch(g + 1, 1 - slot)

    _wait_rows(ys_hbm, loc.at[slot], n_pairs, sem.at[slot])
    y_rows = _rows_from_tiles(loc.at[slot], n_pairs).astype(BF16)
    tw = tw_ref[...]
    lpc = lpc_ref[...]
    pos = lax.broadcasted_iota(I32, (tm, n_pairs), 1)
    weight = jnp.zeros((tm, n_pairs), F32)
    for k in range(TOP_K):
        weight = jnp.where(pos == lpc[:, k:k + 1], tw[:, k:k + 1], weight)
    y = _dot(weight.astype(BF16), y_rows)
    h2 = h_ref[...] + y
    gate = _sigmoid(_dot(_rms(h2, g3_ref[...]).astype(BF16), wpg_ref[...]))
    p = pp_ref[...]
    p = jnp.where(g == lay.n_tiles - 1, jnp.concatenate([p[:tm - n_sample], ps_ref[...]], axis=0), p)
    h3 = h2 + gate * _dot(p.astype(BF16), wpp_ref[...])
    out = _rms(h3, gf_ref[...])
    yp_ref[...] = out

    @pl.when(g == lay.n_tiles - 1)
    def _():
        ysm_ref[...] = out[tm - n_sample:, :]


def _combine(hbuf, tw, lp_cols, p_prompt, p_sample, ys, soff, scnt, sgs, wts, lay):
    tm = lay.tile
    n_prompt, n_sample = p_prompt.shape[0], p_sample.shape[0]
    assert 0 < n_sample <= tm and (lay.n_tiles - 1) * tm < n_prompt
    consts = [wts['g3'], wts['w_ple_gate'], wts['w_ple_proj'], wts['g_final']]
    row = lambda w: pl.BlockSpec((tm, w), lambda g, *_: (g, 0))
    grid_spec = pltpu.PrefetchScalarGridSpec(
        num_scalar_prefetch=3,
        grid=(lay.n_tiles,),
        in_specs=[row(D_MODEL), row(LANES), row(LANES), row(D_PLE),
                  pl.BlockSpec((n_sample, D_PLE), lambda g, *_: (0, 0)), pl.BlockSpec(memory_space=pl.ANY)]
        + [pl.BlockSpec(c.shape, lambda g, *_: (0, 0), pipeline_mode=pl.Buffered(1)) for c in consts],
        out_specs=[row(D_MODEL), pl.BlockSpec((n_sample, D_MODEL), lambda g, *_: (0, 0))],
        scratch_shapes=[pltpu.VMEM((2, TOP_K * tm * ROW_SUBLANES, LANES), F32), pltpu.SemaphoreType.DMA((2,))],
    )
    return pl.pallas_call(
        functools.partial(_combine_kernel, lay=lay, n_sample=n_sample),
        grid_spec=grid_spec,
        out_shape=[jax.ShapeDtypeStruct((n_prompt, D_MODEL), F32), jax.ShapeDtypeStruct((n_sample, D_MODEL), F32)],
        compiler_params=_params(("arbitrary",)),
        name="combine",
    )(soff, scnt, sgs, hbuf, tw, lp_cols, p_prompt, p_sample, ys, *consts)


def _prepare_weights(norm1_g, w_in, w_gk2, b_gk, gla_norm_g, v_norm_g, v_norm_b, w_sp, b_sp, w_o, norm2_g,
                     w_router, b_router, w_gate, b_gate, w_up, b_up, w_down, b_down, norm3_g, w_ple_gate,
                     w_ple_proj, final_g):
    o_gk = 2 * GLA_DK_T + 2 * GLA_DV_T
    w_a = w_in[:, :o_gk].astype(BF16)
    w_b = w_in[:, o_gk + GK_RANK:].astype(BF16)
    w_gk1 = jnp.pad(w_in[:, o_gk:o_gk + GK_RANK], ((0, 0), (0, LANES - GK_RANK))).astype(BF16)
    w_gk2p = jnp.pad(w_gk2, ((0, LANES - GK_RANK), (0, 0))).astype(BF16)
    row = lambda z: z.reshape(1, -1).astype(F32)
    return dict(
        g1=row(norm1_g), w_a=w_a, w_b=w_b, w_gk1=w_gk1, w_gk2=w_gk2p, b_gk=row(b_gk), gla_g=row(gla_norm_g),
        vn_g=row(v_norm_g), vn_b=row(v_norm_b), w_sp=w_sp, b_sp_t=b_sp.T,
        mix_w0=row(jnp.repeat(w_sp[:, 0, 0], MIX_DH)), mix_b0=row(jnp.repeat(b_sp[:, 0], MIX_DH)),
        w_o=w_o.astype(BF16), g2=row(norm2_g),
        w_router=jnp.pad(w_router, ((0, 0), (0, LANES - N_EXPERTS))),
        b_router=jnp.pad(row(b_router), ((0, 0), (0, LANES - N_EXPERTS))),
        w_gate=w_gate, w_up=w_up, w_down=w_down,
        b_gate=b_gate.reshape(N_EXPERTS, 1, D_FF), b_up=b_up.reshape(N_EXPERTS, 1, D_FF),
        b_down=b_down.reshape(N_EXPERTS, 1, D_MODEL),
        g3=row(norm3_g), w_ple_gate=w_ple_gate.astype(BF16), w_ple_proj=w_ple_proj.astype(BF16),
        g_final=row(final_g))


def _routing_tables(counts, counts_before, tile_counts, local_off, lay):
    bm = EXPERT_BLOCK
    padded = (counts + bm - 1) // bm * bm
    pad_end = jnp.cumsum(padded)
    pad_start = pad_end - padded
    global_start = pad_start[None, :] + counts_before
    n_valid = (pad_end[-1] // bm).astype(I32)
    blk = jnp.minimum(jnp.arange(lay.n_blocks, dtype=I32), n_valid - 1)
    block_e = jnp.sum(pad_end[None, :] <= (blk * bm)[:, None], axis=1).astype(I32)
    flat = lambda z: z.reshape(-1).astype(I32)
    return dict(soff=flat(local_off), scnt=flat(tile_counts), sgs=flat(global_start),
                pad_lo=(pad_start + counts).astype(I32), pad_hi=pad_end.astype(I32), block_e=block_e,
                n_valid=n_valid.reshape(1))


def _layer(x_prompt, x_sample, state, p_prompt, p_sample, wts):
    n_p, len_p, _ = x_prompt.shape
    n_s, len_s, _ = x_sample.shape
    assert len_s == 1, "the sample group carries one new token per sequence"
    t_p = n_p * len_p
    t_all = t_p + n_s

    xs = x_sample.reshape(n_s, D_MODEL)
    q, k, a, v, og, ob, vn = _mixer_sample_in(xs, wts)
    st_sample, o = _state_update(state, q, k, a, v)
    h_sample = _mixer_sample_out(xs, o, og, ob, wts)
    hbuf, st_prompt = _mixer_prompt(x_prompt, h_sample, wts)

    lay = _Layout(t_all)
    xp, top_w, lp_cols, lp_rows, counts_before, tile_counts, local_off, counts = _router(hbuf, wts)
    ne = N_EXPERTS
    tab = _routing_tables(counts[0, :ne], counts_before[:, 0, :ne], tile_counts[:, 0, :ne], local_off[:, 0, :ne], lay)
    x_sorted = _dispatch(xp, lp_rows, tab['soff'], tab['scnt'], tab['sgs'], tab['pad_lo'], tab['pad_hi'], lay)
    y_sorted = _experts(tab['block_e'], tab['n_valid'], x_sorted, wts, lay)
    y_prompt, y_sample = _combine(hbuf, top_w, lp_cols, p_prompt.reshape(t_p, D_PLE), p_sample.reshape(n_s, D_PLE),
                                  y_sorted, tab['soff'], tab['scnt'], tab['sgs'], wts, lay)
    return (y_prompt.reshape(n_p, len_p, D_MODEL), y_sample.reshape(n_s, len_s, D_MODEL),
            st_prompt, st_sample, vn.reshape(n_s, len_s, D_GMLP))


def kernel(x_prompt, x_sample, state_gla, p_prompt, p_sample, norm1_g, w_in, w_gk2, b_gk, gla_norm_g, v_norm_g,
           v_norm_b, w_sp, b_sp, w_o, norm2_g, w_router, b_router, w_gate, b_gate, w_up, b_up, w_down, b_down,
           norm3_g, w_ple_gate, w_ple_proj, final_g):
    assert w_in.shape[0] == 1, "single-layer trunk"
    wts = _prepare_weights(norm1_g[0], w_in[0], w_gk2[0], b_gk[0], gla_norm_g[0], v_norm_g[0], v_norm_b[0],
                           w_sp[0], b_sp[0], w_o[0], norm2_g[0], w_router[0], b_router[0], w_gate[0], b_gate[0],
                           w_up[0], b_up[0], w_down[0], b_down[0], norm3_g[0], w_ple_gate[0], w_ple_proj[0],
                           final_g)
    y_p, y_s, st_p, st_s, vn_s = _layer(x_prompt, x_sample, state_gla[0], p_prompt[0], p_sample[0], wts)
    return (y_p, y_s, st_p[None], st_s[None].astype(state_gla.dtype), vn_s[None])
```

```python
import functools

import jax
import jax.numpy as jnp
from jax import lax
from jax.experimental import pallas as pl
from jax.experimental.pallas import tpu as pltpu

F32 = jnp.float32
BF16 = jnp.bfloat16
I32 = jnp.int32

D_MODEL = 1024
GLA_HEADS = 4
GLA_DK = 128
GLA_DV = 256
GLA_DK_T = GLA_HEADS * GLA_DK
GLA_DV_T = GLA_HEADS * GLA_DV
GK_RANK = 16
GATE_NORMALIZER = 16.0
GLA_CHUNK = 64
GLA_CHUNK_LOG2 = 6
D_GMLP = 1024
MIX_HEADS = 4
MIX_DH = D_GMLP // MIX_HEADS
MIX_CHUNK = 128
MIX_CHUNK_LOG2 = 7
assert 1 << GLA_CHUNK_LOG2 == GLA_CHUNK and 1 << MIX_CHUNK_LOG2 == MIX_CHUNK
N_EXPERTS = 32
TOP_K = 4
D_FF = 1024
SWIGLU_LIMIT = 7.0
SWIGLU_ALPHA = 1.702
D_PLE = 256
EPS = 1e-6

LANES = 128
ROW_SUBLANES = D_MODEL // LANES
VMEM_LIMIT_BYTES = 56 * 1024 * 1024

OFF_Q, OFF_K, OFF_V, OFF_R, OFF_U, OFF_VG, OFF_GA, OFF_GB, N_MAIN = (
    0, 512, 1024, 2048, 3072, 4096, 5120, 6144, 7168)

MIXER_TILE = 512
MIXER_PARTS = 2
STATE_TOKENS = 8
EXPERT_BLOCK = 768
EXPERT_PARTS = 3
WEIGHT_CAST_ROWS = 256
STRIP_CHUNK = 16
STRIP_CHUNK_LOG2 = 4
ZERO_ROWS = 64
ZERO_ROWS_LOG2 = 6
assert 1 << STRIP_CHUNK_LOG2 == STRIP_CHUNK and 1 << ZERO_ROWS_LOG2 == ZERO_ROWS


def _dot(a, b):
    return jnp.dot(a, b, preferred_element_type=F32)


def _dot_nt(a, b):
    return lax.dot_general(a, b, (((1,), (1,)), ((), ())), preferred_element_type=F32)


def _dot_tn(a, b):
    return lax.dot_general(a, b, (((0,), (0,)), ((), ())), preferred_element_type=F32)


def _rms(x, g):
    return x * lax.rsqrt(jnp.mean(x * x, axis=-1, keepdims=True) + EPS) * g


def _sigmoid(x):
    return 1.0 / (1.0 + jnp.exp(-x))


def _gelu(x):
    return 0.5 * x * (1.0 + lax.erf(x * (2.0 ** -0.5)))


def _log_sigmoid(x):
    return jnp.minimum(x, 0.0) - jnp.log1p(jnp.exp(-jnp.abs(x)))


def _split3(x):
    hi = x.astype(BF16)
    r1 = x - hi.astype(F32)
    mid = r1.astype(BF16)
    lo = (r1 - mid.astype(F32)).astype(BF16)
    return hi, mid, lo


def _const_spec(shape):
    nd = len(shape)
    return pl.BlockSpec(shape, lambda *_: (0,) * nd, pipeline_mode=pl.Buffered(1))


def _params(sem):
    return pltpu.CompilerParams(dimension_semantics=sem, vmem_limit_bytes=VMEM_LIMIT_BYTES)


def _main_cols(wa_ref, wb_ref, lo, hi):
    if hi <= OFF_U:
        return wa_ref[:, lo:hi]
    return wb_ref[:, lo - OFF_U:hi - OFF_U]


def _project_gla_inputs(n, wm, wgk1_ref, wgk2_ref, bgk_ref):
    q = _dot(n, wm(OFF_Q, OFF_K)) * (GLA_DK ** -0.5)
    k = _dot(n, wm(OFF_K, OFF_V))
    v = _dot(n, wm(OFF_V, OFF_R))
    gk = _dot(n, wgk1_ref[...]).astype(BF16)
    log_a = _log_sigmoid(_dot(gk, wgk2_ref[...]) + bgk_ref[...]) * (1.0 / GATE_NORMALIZER)
    return q, k, v, log_a


def _gmlp_inputs(n, wm, vng_ref, vnb_ref):
    u = _gelu(_dot(n, wm(OFF_U, OFF_VG)))
    vg = _gelu(_dot(n, wm(OFF_VG, OFF_GA)))
    mu = jnp.mean(vg, axis=-1, keepdims=True)
    vc = vg - mu
    var = jnp.mean(vc * vc, axis=-1, keepdims=True)
    vn = vc * lax.rsqrt(var + EPS) * vng_ref[...] + vnb_ref[...]
    ug = u * _sigmoid(_dot(n, wm(OFF_GB, N_MAIN)))
    return ug, vn


def _gla_out_gate(n, wm):
    r = _dot(n, wm(OFF_R, OFF_U))
    ga = _dot(n, wm(OFF_GA, OFF_GB))
    return r * _sigmoid(r) * _sigmoid(ga)


def _head_rms(o, g):
    return o * lax.rsqrt(jnp.mean(o * o, axis=-1, keepdims=True) + EPS) * g


def _mixer_prompt_kernel(x_ref, hs_ref, *refs, tm, n_seq, n_tiles):
    b = pl.program_id(0)
    t = pl.program_id(1)
    h_ref = refs[12]

    @pl.when(b < n_seq)
    def _():
        _mixer_prompt_tile(x_ref, *refs, tm=tm, n_tiles=n_tiles)

    @pl.when((b == n_seq) & (t == 0))
    def _():
        h_ref[0:hs_ref.shape[0], :] = hs_ref[...]


def _mixer_prompt_tile(x_ref, g1_ref, wa_ref, wb_ref, wgk1_ref, wgk2_ref, bgk_ref, glag_ref, vng_ref, vnb_ref,
                       wsp_ref, bspt_ref, wo_ref,
                       h_ref, st_ref,
                       n_scr, oa_scr, s_scr, *, tm, n_tiles):
    t = pl.program_id(1)

    @pl.when(t == 0)
    def _():
        s_scr[...] = jnp.zeros_like(s_scr)

    wm = functools.partial(_main_cols, wa_ref, wb_ref)
    pm = tm // MIXER_PARTS
    for part in range(MIXER_PARTS):
        rows = pl.ds(part * pm, pm)
        _mixer_prompt_part(x_ref.at[0, rows], g1_ref, wm, wgk1_ref, wgk2_ref, bgk_ref, glag_ref, vng_ref, vnb_ref,
                           wsp_ref, bspt_ref, wo_ref, h_ref.at[rows], n_scr.at[rows], oa_scr.at[rows], s_scr, tm=pm)

    @pl.when(t == n_tiles - 1)
    def _():
        for h in range(GLA_HEADS):
            st_ref[0, h] = s_scr[h].T


def _mixer_prompt_part(x_ref, g1_ref, wm, wgk1_ref, wgk2_ref, bgk_ref, glag_ref, vng_ref, vnb_ref,
                       wsp_ref, bspt_ref, wo_ref, h_ref, n_scr, oa_scr, s_scr, *, tm):
    x = x_ref[...]
    n = _rms(x, g1_ref[...]).astype(BF16)
    n_scr[...] = n
    q, k, v, log_a = _project_gla_inputs(n, wm, wgk1_ref, wgk2_ref, bgk_ref)
    v = v.astype(BF16)

    row = lax.broadcasted_iota(I32, (tm, tm), 0)
    col = lax.broadcasted_iota(I32, (tm, tm), 1)
    same_chunk = lax.shift_right_logical(row, GLA_CHUNK_LOG2) == lax.shift_right_logical(col, GLA_CHUNK_LOG2)
    causal = same_chunk & (row >= col)
    tri = causal.astype(BF16)
    blk = same_chunk.astype(BF16)
    hi, mid, lo = _split3(log_a)
    b = _dot(tri, hi) + _dot(tri, mid) + _dot(tri, lo)
    b_end = _dot(blk, hi) + _dot(blk, mid) + _dot(blk, lo)
    qe = (q * jnp.exp(b)).astype(BF16)
    ke = (k * jnp.exp(-b)).astype(BF16)
    kd = (k * jnp.exp(b_end - b)).astype(BF16)
    dec = jnp.exp(b_end)
    glag = glag_ref[...]
    for h in range(GLA_HEADS):
        ks = slice(h * GLA_DK, (h + 1) * GLA_DK)
        vs = slice(h * GLA_DV, (h + 1) * GLA_DV)
        vh = v[:, vs]
        att = jnp.where(causal, _dot_nt(qe[:, ks], ke[:, ks]), 0.0).astype(BF16)
        o_intra = _dot(att, vh)
        s_t = s_scr[h]
        o_inter = []
        for c in range(tm // GLA_CHUNK):
            rs = slice(c * GLA_CHUNK, (c + 1) * GLA_CHUNK)
            o_inter.append(_dot_nt(qe[rs, ks], s_t.astype(BF16)))
            s_t = s_t * dec[c * GLA_CHUNK:c * GLA_CHUNK + 1, ks] + _dot_tn(vh[rs], kd[rs, ks])
        s_scr[h] = s_t
        oa_scr[:, vs] = _head_rms(o_intra + jnp.concatenate(o_inter, axis=0), glag)

    n = n_scr[...]
    oa_scr[...] = oa_scr[...] * _gla_out_gate(n, wm)
    ug, vn = _gmlp_inputs(n, wm, vng_ref, vnb_ref)
    vn = vn.astype(BF16)
    reps = tm // MIX_CHUNK
    mix_mask = (lax.shift_right_logical(row, MIX_CHUNK_LOG2) == lax.shift_right_logical(col, MIX_CHUNK_LOG2)) & (
        row >= col)
    for h in range(MIX_HEADS):
        w = jnp.where(mix_mask, jnp.tile(wsp_ref[h], (reps, reps)), 0.0).astype(BF16)
        cs = slice(h * MIX_DH, (h + 1) * MIX_DH)
        mix = _dot(w, vn[:, cs]) + jnp.tile(bspt_ref[:, h:h + 1], (reps, 1))
        oa_scr[:, cs] = oa_scr[:, cs] + ug[:, cs] * mix

    h_ref[...] = x + _dot(oa_scr[...].astype(BF16), wo_ref[...])


def _mixer_prompt(x, h_sample, wts):
    n_seq, seq_len, _ = x.shape
    n_s = h_sample.shape[0]
    tm = min(MIXER_TILE, seq_len)
    assert seq_len % tm == 0 and tm % MIX_CHUNK == 0 and n_s <= tm
    n_tiles = seq_len // tm
    consts = [wts['g1'], wts['w_a'], wts['w_b'], wts['w_gk1'], wts['w_gk2'], wts['b_gk'], wts['gla_g'], wts['vn_g'],
              wts['vn_b'], wts['w_sp'], wts['b_sp_t'], wts['w_o']]
    kern = functools.partial(_mixer_prompt_kernel, tm=tm, n_seq=n_seq, n_tiles=n_tiles)
    last = n_seq - 1

    def prompt_tile(b, t):
        return jnp.minimum(b, last), jnp.where(b < n_seq, t, n_tiles - 1)

    def x_map(b, t):
        bb, tt = prompt_tile(b, t)
        return bb, tt, 0

    def h_map(b, t):
        return jnp.where(b < n_seq, b * n_tiles + t, n_seq * n_tiles), 0

    return pl.pallas_call(
        kern,
        grid=(n_seq + 1, n_tiles),
        in_specs=[pl.BlockSpec((1, tm, D_MODEL), x_map), _const_spec(h_sample.shape)]
        + [_const_spec(c.shape) for c in consts],
        out_specs=[pl.BlockSpec((tm, D_MODEL), h_map),
                   pl.BlockSpec((1, GLA_HEADS, GLA_DK, GLA_DV), lambda b, t: (jnp.minimum(b, last), 0, 0, 0))],
        out_shape=[jax.ShapeDtypeStruct((n_seq * seq_len + n_s, D_MODEL), F32),
                   jax.ShapeDtypeStruct((n_seq, GLA_HEADS, GLA_DK, GLA_DV), F32)],
        scratch_shapes=[pltpu.VMEM((tm, D_MODEL), BF16),
                        pltpu.VMEM((tm, D_MODEL), F32),
                        pltpu.VMEM((GLA_HEADS, GLA_DV, GLA_DK), F32)],
        compiler_params=_params(("arbitrary", "arbitrary")),
        name="mixer_prompt",
    )(x, h_sample, *consts)


def _mixer_sample_in_kernel(x_ref, g1_ref, wa_ref, wb_ref, wgk1_ref, wgk2_ref, bgk_ref, vng_ref, vnb_ref,
                            mixw_ref, mixb_ref,
                            q_ref, k_ref, a_ref, v_ref, og_ref, ob_ref, vn_ref):
    wm = functools.partial(_main_cols, wa_ref, wb_ref)
    n = _rms(x_ref[...], g1_ref[...]).astype(BF16)
    q, k, v, log_a = _project_gla_inputs(n, wm, wgk1_ref, wgk2_ref, bgk_ref)
    q_ref[...] = q
    k_ref[...] = k
    a_ref[...] = jnp.exp(log_a)
    v_ref[...] = v
    og_ref[...] = _gla_out_gate(n, wm)
    ug, vn = _gmlp_inputs(n, wm, vng_ref, vnb_ref)
    vn_ref[...] = vn
    ob_ref[...] = ug * (mixw_ref[...] * vn + mixb_ref[...])


def _mixer_sample_in(x2, wts):
    n_seq = x2.shape[0]
    consts = [wts['g1'], wts['w_a'], wts['w_b'], wts['w_gk1'], wts['w_gk2'], wts['b_gk'], wts['vn_g'], wts['vn_b'],
              wts['mix_w0'], wts['mix_b0']]
    widths = [GLA_DK_T, GLA_DK_T, GLA_DK_T, GLA_DV_T, D_MODEL, D_MODEL, D_GMLP]
    return pl.pallas_call(
        _mixer_sample_in_kernel,
        grid=(1,),
        in_specs=[_const_spec(x2.shape)] + [_const_spec(c.shape) for c in consts],
        out_specs=[pl.BlockSpec((n_seq, w), lambda i: (0, 0)) for w in widths],
        out_shape=[jax.ShapeDtypeStruct((n_seq, w), F32) for w in widths],
        compiler_params=_params(("arbitrary",)),
        name="mixer_sample_in",
    )(x2, *consts)


def _state_update_kernel(s_ref, q_ref, k_ref, a_ref, v_ref, so_ref, o_ref):
    for j in range(STATE_TOKENS):
        for h in range(GLA_HEADS):
            ks = slice(h * GLA_DK, (h + 1) * GLA_DK)
            vs = slice(h * GLA_DV, (h + 1) * GLA_DV)
            s_new = a_ref[0, ks, j:j + 1] * s_ref[j, h] + k_ref[0, ks, j:j + 1] * v_ref[j:j + 1, vs]
            so_ref[j, h] = s_new
            o_ref[j:j + 1, vs] = jnp.sum(q_ref[0, ks, j:j + 1] * s_new, axis=0, keepdims=True)


def _state_update(state, q, k, a, v):
    n_seq = state.shape[0]
    tb = STATE_TOKENS
    assert n_seq % tb == 0

    def cols(z):
        return z.reshape(n_seq // tb, tb, GLA_DK_T).transpose(0, 2, 1)

    col_spec = pl.BlockSpec((1, GLA_DK_T, tb), lambda i: (i, 0, 0))
    st_spec = pl.BlockSpec((tb, GLA_HEADS, GLA_DK, GLA_DV), lambda i: (i, 0, 0, 0))
    row_spec = pl.BlockSpec((tb, GLA_DV_T), lambda i: (i, 0))
    return pl.pallas_call(
        _state_update_kernel,
        grid=(n_seq // tb,),
        in_specs=[st_spec, col_spec, col_spec, col_spec, row_spec],
        out_specs=[st_spec, row_spec],
        out_shape=[jax.ShapeDtypeStruct(state.shape, F32), jax.ShapeDtypeStruct((n_seq, GLA_DV_T), F32)],
        compiler_params=_params(("arbitrary",)),
        name="state_update",
    )(state, cols(q), cols(k), cols(a), v)


def _mixer_sample_out_kernel(x_ref, o_ref, og_ref, ob_ref, glag_ref, wo_ref, h_ref):
    glag = glag_ref[...]
    parts = []
    for h in range(GLA_HEADS):
        vs = slice(h * GLA_DV, (h + 1) * GLA_DV)
        parts.append(_head_rms(o_ref[:, vs], glag))
    merged = jnp.concatenate(parts, axis=1) * og_ref[...] + ob_ref[...]
    h_ref[...] = x_ref[...] + _dot(merged.astype(BF16), wo_ref[...])


def _mixer_sample_out(x2, o, og, ob, wts):
    n_seq = x2.shape[0]
    consts = [x2, o, og, ob, wts['gla_g'], wts['w_o']]
    return pl.pallas_call(
        _mixer_sample_out_kernel,
        grid=(1,),
        in_specs=[_const_spec(c.shape) for c in consts],
        out_specs=pl.BlockSpec((n_seq, D_MODEL), lambda i: (0, 0)),
        out_shape=jax.ShapeDtypeStruct((n_seq, D_MODEL), F32),
        compiler_params=_params(("arbitrary",)),
        name="mixer_sample_out",
    )(*consts)


def _router_kernel(h_ref, g2_ref, wr_ref, br_ref, xp_ref, tw_ref, lpc_ref, lpr_ref, cb_ref, tc_ref, lo_ref, cnt_ref, cnt_scr,
                   *, tm):
    i = pl.program_id(0)

    @pl.when(i == 0)
    def _():
        cnt_scr[...] = jnp.zeros_like(cnt_scr)

    cb_ref[0] = cnt_scr[...].astype(I32)

    hn = _rms(h_ref[...], g2_ref[...])
    hn_hi = hn.astype(BF16)
    hn_lo = (hn - hn_hi.astype(F32)).astype(BF16)
    w = wr_ref[...]
    w_hi = w.astype(BF16)
    w_lo = (w - w_hi.astype(F32)).astype(BF16)
    logits = _dot(hn_hi, w_hi) + _dot(hn_lo, w_hi) + _dot(hn_hi, w_lo) + br_ref[...]

    _rows_to_tiles(xp_ref, hn)

    lane = lax.broadcasted_iota(I32, (tm, LANES), 1)
    neg = jnp.float32(-jnp.inf)
    l = jnp.where(lane < N_EXPERTS, logits, neg)
    vals, hots = [], []
    for _ in range(TOP_K):
        m = jnp.max(l, axis=-1, keepdims=True)
        idx = jnp.min(jnp.where(l == m, lane, LANES), axis=-1, keepdims=True)
        hot = lane == idx
        l = jnp.where(hot, neg, l)
        vals.append(m)
        hots.append(hot)
    exps = [jnp.exp(v - vals[0]) for v in vals]
    denom = exps[0] + exps[1] + exps[2] + exps[3]

    member = (hots[0] | hots[1] | hots[2] | hots[3]).astype(BF16)
    row = lax.broadcasted_iota(I32, (tm, tm), 0)
    col = lax.broadcasted_iota(I32, (tm, tm), 1)
    in_tile = _dot((row > col).astype(BF16), member)
    tile_cnt = jnp.sum(member.astype(F32), axis=0, keepdims=True)
    e_row = lax.broadcasted_iota(I32, (LANES, LANES), 0)
    e_col = lax.broadcasted_iota(I32, (LANES, LANES), 1)
    lower = (e_row < e_col).astype(BF16)
    c_hi, c_mid, c_lo = _split3(jnp.broadcast_to(tile_cnt, (ROW_SUBLANES, LANES)))
    local_off = (_dot(c_hi, lower) + _dot(c_mid, lower) + _dot(c_lo, lower))[0:1, :]
    tc_ref[0] = tile_cnt.astype(I32)
    lo_ref[0] = local_off.astype(I32)
    cnt_scr[...] = cnt_scr[...] + tile_cnt
    cnt_ref[...] = cnt_scr[...].astype(I32)

    place = local_off + in_tile
    tw = jnp.zeros((tm, LANES), F32)
    lp = jnp.zeros((tm, LANES), F32)
    for kk in range(TOP_K):
        sel = lane == kk
        tw = jnp.where(sel, exps[kk] / denom, tw)
        lp = jnp.where(sel, jnp.sum(jnp.where(hots[kk], place, 0.0), axis=-1, keepdims=True), lp)
    tw_ref[...] = tw
    lpc_ref[...] = lp.astype(I32)
    lpr_ref[0] = lp.T[0:ROW_SUBLANES, :].astype(I32)


def _router_tile(t_all):
    for tm in (512, 384, 256, 128):
        if t_all % tm == 0:
            return tm
    raise ValueError(f"token count {t_all} must be a multiple of 128")


def _router(hbuf, wts):
    t_all = hbuf.shape[0]
    tm = _router_tile(t_all)
    consts = [wts['g2'], wts['w_router'], wts['b_router']]
    row = lambda w: pl.BlockSpec((tm, w), lambda i: (i, 0))
    n_tiles = t_all // tm
    per_tile = pl.BlockSpec((1, 1, LANES), lambda i: (i, 0, 0))
    tile_vec = jax.ShapeDtypeStruct((n_tiles, 1, LANES), I32)
    return pl.pallas_call(
        functools.partial(_router_kernel, tm=tm),
        grid=(t_all // tm,),
        in_specs=[row(D_MODEL)] + [_const_spec(c.shape) for c in consts],
        out_specs=[pl.BlockSpec((tm * ROW_SUBLANES, LANES), lambda i: (i, 0)), row(LANES), row(LANES),
                   pl.BlockSpec((1, ROW_SUBLANES, tm), lambda i: (i, 0, 0)),
                   per_tile, per_tile, per_tile, pl.BlockSpec((1, LANES), lambda i: (0, 0))],
        out_shape=[jax.ShapeDtypeStruct((t_all * ROW_SUBLANES, LANES), F32),
                   jax.ShapeDtypeStruct((t_all, LANES), F32),
                   jax.ShapeDtypeStruct((t_all, LANES), I32),
                   jax.ShapeDtypeStruct((n_tiles, ROW_SUBLANES, tm), I32),
                   tile_vec, tile_vec, tile_vec, jax.ShapeDtypeStruct((1, LANES), I32)],
        scratch_shapes=[pltpu.VMEM((1, LANES), F32)],
        compiler_params=_params(("arbitrary",)),
        name="router",
    )(hbuf, *consts)


def _rows_from_tiles(ref, n_rows):
    return jnp.concatenate([ref[pl.ds(s, n_rows, stride=ROW_SUBLANES), :] for s in range(ROW_SUBLANES)], axis=1)


def _rows_to_tiles(ref, x):
    for s in range(ROW_SUBLANES):
        ref[pl.ds(s, x.shape[0], stride=ROW_SUBLANES), :] = x[:, s * LANES:(s + 1) * LANES]


def _tiles(row, n_rows):
    return pl.ds(pl.multiple_of(row * ROW_SUBLANES, ROW_SUBLANES), n_rows * ROW_SUBLANES)


class _Layout:
    def __init__(self, t_all):
        bm = EXPERT_BLOCK
        self.t_all = t_all
        self.tile = _router_tile(t_all)
        self.n_tiles = t_all // self.tile
        self.n_blocks = -(-(t_all * TOP_K + N_EXPERTS * (bm - 1)) // bm)
        self.sorted_rows = self.n_blocks * bm


def _strip_copies(src_ref, src_row, dst_ref, dst_row, n, sem):
    def copy(offset, size):
        pltpu.make_async_copy(src_ref.at[_tiles(src_row + offset, size)], dst_ref.at[_tiles(dst_row + offset, size)],
                              sem).start()

    n_chunks = lax.shift_right_logical(n, STRIP_CHUNK_LOG2)

    def chunk(j, c):
        copy(j * STRIP_CHUNK, STRIP_CHUNK)
        return c
    lax.fori_loop(0, n_chunks, chunk, 0)

    done = n_chunks * STRIP_CHUNK
    size = STRIP_CHUNK // 2
    while size >= 1:
        part = n & size

        @pl.when(part != 0)
        def _(done=done, size=size):
            copy(done, size)
        done = done + part
        size //= 2


def _wait_rows(hbm_ref, vmem_ref, n_rows, sem):
    pltpu.make_async_copy(hbm_ref.at[pl.ds(0, n_rows * ROW_SUBLANES)], vmem_ref.at[pl.ds(0, n_rows * ROW_SUBLANES)],
                          sem).wait()


def _dispatch_kernel(soff_ref, scnt_ref, sgs_ref, lo_ref, hi_ref, xp_ref, lpr_ref, xs_hbm, loc, zero_scr, sem,
                     *, lay):
    tm = lay.tile
    n_pairs = TOP_K * tm
    g = pl.program_id(0)
    slot = g % 2

    def zero_fill(lo, hi, wait):
        def copies(first, count, size):
            def body(j, c):
                cp = pltpu.make_async_copy(zero_scr.at[pl.ds(0, size * ROW_SUBLANES)],
                                           xs_hbm.at[_tiles(first + j * size, size)], sem.at[2])
                if wait:
                    cp.wait()
                else:
                    cp.start()
                return c
            lax.fori_loop(0, count, body, 0)
        n = hi - lo
        n_big = lax.shift_right_logical(n, ZERO_ROWS_LOG2)
        copies(lo, n_big, ZERO_ROWS)
        done = n_big * ZERO_ROWS
        n_mid = lax.shift_right_logical(n - done, STRIP_CHUNK_LOG2)
        copies(lo + done, n_mid, STRIP_CHUNK)
        done = done + n_mid * STRIP_CHUNK
        copies(lo + done, n - done, 1)

    @pl.when(g == 0)
    def _():
        zero_scr[...] = jnp.zeros_like(zero_scr)
        for wait in (False, True):
            def expert(e, c, wait=wait):
                zero_fill(lo_ref[e], hi_ref[e], wait)
                return c
            lax.fori_loop(0, N_EXPERTS, expert, 0)
            zero_fill(hi_ref[N_EXPERTS - 1], lay.sorted_rows, wait)

    @pl.when(g >= 2)
    def _():
        _wait_rows(xs_hbm, loc.at[slot], n_pairs, sem.at[slot])

    x = _rows_from_tiles(xp_ref, tm).astype(BF16)
    pos = lax.broadcasted_iota(I32, (n_pairs, tm), 0)
    lpr = lpr_ref[0]
    hit = pos == lpr[0:1, :]
    for k in range(1, TOP_K):
        hit = hit | (pos == lpr[k:k + 1, :])
    _rows_to_tiles(loc.at[slot], _dot(hit.astype(BF16), x))

    def strip(e, c):
        s = g * N_EXPERTS + e
        _strip_copies(loc.at[slot], soff_ref[s], xs_hbm, sgs_ref[s], scnt_ref[s], sem.at[slot])
        return c
    lax.fori_loop(0, N_EXPERTS, strip, 0)

    @pl.when(g == lay.n_tiles - 1)
    def _():
        _wait_rows(xs_hbm, loc.at[slot], n_pairs, sem.at[slot])
        if lay.n_tiles > 1:
            _wait_rows(xs_hbm, loc.at[1 - slot], n_pairs, sem.at[1 - slot])


def _dispatch(xp, lp_rows, soff, scnt, sgs, pad_lo, pad_hi, lay):
    tm = lay.tile
    grid_spec = pltpu.PrefetchScalarGridSpec(
        num_scalar_prefetch=5,
        grid=(lay.n_tiles,),
        in_specs=[pl.BlockSpec((tm * ROW_SUBLANES, LANES), lambda g, *_: (g, 0)),
                  pl.BlockSpec((1, ROW_SUBLANES, tm), lambda g, *_: (g, 0, 0))],
        out_specs=pl.BlockSpec(memory_space=pl.ANY),
        scratch_shapes=[pltpu.VMEM((2, TOP_K * tm * ROW_SUBLANES, LANES), F32),
                        pltpu.VMEM((ZERO_ROWS * ROW_SUBLANES, LANES), F32),
                        pltpu.SemaphoreType.DMA((3,))],
    )
    return pl.pallas_call(
        functools.partial(_dispatch_kernel, lay=lay),
        grid_spec=grid_spec,
        out_shape=jax.ShapeDtypeStruct((lay.sorted_rows * ROW_SUBLANES, LANES), F32),
        compiler_params=_params(("arbitrary",)),
        name="dispatch",
    )(soff, scnt, sgs, pad_lo, pad_hi, xp, lp_rows)


def _experts_kernel(be_ref, nv_ref, xs_ref, wg_ref, wu_ref, wd_ref, bg_ref, bu_ref, bd_ref, ys_ref, wbf):
    bm = EXPERT_BLOCK
    i = pl.program_id(0)
    n_valid = nv_ref[0]

    @pl.when(i < n_valid)
    def _():
        @pl.when((i == 0) | (be_ref[i] != be_ref[jnp.maximum(i - 1, 0)]))
        def _():
            for w, w_ref in enumerate((wg_ref, wu_ref, wd_ref)):
                for c in range(0, D_MODEL, WEIGHT_CAST_ROWS):
                    wbf[w, c:c + WEIGHT_CAST_ROWS, :] = w_ref[0, c:c + WEIGHT_CAST_ROWS, :].astype(BF16)

        part = bm // EXPERT_PARTS
        for j in range(EXPERT_PARTS):
            tiles = pl.ds(j * part * ROW_SUBLANES, part * ROW_SUBLANES)
            x = _rows_from_tiles(xs_ref.at[tiles], part).astype(BF16)
            g = jnp.minimum(_dot(x, wbf[0]) + bg_ref[0], SWIGLU_LIMIT)
            u = jnp.clip(_dot(x, wbf[1]) + bu_ref[0], -SWIGLU_LIMIT, SWIGLU_LIMIT)
            mid = ((u + 1.0) * (g * _sigmoid(SWIGLU_ALPHA * g))).astype(BF16)
            _rows_to_tiles(ys_ref.at[tiles], _dot(mid, wbf[2]) + bd_ref[0])

    @pl.when(i >= n_valid)
    def _():
        ys_ref[...] = jnp.zeros_like(ys_ref)


def _experts(block_e, n_valid, xs, wts, lay):
    bm = EXPERT_BLOCK
    tile_rows = bm * ROW_SUBLANES
    wspec = pl.BlockSpec((1, D_MODEL, D_FF), lambda i, be, nv: (be[i], 0, 0))
    bspec = pl.BlockSpec((1, 1, D_FF), lambda i, be, nv: (be[i], 0, 0))
    grid_spec = pltpu.PrefetchScalarGridSpec(
        num_scalar_prefetch=2,
        grid=(lay.n_blocks,),
        in_specs=[pl.BlockSpec((tile_rows, LANES), lambda i, be, nv: (jnp.minimum(i, nv[0] - 1), 0)),
                  wspec, wspec, wspec, bspec, bspec, bspec],
        out_specs=pl.BlockSpec((tile_rows, LANES), lambda i, be, nv: (i, 0)),
        scratch_shapes=[pltpu.VMEM((3, D_MODEL, D_FF), BF16)],
    )
    return pl.pallas_call(
        _experts_kernel,
        grid_spec=grid_spec,
        out_shape=jax.ShapeDtypeStruct((lay.sorted_rows * ROW_SUBLANES, LANES), F32),
        compiler_params=_params(("arbitrary",)),
        name="experts",
    )(block_e, n_valid, xs, wts['w_gate'], wts['w_up'], wts['w_down'], wts['b_gate'], wts['b_up'], wts['b_down'])


def _combine_kernel(soff_ref, scnt_ref, sgs_ref, h_ref, tw_ref, lpc_ref, pp_ref, ps_ref, ys_hbm, g3_ref, wpg_ref,
                    wpp_ref, gf_ref, yp_ref, ysm_ref, loc, sem, *, lay, n_sample):
    tm = lay.tile
    n_pairs = TOP_K * tm
    g = pl.program_id(0)
    slot = g % 2

    def fetch(tile, s):
        def strip(e, c):
            i = tile * N_EXPERTS + e
            _strip_copies(ys_hbm, sgs_ref[i], loc.at[s], soff_ref[i], scnt_ref[i], sem.at[s])
            return c
        lax.fori_loop(0, N_EXPERTS, strip, 0)

    @pl.when(g == 0)
    def _():
        fetch(0, 0)

    @pl.when(g + 1 < lay.n_tiles)
    def _():
        fetch(g + 1, 1 - slot)

    _wait_rows(ys_hbm, loc.at[slot], n_pairs, sem.at[slot])
    y_rows = _rows_from_tiles(loc.at[slot], n_pairs).astype(BF16)
    tw = tw_ref[...]
    lpc = lpc_ref[...]
    pos = lax.broadcasted_iota(I32, (tm, n_pairs), 1)
    weight = jnp.zeros((tm, n_pairs), F32)
    for k in range(TOP_K):
        weight = jnp.where(pos == lpc[:, k:k + 1], tw[:, k:k + 1], weight)
    y = _dot(weight.astype(BF16), y_rows)
    h2 = h_ref[...] + y
    gate = _sigmoid(_dot(_rms(h2, g3_ref[...]).astype(BF16), wpg_ref[...]))
    p = pp_ref[...]
    p = jnp.where(g == lay.n_tiles - 1, jnp.concatenate([p[:tm - n_sample], ps_ref[...]], axis=0), p)
    h3 = h2 + gate * _dot(p.astype(BF16), wpp_ref[...])
    out = _rms(h3, gf_ref[...])
    yp_ref[...] = out

    @pl.when(g == lay.n_tiles - 1)
    def _():
        ysm_ref[...] = out[tm - n_sample:, :]


def _combine(hbuf, tw, lp_cols, p_prompt, p_sample, ys, soff, scnt, sgs, wts, lay):
    tm = lay.tile
    n_prompt, n_sample = p_prompt.shape[0], p_sample.shape[0]
    assert 0 < n_sample <= tm and (lay.n_tiles - 1) * tm < n_prompt
    consts = [wts['g3'], wts['w_ple_gate'], wts['w_ple_proj'], wts['g_final']]
    row = lambda w: pl.BlockSpec((tm, w), lambda g, *_: (g, 0))
    grid_spec = pltpu.PrefetchScalarGridSpec(
        num_scalar_prefetch=3,
        grid=(lay.n_tiles,),
        in_specs=[row(D_MODEL), row(LANES), row(LANES), row(D_PLE),
                  pl.BlockSpec((n_sample, D_PLE), lambda g, *_: (0, 0)), pl.BlockSpec(memory_space=pl.ANY)]
        + [pl.BlockSpec(c.shape, lambda g, *_: (0, 0), pipeline_mode=pl.Buffered(1)) for c in consts],
        out_specs=[row(D_MODEL), pl.BlockSpec((n_sample, D_MODEL), lambda g, *_: (0, 0))],
        scratch_shapes=[pltpu.VMEM((2, TOP_K * tm * ROW_SUBLANES, LANES), F32), pltpu.SemaphoreType.DMA((2,))],
    )
    return pl.pallas_call(
        functools.partial(_combine_kernel, lay=lay, n_sample=n_sample),
        grid_spec=grid_spec,
        out_shape=[jax.ShapeDtypeStruct((n_prompt, D_MODEL), F32), jax.ShapeDtypeStruct((n_sample, D_MODEL), F32)],
        compiler_params=_params(("arbitrary",)),
        name="combine",
    )(soff, scnt, sgs, hbuf, tw, lp_cols, p_prompt, p_sample, ys, *consts)


def _prepare_weights(norm1_g, w_in, w_gk2, b_gk, gla_norm_g, v_norm_g, v_norm_b, w_sp, b_sp, w_o, norm2_g,
                     w_router, b_router, w_gate, b_gate, w_up, b_up, w_down, b_down, norm3_g, w_ple_gate,
                     w_ple_proj, final_g):
    o_gk = 2 * GLA_DK_T + 2 * GLA_DV_T
    w_a = w_in[:, :o_gk].astype(BF16)
    w_b = w_in[:, o_gk + GK_RANK:].astype(BF16)
    w_gk1 = jnp.pad(w_in[:, o_gk:o_gk + GK_RANK], ((0, 0), (0, LANES - GK_RANK))).astype(BF16)
    w_gk2p = jnp.pad(w_gk2, ((0, LANES - GK_RANK), (0, 0))).astype(BF16)
    row = lambda z: z.reshape(1, -1).astype(F32)
    return dict(
        g1=row(norm1_g), w_a=w_a, w_b=w_b, w_gk1=w_gk1, w_gk2=w_gk2p, b_gk=row(b_gk), gla_g=row(gla_norm_g),
        vn_g=row(v_norm_g), vn_b=row(v_norm_b), w_sp=w_sp, b_sp_t=b_sp.T,
        mix_w0=row(jnp.repeat(w_sp[:, 0, 0], MIX_DH)), mix_b0=row(jnp.repeat(b_sp[:, 0], MIX_DH)),
        w_o=w_o.astype(BF16), g2=row(norm2_g),
        w_router=jnp.pad(w_router, ((0, 0), (0, LANES - N_EXPERTS))),
        b_router=jnp.pad(row(b_router), ((0, 0), (0, LANES - N_EXPERTS))),
        w_gate=w_gate, w_up=w_up, w_down=w_down,
        b_gate=b_gate.reshape(N_EXPERTS, 1, D_FF), b_up=b_up.reshape(N_EXPERTS, 1, D_FF),
        b_down=b_down.reshape(N_EXPERTS, 1, D_MODEL),
        g3=row(norm3_g), w_ple_gate=w_ple_gate.astype(BF16), w_ple_proj=w_ple_proj.astype(BF16),
        g_final=row(final_g))


def _routing_tables(counts, counts_before, tile_counts, local_off, lay):
    bm = EXPERT_BLOCK
    padded = (counts + bm - 1) // bm * bm
    pad_end = jnp.cumsum(padded)
    pad_start = pad_end - padded
    global_start = pad_start[None, :] + counts_before
    n_valid = (pad_end[-1] // bm).astype(I32)
    blk = jnp.minimum(jnp.arange(lay.n_blocks, dtype=I32), n_valid - 1)
    block_e = jnp.sum(pad_end[None, :] <= (blk * bm)[:, None], axis=1).astype(I32)
    flat = lambda z: z.reshape(-1).astype(I32)
    return dict(soff=flat(local_off), scnt=flat(tile_counts), sgs=flat(global_start),
                pad_lo=(pad_start + counts).astype(I32), pad_hi=pad_end.astype(I32), block_e=block_e,
                n_valid=n_valid.reshape(1))


def _layer(x_prompt, x_sample, state, p_prompt, p_sample, wts):
    n_p, len_p, _ = x_prompt.shape
    n_s, len_s, _ = x_sample.shape
    assert len_s == 1, "the sample group carries one new token per sequence"
    t_p = n_p * len_p
    t_all = t_p + n_s

    xs = x_sample.reshape(n_s, D_MODEL)
    q, k, a, v, og, ob, vn = _mixer_sample_in(xs, wts)
    st_sample, o = _state_update(state, q, k, a, v)
    h_sample = _mixer_sample_out(xs, o, og, ob, wts)
    hbuf, st_prompt = _mixer_prompt(x_prompt, h_sample, wts)

    lay = _Layout(t_all)
    xp, top_w, lp_cols, lp_rows, counts_before, tile_counts, local_off, counts = _router(hbuf, wts)
    ne = N_EXPERTS
    tab = _routing_tables(counts[0, :ne], counts_before[:, 0, :ne], tile_counts[:, 0, :ne], local_off[:, 0, :ne], lay)
    x_sorted = _dispatch(xp, lp_rows, tab['soff'], tab['scnt'], tab['sgs'], tab['pad_lo'], tab['pad_hi'], lay)
    y_sorted = _experts(tab['block_e'], tab['n_valid'], x_sorted, wts, lay)
    y_prompt, y_sample = _combine(hbuf, top_w, lp_cols, p_prompt.reshape(t_p, D_PLE), p_sample.reshape(n_s, D_PLE),
                                  y_sorted, tab['soff'], tab['scnt'], tab['sgs'], wts, lay)
    return (y_prompt.reshape(n_p, len_p, D_MODEL), y_sample.reshape(n_s, len_s, D_MODEL),
            st_prompt, st_sample, vn.reshape(n_s, len_s, D_GMLP))


def kernel(x_prompt, x_sample, state_gla, p_prompt, p_sample, norm1_g, w_in, w_gk2, b_gk, gla_norm_g, v_norm_g,
           v_norm_b, w_sp, b_sp, w_o, norm2_g, w_router, b_router, w_gate, b_gate, w_up, b_up, w_down, b_down,
           norm3_g, w_ple_gate, w_ple_proj, final_g):
    assert w_in.shape[0] == 1, "single-layer trunk"
    wts = _prepare_weights(norm1_g[0], w_in[0], w_gk2[0], b_gk[0], gla_norm_g[0], v_norm_g[0], v_norm_b[0],
                           w_sp[0], b_sp[0], w_o[0], norm2_g[0], w_router[0], b_router[0], w_gate[0], b_gate[0],
                           w_up[0], b_up[0], w_down[0], b_down[0], norm3_g[0], w_ple_gate[0], w_ple_proj[0],
                           final_g)
    y_p, y_s, st_p, st_s, vn_s = _layer(x_prompt, x_sample, state_gla[0], p_prompt[0], p_sample[0], wts)
    return (y_p, y_s, st_p[None], st_s[None].astype(state_gla.dtype), vn_s[None])
```

```python
import functools

import jax
import jax.numpy as jnp
from jax import lax
from jax.experimental import pallas as pl
from jax.experimental.pallas import tpu as pltpu

F32 = jnp.float32
BF16 = jnp.bfloat16
I32 = jnp.int32

D_MODEL = 1024
GLA_HEADS = 4
GLA_DK = 128
GLA_DV = 256
GLA_DK_T = GLA_HEADS * GLA_DK
GLA_DV_T = GLA_HEADS * GLA_DV
GK_RANK = 16
GATE_NORMALIZER = 16.0
GLA_CHUNK = 64
GLA_CHUNK_LOG2 = 6
D_GMLP = 1024
MIX_HEADS = 4
MIX_DH = D_GMLP // MIX_HEADS
MIX_CHUNK = 128
MIX_CHUNK_LOG2 = 7
assert 1 << GLA_CHUNK_LOG2 == GLA_CHUNK and 1 << MIX_CHUNK_LOG2 == MIX_CHUNK
N_EXPERTS = 32
TOP_K = 4
D_FF = 1024
SWIGLU_LIMIT = 7.0
SWIGLU_ALPHA = 1.702
D_PLE = 256
EPS = 1e-6

LANES = 128
ROW_SUBLANES = D_MODEL // LANES
VMEM_LIMIT_BYTES = 56 * 1024 * 1024

OFF_Q, OFF_K, OFF_V, OFF_R, OFF_U, OFF_VG, OFF_GA, OFF_GB, N_MAIN = (
    0, 512, 1024, 2048, 3072, 4096, 5120, 6144, 7168)

MIXER_TILE = 512
MIXER_PARTS = 2
STATE_TOKENS = 8
EXPERT_BLOCK = 768
EXPERT_PARTS = 3
WEIGHT_CAST_ROWS = 256
STRIP_CHUNK = 16
STRIP_CHUNK_LOG2 = 4
ZERO_ROWS = 64
ZERO_ROWS_LOG2 = 6
assert 1 << STRIP_CHUNK_LOG2 == STRIP_CHUNK and 1 << ZERO_ROWS_LOG2 == ZERO_ROWS


def _dot(a, b):
    return jnp.dot(a, b, preferred_element_type=F32)


def _dot_nt(a, b):
    return lax.dot_general(a, b, (((1,), (1,)), ((), ())), preferred_element_type=F32)


def _dot_tn(a, b):
    return lax.dot_general(a, b, (((0,), (0,)), ((), ())), preferred_element_type=F32)


def _rms(x, g):
    return x * lax.rsqrt(jnp.mean(x * x, axis=-1, keepdims=True) + EPS) * g


def _sigmoid(x):
    return 1.0 / (1.0 + jnp.exp(-x))


def _gelu(x):
    return 0.5 * x * (1.0 + lax.erf(x * (2.0 ** -0.5)))


def _log_sigmoid(x):
    return jnp.minimum(x, 0.0) - jnp.log1p(jnp.exp(-jnp.abs(x)))


def _split3(x):
    hi = x.astype(BF16)
    r1 = x - hi.astype(F32)
    mid = r1.astype(BF16)
    lo = (r1 - mid.astype(F32)).astype(BF16)
    return hi, mid, lo


def _const_spec(shape):
    nd = len(shape)
    return pl.BlockSpec(shape, lambda *_: (0,) * nd, pipeline_mode=pl.Buffered(1))


def _params(sem):
    return pltpu.CompilerParams(dimension_semantics=sem, vmem_limit_bytes=VMEM_LIMIT_BYTES)


def _main_cols(wa_ref, wb_ref, lo, hi):
    if hi <= OFF_U:
        return wa_ref[:, lo:hi]
    return wb_ref[:, lo - OFF_U:hi - OFF_U]


def _project_gla_inputs(n, wm, wgk1_ref, wgk2_ref, bgk_ref):
    q = _dot(n, wm(OFF_Q, OFF_K)) * (GLA_DK ** -0.5)
    k = _dot(n, wm(OFF_K, OFF_V))
    v = _dot(n, wm(OFF_V, OFF_R))
    gk = _dot(n, wgk1_ref[...]).astype(BF16)
    log_a = _log_sigmoid(_dot(gk, wgk2_ref[...]) + bgk_ref[...]) * (1.0 / GATE_NORMALIZER)
    return q, k, v, log_a


def _gmlp_inputs(n, wm, vng_ref, vnb_ref):
    u = _gelu(_dot(n, wm(OFF_U, OFF_VG)))
    vg = _gelu(_dot(n, wm(OFF_VG, OFF_GA)))
    mu = jnp.mean(vg, axis=-1, keepdims=True)
    vc = vg - mu
    var = jnp.mean(vc * vc, axis=-1, keepdims=True)
    vn = vc * lax.rsqrt(var + EPS) * vng_ref[...] + vnb_ref[...]
    ug = u * _sigmoid(_dot(n, wm(OFF_GB, N_MAIN)))
    return ug, vn


def _gla_out_gate(n, wm):
    r = _dot(n, wm(OFF_R, OFF_U))
    ga = _dot(n, wm(OFF_GA, OFF_GB))
    return r * _sigmoid(r) * _sigmoid(ga)


def _head_rms(o, g):
    return o * lax.rsqrt(jnp.mean(o * o, axis=-1, keepdims=True) + EPS) * g


def _mixer_prompt_kernel(x_ref, hs_ref, *refs, tm, n_seq, n_tiles):
    b = pl.program_id(0)
    t = pl.program_id(1)
    h_ref = refs[12]

    @pl.when(b < n_seq)
    def _():
        _mixer_prompt_tile(x_ref, *refs, tm=tm, n_tiles=n_tiles)

    @pl.when((b == n_seq) & (t == 0))
    def _():
        h_ref[0:hs_ref.shape[0], :] = hs_ref[...]


def _mixer_prompt_tile(x_ref, g1_ref, wa_ref, wb_ref, wgk1_ref, wgk2_ref, bgk_ref, glag_ref, vng_ref, vnb_ref,
                       wsp_ref, bspt_ref, wo_ref,
                       h_ref, st_ref,
                       n_scr, oa_scr, s_scr, *, tm, n_tiles):
    t = pl.program_id(1)

    @pl.when(t == 0)
    def _():
        s_scr[...] = jnp.zeros_like(s_scr)

    wm = functools.partial(_main_cols, wa_ref, wb_ref)
    pm = tm // MIXER_PARTS
    for part in range(MIXER_PARTS):
        rows = pl.ds(part * pm, pm)
        _mixer_prompt_part(x_ref.at[0, rows], g1_ref, wm, wgk1_ref, wgk2_ref, bgk_ref, glag_ref, vng_ref, vnb_ref,
                           wsp_ref, bspt_ref, wo_ref, h_ref.at[rows], n_scr.at[rows], oa_scr.at[rows], s_scr, tm=pm)

    @pl.when(t == n_tiles - 1)
    def _():
        for h in range(GLA_HEADS):
            st_ref[0, h] = s_scr[h].T


def _mixer_prompt_part(x_ref, g1_ref, wm, wgk1_ref, wgk2_ref, bgk_ref, glag_ref, vng_ref, vnb_ref,
                       wsp_ref, bspt_ref, wo_ref, h_ref, n_scr, oa_scr, s_scr, *, tm):
    x = x_ref[...]
    n = _rms(x, g1_ref[...]).astype(BF16)
    n_scr[...] = n
    q, k, v, log_a = _project_gla_inputs(n, wm, wgk1_ref, wgk2_ref, bgk_ref)
    v = v.astype(BF16)

    row = lax.broadcasted_iota(I32, (tm, tm), 0)
    col = lax.broadcasted_iota(I32, (tm, tm), 1)
    same_chunk = lax.shift_right_logical(row, GLA_CHUNK_LOG2) == lax.shift_right_logical(col, GLA_CHUNK_LOG2)
    causal = same_chunk & (row >= col)
    tri = causal.astype(BF16)
    blk = same_chunk.astype(BF16)
    hi, mid, lo = _split3(log_a)
    b = _dot(tri, hi) + _dot(tri, mid) + _dot(tri, lo)
    b_end = _dot(blk, hi) + _dot(blk, mid) + _dot(blk, lo)
    qe = (q * jnp.exp(b)).astype(BF16)
    ke = (k * jnp.exp(-b)).astype(BF16)
    kd = (k * jnp.exp(b_end - b)).astype(BF16)
    dec = jnp.exp(b_end)
    glag = glag_ref[...]
    for h in range(GLA_HEADS):
        ks = slice(h * GLA_DK, (h + 1) * GLA_DK)
        vs = slice(h * GLA_DV, (h + 1) * GLA_DV)
        vh = v[:, vs]
        att = jnp.where(causal, _dot_nt(qe[:, ks], ke[:, ks]), 0.0).astype(BF16)
        o_intra = _dot(att, vh)
        s_t = s_scr[h]
        o_inter = []
        for c in range(tm // GLA_CHUNK):
            rs = slice(c * GLA_CHUNK, (c + 1) * GLA_CHUNK)
            o_inter.append(_dot_nt(qe[rs, ks], s_t.astype(BF16)))
            s_t = s_t * dec[c * GLA_CHUNK:c * GLA_CHUNK + 1, ks] + _dot_tn(vh[rs], kd[rs, ks])
        s_scr[h] = s_t
        oa_scr[:, vs] = _head_rms(o_intra + jnp.concatenate(o_inter, axis=0), glag)

    n = n_scr[...]
    oa_scr[...] = oa_scr[...] * _gla_out_gate(n, wm)
    ug, vn = _gmlp_inputs(n, wm, vng_ref, vnb_ref)
    vn = vn.astype(BF16)
    reps = tm // MIX_CHUNK
    mix_mask = (lax.shift_right_logical(row, MIX_CHUNK_LOG2) == lax.shift_right_logical(col, MIX_CHUNK_LOG2)) & (
        row >= col)
    for h in range(MIX_HEADS):
        w = jnp.where(mix_mask, jnp.tile(wsp_ref[h], (reps, reps)), 0.0).astype(BF16)
        cs = slice(h * MIX_DH, (h + 1) * MIX_DH)
        mix = _dot(w, vn[:, cs]) + jnp.tile(bspt_ref[:, h:h + 1], (reps, 1))
        oa_scr[:, cs] = oa_scr[:, cs] + ug[:, cs] * mix

    h_ref[...] = x + _dot(oa_scr[...].astype(BF16), wo_ref[...])


def _mixer_prompt(x, h_sample, wts):
    n_seq, seq_len, _ = x.shape
    n_s = h_sample.shape[0]
    tm = min(MIXER_TILE, seq_len)
    assert seq_len % tm == 0 and tm % MIX_CHUNK == 0 and n_s <= tm
    n_tiles = seq_len // tm
    consts = [wts['g1'], wts['w_a'], wts['w_b'], wts['w_gk1'], wts['w_gk2'], wts['b_gk'], wts['gla_g'], wts['vn_g'],
              wts['vn_b'], wts['w_sp'], wts['b_sp_t'], wts['w_o']]
    kern = functools.partial(_mixer_prompt_kernel, tm=tm, n_seq=n_seq, n_tiles=n_tiles)
    last = n_seq - 1

    def prompt_tile(b, t):
        return jnp.minimum(b, last), jnp.where(b < n_seq, t, n_tiles - 1)

    def x_map(b, t):
        bb, tt = prompt_tile(b, t)
        return bb, tt, 0

    def h_map(b, t):
        return jnp.where(b < n_seq, b * n_tiles + t, n_seq * n_tiles), 0

    return pl.pallas_call(
        kern,
        grid=(n_seq + 1, n_tiles),
        in_specs=[pl.BlockSpec((1, tm, D_MODEL), x_map), _const_spec(h_sample.shape)]
        + [_const_spec(c.shape) for c in consts],
        out_specs=[pl.BlockSpec((tm, D_MODEL), h_map),
                   pl.BlockSpec((1, GLA_HEADS, GLA_DK, GLA_DV), lambda b, t: (jnp.minimum(b, last), 0, 0, 0))],
        out_shape=[jax.ShapeDtypeStruct((n_seq * seq_len + n_s, D_MODEL), F32),
                   jax.ShapeDtypeStruct((n_seq, GLA_HEADS, GLA_DK, GLA_DV), F32)],
        scratch_shapes=[pltpu.VMEM((tm, D_MODEL), BF16),
                        pltpu.VMEM((tm, D_MODEL), F32),
                        pltpu.VMEM((GLA_HEADS, GLA_DV, GLA_DK), F32)],
        compiler_params=_params(("arbitrary", "arbitrary")),
        name="mixer_prompt",
    )(x, h_sample, *consts)


def _mixer_sample_in_kernel(x_ref, g1_ref, wa_ref, wb_ref, wgk1_ref, wgk2_ref, bgk_ref, vng_ref, vnb_ref,
                            mixw_ref, mixb_ref,
                            q_ref, k_ref, a_ref, v_ref, og_ref, ob_ref, vn_ref):
    wm = functools.partial(_main_cols, wa_ref, wb_ref)
    n = _rms(x_ref[...], g1_ref[...]).astype(BF16)
    q, k, v, log_a = _project_gla_inputs(n, wm, wgk1_ref, wgk2_ref, bgk_ref)
    q_ref[...] = q
    k_ref[...] = k
    a_ref[...] = jnp.exp(log_a)
    v_ref[...] = v
    og_ref[...] = _gla_out_gate(n, wm)
    ug, vn = _gmlp_inputs(n, wm, vng_ref, vnb_ref)
    vn_ref[...] = vn
    ob_ref[...] = ug * (mixw_ref[...] * vn + mixb_ref[...])


def _mixer_sample_in(x2, wts):
    n_seq = x2.shape[0]
    consts = [wts['g1'], wts['w_a'], wts['w_b'], wts['w_gk1'], wts['w_gk2'], wts['b_gk'], wts['vn_g'], wts['vn_b'],
              wts['mix_w0'], wts['mix_b0']]
    widths = [GLA_DK_T, GLA_DK_T, GLA_DK_T, GLA_DV_T, D_MODEL, D_MODEL, D_GMLP]
    return pl.pallas_call(
        _mixer_sample_in_kernel,
        grid=(1,),
        in_specs=[_const_spec(x2.shape)] + [_const_spec(c.shape) for c in consts],
        out_specs=[pl.BlockSpec((n_seq, w), lambda i: (0, 0)) for w in widths],
        out_shape=[jax.ShapeDtypeStruct((n_seq, w), F32) for w in widths],
        compiler_params=_params(("arbitrary",)),
        name="mixer_sample_in",
    )(x2, *consts)


def _state_update_kernel(s_ref, q_ref, k_ref, a_ref, v_ref, so_ref, o_ref):
    for j in range(STATE_TOKENS):
        for h in range(GLA_HEADS):
            ks = slice(h * GLA_DK, (h + 1) * GLA_DK)
            vs = slice(h * GLA_DV, (h + 1) * GLA_DV)
            s_new = a_ref[0, ks, j:j + 1] * s_ref[j, h] + k_ref[0, ks, j:j + 1] * v_ref[j:j + 1, vs]
            so_ref[j, h] = s_new
            o_ref[j:j + 1, vs] = jnp.sum(q_ref[0, ks, j:j + 1] * s_new, axis=0, keepdims=True)


def _state_update(state, q, k, a, v):
    n_seq = state.shape[0]
    tb = STATE_TOKENS
    assert n_seq % tb == 0

    def cols(z):
        return z.reshape(n_seq // tb, tb, GLA_DK_T).transpose(0, 2, 1)

    col_spec = pl.BlockSpec((1, GLA_DK_T, tb), lambda i: (i, 0, 0))
    st_spec = pl.BlockSpec((tb, GLA_HEADS, GLA_DK, GLA_DV), lambda i: (i, 0, 0, 0))
    row_spec = pl.BlockSpec((tb, GLA_DV_T), lambda i: (i, 0))
    return pl.pallas_call(
        _state_update_kernel,
        grid=(n_seq // tb,),
        in_specs=[st_spec, col_spec, col_spec, col_spec, row_spec],
        out_specs=[st_spec, row_spec],
        out_shape=[jax.ShapeDtypeStruct(state.shape, F32), jax.ShapeDtypeStruct((n_seq, GLA_DV_T), F32)],
        compiler_params=_params(("arbitrary",)),
        name="state_update",
    )(state, cols(q), cols(k), cols(a), v)


def _mixer_sample_out_kernel(x_ref, o_ref, og_ref, ob_ref, glag_ref, wo_ref, h_ref):
    glag = glag_ref[...]
    parts = []
    for h in range(GLA_HEADS):
        vs = slice(h * GLA_DV, (h + 1) * GLA_DV)
        parts.append(_head_rms(o_ref[:, vs], glag))
    merged = jnp.concatenate(parts, axis=1) * og_ref[...] + ob_ref[...]
    h_ref[...] = x_ref[...] + _dot(merged.astype(BF16), wo_ref[...])


def _mixer_sample_out(x2, o, og, ob, wts):
    n_seq = x2.shape[0]
    consts = [x2, o, og, ob, wts['gla_g'], wts['w_o']]
    return pl.pallas_call(
        _mixer_sample_out_kernel,
        grid=(1,),
        in_specs=[_const_spec(c.shape) for c in consts],
        out_specs=pl.BlockSpec((n_seq, D_MODEL), lambda i: (0, 0)),
        out_shape=jax.ShapeDtypeStruct((n_seq, D_MODEL), F32),
        compiler_params=_params(("arbitrary",)),
        name="mixer_sample_out",
    )(*consts)


def _router_kernel(h_ref, g2_ref, wr_ref, br_ref, xp_ref, tw_ref, lpc_ref, lpr_ref, cb_ref, tc_ref, lo_ref, cnt_ref, cnt_scr,
                   *, tm):
    i = pl.program_id(0)

    @pl.when(i == 0)
    def _():
        cnt_scr[...] = jnp.zeros_like(cnt_scr)

    cb_ref[0] = cnt_scr[...].astype(I32)

    hn = _rms(h_ref[...], g2_ref[...])
    hn_hi = hn.astype(BF16)
    hn_lo = (hn - hn_hi.astype(F32)).astype(BF16)
    w = wr_ref[...]
    w_hi = w.astype(BF16)
    w_lo = (w - w_hi.astype(F32)).astype(BF16)
    logits = _dot(hn_hi, w_hi) + _dot(hn_lo, w_hi) + _dot(hn_hi, w_lo) + br_ref[...]

    _rows_to_tiles(xp_ref, hn)

    lane = lax.broadcasted_iota(I32, (tm, LANES), 1)
    neg = jnp.float32(-jnp.inf)
    l = jnp.where(lane < N_EXPERTS, logits, neg)
    vals, hots = [], []
    for _ in range(TOP_K):
        m = jnp.max(l, axis=-1, keepdims=True)
        idx = jnp.min(jnp.where(l == m, lane, LANES), axis=-1, keepdims=True)
        hot = lane == idx
        l = jnp.where(hot, neg, l)
        vals.append(m)
        hots.append(hot)
    exps = [jnp.exp(v - vals[0]) for v in vals]
    denom = exps[0] + exps[1] + exps[2] + exps[3]

    member = (hots[0] | hots[1] | hots[2] | hots[3]).astype(BF16)
    row = lax.broadcasted_iota(I32, (tm, tm), 0)
    col = lax.broadcasted_iota(I32, (tm, tm), 1)
    in_tile = _dot((row > col).astype(BF16), member)
    tile_cnt = jnp.sum(member.astype(F32), axis=0, keepdims=True)
    e_row = lax.broadcasted_iota(I32, (LANES, LANES), 0)
    e_col = lax.broadcasted_iota(I32, (LANES, LANES), 1)
    lower = (e_row < e_col).astype(BF16)
    c_hi, c_mid, c_lo = _split3(jnp.broadcast_to(tile_cnt, (ROW_SUBLANES, LANES)))
    local_off = (_dot(c_hi, lower) + _dot(c_mid, lower) + _dot(c_lo, lower))[0:1, :]
    tc_ref[0] = tile_cnt.astype(I32)
    lo_ref[0] = local_off.astype(I32)
    cnt_scr[...] = cnt_scr[...] + tile_cnt
    cnt_ref[...] = cnt_scr[...].astype(I32)

    place = local_off + in_tile
    tw = jnp.zeros((tm, LANES), F32)
    lp = jnp.zeros((tm, LANES), F32)
    for kk in range(TOP_K):
        sel = lane == kk
        tw = jnp.where(sel, exps[kk] / denom, tw)
        lp = jnp.where(sel, jnp.sum(jnp.where(hots[kk], place, 0.0), axis=-1, keepdims=True), lp)
    tw_ref[...] = tw
    lpc_ref[...] = lp.astype(I32)
    lpr_ref[0] = lp.T[0:ROW_SUBLANES, :].astype(I32)


def _router_tile(t_all):
    for tm in (512, 384, 256, 128):
        if t_all % tm == 0:
            return tm
    raise ValueError(f"token count {t_all} must be a multiple of 128")


def _router(hbuf, wts):
    t_all = hbuf.shape[0]
    tm = _router_tile(t_all)
    consts = [wts['g2'], wts['w_router'], wts['b_router']]
    row = lambda w: pl.BlockSpec((tm, w), lambda i: (i, 0))
    n_tiles = t_all // tm
    per_tile = pl.BlockSpec((1, 1, LANES), lambda i: (i, 0, 0))
    tile_vec = jax.ShapeDtypeStruct((n_tiles, 1, LANES), I32)
    return pl.pallas_call(
        functools.partial(_router_kernel, tm=tm),
        grid=(t_all // tm,),
        in_specs=[row(D_MODEL)] + [_const_spec(c.shape) for c in consts],
        out_specs=[pl.BlockSpec((tm * ROW_SUBLANES, LANES), lambda i: (i, 0)), row(LANES), row(LANES),
                   pl.BlockSpec((1, ROW_SUBLANES, tm), lambda i: (i, 0, 0)),
                   per_tile, per_tile, per_tile, pl.BlockSpec((1, LANES), lambda i: (0, 0))],
        out_shape=[jax.ShapeDtypeStruct((t_all * ROW_SUBLANES, LANES), F32),
                   jax.ShapeDtypeStruct((t_all, LANES), F32),
                   jax.ShapeDtypeStruct((t_all, LANES), I32),
                   jax.ShapeDtypeStruct((n_tiles, ROW_SUBLANES, tm), I32),
                   tile_vec, tile_vec, tile_vec, jax.ShapeDtypeStruct((1, LANES), I32)],
        scratch_shapes=[pltpu.VMEM((1, LANES), F32)],
        compiler_params=_params(("arbitrary",)),
        name="router",
    )(hbuf, *consts)


def _rows_from_tiles(ref, n_rows):
    return jnp.concatenate([ref[pl.ds(s, n_rows, stride=ROW_SUBLANES), :] for s in range(ROW_SUBLANES)], axis=1)


def _rows_to_tiles(ref, x):
    for s in range(ROW_SUBLANES):
        ref[pl.ds(s, x.shape[0], stride=ROW_SUBLANES), :] = x[:, s * LANES:(s + 1) * LANES]


def _tiles(row, n_rows):
    return pl.ds(pl.multiple_of(row * ROW_SUBLANES, ROW_SUBLANES), n_rows * ROW_SUBLANES)


class _Layout:
    def __init__(self, t_all):
        bm = EXPERT_BLOCK
        self.t_all = t_all
        self.tile = _router_tile(t_all)
        self.n_tiles = t_all // self.tile
        self.n_blocks = -(-(t_all * TOP_K + N_EXPERTS * (bm - 1)) // bm)
        self.sorted_rows = self.n_blocks * bm


def _strip_copies(src_ref, src_row, dst_ref, dst_row, n, sem):
    def copy(offset, size):
        pltpu.make_async_copy(src_ref.at[_tiles(src_row + offset, size)], dst_ref.at[_tiles(dst_row + offset, size)],
                              sem).start()

    n_chunks = lax.shift_right_logical(n, STRIP_CHUNK_LOG2)

    def chunk(j, c):
        copy(j * STRIP_CHUNK, STRIP_CHUNK)
        return c
    lax.fori_loop(0, n_chunks, chunk, 0)

    done = n_chunks * STRIP_CHUNK
    size = STRIP_CHUNK // 2
    while size >= 1:
        part = n & size

        @pl.when(part != 0)
        def _(done=done, size=size):
            copy(done, size)
        done = done + part
        size //= 2


def _wait_rows(hbm_ref, vmem_ref, n_rows, sem):
    pltpu.make_async_copy(hbm_ref.at[pl.ds(0, n_rows * ROW_SUBLANES)], vmem_ref.at[pl.ds(0, n_rows * ROW_SUBLANES)],
                          sem).wait()


def _dispatch_kernel(soff_ref, scnt_ref, sgs_ref, lo_ref, hi_ref, xp_ref, lpr_ref, xs_hbm, loc, zero_scr, sem,
                     *, lay):
    tm = lay.tile
    n_pairs = TOP_K * tm
    g = pl.program_id(0)
    slot = g % 2

    def zero_fill(lo, hi, wait):
        def copies(first, count, size):
            def body(j, c):
                cp = pltpu.make_async_copy(zero_scr.at[pl.ds(0, size * ROW_SUBLANES)],
                                           xs_hbm.at[_tiles(first + j * size, size)], sem.at[2])
                if wait:
                    cp.wait()
                else:
                    cp.start()
                return c
            lax.fori_loop(0, count, body, 0)
        n = hi - lo
        n_big = lax.shift_right_logical(n, ZERO_ROWS_LOG2)
        copies(lo, n_big, ZERO_ROWS)
        done = n_big * ZERO_ROWS
        n_mid = lax.shift_right_logical(n - done, STRIP_CHUNK_LOG2)
        copies(lo + done, n_mid, STRIP_CHUNK)
        done = done + n_mid * STRIP_CHUNK
        copies(lo + done, n - done, 1)

    @pl.when(g == 0)
    def _():
        zero_scr[...] = jnp.zeros_like(zero_scr)
        for wait in (False, True):
            def expert(e, c, wait=wait):
                zero_fill(lo_ref[e], hi_ref[e], wait)
                return c
            lax.fori_loop(0, N_EXPERTS, expert, 0)
            zero_fill(hi_ref[N_EXPERTS - 1], lay.sorted_rows, wait)

    @pl.when(g >= 2)
    def _():
        _wait_rows(xs_hbm, loc.at[slot], n_pairs, sem.at[slot])

    x = _rows_from_tiles(xp_ref, tm).astype(BF16)
    pos = lax.broadcasted_iota(I32, (n_pairs, tm), 0)
    lpr = lpr_ref[0]
    hit = pos == lpr[0:1, :]
    for k in range(1, TOP_K):
        hit = hit | (pos == lpr[k:k + 1, :])
    _rows_to_tiles(loc.at[slot], _dot(hit.astype(BF16), x))

    def strip(e, c):
        s = g * N_EXPERTS + e
        _strip_copies(loc.at[slot], soff_ref[s], xs_hbm, sgs_ref[s], scnt_ref[s], sem.at[slot])
        return c
    lax.fori_loop(0, N_EXPERTS, strip, 0)

    @pl.when(g == lay.n_tiles - 1)
    def _():
        _wait_rows(xs_hbm, loc.at[slot], n_pairs, sem.at[slot])
        if lay.n_tiles > 1:
            _wait_rows(xs_hbm, loc.at[1 - slot], n_pairs, sem.at[1 - slot])


def _dispatch(xp, lp_rows, soff, scnt, sgs, pad_lo, pad_hi, lay):
    tm = lay.tile
    grid_spec = pltpu.PrefetchScalarGridSpec(
        num_scalar_prefetch=5,
        grid=(lay.n_tiles,),
        in_specs=[pl.BlockSpec((tm * ROW_SUBLANES, LANES), lambda g, *_: (g, 0)),
                  pl.BlockSpec((1, ROW_SUBLANES, tm), lambda g, *_: (g, 0, 0))],
        out_specs=pl.BlockSpec(memory_space=pl.ANY),
        scratch_shapes=[pltpu.VMEM((2, TOP_K * tm * ROW_SUBLANES, LANES), F32),
                        pltpu.VMEM((ZERO_ROWS * ROW_SUBLANES, LANES), F32),
                        pltpu.SemaphoreType.DMA((3,))],
    )
    return pl.pallas_call(
        functools.partial(_dispatch_kernel, lay=lay),
        grid_spec=grid_spec,
        out_shape=jax.ShapeDtypeStruct((lay.sorted_rows * ROW_SUBLANES, LANES), F32),
        compiler_params=_params(("arbitrary",)),
        name="dispatch",
    )(soff, scnt, sgs, pad_lo, pad_hi, xp, lp_rows)


def _experts_kernel(be_ref, nv_ref, seg_ref, nx_ref, xs_ref, wg_hbm, wu_hbm, wd_hbm, bg_ref, bu_ref, bd_ref, ys_ref,
                    wf32, wbf, sem):
    bm = EXPERT_BLOCK
    i = pl.program_id(0)
    n_valid = nv_ref[0]

    def weight_copies(expert, slot):
        return [pltpu.make_async_copy(w_hbm.at[expert], wf32.at[slot, w], sem.at[slot])
                for w, w_hbm in enumerate((wg_hbm, wu_hbm, wd_hbm))]

    @pl.when(i < n_valid)
    def _():
        @pl.when((i == 0) | (be_ref[i] != be_ref[jnp.maximum(i - 1, 0)]))
        def _():
            slot = seg_ref[i] % 2

            @pl.when(i == 0)
            def _():
                for cp in weight_copies(be_ref[0], 0):
                    cp.start()

            for cp in weight_copies(be_ref[i], slot):
                cp.wait()

            @pl.when(nx_ref[i] >= 0)
            def _():
                for cp in weight_copies(nx_ref[i], 1 - slot):
                    cp.start()

            for w in range(3):
                for c in range(0, D_MODEL, WEIGHT_CAST_ROWS):
                    wbf[w, c:c + WEIGHT_CAST_ROWS, :] = wf32[slot, w, c:c + WEIGHT_CAST_ROWS, :].astype(BF16)

        part = bm // EXPERT_PARTS
        for j in range(EXPERT_PARTS):
            tiles = pl.ds(j * part * ROW_SUBLANES, part * ROW_SUBLANES)
            x = _rows_from_tiles(xs_ref.at[tiles], part).astype(BF16)
            g = jnp.minimum(_dot(x, wbf[0]) + bg_ref[0], SWIGLU_LIMIT)
            u = jnp.clip(_dot(x, wbf[1]) + bu_ref[0], -SWIGLU_LIMIT, SWIGLU_LIMIT)
            mid = ((u + 1.0) * (g * _sigmoid(SWIGLU_ALPHA * g))).astype(BF16)
            _rows_to_tiles(ys_ref.at[tiles], _dot(mid, wbf[2]) + bd_ref[0])

    @pl.when(i >= n_valid)
    def _():
        ys_ref[...] = jnp.zeros_like(ys_ref)


def _experts(block_e, n_valid, segment, next_e, xs, wts, lay):
    bm = EXPERT_BLOCK
    tile_rows = bm * ROW_SUBLANES
    wspec = pl.BlockSpec(memory_space=pl.ANY)
    bspec = pl.BlockSpec((1, 1, D_FF), lambda i, be, *_: (be[i], 0, 0))
    grid_spec = pltpu.PrefetchScalarGridSpec(
        num_scalar_prefetch=4,
        grid=(lay.n_blocks,),
        in_specs=[pl.BlockSpec((tile_rows, LANES), lambda i, be, nv, *_: (jnp.minimum(i, nv[0] - 1), 0)),
                  wspec, wspec, wspec, bspec, bspec, bspec],
        out_specs=pl.BlockSpec((tile_rows, LANES), lambda i, *_: (i, 0)),
        scratch_shapes=[pltpu.VMEM((2, 3, D_MODEL, D_FF), F32), pltpu.VMEM((3, D_MODEL, D_FF), BF16),
                        pltpu.SemaphoreType.DMA((2,))],
    )
    return pl.pallas_call(
        _experts_kernel,
        grid_spec=grid_spec,
        out_shape=jax.ShapeDtypeStruct((lay.sorted_rows * ROW_SUBLANES, LANES), F32),
        compiler_params=_params(("arbitrary",)),
        name="experts",
    )(block_e, n_valid, segment, next_e, xs, wts['w_gate'], wts['w_up'], wts['w_down'], wts['b_gate'], wts['b_up'],
      wts['b_down'])


def _combine_kernel(soff_ref, scnt_ref, sgs_ref, h_ref, tw_ref, lpc_ref, pp_ref, ps_ref, ys_hbm, g3_ref, wpg_ref,
                    wpp_ref, gf_ref, yp_ref, ysm_ref, loc, sem, *, lay, n_sample):
    tm = lay.tile
    n_pairs = TOP_K * tm
    g = pl.program_id(0)
    slot = g % 2

    def fetch(tile, s):
        def strip(e, c):
            i = tile * N_EXPERTS + e
            _strip_copies(ys_hbm, sgs_ref[i], loc.at[s], soff_ref[i], scnt_ref[i], sem.at[s])
            return c
        lax.fori_loop(0, N_EXPERTS, strip, 0)

    @pl.when(g == 0)
    def _():
        fetch(0, 0)

    @pl.when(g + 1 < lay.n_tiles)
    def _():
        fetch(g + 1, 1 - slot)

    _wait_rows(ys_hbm, loc.at[slot], n_pairs, sem.at[slot])
    y_rows = _rows_from_tiles(loc.at[slot], n_pairs).astype(BF16)
    tw = tw_ref[...]
    lpc = lpc_ref[...]
    pos = lax.broadcasted_iota(I32, (tm, n_pairs), 1)
    weight = jnp.zeros((tm, n_pairs), F32)
    for k in range(TOP_K):
        weight = jnp.where(pos == lpc[:, k:k + 1], tw[:, k:k + 1], weight)
    y = _dot(weight.astype(BF16), y_rows)
    h2 = h_ref[...] + y
    gate = _sigmoid(_dot(_rms(h2, g3_ref[...]).astype(BF16), wpg_ref[...]))
    p = pp_ref[...]
    p = jnp.where(g == lay.n_tiles - 1, jnp.concatenate([p[:tm - n_sample], ps_ref[...]], axis=0), p)
    h3 = h2 + gate * _dot(p.astype(BF16), wpp_ref[...])
    out = _rms(h3, gf_ref[...])
    yp_ref[...] = out

    @pl.when(g == lay.n_tiles - 1)
    def _():
        ysm_ref[...] = out[tm - n_sample:, :]


def _combine(hbuf, tw, lp_cols, p_prompt, p_sample, ys, soff, scnt, sgs, wts, lay):
    tm = lay.tile
    n_prompt, n_sample = p_prompt.shape[0], p_sample.shape[0]
    assert 0 < n_sample <= tm and (lay.n_tiles - 1) * tm < n_prompt
    consts = [wts['g3'], wts['w_ple_gate'], wts['w_ple_proj'], wts['g_final']]
    row = lambda w: pl.BlockSpec((tm, w), lambda g, *_: (g, 0))
    grid_spec = pltpu.PrefetchScalarGridSpec(
        num_scalar_prefetch=3,
        grid=(lay.n_tiles,),
        in_specs=[row(D_MODEL), row(LANES), row(LANES), row(D_PLE),
                  pl.BlockSpec((n_sample, D_PLE), lambda g, *_: (0, 0)), pl.BlockSpec(memory_space=pl.ANY)]
        + [pl.BlockSpec(c.shape, lambda g, *_: (0, 0), pipeline_mode=pl.Buffered(1)) for c in consts],
        out_specs=[row(D_MODEL), pl.BlockSpec((n_sample, D_MODEL), lambda g, *_: (0, 0))],
        scratch_shapes=[pltpu.VMEM((2, TOP_K * tm * ROW_SUBLANES, LANES), F32), pltpu.SemaphoreType.DMA((2,))],
    )
    return pl.pallas_call(
        functools.partial(_combine_kernel, lay=lay, n_sample=n_sample),
        grid_spec=grid_spec,
        out_shape=[jax.ShapeDtypeStruct((n_prompt, D_MODEL), F32), jax.ShapeDtypeStruct((n_sample, D_MODEL), F32)],
        compiler_params=_params(("arbitrary",)),
        name="combine",
    )(soff, scnt, sgs, hbuf, tw, lp_cols, p_prompt, p_sample, ys, *consts)


def _prepare_weights(norm1_g, w_in, w_gk2, b_gk, gla_norm_g, v_norm_g, v_norm_b, w_sp, b_sp, w_o, norm2_g,
                     w_router, b_router, w_gate, b_gate, w_up, b_up, w_down, b_down, norm3_g, w_ple_gate,
                     w_ple_proj, final_g):
    o_gk = 2 * GLA_DK_T + 2 * GLA_DV_T
    w_a = w_in[:, :o_gk].astype(BF16)
    w_b = w_in[:, o_gk + GK_RANK:].astype(BF16)
    w_gk1 = jnp.pad(w_in[:, o_gk:o_gk + GK_RANK], ((0, 0), (0, LANES - GK_RANK))).astype(BF16)
    w_gk2p = jnp.pad(w_gk2, ((0, LANES - GK_RANK), (0, 0))).astype(BF16)
    row = lambda z: z.reshape(1, -1).astype(F32)
    return dict(
        g1=row(norm1_g), w_a=w_a, w_b=w_b, w_gk1=w_gk1, w_gk2=w_gk2p, b_gk=row(b_gk), gla_g=row(gla_norm_g),
        vn_g=row(v_norm_g), vn_b=row(v_norm_b), w_sp=w_sp, b_sp_t=b_sp.T,
        mix_w0=row(jnp.repeat(w_sp[:, 0, 0], MIX_DH)), mix_b0=row(jnp.repeat(b_sp[:, 0], MIX_DH)),
        w_o=w_o.astype(BF16), g2=row(norm2_g),
        w_router=jnp.pad(w_router, ((0, 0), (0, LANES - N_EXPERTS))),
        b_router=jnp.pad(row(b_router), ((0, 0), (0, LANES - N_EXPERTS))),
        w_gate=w_gate, w_up=w_up, w_down=w_down,
        b_gate=b_gate.reshape(N_EXPERTS, 1, D_FF), b_up=b_up.reshape(N_EXPERTS, 1, D_FF),
        b_down=b_down.reshape(N_EXPERTS, 1, D_MODEL),
        g3=row(norm3_g), w_ple_gate=w_ple_gate.astype(BF16), w_ple_proj=w_ple_proj.astype(BF16),
        g_final=row(final_g))


def _routing_tables(counts, counts_before, tile_counts, local_off, lay):
    bm = EXPERT_BLOCK
    padded = (counts + bm - 1) // bm * bm
    pad_end = jnp.cumsum(padded)
    pad_start = pad_end - padded
    global_start = pad_start[None, :] + counts_before
    n_valid = (pad_end[-1] // bm).astype(I32)
    blk = jnp.minimum(jnp.arange(lay.n_blocks, dtype=I32), n_valid - 1)
    block_e = jnp.sum(pad_end[None, :] <= (blk * bm)[:, None], axis=1).astype(I32)
    segment = jnp.cumsum(jnp.concatenate([jnp.zeros((1,), I32), (block_e[1:] != block_e[:-1]).astype(I32)]))
    follows = segment[None, :] == segment[:, None] + 1
    next_e = jnp.where(jnp.any(follows, axis=1), block_e[jnp.argmax(follows, axis=1)], -1).astype(I32)
    flat = lambda z: z.reshape(-1).astype(I32)
    return dict(soff=flat(local_off), scnt=flat(tile_counts), sgs=flat(global_start),
                pad_lo=(pad_start + counts).astype(I32), pad_hi=pad_end.astype(I32), block_e=block_e,
                segment=segment.astype(I32), next_e=next_e, n_valid=n_valid.reshape(1))


def _layer(x_prompt, x_sample, state, p_prompt, p_sample, wts):
    n_p, len_p, _ = x_prompt.shape
    n_s, len_s, _ = x_sample.shape
    assert len_s == 1, "the sample group carries one new token per sequence"
    t_p = n_p * len_p
    t_all = t_p + n_s

    xs = x_sample.reshape(n_s, D_MODEL)
    q, k, a, v, og, ob, vn = _mixer_sample_in(xs, wts)
    st_sample, o = _state_update(state, q, k, a, v)
    h_sample = _mixer_sample_out(xs, o, og, ob, wts)
    hbuf, st_prompt = _mixer_prompt(x_prompt, h_sample, wts)

    lay = _Layout(t_all)
    xp, top_w, lp_cols, lp_rows, counts_before, tile_counts, local_off, counts = _router(hbuf, wts)
    ne = N_EXPERTS
    tab = _routing_tables(counts[0, :ne], counts_before[:, 0, :ne], tile_counts[:, 0, :ne], local_off[:, 0, :ne], lay)
    x_sorted = _dispatch(xp, lp_rows, tab['soff'], tab['scnt'], tab['sgs'], tab['pad_lo'], tab['pad_hi'], lay)
    y_sorted = _experts(tab['block_e'], tab['n_valid'], tab['segment'], tab['next_e'], x_sorted, wts, lay)
    y_prompt, y_sample = _combine(hbuf, top_w, lp_cols, p_prompt.reshape(t_p, D_PLE), p_sample.reshape(n_s, D_PLE),
                                  y_sorted, tab['soff'], tab['scnt'], tab['sgs'], wts, lay)
    return (y_prompt.reshape(n_p, len_p, D_MODEL), y_sample.reshape(n_s, len_s, D_MODEL),
            st_prompt, st_sample, vn.reshape(n_s, len_s, D_GMLP))


def kernel(x_prompt, x_sample, state_gla, p_prompt, p_sample, norm1_g, w_in, w_gk2, b_gk, gla_norm_g, v_norm_g,
           v_norm_b, w_sp, b_sp, w_o, norm2_g, w_router, b_router, w_gate, b_gate, w_up, b_up, w_down, b_down,
           norm3_g, w_ple_gate, w_ple_proj, final_g):
    assert w_in.shape[0] == 1, "single-layer trunk"
    wts = _prepare_weights(norm1_g[0], w_in[0], w_gk2[0], b_gk[0], gla_norm_g[0], v_norm_g[0], v_norm_b[0],
                           w_sp[0], b_sp[0], w_o[0], norm2_g[0], w_router[0], b_router[0], w_gate[0], b_gate[0],
                           w_up[0], b_up[0], w_down[0], b_down[0], norm3_g[0], w_ple_gate[0], w_ple_proj[0],
                           final_g)
    y_p, y_s, st_p, st_s, vn_s = _layer(x_prompt, x_sample, state_gla[0], p_prompt[0], p_sample[0], wts)
    return (y_p, y_s, st_p[None], st_s[None].astype(state_gla.dtype), vn_s[None])
```

```python
import functools

import jax
import jax.numpy as jnp
from jax import lax
from jax.experimental import pallas as pl
from jax.experimental.pallas import tpu as pltpu

F32 = jnp.float32
BF16 = jnp.bfloat16
I32 = jnp.int32

D_MODEL = 1024
GLA_HEADS = 4
GLA_DK = 128
GLA_DV = 256
GLA_DK_T = GLA_HEADS * GLA_DK
GLA_DV_T = GLA_HEADS * GLA_DV
GK_RANK = 16
GATE_NORMALIZER = 16.0
GLA_CHUNK = 64
GLA_CHUNK_LOG2 = 6
D_GMLP = 1024
MIX_HEADS = 4
MIX_DH = D_GMLP // MIX_HEADS
MIX_CHUNK = 128
MIX_CHUNK_LOG2 = 7
assert 1 << GLA_CHUNK_LOG2 == GLA_CHUNK and 1 << MIX_CHUNK_LOG2 == MIX_CHUNK
N_EXPERTS = 32
TOP_K = 4
D_FF = 1024
SWIGLU_LIMIT = 7.0
SWIGLU_ALPHA = 1.702
D_PLE = 256
EPS = 1e-6

LANES = 128
ROW_SUBLANES = D_MODEL // LANES
VMEM_LIMIT_BYTES = 56 * 1024 * 1024

OFF_Q, OFF_K, OFF_V, OFF_R, OFF_U, OFF_VG, OFF_GA, OFF_GB, N_MAIN = (
    0, 512, 1024, 2048, 3072, 4096, 5120, 6144, 7168)

MIXER_TILE = 1024
MIXER_PARTS = 4
STATE_TOKENS = 8
EXPERT_BLOCK = 768
EXPERT_PARTS = 3
WEIGHT_CAST_ROWS = 256
STRIP_CHUNK = 16
STRIP_CHUNK_LOG2 = 4
ZERO_ROWS = 64
ZERO_ROWS_LOG2 = 6
assert 1 << STRIP_CHUNK_LOG2 == STRIP_CHUNK and 1 << ZERO_ROWS_LOG2 == ZERO_ROWS


def _dot(a, b):
    return jnp.dot(a, b, preferred_element_type=F32)


def _dot_nt(a, b):
    return lax.dot_general(a, b, (((1,), (1,)), ((), ())), preferred_element_type=F32)


def _dot_tn(a, b):
    return lax.dot_general(a, b, (((0,), (0,)), ((), ())), preferred_element_type=F32)


def _rms(x, g):
    return x * lax.rsqrt(jnp.mean(x * x, axis=-1, keepdims=True) + EPS) * g


def _sigmoid(x):
    return 1.0 / (1.0 + jnp.exp(-x))


def _gelu(x):
    return 0.5 * x * (1.0 + lax.erf(x * (2.0 ** -0.5)))


def _log_sigmoid(x):
    return jnp.minimum(x, 0.0) - jnp.log1p(jnp.exp(-jnp.abs(x)))


def _split3(x):
    hi = x.astype(BF16)
    r1 = x - hi.astype(F32)
    mid = r1.astype(BF16)
    lo = (r1 - mid.astype(F32)).astype(BF16)
    return hi, mid, lo


def _const_spec(shape):
    nd = len(shape)
    return pl.BlockSpec(shape, lambda *_: (0,) * nd, pipeline_mode=pl.Buffered(1))


def _params(sem):
    return pltpu.CompilerParams(dimension_semantics=sem, vmem_limit_bytes=VMEM_LIMIT_BYTES)


def _main_cols(wa_ref, wb_ref, lo, hi):
    if hi <= OFF_U:
        return wa_ref[:, lo:hi]
    return wb_ref[:, lo - OFF_U:hi - OFF_U]


def _project_gla_inputs(n, wm, wgk1_ref, wgk2_ref, bgk_ref):
    q = _dot(n, wm(OFF_Q, OFF_K)) * (GLA_DK ** -0.5)
    k = _dot(n, wm(OFF_K, OFF_V))
    v = _dot(n, wm(OFF_V, OFF_R))
    gk = _dot(n, wgk1_ref[...]).astype(BF16)
    log_a = _log_sigmoid(_dot(gk, wgk2_ref[...]) + bgk_ref[...]) * (1.0 / GATE_NORMALIZER)
    return q, k, v, log_a


def _gmlp_inputs(n, wm, vng_ref, vnb_ref):
    u = _gelu(_dot(n, wm(OFF_U, OFF_VG)))
    vg = _gelu(_dot(n, wm(OFF_VG, OFF_GA)))
    mu = jnp.mean(vg, axis=-1, keepdims=True)
    vc = vg - mu
    var = jnp.mean(vc * vc, axis=-1, keepdims=True)
    vn = vc * lax.rsqrt(var + EPS) * vng_ref[...] + vnb_ref[...]
    ug = u * _sigmoid(_dot(n, wm(OFF_GB, N_MAIN)))
    return ug, vn


def _gla_out_gate(n, wm):
    r = _dot(n, wm(OFF_R, OFF_U))
    ga = _dot(n, wm(OFF_GA, OFF_GB))
    return r * _sigmoid(r) * _sigmoid(ga)


def _head_rms(o, g):
    return o * lax.rsqrt(jnp.mean(o * o, axis=-1, keepdims=True) + EPS) * g


def _mixer_prompt_kernel(x_ref, hs_ref, *refs, tm, n_seq, n_tiles):
    b = pl.program_id(0)
    t = pl.program_id(1)
    h_ref = refs[12]

    @pl.when(b < n_seq)
    def _():
        _mixer_prompt_tile(x_ref, *refs, tm=tm, n_tiles=n_tiles)

    @pl.when((b == n_seq) & (t == 0))
    def _():
        h_ref[0:hs_ref.shape[0], :] = hs_ref[...]


def _mixer_prompt_tile(x_ref, g1_ref, wa_ref, wb_ref, wgk1_ref, wgk2_ref, bgk_ref, glag_ref, vng_ref, vnb_ref,
                       wsp_ref, bspt_ref, wo_ref,
                       h_ref, st_ref,
                       n_scr, oa_scr, s_scr, *, tm, n_tiles):
    t = pl.program_id(1)

    @pl.when(t == 0)
    def _():
        s_scr[...] = jnp.zeros_like(s_scr)

    wm = functools.partial(_main_cols, wa_ref, wb_ref)
    pm = tm // MIXER_PARTS
    for part in range(MIXER_PARTS):
        rows = pl.ds(part * pm, pm)
        _mixer_prompt_part(x_ref.at[0, rows], g1_ref, wm, wgk1_ref, wgk2_ref, bgk_ref, glag_ref, vng_ref, vnb_ref,
                           wsp_ref, bspt_ref, wo_ref, h_ref.at[rows], n_scr.at[rows], oa_scr.at[rows], s_scr, tm=pm)

    @pl.when(t == n_tiles - 1)
    def _():
        for h in range(GLA_HEADS):
            st_ref[0, h] = s_scr[h].T


def _mixer_prompt_part(x_ref, g1_ref, wm, wgk1_ref, wgk2_ref, bgk_ref, glag_ref, vng_ref, vnb_ref,
                       wsp_ref, bspt_ref, wo_ref, h_ref, n_scr, oa_scr, s_scr, *, tm):
    x = x_ref[...]
    n = _rms(x, g1_ref[...]).astype(BF16)
    n_scr[...] = n
    q, k, v, log_a = _project_gla_inputs(n, wm, wgk1_ref, wgk2_ref, bgk_ref)
    v = v.astype(BF16)

    row = lax.broadcasted_iota(I32, (tm, tm), 0)
    col = lax.broadcasted_iota(I32, (tm, tm), 1)
    same_chunk = lax.shift_right_logical(row, GLA_CHUNK_LOG2) == lax.shift_right_logical(col, GLA_CHUNK_LOG2)
    causal = same_chunk & (row >= col)
    tri = causal.astype(BF16)
    blk = same_chunk.astype(BF16)
    hi, mid, lo = _split3(log_a)
    b = _dot(tri, hi) + _dot(tri, mid) + _dot(tri, lo)
    b_end = _dot(blk, hi) + _dot(blk, mid) + _dot(blk, lo)
    qe = (q * jnp.exp(b)).astype(BF16)
    ke = (k * jnp.exp(-b)).astype(BF16)
    kd = (k * jnp.exp(b_end - b)).astype(BF16)
    dec = jnp.exp(b_end)
    glag = glag_ref[...]
    for h in range(GLA_HEADS):
        ks = slice(h * GLA_DK, (h + 1) * GLA_DK)
        vs = slice(h * GLA_DV, (h + 1) * GLA_DV)
        vh = v[:, vs]
        att = jnp.where(causal, _dot_nt(qe[:, ks], ke[:, ks]), 0.0).astype(BF16)
        o_intra = _dot(att, vh)
        s_t = s_scr[h]
        o_inter = []
        for c in range(tm // GLA_CHUNK):
            rs = slice(c * GLA_CHUNK, (c + 1) * GLA_CHUNK)
            o_inter.append(_dot_nt(qe[rs, ks], s_t.astype(BF16)))
            s_t = s_t * dec[c * GLA_CHUNK:c * GLA_CHUNK + 1, ks] + _dot_tn(vh[rs], kd[rs, ks])
        s_scr[h] = s_t
        oa_scr[:, vs] = _head_rms(o_intra + jnp.concatenate(o_inter, axis=0), glag)

    n = n_scr[...]
    oa_scr[...] = oa_scr[...] * _gla_out_gate(n, wm)
    ug, vn = _gmlp_inputs(n, wm, vng_ref, vnb_ref)
    vn = vn.astype(BF16)
    reps = tm // MIX_CHUNK
    mix_mask = (lax.shift_right_logical(row, MIX_CHUNK_LOG2) == lax.shift_right_logical(col, MIX_CHUNK_LOG2)) & (
        row >= col)
    for h in range(MIX_HEADS):
        w = jnp.where(mix_mask, jnp.tile(wsp_ref[h], (reps, reps)), 0.0).astype(BF16)
        cs = slice(h * MIX_DH, (h + 1) * MIX_DH)
        mix = _dot(w, vn[:, cs]) + jnp.tile(bspt_ref[:, h:h + 1], (reps, 1))
        oa_scr[:, cs] = oa_scr[:, cs] + ug[:, cs] * mix

    h_ref[...] = x + _dot(oa_scr[...].astype(BF16), wo_ref[...])


def _mixer_prompt(x, h_sample, wts):
    n_seq, seq_len, _ = x.shape
    n_s = h_sample.shape[0]
    tm = min(MIXER_TILE, seq_len)
    assert seq_len % tm == 0 and tm % MIX_CHUNK == 0 and n_s <= tm
    n_tiles = seq_len // tm
    consts = [wts['g1'], wts['w_a'], wts['w_b'], wts['w_gk1'], wts['w_gk2'], wts['b_gk'], wts['gla_g'], wts['vn_g'],
              wts['vn_b'], wts['w_sp'], wts['b_sp_t'], wts['w_o']]
    kern = functools.partial(_mixer_prompt_kernel, tm=tm, n_seq=n_seq, n_tiles=n_tiles)
    last = n_seq - 1

    def prompt_tile(b, t):
        return jnp.minimum(b, last), jnp.where(b < n_seq, t, n_tiles - 1)

    def x_map(b, t):
        bb, tt = prompt_tile(b, t)
        return bb, tt, 0

    def h_map(b, t):
        return jnp.where(b < n_seq, b * n_tiles + t, n_seq * n_tiles), 0

    return pl.pallas_call(
        kern,
        grid=(n_seq + 1, n_tiles),
        in_specs=[pl.BlockSpec((1, tm, D_MODEL), x_map), _const_spec(h_sample.shape)]
        + [_const_spec(c.shape) for c in consts],
        out_specs=[pl.BlockSpec((tm, D_MODEL), h_map),
                   pl.BlockSpec((1, GLA_HEADS, GLA_DK, GLA_DV), lambda b, t: (jnp.minimum(b, last), 0, 0, 0))],
        out_shape=[jax.ShapeDtypeStruct((n_seq * seq_len + n_s, D_MODEL), F32),
                   jax.ShapeDtypeStruct((n_seq, GLA_HEADS, GLA_DK, GLA_DV), F32)],
        scratch_shapes=[pltpu.VMEM((tm, D_MODEL), BF16),
                        pltpu.VMEM((tm, D_MODEL), F32),
                        pltpu.VMEM((GLA_HEADS, GLA_DV, GLA_DK), F32)],
        compiler_params=_params(("arbitrary", "arbitrary")),
        name="mixer_prompt",
    )(x, h_sample, *consts)


def _mixer_sample_in_kernel(x_ref, g1_ref, wa_ref, wb_ref, wgk1_ref, wgk2_ref, bgk_ref, vng_ref, vnb_ref,
                            mixw_ref, mixb_ref,
                            q_ref, k_ref, a_ref, v_ref, og_ref, ob_ref, vn_ref):
    wm = functools.partial(_main_cols, wa_ref, wb_ref)
    n = _rms(x_ref[...], g1_ref[...]).astype(BF16)
    q, k, v, log_a = _project_gla_inputs(n, wm, wgk1_ref, wgk2_ref, bgk_ref)
    q_ref[...] = q
    k_ref[...] = k
    a_ref[...] = jnp.exp(log_a)
    v_ref[...] = v
    og_ref[...] = _gla_out_gate(n, wm)
    ug, vn = _gmlp_inputs(n, wm, vng_ref, vnb_ref)
    vn_ref[...] = vn
    ob_ref[...] = ug * (mixw_ref[...] * vn + mixb_ref[...])


def _mixer_sample_in(x2, wts):
    n_seq = x2.shape[0]
    consts = [wts['g1'], wts['w_a'], wts['w_b'], wts['w_gk1'], wts['w_gk2'], wts['b_gk'], wts['vn_g'], wts['vn_b'],
              wts['mix_w0'], wts['mix_b0']]
    widths = [GLA_DK_T, GLA_DK_T, GLA_DK_T, GLA_DV_T, D_MODEL, D_MODEL, D_GMLP]
    return pl.pallas_call(
        _mixer_sample_in_kernel,
        grid=(1,),
        in_specs=[_const_spec(x2.shape)] + [_const_spec(c.shape) for c in consts],
        out_specs=[pl.BlockSpec((n_seq, w), lambda i: (0, 0)) for w in widths],
        out_shape=[jax.ShapeDtypeStruct((n_seq, w), F32) for w in widths],
        compiler_params=_params(("arbitrary",)),
        name="mixer_sample_in",
    )(x2, *consts)


def _state_update_kernel(s_ref, q_ref, k_ref, a_ref, v_ref, so_ref, o_ref):
    for j in range(STATE_TOKENS):
        for h in range(GLA_HEADS):
            ks = slice(h * GLA_DK, (h + 1) * GLA_DK)
            vs = slice(h * GLA_DV, (h + 1) * GLA_DV)
            s_new = a_ref[0, ks, j:j + 1] * s_ref[j, h] + k_ref[0, ks, j:j + 1] * v_ref[j:j + 1, vs]
            so_ref[j, h] = s_new
            o_ref[j:j + 1, vs] = jnp.sum(q_ref[0, ks, j:j + 1] * s_new, axis=0, keepdims=True)


def _state_update(state, q, k, a, v):
    n_seq = state.shape[0]
    tb = STATE_TOKENS
    assert n_seq % tb == 0

    def cols(z):
        return z.reshape(n_seq // tb, tb, GLA_DK_T).transpose(0, 2, 1)

    col_spec = pl.BlockSpec((1, GLA_DK_T, tb), lambda i: (i, 0, 0))
    st_spec = pl.BlockSpec((tb, GLA_HEADS, GLA_DK, GLA_DV), lambda i: (i, 0, 0, 0))
    row_spec = pl.BlockSpec((tb, GLA_DV_T), lambda i: (i, 0))
    return pl.pallas_call(
        _state_update_kernel,
        grid=(n_seq // tb,),
        in_specs=[st_spec, col_spec, col_spec, col_spec, row_spec],
        out_specs=[st_spec, row_spec],
        out_shape=[jax.ShapeDtypeStruct(state.shape, F32), jax.ShapeDtypeStruct((n_seq, GLA_DV_T), F32)],
        compiler_params=_params(("arbitrary",)),
        name="state_update",
    )(state, cols(q), cols(k), cols(a), v)


def _mixer_sample_out_kernel(x_ref, o_ref, og_ref, ob_ref, glag_ref, wo_ref, h_ref):
    glag = glag_ref[...]
    parts = []
    for h in range(GLA_HEADS):
        vs = slice(h * GLA_DV, (h + 1) * GLA_DV)
        parts.append(_head_rms(o_ref[:, vs], glag))
    merged = jnp.concatenate(parts, axis=1) * og_ref[...] + ob_ref[...]
    h_ref[...] = x_ref[...] + _dot(merged.astype(BF16), wo_ref[...])


def _mixer_sample_out(x2, o, og, ob, wts):
    n_seq = x2.shape[0]
    consts = [x2, o, og, ob, wts['gla_g'], wts['w_o']]
    return pl.pallas_call(
        _mixer_sample_out_kernel,
        grid=(1,),
        in_specs=[_const_spec(c.shape) for c in consts],
        out_specs=pl.BlockSpec((n_seq, D_MODEL), lambda i: (0, 0)),
        out_shape=jax.ShapeDtypeStruct((n_seq, D_MODEL), F32),
        compiler_params=_params(("arbitrary",)),
        name="mixer_sample_out",
    )(*consts)


def _router_kernel(h_ref, g2_ref, wr_ref, br_ref, xp_ref, tw_ref, lpc_ref, lpr_ref, cb_ref, tc_ref, lo_ref, cnt_ref, cnt_scr,
                   *, tm):
    i = pl.program_id(0)

    @pl.when(i == 0)
    def _():
        cnt_scr[...] = jnp.zeros_like(cnt_scr)

    cb_ref[0] = cnt_scr[...].astype(I32)

    hn = _rms(h_ref[...], g2_ref[...])
    hn_hi = hn.astype(BF16)
    hn_lo = (hn - hn_hi.astype(F32)).astype(BF16)
    w = wr_ref[...]
    w_hi = w.astype(BF16)
    w_lo = (w - w_hi.astype(F32)).astype(BF16)
    logits = _dot(hn_hi, w_hi) + _dot(hn_lo, w_hi) + _dot(hn_hi, w_lo) + br_ref[...]

    _rows_to_tiles(xp_ref, hn)

    lane = lax.broadcasted_iota(I32, (tm, LANES), 1)
    neg = jnp.float32(-jnp.inf)
    l = jnp.where(lane < N_EXPERTS, logits, neg)
    vals, hots = [], []
    for _ in range(TOP_K):
        m = jnp.max(l, axis=-1, keepdims=True)
        idx = jnp.min(jnp.where(l == m, lane, LANES), axis=-1, keepdims=True)
        hot = lane == idx
        l = jnp.where(hot, neg, l)
        vals.append(m)
        hots.append(hot)
    exps = [jnp.exp(v - vals[0]) for v in vals]
    denom = exps[0] + exps[1] + exps[2] + exps[3]

    member = (hots[0] | hots[1] | hots[2] | hots[3]).astype(BF16)
    row = lax.broadcasted_iota(I32, (tm, tm), 0)
    col = lax.broadcasted_iota(I32, (tm, tm), 1)
    in_tile = _dot((row > col).astype(BF16), member)
    tile_cnt = jnp.sum(member.astype(F32), axis=0, keepdims=True)
    e_row = lax.broadcasted_iota(I32, (LANES, LANES), 0)
    e_col = lax.broadcasted_iota(I32, (LANES, LANES), 1)
    lower = (e_row < e_col).astype(BF16)
    c_hi, c_mid, c_lo = _split3(jnp.broadcast_to(tile_cnt, (ROW_SUBLANES, LANES)))
    local_off = (_dot(c_hi, lower) + _dot(c_mid, lower) + _dot(c_lo, lower))[0:1, :]
    tc_ref[0] = tile_cnt.astype(I32)
    lo_ref[0] = local_off.astype(I32)
    cnt_scr[...] = cnt_scr[...] + tile_cnt
    cnt_ref[...] = cnt_scr[...].astype(I32)

    place = local_off + in_tile
    tw = jnp.zeros((tm, LANES), F32)
    lp = jnp.zeros((tm, LANES), F32)
    for kk in range(TOP_K):
        sel = lane == kk
        tw = jnp.where(sel, exps[kk] / denom, tw)
        lp = jnp.where(sel, jnp.sum(jnp.where(hots[kk], place, 0.0), axis=-1, keepdims=True), lp)
    tw_ref[...] = tw
    lpc_ref[...] = lp.astype(I32)
    lpr_ref[0] = lp.T[0:ROW_SUBLANES, :].astype(I32)


def _router_tile(t_all):
    for tm in (512, 384, 256, 128):
        if t_all % tm == 0:
            return tm
    raise ValueError(f"token count {t_all} must be a multiple of 128")


def _router(hbuf, wts):
    t_all = hbuf.shape[0]
    tm = _router_tile(t_all)
    consts = [wts['g2'], wts['w_router'], wts['b_router']]
    row = lambda w: pl.BlockSpec((tm, w), lambda i: (i, 0))
    n_tiles = t_all // tm
    per_tile = pl.BlockSpec((1, 1, LANES), lambda i: (i, 0, 0))
    tile_vec = jax.ShapeDtypeStruct((n_tiles, 1, LANES), I32)
    return pl.pallas_call(
        functools.partial(_router_kernel, tm=tm),
        grid=(t_all // tm,),
        in_specs=[row(D_MODEL)] + [_const_spec(c.shape) for c in consts],
        out_specs=[pl.BlockSpec((tm * ROW_SUBLANES, LANES), lambda i: (i, 0)), row(LANES), row(LANES),
                   pl.BlockSpec((1, ROW_SUBLANES, tm), lambda i: (i, 0, 0)),
                   per_tile, per_tile, per_tile, pl.BlockSpec((1, LANES), lambda i: (0, 0))],
        out_shape=[jax.ShapeDtypeStruct((t_all * ROW_SUBLANES, LANES), F32),
                   jax.ShapeDtypeStruct((t_all, LANES), F32),
                   jax.ShapeDtypeStruct((t_all, LANES), I32),
                   jax.ShapeDtypeStruct((n_tiles, ROW_SUBLANES, tm), I32),
                   tile_vec, tile_vec, tile_vec, jax.ShapeDtypeStruct((1, LANES), I32)],
        scratch_shapes=[pltpu.VMEM((1, LANES), F32)],
        compiler_params=_params(("arbitrary",)),
        name="router",
    )(hbuf, *consts)


def _rows_from_tiles(ref, n_rows):
    return jnp.concatenate([ref[pl.ds(s, n_rows, stride=ROW_SUBLANES), :] for s in range(ROW_SUBLANES)], axis=1)


def _rows_to_tiles(ref, x):
    for s in range(ROW_SUBLANES):
        ref[pl.ds(s, x.shape[0], stride=ROW_SUBLANES), :] = x[:, s * LANES:(s + 1) * LANES]


def _tiles(row, n_rows):
    return pl.ds(pl.multiple_of(row * ROW_SUBLANES, ROW_SUBLANES), n_rows * ROW_SUBLANES)


class _Layout:
    def __init__(self, t_all):
        bm = EXPERT_BLOCK
        self.t_all = t_all
        self.tile = _router_tile(t_all)
        self.n_tiles = t_all // self.tile
        self.n_blocks = -(-(t_all * TOP_K + N_EXPERTS * (bm - 1)) // bm)
        self.sorted_rows = self.n_blocks * bm


def _strip_copies(src_ref, src_row, dst_ref, dst_row, n, sem):
    def copy(offset, size):
        pltpu.make_async_copy(src_ref.at[_tiles(src_row + offset, size)], dst_ref.at[_tiles(dst_row + offset, size)],
                              sem).start()

    n_chunks = lax.shift_right_logical(n, STRIP_CHUNK_LOG2)

    def chunk(j, c):
        copy(j * STRIP_CHUNK, STRIP_CHUNK)
        return c
    lax.fori_loop(0, n_chunks, chunk, 0)

    done = n_chunks * STRIP_CHUNK
    size = STRIP_CHUNK // 2
    while size >= 1:
        part = n & size

        @pl.when(part != 0)
        def _(done=done, size=size):
            copy(done, size)
        done = done + part
        size //= 2


def _wait_rows(hbm_ref, vmem_ref, n_rows, sem):
    pltpu.make_async_copy(hbm_ref.at[pl.ds(0, n_rows * ROW_SUBLANES)], vmem_ref.at[pl.ds(0, n_rows * ROW_SUBLANES)],
                          sem).wait()


def _dispatch_kernel(soff_ref, scnt_ref, sgs_ref, lo_ref, hi_ref, xp_ref, lpr_ref, xs_hbm, loc, zero_scr, sem,
                     *, lay):
    tm = lay.tile
    n_pairs = TOP_K * tm
    g = pl.program_id(0)
    slot = g % 2

    def zero_fill(lo, hi, wait):
        def copies(first, count, size):
            def body(j, c):
                cp = pltpu.make_async_copy(zero_scr.at[pl.ds(0, size * ROW_SUBLANES)],
                                           xs_hbm.at[_tiles(first + j * size, size)], sem.at[2])
                if wait:
                    cp.wait()
                else:
                    cp.start()
                return c
            lax.fori_loop(0, count, body, 0)
        n = hi - lo
        n_big = lax.shift_right_logical(n, ZERO_ROWS_LOG2)
        copies(lo, n_big, ZERO_ROWS)
        done = n_big * ZERO_ROWS
        n_mid = lax.shift_right_logical(n - done, STRIP_CHUNK_LOG2)
        copies(lo + done, n_mid, STRIP_CHUNK)
        done = done + n_mid * STRIP_CHUNK
        copies(lo + done, n - done, 1)

    @pl.when(g == 0)
    def _():
        zero_scr[...] = jnp.zeros_like(zero_scr)
        for wait in (False, True):
            def expert(e, c, wait=wait):
                zero_fill(lo_ref[e], hi_ref[e], wait)
                return c
            lax.fori_loop(0, N_EXPERTS, expert, 0)
            zero_fill(hi_ref[N_EXPERTS - 1], lay.sorted_rows, wait)

    @pl.when(g >= 2)
    def _():
        _wait_rows(xs_hbm, loc.at[slot], n_pairs, sem.at[slot])

    x = _rows_from_tiles(xp_ref, tm).astype(BF16)
    pos = lax.broadcasted_iota(I32, (n_pairs, tm), 0)
    lpr = lpr_ref[0]
    hit = pos == lpr[0:1, :]
    for k in range(1, TOP_K):
        hit = hit | (pos == lpr[k:k + 1, :])
    _rows_to_tiles(loc.at[slot], _dot(hit.astype(BF16), x))

    def strip(e, c):
        s = g * N_EXPERTS + e
        _strip_copies(loc.at[slot], soff_ref[s], xs_hbm, sgs_ref[s], scnt_ref[s], sem.at[slot])
        return c
    lax.fori_loop(0, N_EXPERTS, strip, 0)

    @pl.when(g == lay.n_tiles - 1)
    def _():
        _wait_rows(xs_hbm, loc.at[slot], n_pairs, sem.at[slot])
        if lay.n_tiles > 1:
            _wait_rows(xs_hbm, loc.at[1 - slot], n_pairs, sem.at[1 - slot])


def _dispatch(xp, lp_rows, soff, scnt, sgs, pad_lo, pad_hi, lay):
    tm = lay.tile
    grid_spec = pltpu.PrefetchScalarGridSpec(
        num_scalar_prefetch=5,
        grid=(lay.n_tiles,),
        in_specs=[pl.BlockSpec((tm * ROW_SUBLANES, LANES), lambda g, *_: (g, 0)),
                  pl.BlockSpec((1, ROW_SUBLANES, tm), lambda g, *_: (g, 0, 0))],
        out_specs=pl.BlockSpec(memory_space=pl.ANY),
        scratch_shapes=[pltpu.VMEM((2, TOP_K * tm * ROW_SUBLANES, LANES), F32),
                        pltpu.VMEM((ZERO_ROWS * ROW_SUBLANES, LANES), F32),
                        pltpu.SemaphoreType.DMA((3,))],
    )
    return pl.pallas_call(
        functools.partial(_dispatch_kernel, lay=lay),
        grid_spec=grid_spec,
        out_shape=jax.ShapeDtypeStruct((lay.sorted_rows * ROW_SUBLANES, LANES), F32),
        compiler_params=_params(("arbitrary",)),
        name="dispatch",
    )(soff, scnt, sgs, pad_lo, pad_hi, xp, lp_rows)


def _experts_kernel(be_ref, nv_ref, seg_ref, nx_ref, xs_ref, wg_hbm, wu_hbm, wd_hbm, bg_ref, bu_ref, bd_ref, ys_ref,
                    wf32, wbf, sem):
    bm = EXPERT_BLOCK
    i = pl.program_id(0)
    n_valid = nv_ref[0]

    def weight_copies(expert, slot):
        return [pltpu.make_async_copy(w_hbm.at[expert], wf32.at[slot, w], sem.at[slot])
                for w, w_hbm in enumerate((wg_hbm, wu_hbm, wd_hbm))]

    @pl.when(i < n_valid)
    def _():
        @pl.when((i == 0) | (be_ref[i] != be_ref[jnp.maximum(i - 1, 0)]))
        def _():
            slot = seg_ref[i] % 2

            @pl.when(i == 0)
            def _():
                for cp in weight_copies(be_ref[0], 0):
                    cp.start()

            for cp in weight_copies(be_ref[i], slot):
                cp.wait()

            @pl.when(nx_ref[i] >= 0)
            def _():
                for cp in weight_copies(nx_ref[i], 1 - slot):
                    cp.start()

            for w in range(3):
                for c in range(0, D_MODEL, WEIGHT_CAST_ROWS):
                    wbf[w, c:c + WEIGHT_CAST_ROWS, :] = wf32[slot, w, c:c + WEIGHT_CAST_ROWS, :].astype(BF16)

        part = bm // EXPERT_PARTS
        for j in range(EXPERT_PARTS):
            tiles = pl.ds(j * part * ROW_SUBLANES, part * ROW_SUBLANES)
            x = _rows_from_tiles(xs_ref.at[tiles], part).astype(BF16)
            g = jnp.minimum(_dot(x, wbf[0]) + bg_ref[0], SWIGLU_LIMIT)
            u = jnp.clip(_dot(x, wbf[1]) + bu_ref[0], -SWIGLU_LIMIT, SWIGLU_LIMIT)
            mid = ((u + 1.0) * (g * _sigmoid(SWIGLU_ALPHA * g))).astype(BF16)
            _rows_to_tiles(ys_ref.at[tiles], _dot(mid, wbf[2]) + bd_ref[0])

    @pl.when(i >= n_valid)
    def _():
        ys_ref[...] = jnp.zeros_like(ys_ref)


def _experts(block_e, n_valid, segment, next_e, xs, wts, lay):
    bm = EXPERT_BLOCK
    tile_rows = bm * ROW_SUBLANES
    wspec = pl.BlockSpec(memory_space=pl.ANY)
    bspec = pl.BlockSpec((1, 1, D_FF), lambda i, be, *_: (be[i], 0, 0))
    grid_spec = pltpu.PrefetchScalarGridSpec(
        num_scalar_prefetch=4,
        grid=(lay.n_blocks,),
        in_specs=[pl.BlockSpec((tile_rows, LANES), lambda i, be, nv, *_: (jnp.minimum(i, nv[0] - 1), 0)),
                  wspec, wspec, wspec, bspec, bspec, bspec],
        out_specs=pl.BlockSpec((tile_rows, LANES), lambda i, *_: (i, 0)),
        scratch_shapes=[pltpu.VMEM((2, 3, D_MODEL, D_FF), F32), pltpu.VMEM((3, D_MODEL, D_FF), BF16),
                        pltpu.SemaphoreType.DMA((2,))],
    )
    return pl.pallas_call(
        _experts_kernel,
        grid_spec=grid_spec,
        out_shape=jax.ShapeDtypeStruct((lay.sorted_rows * ROW_SUBLANES, LANES), F32),
        compiler_params=_params(("arbitrary",)),
        name="experts",
    )(block_e, n_valid, segment, next_e, xs, wts['w_gate'], wts['w_up'], wts['w_down'], wts['b_gate'], wts['b_up'],
      wts['b_down'])


def _combine_kernel(soff_ref, scnt_ref, sgs_ref, h_ref, tw_ref, lpc_ref, pp_ref, ps_ref, ys_hbm, g3_ref, wpg_ref,
                    wpp_ref, gf_ref, yp_ref, ysm_ref, loc, sem, *, lay, n_sample):
    tm = lay.tile
    n_pairs = TOP_K * tm
    g = pl.program_id(0)
    slot = g % 2

    def fetch(tile, s):
        def strip(e, c):
            i = tile * N_EXPERTS + e
            _strip_copies(ys_hbm, sgs_ref[i], loc.at[s], soff_ref[i], scnt_ref[i], sem.at[s])
            return c
        lax.fori_loop(0, N_EXPERTS, strip, 0)

    @pl.when(g == 0)
    def _():
        fetch(0, 0)

    @pl.when(g + 1 < lay.n_tiles)
    def _():
        fetch(g + 1, 1 - slot)

    _wait_rows(ys_hbm, loc.at[slot], n_pairs, sem.at[slot])
    y_rows = _rows_from_tiles(loc.at[slot], n_pairs).astype(BF16)
    tw = tw_ref[...]
    lpc = lpc_ref[...]
    pos = lax.broadcasted_iota(I32, (tm, n_pairs), 1)
    weight = jnp.zeros((tm, n_pairs), F32)
    for k in range(TOP_K):
        weight = jnp.where(pos == lpc[:, k:k + 1], tw[:, k:k + 1], weight)
    y = _dot(weight.astype(BF16), y_rows)
    h2 = h_ref[...] + y
    gate = _sigmoid(_dot(_rms(h2, g3_ref[...]).astype(BF16), wpg_ref[...]))
    p = pp_ref[...]
    p = jnp.where(g == lay.n_tiles - 1, jnp.concatenate([p[:tm - n_sample], ps_ref[...]], axis=0), p)
    h3 = h2 + gate * _dot(p.astype(BF16), wpp_ref[...])
    out = _rms(h3, gf_ref[...])
    yp_ref[...] = out

    @pl.when(g == lay.n_tiles - 1)
    def _():
        ysm_ref[...] = out[tm - n_sample:, :]


def _combine(hbuf, tw, lp_cols, p_prompt, p_sample, ys, soff, scnt, sgs, wts, lay):
    tm = lay.tile
    n_prompt, n_sample = p_prompt.shape[0], p_sample.shape[0]
    assert 0 < n_sample <= tm and (lay.n_tiles - 1) * tm < n_prompt
    consts = [wts['g3'], wts['w_ple_gate'], wts['w_ple_proj'], wts['g_final']]
    row = lambda w: pl.BlockSpec((tm, w), lambda g, *_: (g, 0))
    grid_spec = pltpu.PrefetchScalarGridSpec(
        num_scalar_prefetch=3,
        grid=(lay.n_tiles,),
        in_specs=[row(D_MODEL), row(LANES), row(LANES), row(D_PLE),
                  pl.BlockSpec((n_sample, D_PLE), lambda g, *_: (0, 0)), pl.BlockSpec(memory_space=pl.ANY)]
        + [pl.BlockSpec(c.shape, lambda g, *_: (0, 0), pipeline_mode=pl.Buffered(1)) for c in consts],
        out_specs=[row(D_MODEL), pl.BlockSpec((n_sample, D_MODEL), lambda g, *_: (0, 0))],
        scratch_shapes=[pltpu.VMEM((2, TOP_K * tm * ROW_SUBLANES, LANES), F32), pltpu.SemaphoreType.DMA((2,))],
    )
    return pl.pallas_call(
        functools.partial(_combine_kernel, lay=lay, n_sample=n_sample),
        grid_spec=grid_spec,
        out_shape=[jax.ShapeDtypeStruct((n_prompt, D_MODEL), F32), jax.ShapeDtypeStruct((n_sample, D_MODEL), F32)],
        compiler_params=_params(("arbitrary",)),
        name="combine",
    )(soff, scnt, sgs, hbuf, tw, lp_cols, p_prompt, p_sample, ys, *consts)


def _prepare_weights(norm1_g, w_in, w_gk2, b_gk, gla_norm_g, v_norm_g, v_norm_b, w_sp, b_sp, w_o, norm2_g,
                     w_router, b_router, w_gate, b_gate, w_up, b_up, w_down, b_down, norm3_g, w_ple_gate,
                     w_ple_proj, final_g):
    o_gk = 2 * GLA_DK_T + 2 * GLA_DV_T
    w_a = w_in[:, :o_gk].astype(BF16)
    w_b = w_in[:, o_gk + GK_RANK:].astype(BF16)
    w_gk1 = jnp.pad(w_in[:, o_gk:o_gk + GK_RANK], ((0, 0), (0, LANES - GK_RANK))).astype(BF16)
    w_gk2p = jnp.pad(w_gk2, ((0, LANES - GK_RANK), (0, 0))).astype(BF16)
    row = lambda z: z.reshape(1, -1).astype(F32)
    return dict(
        g1=row(norm1_g), w_a=w_a, w_b=w_b, w_gk1=w_gk1, w_gk2=w_gk2p, b_gk=row(b_gk), gla_g=row(gla_norm_g),
        vn_g=row(v_norm_g), vn_b=row(v_norm_b), w_sp=w_sp, b_sp_t=b_sp.T,
        mix_w0=row(jnp.repeat(w_sp[:, 0, 0], MIX_DH)), mix_b0=row(jnp.repeat(b_sp[:, 0], MIX_DH)),
        w_o=w_o.astype(BF16), g2=row(norm2_g),
        w_router=jnp.pad(w_router, ((0, 0), (0, LANES - N_EXPERTS))),
        b_router=jnp.pad(row(b_router), ((0, 0), (0, LANES - N_EXPERTS))),
        w_gate=w_gate, w_up=w_up, w_down=w_down,
        b_gate=b_gate.reshape(N_EXPERTS, 1, D_FF), b_up=b_up.reshape(N_EXPERTS, 1, D_FF),
        b_down=b_down.reshape(N_EXPERTS, 1, D_MODEL),
        g3=row(norm3_g), w_ple_gate=w_ple_gate.astype(BF16), w_ple_proj=w_ple_proj.astype(BF16),
        g_final=row(final_g))


def _routing_tables(counts, counts_before, tile_counts, local_off, lay):
    bm = EXPERT_BLOCK
    padded = (counts + bm - 1) // bm * bm
    pad_end = jnp.cumsum(padded)
    pad_start = pad_end - padded
    global_start = pad_start[None, :] + counts_before
    n_valid = (pad_end[-1] // bm).astype(I32)
    blk = jnp.minimum(jnp.arange(lay.n_blocks, dtype=I32), n_valid - 1)
    block_e = jnp.sum(pad_end[None, :] <= (blk * bm)[:, None], axis=1).astype(I32)
    segment = jnp.cumsum(jnp.concatenate([jnp.zeros((1,), I32), (block_e[1:] != block_e[:-1]).astype(I32)]))
    follows = segment[None, :] == segment[:, None] + 1
    next_e = jnp.where(jnp.any(follows, axis=1), block_e[jnp.argmax(follows, axis=1)], -1).astype(I32)
    flat = lambda z: z.reshape(-1).astype(I32)
    return dict(soff=flat(local_off), scnt=flat(tile_counts), sgs=flat(global_start),
                pad_lo=(pad_start + counts).astype(I32), pad_hi=pad_end.astype(I32), block_e=block_e,
                segment=segment.astype(I32), next_e=next_e, n_valid=n_valid.reshape(1))


def _layer(x_prompt, x_sample, state, p_prompt, p_sample, wts):
    n_p, len_p, _ = x_prompt.shape
    n_s, len_s, _ = x_sample.shape
    assert len_s == 1, "the sample group carries one new token per sequence"
    t_p = n_p * len_p
    t_all = t_p + n_s

    xs = x_sample.reshape(n_s, D_MODEL)
    q, k, a, v, og, ob, vn = _mixer_sample_in(xs, wts)
    st_sample, o = _state_update(state, q, k, a, v)
    h_sample = _mixer_sample_out(xs, o, og, ob, wts)
    hbuf, st_prompt = _mixer_prompt(x_prompt, h_sample, wts)

    lay = _Layout(t_all)
    xp, top_w, lp_cols, lp_rows, counts_before, tile_counts, local_off, counts = _router(hbuf, wts)
    ne = N_EXPERTS
    tab = _routing_tables(counts[0, :ne], counts_before[:, 0, :ne], tile_counts[:, 0, :ne], local_off[:, 0, :ne], lay)
    x_sorted = _dispatch(xp, lp_rows, tab['soff'], tab['scnt'], tab['sgs'], tab['pad_lo'], tab['pad_hi'], lay)
    y_sorted = _experts(tab['block_e'], tab['n_valid'], tab['segment'], tab['next_e'], x_sorted, wts, lay)
    y_prompt, y_sample = _combine(hbuf, top_w, lp_cols, p_prompt.reshape(t_p, D_PLE), p_sample.reshape(n_s, D_PLE),
                                  y_sorted, tab['soff'], tab['scnt'], tab['sgs'], wts, lay)
    return (y_prompt.reshape(n_p, len_p, D_MODEL), y_sample.reshape(n_s, len_s, D_MODEL),
            st_prompt, st_sample, vn.reshape(n_s, len_s, D_GMLP))


def kernel(x_prompt, x_sample, state_gla, p_prompt, p_sample, norm1_g, w_in, w_gk2, b_gk, gla_norm_g, v_norm_g,
           v_norm_b, w_sp, b_sp, w_o, norm2_g, w_router, b_router, w_gate, b_gate, w_up, b_up, w_down, b_down,
           norm3_g, w_ple_gate, w_ple_proj, final_g):
    assert w_in.shape[0] == 1, "single-layer trunk"
    wts = _prepare_weights(norm1_g[0], w_in[0], w_gk2[0], b_gk[0], gla_norm_g[0], v_norm_g[0], v_norm_b[0],
                           w_sp[0], b_sp[0], w_o[0], norm2_g[0], w_router[0], b_router[0], w_gate[0], b_gate[0],
                           w_up[0], b_up[0], w_down[0], b_down[0], norm3_g[0], w_ple_gate[0], w_ple_proj[0],
                           final_g)
    y_p, y_s, st_p, st_s, vn_s = _layer(x_prompt, x_sample, state_gla[0], p_prompt[0], p_sample[0], wts)
    return (y_p, y_s, st_p[None], st_s[None].astype(state_gla.dtype), vn_s[None])
```

```python
import functools

import jax
import jax.numpy as jnp
from jax import lax
from jax.experimental import pallas as pl
from jax.experimental.pallas import tpu as pltpu

F32 = jnp.float32
BF16 = jnp.bfloat16
I32 = jnp.int32

D_MODEL = 1024
GLA_HEADS = 4
GLA_DK = 128
GLA_DV = 256
GLA_DK_T = GLA_HEADS * GLA_DK
GLA_DV_T = GLA_HEADS * GLA_DV
GK_RANK = 16
GATE_NORMALIZER = 16.0
GLA_CHUNK = 64
GLA_CHUNK_LOG2 = 6
D_GMLP = 1024
MIX_HEADS = 4
MIX_DH = D_GMLP // MIX_HEADS
MIX_CHUNK = 128
MIX_CHUNK_LOG2 = 7
assert 1 << GLA_CHUNK_LOG2 == GLA_CHUNK and 1 << MIX_CHUNK_LOG2 == MIX_CHUNK
N_EXPERTS = 32
TOP_K = 4
D_FF = 1024
SWIGLU_LIMIT = 7.0
SWIGLU_ALPHA = 1.702
D_PLE = 256
EPS = 1e-6

LANES = 128
ROW_SUBLANES = D_MODEL // LANES
VMEM_LIMIT_BYTES = 56 * 1024 * 1024

OFF_Q, OFF_K, OFF_V, OFF_R, OFF_U, OFF_VG, OFF_GA, OFF_GB, N_MAIN = (
    0, 512, 1024, 2048, 3072, 4096, 5120, 6144, 7168)

MIXER_TILE = 512
MIXER_PARTS = 2
STATE_TOKENS = 8
EXPERT_BLOCK = 768
EXPERT_PARTS = 3
WEIGHT_CAST_ROWS = 256
STRIP_CHUNK = 16
STRIP_CHUNK_LOG2 = 4
ZERO_ROWS = 64
ZERO_ROWS_LOG2 = 6
assert 1 << STRIP_CHUNK_LOG2 == STRIP_CHUNK and 1 << ZERO_ROWS_LOG2 == ZERO_ROWS


def _dot(a, b):
    return jnp.dot(a, b, preferred_element_type=F32)


def _dot_nt(a, b):
    return lax.dot_general(a, b, (((1,), (1,)), ((), ())), preferred_element_type=F32)


def _dot_tn(a, b):
    return lax.dot_general(a, b, (((0,), (0,)), ((), ())), preferred_element_type=F32)


def _rms(x, g):
    return x * lax.rsqrt(jnp.mean(x * x, axis=-1, keepdims=True) + EPS) * g


def _sigmoid(x):
    return 1.0 / (1.0 + jnp.exp(-x))


def _gelu(x):
    return 0.5 * x * (1.0 + lax.erf(x * (2.0 ** -0.5)))


def _log_sigmoid(x):
    return jnp.minimum(x, 0.0) - jnp.log1p(jnp.exp(-jnp.abs(x)))


def _split3(x):
    hi = x.astype(BF16)
    r1 = x - hi.astype(F32)
    mid = r1.astype(BF16)
    lo = (r1 - mid.astype(F32)).astype(BF16)
    return hi, mid, lo


def _const_spec(shape):
    nd = len(shape)
    return pl.BlockSpec(shape, lambda *_: (0,) * nd, pipeline_mode=pl.Buffered(1))


def _params(sem):
    return pltpu.CompilerParams(dimension_semantics=sem, vmem_limit_bytes=VMEM_LIMIT_BYTES)


def _main_cols(wa_ref, wb_ref, lo, hi):
    if hi <= OFF_U:
        return wa_ref[:, lo:hi]
    return wb_ref[:, lo - OFF_U:hi - OFF_U]


def _project_gla_inputs(n, wm, wgk1_ref, wgk2_ref, bgk_ref):
    q = _dot(n, wm(OFF_Q, OFF_K)) * (GLA_DK ** -0.5)
    k = _dot(n, wm(OFF_K, OFF_V))
    v = _dot(n, wm(OFF_V, OFF_R))
    gk = _dot(n, wgk1_ref[...]).astype(BF16)
    log_a = _log_sigmoid(_dot(gk, wgk2_ref[...]) + bgk_ref[...]) * (1.0 / GATE_NORMALIZER)
    return q, k, v, log_a


def _gmlp_inputs(n, wm, vng_ref, vnb_ref):
    u = _gelu(_dot(n, wm(OFF_U, OFF_VG)))
    vg = _gelu(_dot(n, wm(OFF_VG, OFF_GA)))
    mu = jnp.mean(vg, axis=-1, keepdims=True)
    vc = vg - mu
    var = jnp.mean(vc * vc, axis=-1, keepdims=True)
    vn = vc * lax.rsqrt(var + EPS) * vng_ref[...] + vnb_ref[...]
    ug = u * _sigmoid(_dot(n, wm(OFF_GB, N_MAIN)))
    return ug, vn


def _gla_out_gate(n, wm):
    r = _dot(n, wm(OFF_R, OFF_U))
    ga = _dot(n, wm(OFF_GA, OFF_GB))
    return r * _sigmoid(r) * _sigmoid(ga)


def _head_rms(o, g):
    return o * lax.rsqrt(jnp.mean(o * o, axis=-1, keepdims=True) + EPS) * g


def _mixer_prompt_kernel(x_ref, hs_ref, *refs, tm, n_seq, n_tiles):
    b = pl.program_id(0)
    t = pl.program_id(1)
    h_ref = refs[12]

    @pl.when(b < n_seq)
    def _():
        _mixer_prompt_tile(x_ref, *refs, tm=tm, n_tiles=n_tiles)

    @pl.when((b == n_seq) & (t == 0))
    def _():
        h_ref[0:hs_ref.shape[0], :] = hs_ref[...]


def _mixer_prompt_tile(x_ref, g1_ref, wa_ref, wb_ref, wgk1_ref, wgk2_ref, bgk_ref, glag_ref, vng_ref, vnb_ref,
                       wsp_ref, bspt_ref, wo_ref,
                       h_ref, st_ref,
                       n_scr, oa_scr, s_scr, *, tm, n_tiles):
    t = pl.program_id(1)

    @pl.when(t == 0)
    def _():
        s_scr[...] = jnp.zeros_like(s_scr)

    wm = functools.partial(_main_cols, wa_ref, wb_ref)
    pm = tm // MIXER_PARTS
    for part in range(MIXER_PARTS):
        rows = pl.ds(part * pm, pm)
        _mixer_prompt_part(x_ref.at[0, rows], g1_ref, wm, wgk1_ref, wgk2_ref, bgk_ref, glag_ref, vng_ref, vnb_ref,
                           wsp_ref, bspt_ref, wo_ref, h_ref.at[rows], n_scr.at[rows], oa_scr.at[rows], s_scr, tm=pm)

    @pl.when(t == n_tiles - 1)
    def _():
        for h in range(GLA_HEADS):
            st_ref[0, h] = s_scr[h].T


def _mixer_prompt_part(x_ref, g1_ref, wm, wgk1_ref, wgk2_ref, bgk_ref, glag_ref, vng_ref, vnb_ref,
                       wsp_ref, bspt_ref, wo_ref, h_ref, n_scr, oa_scr, s_scr, *, tm):
    x = x_ref[...]
    n = _rms(x, g1_ref[...]).astype(BF16)
    n_scr[...] = n
    q, k, v, log_a = _project_gla_inputs(n, wm, wgk1_ref, wgk2_ref, bgk_ref)
    v = v.astype(BF16)

    row = lax.broadcasted_iota(I32, (tm, tm), 0)
    col = lax.broadcasted_iota(I32, (tm, tm), 1)
    same_chunk = lax.shift_right_logical(row, GLA_CHUNK_LOG2) == lax.shift_right_logical(col, GLA_CHUNK_LOG2)
    causal = same_chunk & (row >= col)
    tri = causal.astype(BF16)
    blk = same_chunk.astype(BF16)
    hi, mid, lo = _split3(log_a)
    b = _dot(tri, hi) + _dot(tri, mid) + _dot(tri, lo)
    b_end = _dot(blk, hi) + _dot(blk, mid) + _dot(blk, lo)
    qe = (q * jnp.exp(b)).astype(BF16)
    ke = (k * jnp.exp(-b)).astype(BF16)
    kd = (k * jnp.exp(b_end - b)).astype(BF16)
    dec = jnp.exp(b_end)
    glag = glag_ref[...]
    for h in range(GLA_HEADS):
        ks = slice(h * GLA_DK, (h + 1) * GLA_DK)
        vs = slice(h * GLA_DV, (h + 1) * GLA_DV)
        vh = v[:, vs]
        att = jnp.where(causal, _dot_nt(qe[:, ks], ke[:, ks]), 0.0).astype(BF16)
        o_intra = _dot(att, vh)
        s_t = s_scr[h]
        o_inter = []
        for c in range(tm // GLA_CHUNK):
            rs = slice(c * GLA_CHUNK, (c + 1) * GLA_CHUNK)
            o_inter.append(_dot_nt(qe[rs, ks], s_t.astype(BF16)))
            s_t = s_t * dec[c * GLA_CHUNK:c * GLA_CHUNK + 1, ks] + _dot_tn(vh[rs], kd[rs, ks])
        s_scr[h] = s_t
        oa_scr[:, vs] = _head_rms(o_intra + jnp.concatenate(o_inter, axis=0), glag)

    n = n_scr[...]
    oa_scr[...] = oa_scr[...] * _gla_out_gate(n, wm)
    ug, vn = _gmlp_inputs(n, wm, vng_ref, vnb_ref)
    vn = vn.astype(BF16)
    reps = tm // MIX_CHUNK
    mix_mask = (lax.shift_right_logical(row, MIX_CHUNK_LOG2) == lax.shift_right_logical(col, MIX_CHUNK_LOG2)) & (
        row >= col)
    for h in range(MIX_HEADS):
        w = jnp.where(mix_mask, jnp.tile(wsp_ref[h], (reps, reps)), 0.0).astype(BF16)
        cs = slice(h * MIX_DH, (h + 1) * MIX_DH)
        mix = _dot(w, vn[:, cs]) + jnp.tile(bspt_ref[:, h:h + 1], (reps, 1))
        oa_scr[:, cs] = oa_scr[:, cs] + ug[:, cs] * mix

    h_ref[...] = x + _dot(oa_scr[...].astype(BF16), wo_ref[...])


def _mixer_prompt(x, h_sample, wts):
    n_seq, seq_len, _ = x.shape
    n_s = h_sample.shape[0]
    tm = min(MIXER_TILE, seq_len)
    assert seq_len % tm == 0 and tm % MIX_CHUNK == 0 and n_s <= tm
    n_tiles = seq_len // tm
    consts = [wts['g1'], wts['w_a'], wts['w_b'], wts['w_gk1'], wts['w_gk2'], wts['b_gk'], wts['gla_g'], wts['vn_g'],
              wts['vn_b'], wts['w_sp'], wts['b_sp_t'], wts['w_o']]
    kern = functools.partial(_mixer_prompt_kernel, tm=tm, n_seq=n_seq, n_tiles=n_tiles)
    last = n_seq - 1

    def prompt_tile(b, t):
        return jnp.minimum(b, last), jnp.where(b < n_seq, t, n_tiles - 1)

    def x_map(b, t):
        bb, tt = prompt_tile(b, t)
        return bb, tt, 0

    def h_map(b, t):
        return jnp.where(b < n_seq, b * n_tiles + t, n_seq * n_tiles), 0

    return pl.pallas_call(
        kern,
        grid=(n_seq + 1, n_tiles),
        in_specs=[pl.BlockSpec((1, tm, D_MODEL), x_map), _const_spec(h_sample.shape)]
        + [_const_spec(c.shape) for c in consts],
        out_specs=[pl.BlockSpec((tm, D_MODEL), h_map),
                   pl.BlockSpec((1, GLA_HEADS, GLA_DK, GLA_DV), lambda b, t: (jnp.minimum(b, last), 0, 0, 0))],
        out_shape=[jax.ShapeDtypeStruct((n_seq * seq_len + n_s, D_MODEL), F32),
                   jax.ShapeDtypeStruct((n_seq, GLA_HEADS, GLA_DK, GLA_DV), F32)],
        scratch_shapes=[pltpu.VMEM((tm, D_MODEL), BF16),
                        pltpu.VMEM((tm, D_MODEL), F32),
                        pltpu.VMEM((GLA_HEADS, GLA_DV, GLA_DK), F32)],
        compiler_params=_params(("arbitrary", "arbitrary")),
        name="mixer_prompt",
    )(x, h_sample, *consts)


def _mixer_sample_in_kernel(x_ref, g1_ref, wa_ref, wb_ref, wgk1_ref, wgk2_ref, bgk_ref, vng_ref, vnb_ref,
                            mixw_ref, mixb_ref,
                            q_ref, k_ref, a_ref, v_ref, og_ref, ob_ref, vn_ref):
    wm = functools.partial(_main_cols, wa_ref, wb_ref)
    n = _rms(x_ref[...], g1_ref[...]).astype(BF16)
    q, k, v, log_a = _project_gla_inputs(n, wm, wgk1_ref, wgk2_ref, bgk_ref)
    q_ref[...] = q
    k_ref[...] = k
    a_ref[...] = jnp.exp(log_a)
    v_ref[...] = v
    og_ref[...] = _gla_out_gate(n, wm)
    ug, vn = _gmlp_inputs(n, wm, vng_ref, vnb_ref)
    vn_ref[...] = vn
    ob_ref[...] = ug * (mixw_ref[...] * vn + mixb_ref[...])


def _mixer_sample_in(x2, wts):
    n_seq = x2.shape[0]
    consts = [wts['g1'], wts['w_a'], wts['w_b'], wts['w_gk1'], wts['w_gk2'], wts['b_gk'], wts['vn_g'], wts['vn_b'],
              wts['mix_w0'], wts['mix_b0']]
    widths = [GLA_DK_T, GLA_DK_T, GLA_DK_T, GLA_DV_T, D_MODEL, D_MODEL, D_GMLP]
    return pl.pallas_call(
        _mixer_sample_in_kernel,
        grid=(1,),
        in_specs=[_const_spec(x2.shape)] + [_const_spec(c.shape) for c in consts],
        out_specs=[pl.BlockSpec((n_seq, w), lambda i: (0, 0)) for w in widths],
        out_shape=[jax.ShapeDtypeStruct((n_seq, w), F32) for w in widths],
        compiler_params=_params(("arbitrary",)),
        name="mixer_sample_in",
    )(x2, *consts)


def _state_update_kernel(s_ref, q_ref, k_ref, a_ref, v_ref, so_ref, o_ref):
    for j in range(STATE_TOKENS):
        for h in range(GLA_HEADS):
            ks = slice(h * GLA_DK, (h + 1) * GLA_DK)
            vs = slice(h * GLA_DV, (h + 1) * GLA_DV)
            s_new = a_ref[0, ks, j:j + 1] * s_ref[j, h] + k_ref[0, ks, j:j + 1] * v_ref[j:j + 1, vs]
            so_ref[j, h] = s_new
            o_ref[j:j + 1, vs] = jnp.sum(q_ref[0, ks, j:j + 1] * s_new, axis=0, keepdims=True)


def _state_update(state, q, k, a, v):
    n_seq = state.shape[0]
    tb = STATE_TOKENS
    assert n_seq % tb == 0

    def cols(z):
        return z.reshape(n_seq // tb, tb, GLA_DK_T).transpose(0, 2, 1)

    col_spec = pl.BlockSpec((1, GLA_DK_T, tb), lambda i: (i, 0, 0))
    st_spec = pl.BlockSpec((tb, GLA_HEADS, GLA_DK, GLA_DV), lambda i: (i, 0, 0, 0))
    row_spec = pl.BlockSpec((tb, GLA_DV_T), lambda i: (i, 0))
    return pl.pallas_call(
        _state_update_kernel,
        grid=(n_seq // tb,),
        in_specs=[st_spec, col_spec, col_spec, col_spec, row_spec],
        out_specs=[st_spec, row_spec],
        out_shape=[jax.ShapeDtypeStruct(state.shape, F32), jax.ShapeDtypeStruct((n_seq, GLA_DV_T), F32)],
        compiler_params=_params(("arbitrary",)),
        name="state_update",
    )(state, cols(q), cols(k), cols(a), v)


def _mixer_sample_out_kernel(x_ref, o_ref, og_ref, ob_ref, glag_ref, wo_ref, h_ref):
    glag = glag_ref[...]
    parts = []
    for h in range(GLA_HEADS):
        vs = slice(h * GLA_DV, (h + 1) * GLA_DV)
        parts.append(_head_rms(o_ref[:, vs], glag))
    merged = jnp.concatenate(parts, axis=1) * og_ref[...] + ob_ref[...]
    h_ref[...] = x_ref[...] + _dot(merged.astype(BF16), wo_ref[...])


def _mixer_sample_out(x2, o, og, ob, wts):
    n_seq = x2.shape[0]
    consts = [x2, o, og, ob, wts['gla_g'], wts['w_o']]
    return pl.pallas_call(
        _mixer_sample_out_kernel,
        grid=(1,),
        in_specs=[_const_spec(c.shape) for c in consts],
        out_specs=pl.BlockSpec((n_seq, D_MODEL), lambda i: (0, 0)),
        out_shape=jax.ShapeDtypeStruct((n_seq, D_MODEL), F32),
        compiler_params=_params(("arbitrary",)),
        name="mixer_sample_out",
    )(*consts)


def _router_kernel(h_ref, g2_ref, wr_ref, br_ref, xp_ref, tw_ref, lpc_ref, lpr_ref, cb_ref, tc_ref, lo_ref, cnt_ref, cnt_scr,
                   *, tm):
    i = pl.program_id(0)

    @pl.when(i == 0)
    def _():
        cnt_scr[...] = jnp.zeros_like(cnt_scr)

    cb_ref[0] = cnt_scr[...].astype(I32)

    hn = _rms(h_ref[...], g2_ref[...])
    hn_hi = hn.astype(BF16)
    hn_lo = (hn - hn_hi.astype(F32)).astype(BF16)
    w = wr_ref[...]
    w_hi = w.astype(BF16)
    w_lo = (w - w_hi.astype(F32)).astype(BF16)
    logits = _dot(hn_hi, w_hi) + _dot(hn_lo, w_hi) + _dot(hn_hi, w_lo) + br_ref[...]

    _rows_to_tiles(xp_ref, hn)

    lane = lax.broadcasted_iota(I32, (tm, LANES), 1)
    neg = jnp.float32(-jnp.inf)
    l = jnp.where(lane < N_EXPERTS, logits, neg)
    vals, hots = [], []
    for _ in range(TOP_K):
        m = jnp.max(l, axis=-1, keepdims=True)
        idx = jnp.min(jnp.where(l == m, lane, LANES), axis=-1, keepdims=True)
        hot = lane == idx
        l = jnp.where(hot, neg, l)
        vals.append(m)
        hots.append(hot)
    exps = [jnp.exp(v - vals[0]) for v in vals]
    denom = exps[0] + exps[1] + exps[2] + exps[3]

    member = (hots[0] | hots[1] | hots[2] | hots[3]).astype(BF16)
    row = lax.broadcasted_iota(I32, (tm, tm), 0)
    col = lax.broadcasted_iota(I32, (tm, tm), 1)
    in_tile = _dot((row > col).astype(BF16), member)
    tile_cnt = jnp.sum(member.astype(F32), axis=0, keepdims=True)
    e_row = lax.broadcasted_iota(I32, (LANES, LANES), 0)
    e_col = lax.broadcasted_iota(I32, (LANES, LANES), 1)
    lower = (e_row < e_col).astype(BF16)
    c_hi, c_mid, c_lo = _split3(jnp.broadcast_to(tile_cnt, (ROW_SUBLANES, LANES)))
    local_off = (_dot(c_hi, lower) + _dot(c_mid, lower) + _dot(c_lo, lower))[0:1, :]
    tc_ref[0] = tile_cnt.astype(I32)
    lo_ref[0] = local_off.astype(I32)
    cnt_scr[...] = cnt_scr[...] + tile_cnt
    cnt_ref[...] = cnt_scr[...].astype(I32)

    place = local_off + in_tile
    tw = jnp.zeros((tm, LANES), F32)
    lp = jnp.zeros((tm, LANES), F32)
    for kk in range(TOP_K):
        sel = lane == kk
        tw = jnp.where(sel, exps[kk] / denom, tw)
        lp = jnp.where(sel, jnp.sum(jnp.where(hots[kk], place, 0.0), axis=-1, keepdims=True), lp)
    tw_ref[...] = tw
    lpc_ref[...] = lp.astype(I32)
    lpr_ref[0] = lp.T[0:ROW_SUBLANES, :].astype(I32)


def _router_tile(t_all):
    for tm in (512, 384, 256, 128):
        if t_all % tm == 0:
            return tm
    raise ValueError(f"token count {t_all} must be a multiple of 128")


def _router(hbuf, wts):
    t_all = hbuf.shape[0]
    tm = _router_tile(t_all)
    consts = [wts['g2'], wts['w_router'], wts['b_router']]
    row = lambda w: pl.BlockSpec((tm, w), lambda i: (i, 0))
    n_tiles = t_all // tm
    per_tile = pl.BlockSpec((1, 1, LANES), lambda i: (i, 0, 0))
    tile_vec = jax.ShapeDtypeStruct((n_tiles, 1, LANES), I32)
    return pl.pallas_call(
        functools.partial(_router_kernel, tm=tm),
        grid=(t_all // tm,),
        in_specs=[row(D_MODEL)] + [_const_spec(c.shape) for c in consts],
        out_specs=[pl.BlockSpec((tm * ROW_SUBLANES, LANES), lambda i: (i, 0)), row(LANES), row(LANES),
                   pl.BlockSpec((1, ROW_SUBLANES, tm), lambda i: (i, 0, 0)),
                   per_tile, per_tile, per_tile, pl.BlockSpec((1, LANES), lambda i: (0, 0))],
        out_shape=[jax.ShapeDtypeStruct((t_all * ROW_SUBLANES, LANES), F32),
                   jax.ShapeDtypeStruct((t_all, LANES), F32),
                   jax.ShapeDtypeStruct((t_all, LANES), I32),
                   jax.ShapeDtypeStruct((n_tiles, ROW_SUBLANES, tm), I32),
                   tile_vec, tile_vec, tile_vec, jax.ShapeDtypeStruct((1, LANES), I32)],
        scratch_shapes=[pltpu.VMEM((1, LANES), F32)],
        compiler_params=_params(("arbitrary",)),
        name="router",
    )(hbuf, *consts)


def _rows_from_tiles(ref, n_rows):
    return jnp.concatenate([ref[pl.ds(s, n_rows, stride=ROW_SUBLANES), :] for s in range(ROW_SUBLANES)], axis=1)


def _rows_to_tiles(ref, x):
    for s in range(ROW_SUBLANES):
        ref[pl.ds(s, x.shape[0], stride=ROW_SUBLANES), :] = x[:, s * LANES:(s + 1) * LANES]


def _tiles(row, n_rows):
    return pl.ds(pl.multiple_of(row * ROW_SUBLANES, ROW_SUBLANES), n_rows * ROW_SUBLANES)


class _Layout:
    def __init__(self, t_all):
        bm = EXPERT_BLOCK
        self.t_all = t_all
        self.tile = _router_tile(t_all)
        self.n_tiles = t_all // self.tile
        self.n_blocks = -(-(t_all * TOP_K + N_EXPERTS * (bm - 1)) // bm)
        self.sorted_rows = self.n_blocks * bm


def _strip_copies(src_ref, src_row, dst_ref, dst_row, n, sem):
    def copy(offset, size):
        pltpu.make_async_copy(src_ref.at[_tiles(src_row + offset, size)], dst_ref.at[_tiles(dst_row + offset, size)],
                              sem).start()

    n_chunks = lax.shift_right_logical(n, STRIP_CHUNK_LOG2)

    def chunk(j, c):
        copy(j * STRIP_CHUNK, STRIP_CHUNK)
        return c
    lax.fori_loop(0, n_chunks, chunk, 0)

    done = n_chunks * STRIP_CHUNK
    size = STRIP_CHUNK // 2
    while size >= 1:
        part = n & size

        @pl.when(part != 0)
        def _(done=done, size=size):
            copy(done, size)
        done = done + part
        size //= 2


def _wait_rows(hbm_ref, vmem_ref, n_rows, sem):
    pltpu.make_async_copy(hbm_ref.at[pl.ds(0, n_rows * ROW_SUBLANES)], vmem_ref.at[pl.ds(0, n_rows * ROW_SUBLANES)],
                          sem).wait()


def _dispatch_kernel(soff_ref, scnt_ref, sgs_ref, lo_ref, hi_ref, xp_ref, lpr_ref, xs_hbm, loc, zero_scr, sem,
                     *, lay):
    tm = lay.tile
    n_pairs = TOP_K * tm
    g = pl.program_id(0)
    slot = g % 2

    def zero_fill(lo, hi, wait):
        def copies(first, count, size):
            def body(j, c):
                cp = pltpu.make_async_copy(zero_scr.at[pl.ds(0, size * ROW_SUBLANES)],
                                           xs_hbm.at[_tiles(first + j * size, size)], sem.at[2])
                if wait:
                    cp.wait()
                else:
                    cp.start()
                return c
            lax.fori_loop(0, count, body, 0)
        n = hi - lo
        n_big = lax.shift_right_logical(n, ZERO_ROWS_LOG2)
        copies(lo, n_big, ZERO_ROWS)
        done = n_big * ZERO_ROWS
        n_mid = lax.shift_right_logical(n - done, STRIP_CHUNK_LOG2)
        copies(lo + done, n_mid, STRIP_CHUNK)
        done = done + n_mid * STRIP_CHUNK
        copies(lo + done, n - done, 1)

    @pl.when(g == 0)
    def _():
        zero_scr[...] = jnp.zeros_like(zero_scr)
        for wait in (False, True):
            def expert(e, c, wait=wait):
                zero_fill(lo_ref[e], hi_ref[e], wait)
                return c
            lax.fori_loop(0, N_EXPERTS, expert, 0)
            zero_fill(hi_ref[N_EXPERTS - 1], lay.sorted_rows, wait)

    @pl.when(g >= 2)
    def _():
        _wait_rows(xs_hbm, loc.at[slot], n_pairs, sem.at[slot])

    x = _rows_from_tiles(xp_ref, tm).astype(BF16)
    pos = lax.broadcasted_iota(I32, (n_pairs, tm), 0)
    lpr = lpr_ref[0]
    hit = pos == lpr[0:1, :]
    for k in range(1, TOP_K):
        hit = hit | (pos == lpr[k:k + 1, :])
    _rows_to_tiles(loc.at[slot], _dot(hit.astype(BF16), x))

    def strip(e, c):
        s = g * N_EXPERTS + e
        _strip_copies(loc.at[slot], soff_ref[s], xs_hbm, sgs_ref[s], scnt_ref[s], sem.at[slot])
        return c
    lax.fori_loop(0, N_EXPERTS, strip, 0)

    @pl.when(g == lay.n_tiles - 1)
    def _():
        _wait_rows(xs_hbm, loc.at[slot], n_pairs, sem.at[slot])
        if lay.n_tiles > 1:
            _wait_rows(xs_hbm, loc.at[1 - slot], n_pairs, sem.at[1 - slot])


def _dispatch(xp, lp_rows, soff, scnt, sgs, pad_lo, pad_hi, lay):
    tm = lay.tile
    grid_spec = pltpu.PrefetchScalarGridSpec(
        num_scalar_prefetch=5,
        grid=(lay.n_tiles,),
        in_specs=[pl.BlockSpec((tm * ROW_SUBLANES, LANES), lambda g, *_: (g, 0)),
                  pl.BlockSpec((1, ROW_SUBLANES, tm), lambda g, *_: (g, 0, 0))],
        out_specs=pl.BlockSpec(memory_space=pl.ANY),
        scratch_shapes=[pltpu.VMEM((2, TOP_K * tm * ROW_SUBLANES, LANES), F32),
                        pltpu.VMEM((ZERO_ROWS * ROW_SUBLANES, LANES), F32),
                        pltpu.SemaphoreType.DMA((3,))],
    )
    return pl.pallas_call(
        functools.partial(_dispatch_kernel, lay=lay),
        grid_spec=grid_spec,
        out_shape=jax.ShapeDtypeStruct((lay.sorted_rows * ROW_SUBLANES, LANES), F32),
        compiler_params=_params(("arbitrary",)),
        name="dispatch",
    )(soff, scnt, sgs, pad_lo, pad_hi, xp, lp_rows)


def _experts_kernel(be_ref, nv_ref, seg_ref, nx_ref, xs_ref, wg_hbm, wu_hbm, wd_hbm, bg_ref, bu_ref, bd_ref, ys_ref,
                    wf32, wbf, sem):
    bm = EXPERT_BLOCK
    i = pl.program_id(0)
    n_valid = nv_ref[0]

    def weight_copies(expert, slot):
        return [pltpu.make_async_copy(w_hbm.at[expert], wf32.at[slot, w], sem.at[slot])
                for w, w_hbm in enumerate((wg_hbm, wu_hbm, wd_hbm))]

    @pl.when(i < n_valid)
    def _():
        @pl.when((i == 0) | (be_ref[i] != be_ref[jnp.maximum(i - 1, 0)]))
        def _():
            slot = seg_ref[i] % 2

            @pl.when(i == 0)
            def _():
                for cp in weight_copies(be_ref[0], 0):
                    cp.start()

            for cp in weight_copies(be_ref[i], slot):
                cp.wait()

            @pl.when(nx_ref[i] >= 0)
            def _():
                for cp in weight_copies(nx_ref[i], 1 - slot):
                    cp.start()

            for w in range(3):
                for c in range(0, D_MODEL, WEIGHT_CAST_ROWS):
                    wbf[w, c:c + WEIGHT_CAST_ROWS, :] = wf32[slot, w, c:c + WEIGHT_CAST_ROWS, :].astype(BF16)

        part = bm // EXPERT_PARTS
        for j in range(EXPERT_PARTS):
            tiles = pl.ds(j * part * ROW_SUBLANES, part * ROW_SUBLANES)
            x = _rows_from_tiles(xs_ref.at[tiles], part).astype(BF16)
            g = jnp.minimum(_dot(x, wbf[0]) + bg_ref[0], SWIGLU_LIMIT)
            u = jnp.clip(_dot(x, wbf[1]) + bu_ref[0], -SWIGLU_LIMIT, SWIGLU_LIMIT)
            mid = ((u + 1.0) * (g * _sigmoid(SWIGLU_ALPHA * g))).astype(BF16)
            _rows_to_tiles(ys_ref.at[tiles], _dot(mid, wbf[2]) + bd_ref[0])


def _experts(block_e, n_valid, segment, next_e, xs, wts, lay):
    bm = EXPERT_BLOCK
    tile_rows = bm * ROW_SUBLANES
    wspec = pl.BlockSpec(memory_space=pl.ANY)
    bspec = pl.BlockSpec((1, 1, D_FF), lambda i, be, *_: (be[i], 0, 0))
    grid_spec = pltpu.PrefetchScalarGridSpec(
        num_scalar_prefetch=4,
        grid=(lay.n_blocks,),
        in_specs=[pl.BlockSpec((tile_rows, LANES), lambda i, be, nv, *_: (jnp.minimum(i, nv[0] - 1), 0)),
                  wspec, wspec, wspec, bspec, bspec, bspec],
        out_specs=pl.BlockSpec((tile_rows, LANES), lambda i, be, nv, *_: (jnp.minimum(i, nv[0] - 1), 0)),
        scratch_shapes=[pltpu.VMEM((2, 3, D_MODEL, D_FF), F32), pltpu.VMEM((3, D_MODEL, D_FF), BF16),
                        pltpu.SemaphoreType.DMA((2,))],
    )
    return pl.pallas_call(
        _experts_kernel,
        grid_spec=grid_spec,
        out_shape=jax.ShapeDtypeStruct((lay.sorted_rows * ROW_SUBLANES, LANES), F32),
        input_output_aliases={4: 0},
        compiler_params=_params(("arbitrary",)),
        name="experts",
    )(block_e, n_valid, segment, next_e, xs, wts['w_gate'], wts['w_up'], wts['w_down'], wts['b_gate'], wts['b_up'],
      wts['b_down'])


def _combine_kernel(soff_ref, scnt_ref, sgs_ref, h_ref, tw_ref, lpc_ref, pp_ref, ps_ref, ys_hbm, g3_ref, wpg_ref,
                    wpp_ref, gf_ref, yp_ref, ysm_ref, loc, sem, *, lay, n_sample):
    tm = lay.tile
    n_pairs = TOP_K * tm
    g = pl.program_id(0)
    slot = g % 2

    def fetch(tile, s):
        def strip(e, c):
            i = tile * N_EXPERTS + e
            _strip_copies(ys_hbm, sgs_ref[i], loc.at[s], soff_ref[i], scnt_ref[i], sem.at[s])
            return c
        lax.fori_loop(0, N_EXPERTS, strip, 0)

    @pl.when(g == 0)
    def _():
        fetch(0, 0)

    @pl.when(g + 1 < lay.n_tiles)
    def _():
        fetch(g + 1, 1 - slot)

    _wait_rows(ys_hbm, loc.at[slot], n_pairs, sem.at[slot])
    y_rows = _rows_from_tiles(loc.at[slot], n_pairs).astype(BF16)
    tw = tw_ref[...]
    lpc = lpc_ref[...]
    pos = lax.broadcasted_iota(I32, (tm, n_pairs), 1)
    weight = jnp.zeros((tm, n_pairs), F32)
    for k in range(TOP_K):
        weight = jnp.where(pos == lpc[:, k:k + 1], tw[:, k:k + 1], weight)
    y = _dot(weight.astype(BF16), y_rows)
    h2 = h_ref[...] + y
    gate = _sigmoid(_dot(_rms(h2, g3_ref[...]).astype(BF16), wpg_ref[...]))
    p = pp_ref[...]
    p = jnp.where(g == lay.n_tiles - 1, jnp.concatenate([p[:tm - n_sample], ps_ref[...]], axis=0), p)
    h3 = h2 + gate * _dot(p.astype(BF16), wpp_ref[...])
    out = _rms(h3, gf_ref[...])
    yp_ref[...] = out

    @pl.when(g == lay.n_tiles - 1)
    def _():
        ysm_ref[...] = out[tm - n_sample:, :]


def _combine(hbuf, tw, lp_cols, p_prompt, p_sample, ys, soff, scnt, sgs, wts, lay):
    tm = lay.tile
    n_prompt, n_sample = p_prompt.shape[0], p_sample.shape[0]
    assert 0 < n_sample <= tm and (lay.n_tiles - 1) * tm < n_prompt
    consts = [wts['g3'], wts['w_ple_gate'], wts['w_ple_proj'], wts['g_final']]
    row = lambda w: pl.BlockSpec((tm, w), lambda g, *_: (g, 0))
    grid_spec = pltpu.PrefetchScalarGridSpec(
        num_scalar_prefetch=3,
        grid=(lay.n_tiles,),
        in_specs=[row(D_MODEL), row(LANES), row(LANES), row(D_PLE),
                  pl.BlockSpec((n_sample, D_PLE), lambda g, *_: (0, 0)), pl.BlockSpec(memory_space=pl.ANY)]
        + [pl.BlockSpec(c.shape, lambda g, *_: (0, 0), pipeline_mode=pl.Buffered(1)) for c in consts],
        out_specs=[row(D_MODEL), pl.BlockSpec((n_sample, D_MODEL), lambda g, *_: (0, 0))],
        scratch_shapes=[pltpu.VMEM((2, TOP_K * tm * ROW_SUBLANES, LANES), F32), pltpu.SemaphoreType.DMA((2,))],
    )
    return pl.pallas_call(
        functools.partial(_combine_kernel, lay=lay, n_sample=n_sample),
        grid_spec=grid_spec,
        out_shape=[jax.ShapeDtypeStruct((n_prompt, D_MODEL), F32), jax.ShapeDtypeStruct((n_sample, D_MODEL), F32)],
        compiler_params=_params(("arbitrary",)),
        name="combine",
    )(soff, scnt, sgs, hbuf, tw, lp_cols, p_prompt, p_sample, ys, *consts)


def _prepare_weights(norm1_g, w_in, w_gk2, b_gk, gla_norm_g, v_norm_g, v_norm_b, w_sp, b_sp, w_o, norm2_g,
                     w_router, b_router, w_gate, b_gate, w_up, b_up, w_down, b_down, norm3_g, w_ple_gate,
                     w_ple_proj, final_g):
    o_gk = 2 * GLA_DK_T + 2 * GLA_DV_T
    w_a = w_in[:, :o_gk].astype(BF16)
    w_b = w_in[:, o_gk + GK_RANK:].astype(BF16)
    w_gk1 = jnp.pad(w_in[:, o_gk:o_gk + GK_RANK], ((0, 0), (0, LANES - GK_RANK))).astype(BF16)
    w_gk2p = jnp.pad(w_gk2, ((0, LANES - GK_RANK), (0, 0))).astype(BF16)
    row = lambda z: z.reshape(1, -1).astype(F32)
    return dict(
        g1=row(norm1_g), w_a=w_a, w_b=w_b, w_gk1=w_gk1, w_gk2=w_gk2p, b_gk=row(b_gk), gla_g=row(gla_norm_g),
        vn_g=row(v_norm_g), vn_b=row(v_norm_b), w_sp=w_sp, b_sp_t=b_sp.T,
        mix_w0=row(jnp.repeat(w_sp[:, 0, 0], MIX_DH)), mix_b0=row(jnp.repeat(b_sp[:, 0], MIX_DH)),
        w_o=w_o.astype(BF16), g2=row(norm2_g),
        w_router=jnp.pad(w_router, ((0, 0), (0, LANES - N_EXPERTS))),
        b_router=jnp.pad(row(b_router), ((0, 0), (0, LANES - N_EXPERTS))),
        w_gate=w_gate, w_up=w_up, w_down=w_down,
        b_gate=b_gate.reshape(N_EXPERTS, 1, D_FF), b_up=b_up.reshape(N_EXPERTS, 1, D_FF),
        b_down=b_down.reshape(N_EXPERTS, 1, D_MODEL),
        g3=row(norm3_g), w_ple_gate=w_ple_gate.astype(BF16), w_ple_proj=w_ple_proj.astype(BF16),
        g_final=row(final_g))


def _routing_tables(counts, counts_before, tile_counts, local_off, lay):
    bm = EXPERT_BLOCK
    padded = (counts + bm - 1) // bm * bm
    pad_end = jnp.cumsum(padded)
    pad_start = pad_end - padded
    global_start = pad_start[None, :] + counts_before
    n_valid = (pad_end[-1] // bm).astype(I32)
    blk = jnp.minimum(jnp.arange(lay.n_blocks, dtype=I32), n_valid - 1)
    block_e = jnp.sum(pad_end[None, :] <= (blk * bm)[:, None], axis=1).astype(I32)
    segment = jnp.cumsum(jnp.concatenate([jnp.zeros((1,), I32), (block_e[1:] != block_e[:-1]).astype(I32)]))
    follows = segment[None, :] == segment[:, None] + 1
    next_e = jnp.where(jnp.any(follows, axis=1), block_e[jnp.argmax(follows, axis=1)], -1).astype(I32)
    flat = lambda z: z.reshape(-1).astype(I32)
    return dict(soff=flat(local_off), scnt=flat(tile_counts), sgs=flat(global_start),
                pad_lo=(pad_start + counts).astype(I32), pad_hi=pad_end.astype(I32), block_e=block_e,
                segment=segment.astype(I32), next_e=next_e, n_valid=n_valid.reshape(1))


def _layer(x_prompt, x_sample, state, p_prompt, p_sample, wts):
    n_p, len_p, _ = x_prompt.shape
    n_s, len_s, _ = x_sample.shape
    assert len_s == 1, "the sample group carries one new token per sequence"
    t_p = n_p * len_p
    t_all = t_p + n_s

    xs = x_sample.reshape(n_s, D_MODEL)
    q, k, a, v, og, ob, vn = _mixer_sample_in(xs, wts)
    st_sample, o = _state_update(state, q, k, a, v)
    h_sample = _mixer_sample_out(xs, o, og, ob, wts)
    hbuf, st_prompt = _mixer_prompt(x_prompt, h_sample, wts)

    lay = _Layout(t_all)
    xp, top_w, lp_cols, lp_rows, counts_before, tile_counts, local_off, counts = _router(hbuf, wts)
    ne = N_EXPERTS
    tab = _routing_tables(counts[0, :ne], counts_before[:, 0, :ne], tile_counts[:, 0, :ne], local_off[:, 0, :ne], lay)
    x_sorted = _dispatch(xp, lp_rows, tab['soff'], tab['scnt'], tab['sgs'], tab['pad_lo'], tab['pad_hi'], lay)
    y_sorted = _experts(tab['block_e'], tab['n_valid'], tab['segment'], tab['next_e'], x_sorted, wts, lay)
    y_prompt, y_sample = _combine(hbuf, top_w, lp_cols, p_prompt.reshape(t_p, D_PLE), p_sample.reshape(n_s, D_PLE),
                                  y_sorted, tab['soff'], tab['scnt'], tab['sgs'], wts, lay)
    return (y_prompt.reshape(n_p, len_p, D_MODEL), y_sample.reshape(n_s, len_s, D_MODEL),
            st_prompt, st_sample, vn.reshape(n_s, len_s, D_GMLP))


def kernel(x_prompt, x_sample, state_gla, p_prompt, p_sample, norm1_g, w_in, w_gk2, b_gk, gla_norm_g, v_norm_g,
           v_norm_b, w_sp, b_sp, w_o, norm2_g, w_router, b_router, w_gate, b_gate, w_up, b_up, w_down, b_down,
           norm3_g, w_ple_gate, w_ple_proj, final_g):
    assert w_in.shape[0] == 1, "single-layer trunk"
    wts = _prepare_weights(norm1_g[0], w_in[0], w_gk2[0], b_gk[0], gla_norm_g[0], v_norm_g[0], v_norm_b[0],
                           w_sp[0], b_sp[0], w_o[0], norm2_g[0], w_router[0], b_router[0], w_gate[0], b_gate[0],
                           w_up[0], b_up[0], w_down[0], b_down[0], norm3_g[0], w_ple_gate[0], w_ple_proj[0],
                           final_g)
    y_p, y_s, st_p, st_s, vn_s = _layer(x_prompt, x_sample, state_gla[0], p_prompt[0], p_sample[0], wts)
    return (y_p, y_s, st_p[None], st_s[None].astype(state_gla.dtype), vn_s[None])
```
